```python
import jax
import jax.numpy as jnp
from jax import lax

D_MODEL = 1024
BATCH = 1
SEQ = 16384
DEPTH = 2
DEC_BATCH = 32
DEC_SEQ = 8
PAST_LEN = 16384
PAGE_SIZE = 128

HEAD_DIM = 64
MIX_W = 256
N_BRANCH = 4
A_HEADS = 4
IDX_HEADS = 4
IDX_DIM = 64
A_TOPK = 256
B_HEADS = 4
CMP_LEN = 32
CMP_STRIDE = 16
SLC_BLOCK = 64
N_SLC = 16
WINDOW = 512
CONV_W = 3
D_HEADS = 4
D_DK = 64
D_DV = 64
GATE_RANK = 16
GATE_TAU = 16.0
GLA_CHUNK = 64
D_FF = 2816
Q_BLOCK = 128
EPS = 1e-6

IN_SPLITS = (
    ('a_q', A_HEADS * HEAD_DIM), ('a_k', A_HEADS * HEAD_DIM), ('a_v', A_HEADS * HEAD_DIM),
    ('a_qi', IDX_HEADS * IDX_DIM), ('a_ki', IDX_DIM), ('a_wi', IDX_HEADS),
    ('b_q', B_HEADS * HEAD_DIM), ('b_cmp', 2 * HEAD_DIM), ('b_slc', 2 * HEAD_DIM), ('b_win', 2 * HEAD_DIM),
    ('b_g', 3 * B_HEADS),
    ('c_in', MIX_W), ('c_b', MIX_W), ('c_c', MIX_W),
    ('d_q', D_HEADS * D_DK), ('d_k', D_HEADS * D_DK), ('d_v', D_HEADS * D_DV), ('d_r', D_HEADS * D_DV),
    ('d_a', GATE_RANK),
    ('gate', N_BRANCH * D_MODEL),
)
N_IN = sum(n for _, n in IN_SPLITS)

kernel_name = 'hybrid_dsa_nsa_conv_gla_decoder_step'


def rms_norm(x, g):
    x32 = x.astype(jnp.float32)
    y = x32 * lax.rsqrt(jnp.mean(x32 * x32, axis=-1, keepdims=True) + EPS)
    return (y * g.astype(jnp.float32)).astype(x.dtype)


def heads(t, n):
    return t.reshape(t.shape[:-1] + (n, t.shape[-1] // n))


def split_proj(p):
    out, off = {}, 0
    for name, n in IN_SPLITS:
        out[name] = p[..., off:off + n]
        off += n
    return out


def masked_softmax(s, mask):
    s = jnp.where(mask, s, -jnp.inf)
    m = jnp.max(s, axis=-1, keepdims=True)
    e = jnp.where(mask, jnp.exp(s - jnp.where(jnp.isfinite(m), m, 0.0)), 0.0)
    return e / jnp.maximum(jnp.sum(e, axis=-1, keepdims=True), 1e-30)


def take_rows(rows, pos):
    return jax.vmap(lambda r, p: r[p])(rows, pos)


def fetch_paged(pool, layer, page_table, new_rows, pos):
    past_len = page_table.shape[1] * PAGE_SIZE
    flat = pos.reshape(pos.shape[0], -1)
    past = jnp.clip(flat, 0, past_len - 1)
    phys = jnp.take_along_axis(page_table, past // PAGE_SIZE, axis=1)
    rows_past = pool[layer, phys, past % PAGE_SIZE]
    rows_new = take_rows(new_rows, jnp.clip(flat - past_len, 0, new_rows.shape[1] - 1))
    rows = jnp.where((flat < past_len)[..., None], rows_past, rows_new)
    return rows.reshape(pos.shape + rows.shape[-1:])


def gather_past(pool, layer, page_table):
    b, n = page_table.shape
    return pool[layer, page_table].reshape(b, n * PAGE_SIZE, pool.shape[-1])


def compress_blocks(rows, w_pos):
    b, length, f = rows.shape
    n16 = -(-length // CMP_STRIDE)
    rows = jnp.pad(rows, ((0, 0), (0, n16 * CMP_STRIDE - length), (0, 0))).reshape(b, n16, CMP_STRIDE, f)
    w = jax.nn.softmax(w_pos.astype(jnp.float32)).astype(rows.dtype)
    first = jnp.einsum('bnjf,j->bnf', rows, w[:CMP_STRIDE])
    second = jnp.einsum('bnjf,j->bnf', rows, w[CMP_STRIDE:])
    return first[:, :-1] + second[:, 1:]


def dsa_attend(q, qi, wi, ki_all, qpos, fetch_kv, k_sel):
    length = ki_all.shape[1]
    dots = jnp.einsum('bthd,bsd->bths', qi, ki_all)
    score = jnp.einsum('bth,bths->bts', wi, jax.nn.relu(dots)).astype(jnp.float32)
    causal = jnp.arange(length)[None, :] <= qpos[:, None]
    score = jnp.where(causal[None], score, -jnp.inf)
    _, sel = lax.top_k(score, k_sel)
    k_rows, v_rows = fetch_kv(sel)
    s = jnp.einsum('bthd,btkhd->bthk', q, k_rows).astype(jnp.float32) * HEAD_DIM ** -0.5
    p = masked_softmax(s, (sel <= qpos[None, :, None])[:, :, None, :])
    return jnp.einsum('bthk,btkhd->bthd', p.astype(v_rows.dtype), v_rows)


def nsa_attend(q, gl, kc, vc, fetch_slc, kw, vw, kwpos, qpos, seq_len):
    scale = HEAD_DIM ** -0.5
    nc = kc.shape[1]
    c_start = jnp.arange(nc) * CMP_STRIDE
    c_mask = (c_start + CMP_LEN - 1)[None, :] <= qpos[:, None]
    s_c = jnp.einsum('bthd,bnd->bthn', q, kc).astype(jnp.float32) * scale
    p_c = masked_softmax(s_c, c_mask[None, :, None, :])
    o_c = jnp.einsum('bthn,bnd->bthd', p_c.astype(vc.dtype), vc)
    ns = -(-seq_len // SLC_BLOCK)
    s_start = jnp.arange(ns) * SLC_BLOCK
    overlap = (c_start[:, None] < s_start[None, :] + SLC_BLOCK) & (c_start[:, None] + CMP_LEN > s_start[None, :])
    imp = jnp.einsum('btn,nj->btj', jnp.sum(p_c, axis=2), overlap.astype(jnp.float32))
    cur = qpos[:, None] // SLC_BLOCK
    blk = jnp.arange(ns)[None, :]
    forced = (blk == 0) | (blk == cur) | (blk == cur - 1)
    imp = jnp.where(forced, jnp.inf, jnp.where(blk <= cur, imp, -jnp.inf))
    _, sel = lax.top_k(imp, min(N_SLC, ns))
    pos = (sel[..., None] * SLC_BLOCK + jnp.arange(SLC_BLOCK)).reshape(sel.shape[:2] + (-1,))
    ks, vs = fetch_slc(pos)
    s_s = jnp.einsum('bthd,btkd->bthk', q, ks).astype(jnp.float32) * scale
    p_s = masked_softmax(s_s, (pos <= qpos[None, :, None])[:, :, None, :])
    o_s = jnp.einsum('bthk,btkd->bthd', p_s.astype(vs.dtype), vs)
    rel = qpos[:, None] - kwpos[None, :]
    w_mask = (rel >= 0) & (rel < WINDOW) & (kwpos[None, :] >= 0)
    s_w = jnp.einsum('bthd,bsd->bths', q, kw).astype(jnp.float32) * scale
    p_w = masked_softmax(s_w, w_mask[None, :, None, :])
    o_w = jnp.einsum('bths,bsd->bthd', p_w.astype(vw.dtype), vw)
    g = jax.nn.sigmoid(gl.astype(jnp.float32)).astype(q.dtype)
    return g[..., 0:1] * o_c + g[..., 1:2] * o_s + g[..., 2:3] * o_w


def causal_dwconv(u, prev, w):
    t = u.shape[1]
    full = jnp.concatenate([prev.astype(u.dtype), u], axis=1)
    y = full[:, 0:t] * w[0]
    for j in range(1, CONV_W):
        y = y + full[:, j:j + t] * w[j]
    return y, full[:, t:]


def gla_chunked(q, k, v, log_a, s0):
    b, t, h, dk = q.shape
    c = min(GLA_CHUNK, t)
    n = -(-t // c)
    pad = n * c - t

    def prep(z):
        z = jnp.pad(z.astype(jnp.float32), ((0, 0), (0, pad), (0, 0), (0, 0)))
        return z.reshape(b, n, c, h, -1).transpose(1, 0, 3, 2, 4)

    qs, ks, vs, las = prep(q * dk ** -0.5), prep(k), prep(v), prep(log_a)
    causal = jnp.tril(jnp.ones((c, c), dtype=bool))

    def step(s, inp):
        qc, kc, vc, lac = inp
        cum = jnp.cumsum(lac, axis=2)
        o_inter = jnp.einsum('bhtd,bhde->bhte', qc * jnp.exp(cum), s)
        diff = cum[:, :, :, None, :] - cum[:, :, None, :, :]
        decay = jnp.exp(jnp.where(causal[:, :, None], diff, -jnp.inf))
        att = jnp.einsum('bhtd,bhsd,bhtsd->bhts', qc, kc, decay)
        o = o_inter + jnp.einsum('bhts,bhse->bhte', att, vc)
        last = cum[:, :, -1:, :]
        s_new = s * jnp.exp(last[:, :, 0, :, None]) + jnp.einsum('bhsd,bhse->bhde', kc * jnp.exp(last - cum), vc)
        return s_new, o

    s_fin, o = lax.scan(step, s0.astype(jnp.float32), (qs, ks, vs, las))
    o = o.transpose(1, 0, 3, 2, 4).reshape(b, n * c, h, -1)[:, :t]
    return o, s_fin


def prompt_attend(pr, lp):
    b, s, _ = pr['a_q'].shape
    k_a, v_a = heads(pr['a_k'], A_HEADS), heads(pr['a_v'], A_HEADS)
    ki = pr['a_ki']
    kc, vc = jnp.split(compress_blocks(pr['b_cmp'], lp['w_cmp_pos']), 2, axis=-1)
    ks, vs = jnp.split(pr['b_slc'], 2, axis=-1)
    win_pad = jnp.pad(pr['b_win'], ((0, 0), (WINDOW, 0), (0, 0)))
    k_sel = min(A_TOPK, s // 4)
    nqb = s // Q_BLOCK

    def blocks(t):
        return jnp.moveaxis(t.reshape((b, nqb, Q_BLOCK) + t.shape[2:]), 1, 0)

    def body(inp):
        blk_i, qa_blk, qi_blk, wi_blk, qb_blk, gb_blk = inp
        qpos = blk_i * Q_BLOCK + jnp.arange(Q_BLOCK)
        o_a = dsa_attend(qa_blk, qi_blk, wi_blk, ki, qpos,
                         lambda pos: (take_rows(k_a, pos), take_rows(v_a, pos)), k_sel)
        win = lax.dynamic_slice_in_dim(win_pad, blk_i * Q_BLOCK, Q_BLOCK + WINDOW, axis=1)
        kw, vw = jnp.split(win, 2, axis=-1)
        kwpos = blk_i * Q_BLOCK - WINDOW + jnp.arange(Q_BLOCK + WINDOW)
        o_b = nsa_attend(qb_blk, gb_blk, kc, vc, lambda pos: (take_rows(ks, pos), take_rows(vs, pos)),
                         kw, vw, kwpos, qpos, s)
        return o_a, o_b

    o_a, o_b = lax.map(body, (jnp.arange(nqb), blocks(heads(pr['a_q'], A_HEADS)), blocks(heads(pr['a_qi'], IDX_HEADS)),
                              blocks(pr['a_wi']), blocks(heads(pr['b_q'], B_HEADS)), blocks(heads(pr['b_g'], B_HEADS))))

    def unblock(t):
        return jnp.moveaxis(t, 0, 1).reshape((b, s) + t.shape[3:])

    new_rows = (jnp.concatenate([pr['a_k'], pr['a_v']], axis=-1), pr['a_ki'], pr['b_cmp'], pr['b_slc'],
                pr['b_win'][:, s - min(WINDOW, s):])
    return unblock(o_a), unblock(o_b), new_rows


def sample_attend(pr, lp, layer, pool_a_kv, pool_a_idx, pool_b_cmp, pool_b_slc, win_buf, page_table):
    b, t, _ = pr['a_q'].shape
    past_len = page_table.shape[1] * PAGE_SIZE
    seq_len = past_len + t
    qpos = past_len + jnp.arange(t)
    a_kv_new = jnp.concatenate([pr['a_k'], pr['a_v']], axis=-1)

    def fetch_a(pos):
        k, v = jnp.split(fetch_paged(pool_a_kv, layer, page_table, a_kv_new, pos), 2, axis=-1)
        return heads(k, A_HEADS), heads(v, A_HEADS)

    ki_all = jnp.concatenate([gather_past(pool_a_idx, layer, page_table).astype(pr['a_ki'].dtype), pr['a_ki']], axis=1)
    o_a = dsa_attend(heads(pr['a_q'], A_HEADS), heads(pr['a_qi'], IDX_HEADS), pr['a_wi'], ki_all, qpos,
                     fetch_a, min(A_TOPK, seq_len // 4))
    cmp_all = jnp.concatenate([gather_past(pool_b_cmp, layer, page_table).astype(pr['b_cmp'].dtype), pr['b_cmp']], axis=1)
    kc, vc = jnp.split(compress_blocks(cmp_all, lp['w_cmp_pos']), 2, axis=-1)

    def fetch_b(pos):
        return tuple(jnp.split(fetch_paged(pool_b_slc, layer, page_table, pr['b_slc'], pos), 2, axis=-1))

    w_buf = win_buf.shape[1]
    win = jnp.concatenate([win_buf.astype(pr['b_win'].dtype), pr['b_win']], axis=1)
    kw, vw = jnp.split(win, 2, axis=-1)
    kwpos = past_len - w_buf + jnp.arange(w_buf + t)
    o_b = nsa_attend(heads(pr['b_q'], B_HEADS), heads(pr['b_g'], B_HEADS), kc, vc, fetch_b, kw, vw, kwpos, qpos, seq_len)
    new_rows = (a_kv_new, pr['a_ki'], pr['b_cmp'], pr['b_slc'], win[:, t:])
    return o_a, o_b, new_rows


def layer_step(x, c, lp, attend, conv_prev, gla_prev, ffn_prev):
    b, t, _ = x.shape
    mod = (c @ lp['w_ada'] + lp['b_ada'])[:, None, :]
    sh1, sc1, g1, sh2, sc2, g2 = jnp.split(mod, 6, axis=-1)
    h = rms_norm(x, lp['g_norm1']) * (1 + sc1) + sh1
    pr = split_proj(h @ lp['w_in'])
    o_a, o_b, attn_state = attend(pr)
    o_c, conv_state = causal_dwconv(pr['c_c'] * pr['c_in'], conv_prev, lp['conv_c'])
    o_c = pr['c_b'] * o_c
    la = jax.nn.log_sigmoid((pr['d_a'] @ lp['w_alpha'] + lp['b_alpha']).astype(jnp.float32)) / GATE_TAU
    o_d, gla_state = gla_chunked(heads(pr['d_q'], D_HEADS), heads(pr['d_k'], D_HEADS), heads(pr['d_v'], D_HEADS),
                                 heads(la, D_HEADS), gla_prev)
    o_d = rms_norm(o_d.astype(x.dtype), lp['g_gla']).reshape(b, t, MIX_W) * jax.nn.silu(pr['d_r'])
    branches = jnp.stack([o_a.reshape(b, t, MIX_W), o_b.reshape(b, t, MIX_W), o_c, o_d], axis=2)
    gates = jax.nn.sigmoid(pr['gate'].reshape(b, t, N_BRANCH, D_MODEL))
    merged = jnp.sum(gates * jnp.einsum('btim,imd->btid', branches, lp['w_branch']), axis=2)
    x = x + g1 * (merged @ lp['w_out'])
    h2 = rms_norm(x, lp['g_norm2']) * (1 + sc2) + sh2
    a, g = jnp.split(h2 @ lp['w_ffn_in'], 2, axis=-1)
    a_conv, ffn_state = causal_dwconv(a, ffn_prev, lp['conv_ffn'])
    x = x + g2 * ((jax.nn.silu(a_conv + lp['b_ffn']) * g) @ lp['w_ffn_out'])
    return x, attn_state + (conv_state, gla_state.astype(x.dtype), ffn_state)


def setup_inputs(seed: int = 0) -> dict:
    key = jax.random.key(seed)
    keys = iter(jax.random.split(key, 40))

    def nrm(shape, scale):
        return scale * jax.random.normal(next(keys), shape, jnp.float32)

    def gain(shape):
        return 1.0 + nrm(shape, 0.01)

    n_pages = PAST_LEN // PAGE_SIZE
    n_pool = (DEC_BATCH * n_pages * 5) // 4
    w_buf = min(WINDOW, PAST_LEN)
    perm = jax.random.permutation(next(keys), n_pool)
    page_table = perm[:DEC_BATCH * n_pages].reshape(DEC_BATCH, n_pages).astype(jnp.int32)
    return {
        'x_prompt': nrm((BATCH, SEQ, D_MODEL), 1.0),
        'x_sample': nrm((DEC_BATCH, DEC_SEQ, D_MODEL), 1.0),
        'cache_a_kv': nrm((DEPTH, n_pool, PAGE_SIZE, 2 * MIX_W), 1.0),
        'cache_a_idx': nrm((DEPTH, n_pool, PAGE_SIZE, IDX_DIM), 1.0),
        'cache_b_cmp': nrm((DEPTH, n_pool, PAGE_SIZE, 2 * HEAD_DIM), 1.0),
        'cache_b_slc': nrm((DEPTH, n_pool, PAGE_SIZE, 2 * HEAD_DIM), 1.0),
        'state_b_win': nrm((DEPTH, DEC_BATCH, w_buf, 2 * HEAD_DIM), 1.0),
        'state_c_conv': nrm((DEPTH, DEC_BATCH, CONV_W - 1, MIX_W), 1.0),
        'state_d_gla': nrm((DEPTH, DEC_BATCH, D_HEADS, D_DK, D_DV), 1.0),
        'state_ffn_conv': nrm((DEPTH, DEC_BATCH, CONV_W - 1, D_FF), 1.0),
        'page_table': page_table,
        'c_prompt': nrm((BATCH, D_MODEL), 1.0),
        'c_sample': nrm((DEC_BATCH, D_MODEL), 1.0),
        'w_ada': nrm((DEPTH, D_MODEL, 6 * D_MODEL), 0.5 * D_MODEL ** -0.5),
        'b_ada': nrm((DEPTH, 6 * D_MODEL), 0.01),
        'g_norm1': gain((DEPTH, D_MODEL)),
        'w_in': nrm((DEPTH, D_MODEL, N_IN), D_MODEL ** -0.5),
        'w_cmp_pos': nrm((DEPTH, CMP_LEN), 0.1),
        'conv_c': nrm((DEPTH, CONV_W, MIX_W), CONV_W ** -0.5),
        'w_alpha': nrm((DEPTH, GATE_RANK, D_HEADS * D_DK), GATE_RANK ** -0.5),
        'b_alpha': nrm((DEPTH, D_HEADS * D_DK), 0.1),
        'g_gla': gain((DEPTH, D_DV)),
        'w_branch': nrm((DEPTH, N_BRANCH, MIX_W, D_MODEL), MIX_W ** -0.5),
        'w_out': nrm((DEPTH, D_MODEL, D_MODEL), D_MODEL ** -0.5),
        'g_norm2': gain((DEPTH, D_MODEL)),
        'w_ffn_in': nrm((DEPTH, D_MODEL, 2 * D_FF), D_MODEL ** -0.5),
        'conv_ffn': nrm((DEPTH, CONV_W, D_FF), CONV_W ** -0.5),
        'b_ffn': nrm((DEPTH, D_FF), 0.01),
        'w_ffn_out': nrm((DEPTH, D_FF, D_MODEL), D_FF ** -0.5),
        'g_final': gain((D_MODEL,)),
    }


def reference(x_prompt, x_sample, cache_a_kv, cache_a_idx, cache_b_cmp, cache_b_slc, state_b_win,
              state_c_conv, state_d_gla, state_ffn_conv, page_table, c_prompt, c_sample,
              w_ada, b_ada, g_norm1, w_in, w_cmp_pos, conv_c, w_alpha, b_alpha, g_gla,
              w_branch, w_out, g_norm2, w_ffn_in, conv_ffn, b_ffn, w_ffn_out, g_final):
    xp, xs = x_prompt, x_sample
    bp = xp.shape[0]
    states_p, states_s = [], []
    for l in range(DEPTH):
        lp = {'w_ada': w_ada[l], 'b_ada': b_ada[l], 'g_norm1': g_norm1[l], 'w_in': w_in[l],
              'w_cmp_pos': w_cmp_pos[l], 'conv_c': conv_c[l], 'w_alpha': w_alpha[l], 'b_alpha': b_alpha[l],
              'g_gla': g_gla[l], 'w_branch': w_branch[l], 'w_out': w_out[l], 'g_norm2': g_norm2[l],
              'w_ffn_in': w_ffn_in[l], 'conv_ffn': conv_ffn[l], 'b_ffn': b_ffn[l], 'w_ffn_out': w_ffn_out[l]}
        xp, st_p = layer_step(xp, c_prompt, lp, lambda pr, lp=lp: prompt_attend(pr, lp),
                              jnp.zeros((bp, CONV_W - 1, MIX_W), xp.dtype),
                              jnp.zeros((bp, D_HEADS, D_DK, D_DV), xp.dtype),
                              jnp.zeros((bp, CONV_W - 1, D_FF), xp.dtype))
        states_p.append(st_p)
        xs, st_s = layer_step(xs, c_sample, lp,
                              lambda pr, lp=lp, l=l: sample_attend(pr, lp, l, cache_a_kv, cache_a_idx, cache_b_cmp,
                                                                   cache_b_slc, state_b_win[l], page_table),
                              state_c_conv[l], state_d_gla[l], state_ffn_conv[l])
        states_s.append(st_s)
    y_prompt = rms_norm(xp, g_final)
    y_sample = rms_norm(xs, g_final)
    (a_kv_p, a_idx_p, b_cmp_p, b_slc_p, b_win_p, c_conv_p, d_gla_p, ffn_conv_p) = [jnp.stack(z) for z in zip(*states_p)]
    (a_kv_s, a_idx_s, b_cmp_s, b_slc_s, b_win_s, c_conv_s, d_gla_s, ffn_conv_s) = [jnp.stack(z) for z in zip(*states_s)]
    return (y_prompt, y_sample, a_kv_p, a_kv_s, a_idx_p, a_idx_s, b_cmp_p, b_cmp_s, b_slc_p, b_slc_s,
            b_win_p, b_win_s, c_conv_p, c_conv_s, d_gla_p, d_gla_s, ffn_conv_p, ffn_conv_s)
```

```python
import functools

import numpy as np
import jax
import jax.numpy as jnp
from jax import lax
from jax.experimental import pallas as pl
from jax.experimental.pallas import tpu as pltpu

F32 = jnp.float32
BF16 = jnp.bfloat16
HI = lax.Precision.HIGHEST

D_MODEL = 1024
PAGE = 128
HEAD_DIM = 64
MIX_W = 256
N_HEADS = 4
IDX_DIM = 64
A_TOPK = 256
CMP_LEN = 32
CMP_STRIDE = 16
SLC_BLOCK = 64
N_SLC = 16
WINDOW = 512
CONV_W = 3
GATE_RANK = 16
GATE_TAU = 16.0
GLA_CHUNK = 64
D_FF = 2816
EPS = 1e-6
NEG = -1e30
INT_MIN = -2 ** 31
KEY_TILE = 512

_IN_SPLITS = (('a_q', 256), ('a_k', 256), ('a_v', 256), ('a_qi', 256), ('a_ki', 64), ('a_wi', 4),
              ('b_q', 256), ('b_cmp', 128), ('b_slc', 128), ('b_win', 128), ('b_g', 12),
              ('c_in', 256), ('c_b', 256), ('c_c', 256),
              ('d_q', 256), ('d_k', 256), ('d_v', 256), ('d_r', 256), ('d_a', 16), ('gate', 4096))
_OFF = dict(gate=0, a_k=4096, a_v=4352, a_q=4608, a_qi=4864, b_q=5120, b_cmp=5376, b_slc=5504, b_win=5632,
            misc=5760, c_in=5888, c_b=6144, c_c=6400, d_q=6656, d_k=6912, d_v=7168, d_r=7424)
NP = 7680
MISC_KI, MISC_WI, MISC_G, MISC_DA = 0, 64, 68, 80
VMEM_LIMIT = 56 * 1024 * 1024


def _cp(sem):
    return pltpu.CompilerParams(dimension_semantics=sem, vmem_limit_bytes=VMEM_LIMIT)


def _dot(a, b):
    return jnp.dot(a, b, preferred_element_type=F32)


def _dot_nt(a, b):
    return lax.dot_general(a, b, (((1,), (1,)), ((), ())), preferred_element_type=F32)


def _dot_hi(a, b):
    return jnp.dot(a, b, preferred_element_type=F32, precision=HI)


def _const_spec(shape):
    nd = len(shape)
    return pl.BlockSpec(shape, lambda *a: (0,) * nd, pipeline_mode=pl.Buffered(1))


def _iota(shape, dim):
    return lax.broadcasted_iota(jnp.int32, shape, dim)


def _ada_kernel(c_ref, w_ref, b_ref, o_ref):
    o_ref[0] = _dot_hi(c_ref[...], w_ref[0]) + b_ref[0]


def _ada(c_all, w_ada, b_ada):
    depth, d, n6 = w_ada.shape
    rows = c_all.shape[0]
    tn = 1024
    return pl.pallas_call(
        _ada_kernel, grid=(depth, n6 // tn),
        in_specs=[pl.BlockSpec((rows, d), lambda l, j: (0, 0)),
                  pl.BlockSpec((1, d, tn), lambda l, j: (l, 0, j)),
                  pl.BlockSpec((1, 1, tn), lambda l, j: (l, 0, j))],
        out_specs=pl.BlockSpec((1, rows, tn), lambda l, j: (l, 0, j)),
        out_shape=jax.ShapeDtypeStruct((depth, rows, n6), F32),
        compiler_params=_cp(("arbitrary", "arbitrary")), name="ada",
    )(c_all, w_ada, b_ada.reshape(depth, 1, n6))


def _norm_mod(x, g, sc, sh):
    y = x * lax.rsqrt(jnp.mean(x * x, axis=-1, keepdims=True) + EPS) * g
    return y * (1.0 + sc) + sh


def _proj_kernel(x_ref, g_ref, sc_ref, sh_ref, w_ref, o32_ref, o16_ref, h_ref):
    @pl.when(pl.program_id(1) == 0)
    def _():
        h_ref[...] = _norm_mod(x_ref[...], g_ref[...], sc_ref[0], sh_ref[0]).astype(BF16)

    acc = _dot(h_ref[...], w_ref[...])
    o32_ref[...] = acc
    o16_ref[...] = acc.astype(BF16)


def _mod_spec(mod, tm, rows_per_group):
    mb = mod.shape[1]
    tiles = max(rows_per_group // tm, 1)
    return pl.BlockSpec((1, mb, D_MODEL), lambda i, *_: (i // tiles, 0, 0))


def _proj(x, g, sc, sh, w, tm, rows_per_group):
    rows = x.shape[0]
    tn = 768
    return pl.pallas_call(
        _proj_kernel, grid=(rows // tm, NP // tn),
        in_specs=[pl.BlockSpec((tm, D_MODEL), lambda i, j: (i, 0)),
                  pl.BlockSpec((1, D_MODEL), lambda i, j: (0, 0)),
                  _mod_spec(sc, tm, rows_per_group), _mod_spec(sh, tm, rows_per_group),
                  pl.BlockSpec((D_MODEL, tn), lambda i, j: (0, j))],
        out_specs=[pl.BlockSpec((tm, tn), lambda i, j: (i, j)), pl.BlockSpec((tm, tn), lambda i, j: (i, j))],
        out_shape=[jax.ShapeDtypeStruct((rows, NP), F32), jax.ShapeDtypeStruct((rows, NP), BF16)],
        scratch_shapes=[pltpu.VMEM((tm, D_MODEL), BF16)],
        compiler_params=_cp(("arbitrary", "arbitrary")), name="proj",
    )(x, g, sc, sh, w)


def _gather_kernel(pt_ref, *refs, layer, n_seq, n_pages, n_arr, window):
    pools, tails, outs = refs[:n_arr], refs[n_arr:2 * n_arr], refs[2 * n_arr:3 * n_arr]
    sem_p, sem_t = refs[3 * n_arr], refs[3 * n_arr + 1]
    total = n_seq * n_pages
    tail_rows = tails[0].shape[1]

    def page_copy(k, s):
        b = s // n_pages
        p = s - b * n_pages
        return pltpu.make_async_copy(pools[k].at[layer, pt_ref[b, p]], outs[k].at[b, pl.ds(p * PAGE, PAGE)],
                                     sem_p.at[k])

    def tail_copy(k, b):
        return pltpu.make_async_copy(tails[k].at[b], outs[k].at[b, pl.ds(n_pages * PAGE, tail_rows)], sem_t.at[k])

    def issue(s, c):
        for k in range(n_arr):
            page_copy(k, s).start()

        @pl.when(s >= window)
        def _():
            for k in range(n_arr):
                page_copy(k, s - window).wait()
        return c

    lax.fori_loop(0, total, issue, 0)

    def drain(s, c):
        for k in range(n_arr):
            page_copy(k, s).wait()
        return c

    lax.fori_loop(total - window, total, drain, 0)

    def tail_issue(b, c):
        for k in range(n_arr):
            tail_copy(k, b).start()
        return c

    lax.fori_loop(0, n_seq, tail_issue, 0)

    def tail_drain(b, c):
        for k in range(n_arr):
            tail_copy(k, b).wait()
        return c

    lax.fori_loop(0, n_seq, tail_drain, 0)


def _gather_past(page_table, pools, tails, layer):
    n_seq, n_pages = page_table.shape
    n_arr = len(pools)
    lp = n_pages * PAGE + tails[0].shape[1]
    any_spec = pl.BlockSpec(memory_space=pl.ANY)
    kern = functools.partial(_gather_kernel, layer=layer, n_seq=n_seq, n_pages=n_pages, n_arr=n_arr,
                             window=min(8, n_seq * n_pages))
    return pl.pallas_call(
        kern,
        in_specs=[pl.BlockSpec(memory_space=pltpu.SMEM)] + [any_spec] * (2 * n_arr),
        out_specs=[any_spec] * n_arr,
        out_shape=[jax.ShapeDtypeStruct((n_seq, lp, p.shape[-1]), p.dtype) for p in pools],
        scratch_shapes=[pltpu.SemaphoreType.DMA((n_arr,)), pltpu.SemaphoreType.DMA((n_arr,))],
        name="gather_past",
    )(page_table, *pools, *tails)


def _stack_heads(q, width):
    t = q.shape[0]
    parts = []
    for h in range(N_HEADS):
        p = q[:, h * HEAD_DIM:(h + 1) * HEAD_DIM]
        if width > HEAD_DIM:
            p = jnp.concatenate([p, jnp.zeros((t, width - HEAD_DIM), q.dtype)], axis=1)
        parts.append(p)
    return jnp.concatenate(parts, axis=0)


def _dsa_sel_kernel(qi_ref, misc_ref, ki_ref, bias_ref, s_ref, *, tq, nk, k_sel, qoff, fk):
    tk = KEY_TILE
    i = pl.program_id(1)
    q0 = qoff + i * tq
    qpos = q0 + _iota((tq, 1), 0)
    nlim = jnp.minimum((q0 + tq - 1) // tk + 1, nk)
    qs = _stack_heads(qi_ref[...].astype(BF16), fk)
    wi = misc_ref[:, MISC_WI:MISC_WI + N_HEADS]

    def score_tile(j, c):
        kt = ki_ref[0, pl.ds(pl.multiple_of(j * tk, tk), tk), :].astype(BF16)
        d = _dot_nt(qs, kt).reshape(N_HEADS, tq, tk)
        sc = wi[:, 0:1] * jnp.maximum(d[0], 0.0)
        for h in range(1, N_HEADS):
            sc = sc + wi[:, h:h + 1] * jnp.maximum(d[h], 0.0)
        bits = lax.bitcast_convert_type(sc, jnp.int32)
        key = bits ^ ((bits >> 31) & 0x7FFFFFFF)
        key = jnp.where(key == -1, 0, key)
        kpos = j * tk + _iota((1, tk), 1)
        s_ref[j] = jnp.where(kpos <= qpos, key, INT_MIN)
        return c

    lax.fori_loop(0, nlim, score_tile, 0)

    def count(pred):
        def body(j, acc):
            return acc + jnp.where(pred(s_ref[j]), 1.0, 0.0)
        return jnp.sum(lax.fori_loop(0, nlim, body, jnp.zeros((tq, tk), F32)), axis=-1, keepdims=True)

    def bit_step(t, c):
        trial = c + lax.shift_left(jnp.int32(1), 31 - t)
        return jnp.where(count(lambda s: s >= trial) >= k_sel, trial, c)

    thr = lax.fori_loop(0, 32, bit_step, jnp.full((tq, 1), INT_MIN, jnp.int32))
    ties_allowed = k_sel - count(lambda s: s > thr)
    tri = jnp.where(_iota((tk, tk), 0) <= _iota((tk, tk), 1), 1.0, 0.0).astype(BF16)

    def out_tile(j, carry):
        key = s_ref[j]
        eq = jnp.where(key == thr, 1.0, 0.0)
        rank = _dot(eq.astype(BF16), tri) + carry
        kpos = j * tk + _iota((1, tk), 1)
        take = jnp.where(key > thr, 1.0, jnp.where(rank <= ties_allowed, eq, 0.0))
        take = jnp.where(kpos <= qpos, take, 0.0)
        bias_ref[j] = jnp.where(take > 0.5, 0.0, NEG).astype(bias_ref.dtype)
        return rank[:, tk - 1:tk]

    lax.fori_loop(0, nlim, out_tile, jnp.zeros((tq, 1), F32))

    def fill(j, c):
        bias_ref[j] = jnp.full((tq, tk), NEG, bias_ref.dtype)
        return c

    lax.fori_loop(nlim, nk, fill, 0)


def _dsa_select(q_arr, p32, ki3, ki_col, fk, *, n_seq, t_seq, tq, qoff, k_sel, bias_dtype):
    lp = ki3.shape[1]
    nk, nq = lp // KEY_TILE, t_seq // tq
    rows = n_seq * t_seq
    kern = functools.partial(_dsa_sel_kernel, tq=tq, nk=nk, k_sel=k_sel, qoff=qoff, fk=fk)
    return pl.pallas_call(
        kern, grid=(n_seq, nq),
        in_specs=[pl.BlockSpec((tq, MIX_W), lambda b, i: (b * nq + i, _OFF['a_qi'] // MIX_W)),
                  pl.BlockSpec((tq, 128), lambda b, i: (b * nq + i, _OFF['misc'] // 128)),
                  pl.BlockSpec((1, lp, fk), lambda b, i: (b, 0, ki_col))],
        out_specs=pl.BlockSpec((nk, tq, KEY_TILE), lambda b, i: (0, b * nq + i, 0)),
        out_shape=jax.ShapeDtypeStruct((nk, rows, KEY_TILE), bias_dtype),
        scratch_shapes=[pltpu.VMEM((nk, tq, KEY_TILE), jnp.int32)],
        compiler_params=_cp(("arbitrary", "arbitrary")), name="dsa_select",
    )(q_arr, p32, ki3)


def _flash_kernel(sb, sq, sk, sfl, q_ref, kv_ref, bias_ref, *rest, tq, tkm, shared, gated):
    if gated:
        misc_ref, add_ref, o_ref, m_ref, l_ref, acc_ref = rest
    else:
        o_ref, m_ref, l_ref, acc_ref = rest
    tk = KEY_TILE
    flags = sfl[pl.program_id(0)]
    dv = HEAD_DIM if shared else MIX_W
    lane_head = _iota((1, MIX_W), 1) // HEAD_DIM

    @pl.when((flags & 1) != 0)
    def _():
        m_ref[...] = jnp.full(m_ref.shape, NEG, F32)
        l_ref[...] = jnp.zeros(l_ref.shape, F32)
        acc_ref[...] = jnp.zeros(acc_ref.shape, F32)

    q = (q_ref[...] * (HEAD_DIM ** -0.5)).astype(BF16)
    if shared:
        qs = _stack_heads(q, HEAD_DIM)
    else:
        qs = jnp.concatenate([jnp.where(lane_head == h, q, jnp.zeros_like(q)) for h in range(N_HEADS)], axis=0)
    for u in range(tkm):
        kvt = kv_ref[0, u * tk:(u + 1) * tk, :]
        k = kvt[:, :dv].astype(BF16)
        v = kvt[:, dv:2 * dv].astype(BF16)
        s = _dot_nt(qs, k).reshape(N_HEADS, tq, tk) + bias_ref[u].astype(F32)[None]
        m_old = m_ref[...]
        m_new = jnp.maximum(m_old, jnp.max(s, axis=-1, keepdims=True))
        alpha = jnp.exp(m_old - m_new)
        p = jnp.exp(s - m_new)
        l_ref[...] = alpha * l_ref[...] + jnp.sum(p, axis=-1, keepdims=True)
        pv = _dot(p.reshape(N_HEADS * tq, tk).astype(BF16), v).reshape(N_HEADS, tq, dv)
        acc_ref[...] = alpha * acc_ref[...] + pv
        m_ref[...] = m_new

    @pl.when((flags & 2) != 0)
    def _():
        o = acc_ref[...] / l_ref[...]
        if shared:
            out = jnp.concatenate([o[h] for h in range(N_HEADS)], axis=1)
        else:
            out = jnp.where(lane_head == 0, o[0], 0.0)
            for h in range(1, N_HEADS):
                out = out + jnp.where(lane_head == h, o[h], 0.0)
        if gated:
            g = jax.nn.sigmoid(misc_ref[...])
            gate = jnp.zeros((tq, MIX_W), F32)
            for h in range(N_HEADS):
                c = MISC_G + 3 * h + 1
                gate = gate + jnp.where(lane_head == h, g[:, c:c + 1], 0.0)
            out = add_ref[...].astype(F32) + gate * out
        o_ref[...] = out.astype(o_ref.dtype)


def _flash_steps(n_seq, t_seq, tq, tk, lp, qoff):
    sb, sq, sk, sfl = [], [], [], []
    for b in range(n_seq):
        for i in range(t_seq // tq):
            nlim = min((qoff + (i + 1) * tq - 1) // tk + 1, lp // tk)
            for j in range(nlim):
                sb.append(b), sq.append(i), sk.append(j)
                sfl.append((1 if j == 0 else 0) | (2 if j == nlim - 1 else 0))
    return [jnp.asarray(np.asarray(a, np.int32)) for a in (sb, sq, sk, sfl)]


def _flash(q_arr, q_col, kv3, kv_col, bias, *, n_seq, t_seq, tq, tkm, qoff, shared, out_dtype, gate_args=None):
    lp = kv3.shape[1]
    tk = tkm * KEY_TILE
    nq = t_seq // tq
    rows = n_seq * t_seq
    wblk = 2 * (HEAD_DIM if shared else MIX_W)
    dv = HEAD_DIM if shared else MIX_W
    steps = _flash_steps(n_seq, t_seq, tq, tk, lp, qoff)
    gated = gate_args is not None
    row = lambda s, sb, sq, sk, sfl: sb[s] * nq + sq[s]
    in_specs = [pl.BlockSpec((tq, MIX_W), lambda s, sb, sq, sk, sfl: (row(s, sb, sq, sk, sfl), q_col)),
                pl.BlockSpec((1, tk, wblk), lambda s, sb, sq, sk, sfl: (sb[s], sk[s], kv_col)),
                pl.BlockSpec((tkm, tq, KEY_TILE), lambda s, sb, sq, sk, sfl: (sk[s], row(s, sb, sq, sk, sfl), 0))]
    args = [q_arr, kv3, bias]
    if gated:
        in_specs += [pl.BlockSpec((tq, 128), lambda s, sb, sq, sk, sfl: (row(s, sb, sq, sk, sfl), _OFF['misc'] // 128)),
                     pl.BlockSpec((tq, MIX_W), lambda s, sb, sq, sk, sfl: (row(s, sb, sq, sk, sfl), 0))]
        args += list(gate_args)
    kern = functools.partial(_flash_kernel, tq=tq, tkm=tkm, shared=shared, gated=gated)
    return pl.pallas_call(
        kern,
        grid_spec=pltpu.PrefetchScalarGridSpec(
            num_scalar_prefetch=4, grid=(int(steps[0].shape[0]),), in_specs=in_specs,
            out_specs=pl.BlockSpec((tq, MIX_W), lambda s, sb, sq, sk, sfl: (row(s, sb, sq, sk, sfl), 0)),
            scratch_shapes=[pltpu.VMEM((N_HEADS, tq, 1), F32), pltpu.VMEM((N_HEADS, tq, 1), F32),
                            pltpu.VMEM((N_HEADS, tq, dv), F32)]),
        out_shape=jax.ShapeDtypeStruct((rows, MIX_W), out_dtype),
        compiler_params=_cp(("arbitrary",)), name="flash_shared" if shared else "flash_heads",
    )(*steps, *args)


def _compress_kernel(x_ref, w_ref, o_ref, *, n_blocks, ncp):
    w = w_ref[...]
    e = jnp.exp(w - jnp.max(w, axis=-1, keepdims=True))
    w = e / jnp.sum(e, axis=-1, keepdims=True)
    n16 = x_ref.shape[1]
    width = x_ref.shape[2] // CMP_STRIDE
    first = jnp.zeros((n16, width), F32)
    second = jnp.zeros((n16, width), F32)
    for j in range(CMP_STRIDE):
        xj = x_ref[0, :, j * width:(j + 1) * width]
        first = first + xj * w[:, j:j + 1]
        second = second + xj * w[:, CMP_STRIDE + j:CMP_STRIDE + j + 1]
    shifted = jnp.concatenate([second[1:], jnp.zeros((1, width), F32)], axis=0)
    out = first + shifted
    if n16 < ncp:
        out = jnp.concatenate([out, jnp.zeros((ncp - n16, width), F32)], axis=0)
    out = out[:ncp]
    o_ref[0] = jnp.where(_iota((ncp, 1), 0) < n_blocks, out, 0.0)


def _compress(rows3, w_pos, length, ncp):
    n_seq, lp, width = rows3.shape
    n16 = lp // CMP_STRIDE
    n_blocks = -(-length // CMP_STRIDE) - 1
    kern = functools.partial(_compress_kernel, n_blocks=n_blocks, ncp=ncp)
    return pl.pallas_call(
        kern, grid=(n_seq,),
        in_specs=[pl.BlockSpec((1, n16, CMP_STRIDE * width), lambda b: (b, 0, 0)),
                  pl.BlockSpec((1, CMP_LEN), lambda b: (0, 0))],
        out_specs=pl.BlockSpec((1, ncp, width), lambda b: (b, 0, 0)),
        out_shape=jax.ShapeDtypeStruct((n_seq, ncp, width), F32),
        compiler_params=_cp(("arbitrary",)), name="compress",
    )(rows3.reshape(n_seq, n16, CMP_STRIDE * width), w_pos.reshape(1, CMP_LEN))


def _masked_softmax(s, mask):
    s = jnp.where(mask, s, NEG)
    m = jnp.max(s, axis=-1, keepdims=True)
    e = jnp.where(mask, jnp.exp(s - m), 0.0)
    return e / jnp.maximum(jnp.sum(e, axis=-1, keepdims=True), 1e-30)


def _nsa_sel_kernel(q_ref, misc_ref, kc_ref, win_ref, ocw_ref, bias_ref, *, tq, nk, qoff, n_cmp, n_blk, nbp, n_top,
                    win_rows, win_dyn, win_pos0):
    tk = KEY_TILE
    i = pl.program_id(1)
    q0 = qoff + i * tq
    qpos = q0 + _iota((tq, 1), 0)
    nlim = jnp.minimum((q0 + tq - 1) // tk + 1, nk)
    qs = _stack_heads((q_ref[...] * (HEAD_DIM ** -0.5)).astype(BF16), HEAD_DIM)
    ncp = kc_ref.shape[1]

    kcv = kc_ref[0]
    kc = kcv[:, :HEAD_DIM].astype(BF16)
    vc = kcv[:, HEAD_DIM:].astype(BF16)
    cidx = _iota((1, ncp), 1)
    c_mask = ((cidx * CMP_STRIDE + CMP_LEN - 1) <= qpos) & (cidx < n_cmp)
    s_c = _dot_nt(qs, kc).reshape(N_HEADS, tq, ncp)
    p_c = _masked_softmax(s_c, c_mask[None])
    o_c = _dot(p_c.reshape(N_HEADS * tq, ncp).astype(BF16), vc).reshape(N_HEADS, tq, HEAD_DIM)

    c_start = _iota((ncp, 1), 0) * CMP_STRIDE
    s_start = _iota((1, nbp), 1) * SLC_BLOCK
    overlap = (c_start < s_start + SLC_BLOCK) & (c_start + CMP_LEN > s_start) & (_iota((ncp, 1), 0) < n_cmp)
    imp = _dot_hi(p_c[0] + p_c[1] + p_c[2] + p_c[3], jnp.where(overlap, 1.0, 0.0))
    blk = _iota((1, nbp), 1)
    cur = qpos // SLC_BLOCK
    forced = (blk == 0) | (blk == cur) | (blk == cur - 1)
    live = jnp.where(forced, jnp.inf, jnp.where((blk <= cur) & (blk < n_blk), imp, -jnp.inf))
    blk_f = blk.astype(F32)

    def pick(t, carry):
        live, sel = carry
        top = jnp.max(live, axis=-1, keepdims=True)
        first = jnp.min(jnp.where(live == top, blk_f, float(nbp)), axis=-1, keepdims=True)
        hit = blk_f == first
        return jnp.where(hit, -jnp.inf, live), jnp.where(hit, 1.0, sel)

    _, sel = lax.fori_loop(0, n_top, pick, (live, jnp.zeros((tq, nbp), F32)))
    sel = sel.astype(BF16)
    row_blk = _iota((nbp, 1), 0)

    def bias_tile(j, c):
        kpos = j * tk + _iota((1, tk), 1)
        expand = jnp.where(row_blk == kpos // SLC_BLOCK, 1.0, 0.0).astype(BF16)
        on = _dot(sel, expand)
        ok = (on > 0.5) & (kpos <= qpos)
        bias_ref[j] = jnp.where(ok, 0.0, NEG).astype(bias_ref.dtype)
        return c

    lax.fori_loop(0, nlim, bias_tile, 0)

    def fill(j, c):
        bias_ref[j] = jnp.full((tq, tk), NEG, bias_ref.dtype)
        return c

    lax.fori_loop(nlim, nk, fill, 0)

    if win_dyn:
        start = pl.multiple_of(jnp.maximum(q0 - WINDOW, 0), 8)
        wkv = win_ref[0, pl.ds(start, win_rows), :]
        kwpos = start + _iota((1, win_rows), 1)
    else:
        wkv = win_ref[0]
        kwpos = win_pos0 + _iota((1, win_rows), 1)
    kw = wkv[:, :HEAD_DIM].astype(BF16)
    vw = wkv[:, HEAD_DIM:].astype(BF16)
    rel = qpos - kwpos
    w_mask = (rel >= 0) & (rel < WINDOW) & (kwpos >= 0)
    s_w = _dot_nt(qs, kw).reshape(N_HEADS, tq, win_rows)
    p_w = _masked_softmax(s_w, w_mask[None])
    o_w = _dot(p_w.reshape(N_HEADS * tq, win_rows).astype(BF16), vw).reshape(N_HEADS, tq, HEAD_DIM)

    g = jax.nn.sigmoid(misc_ref[...])
    parts = []
    for h in range(N_HEADS):
        c = MISC_G + 3 * h
        parts.append(g[:, c:c + 1] * o_c[h] + g[:, c + 2:c + 3] * o_w[h])
    ocw_ref[...] = jnp.concatenate(parts, axis=1).astype(ocw_ref.dtype)


def _nsa_select(q_arr, p32, kcvc, win3, win_col, *, n_seq, t_seq, tq, qoff, seq_len, lp, win_dyn, win_pos0, bias_dtype,
                out_dtype):
    nk, nq = lp // KEY_TILE, t_seq // tq
    rows = n_seq * t_seq
    n_cmp = -(-seq_len // CMP_STRIDE) - 1
    n_blk = -(-seq_len // SLC_BLOCK)
    nbp = -(-n_blk // 128) * 128
    win_rows = (tq + WINDOW) if win_dyn else win3.shape[1]
    kern = functools.partial(_nsa_sel_kernel, tq=tq, nk=nk, qoff=qoff, n_cmp=n_cmp, n_blk=n_blk, nbp=nbp,
                             n_top=min(N_SLC, n_blk), win_rows=win_rows, win_dyn=win_dyn, win_pos0=win_pos0)
    return pl.pallas_call(
        kern, grid=(n_seq, nq),
        in_specs=[pl.BlockSpec((tq, MIX_W), lambda b, i: (b * nq + i, _OFF['b_q'] // MIX_W)),
                  pl.BlockSpec((tq, 128), lambda b, i: (b * nq + i, _OFF['misc'] // 128)),
                  pl.BlockSpec((1,) + kcvc.shape[1:], lambda b, i: (b, 0, 0)),
                  pl.BlockSpec((1, win3.shape[1], 128), lambda b, i: (b, 0, win_col))],
        out_specs=[pl.BlockSpec((tq, MIX_W), lambda b, i: (b * nq + i, 0)),
                   pl.BlockSpec((nk, tq, KEY_TILE), lambda b, i: (0, b * nq + i, 0))],
        out_shape=[jax.ShapeDtypeStruct((rows, MIX_W), out_dtype),
                   jax.ShapeDtypeStruct((nk, rows, KEY_TILE), bias_dtype)],
        compiler_params=_cp(("arbitrary", "arbitrary")), name="nsa_select",
    )(q_arr, p32, kcvc, win3)


def _gla_tables(c):
    levels = []
    b = c
    while b >= 2:
        levels.append(b)
        b //= 2
    r = np.arange(c)
    mats = [(r[None, :] <= r[:, None]), (r[None, :] > r[:, None])]
    qm, km = [], []
    for b in levels:
        mid = (r // b) * b + b // 2
        upper = r >= mid
        qm.append(upper[:, None] & (r[None, :] >= mid[:, None]) & (r[None, :] <= r[:, None]))
        km.append((~upper)[:, None] & (r[None, :] > r[:, None]) & (r[None, :] < mid[:, None]))
    return np.concatenate(mats + qm + km, axis=0).astype(np.float32), levels


def _gla_kernel(q_ref, k_ref, v_ref, r_ref, misc_ref, wal_ref, bal_ref, gg_ref, m_ref, s0_ref, o_ref, st_ref, s_scr,
                *, tt, c, levels, n_tiles):
    t_idx = pl.program_id(1)
    nl = len(levels)

    @pl.when(t_idx == 0)
    def _():
        s_scr[...] = s0_ref[0]

    lane_head = _iota((1, MIX_W), 1) // HEAD_DIM
    same_head = (_iota((MIX_W, 1), 0) // HEAD_DIM) == lane_head
    eye = jnp.where(_iota((MIX_W, MIX_W), 0) == _iota((MIX_W, MIX_W), 1), 1.0, 0.0).astype(BF16)
    ones_head = jnp.where(same_head, 1.0, 0.0)
    rr, cc = _iota((c, c), 0), _iota((c, c), 1)
    pair_masks = [((rr // b) == (cc // b)) & ((rr % b) >= b // 2) & ((cc % b) < b // 2) for b in levels]
    diag_mask = rr == cc

    z = _dot_hi(misc_ref[...], wal_ref[...]) + bal_ref[...]
    la = (jnp.minimum(z, 0.0) - jnp.log1p(jnp.exp(-jnp.abs(z)))) * (1.0 / GATE_TAU)
    m_all = m_ref[...]

    def bd(x):
        return jnp.concatenate([jnp.where(lane_head == h, x, jnp.zeros_like(x)) for h in range(N_HEADS)], axis=0)

    def unbd(x):
        out = jnp.where(lane_head == 0, x[0:c], 0.0)
        for h in range(1, N_HEADS):
            out = out + jnp.where(lane_head == h, x[h * c:(h + 1) * c], 0.0)
        return out

    for ci in range(tt // c):
        sl = slice(ci * c, (ci + 1) * c)
        e = jnp.exp(_dot_hi(m_all, la[sl]))
        q = q_ref[sl, :] * (HEAD_DIM ** -0.5)
        k = k_ref[sl, :]
        v = v_ref[sl, :].astype(BF16)
        e_cum, e_rest = e[0:c], e[c:2 * c]
        att = jnp.where(diag_mask[None], _dot_nt(bd(q.astype(BF16)), k.astype(BF16)).reshape(N_HEADS, c, c), 0.0)
        for li in range(nl):
            eq = e[(2 + li) * c:(3 + li) * c]
            ek = e[(2 + nl + li) * c:(3 + nl + li) * c]
            a = _dot_nt(bd((q * eq).astype(BF16)), (k * ek).astype(BF16)).reshape(N_HEADS, c, c)
            att = att + jnp.where(pair_masks[li][None], a, 0.0)
        o_intra = unbd(_dot(att.reshape(N_HEADS * c, c).astype(BF16), v))
        st = s_scr[...]
        o_inter = _dot_nt((q * e_cum).astype(BF16), st.astype(BF16))
        v_t = _dot_nt(eye, v).astype(BF16)
        upd = _dot(v_t, (k * e_rest).astype(BF16))
        s_scr[...] = st * e_cum[c - 1:c, :] + jnp.where(same_head, upd, 0.0)
        o = o_inter + o_intra
        ms = _dot_hi(o * o, ones_head) * (1.0 / HEAD_DIM)
        o = o * lax.rsqrt(ms + EPS) * gg_ref[...]
        r = r_ref[sl, :]
        o_ref[sl, :] = (o * (r * jax.nn.sigmoid(r))).astype(o_ref.dtype)

    @pl.when(t_idx == n_tiles - 1)
    def _():
        st_ref[0] = s_scr[...]


def _gla(p32, wal_pad, b_alpha, g_gla4, st0, *, n_seq, t_seq, tt, out_dtype):
    c = min(GLA_CHUNK, t_seq)
    m_all, levels = _gla_tables(c)
    n_tiles = t_seq // tt
    rows = n_seq * t_seq
    col = lambda name: pl.BlockSpec((tt, MIX_W), lambda b, t, o=_OFF[name] // MIX_W: (b * n_tiles + t, o))
    kern = functools.partial(_gla_kernel, tt=tt, c=c, levels=levels, n_tiles=n_tiles)
    return pl.pallas_call(
        kern, grid=(n_seq, n_tiles),
        in_specs=[col('d_q'), col('d_k'), col('d_v'), col('d_r'),
                  pl.BlockSpec((tt, 128), lambda b, t: (b * n_tiles + t, _OFF['misc'] // 128)),
                  pl.BlockSpec((128, MIX_W), lambda b, t: (0, 0)),
                  pl.BlockSpec((1, MIX_W), lambda b, t: (0, 0)),
                  pl.BlockSpec((1, MIX_W), lambda b, t: (0, 0)),
                  pl.BlockSpec(m_all.shape, lambda b, t: (0, 0)),
                  pl.BlockSpec((1, MIX_W, MIX_W), lambda b, t: (b, 0, 0))],
        out_specs=[pl.BlockSpec((tt, MIX_W), lambda b, t: (b * n_tiles + t, 0)),
                   pl.BlockSpec((1, MIX_W, MIX_W), lambda b, t: (b, 0, 0))],
        out_shape=[jax.ShapeDtypeStruct((rows, MIX_W), out_dtype),
                   jax.ShapeDtypeStruct((n_seq, MIX_W, MIX_W), F32)],
        scratch_shapes=[pltpu.VMEM((MIX_W, MIX_W), F32)],
        compiler_params=_cp(("arbitrary", "arbitrary")), name="gla",
    )(p32, p32, p32, p32, p32, wal_pad, b_alpha.reshape(1, MIX_W), g_gla4, jnp.asarray(m_all), st0)


def _dwconv3(u, w, carry_ref, fix_refs, t_seq, tm, first_tile):
    row = _iota((tm, 1), 0)
    u1 = pltpu.roll(u, 1, axis=0)
    u2 = pltpu.roll(u, 2, axis=0)
    if fix_refs is None:
        prev = jnp.where(first_tile, 0.0, carry_ref[0:2, :])
        u1 = jnp.where(row == 0, prev[1:2], u1)
        u2 = jnp.where(row == 0, prev[0:1], jnp.where(row == 1, prev[1:2], u2))
        carry_ref[0:2, :] = u[tm - 2:tm]
    else:
        pos = row % t_seq
        u1 = jnp.where(pos == 0, fix_refs[0][...], u1)
        u2 = jnp.where(pos < 2, fix_refs[1][...], u2)
    return w[0:1] * u2 + w[1:2] * u1 + w[2:3] * u


def _merge_kernel(oa_ref, ob_ref, od_ref, cin_ref, cb_ref, cc_ref, gate_ref, x_ref, g1_ref, wconv_ref, wb_ref,
                  wo_ref, *rest, tm, t_seq, tiles_per_seq, per_row):
    if per_row:
        fix1_ref, fix2_ref, o_ref, carry_ref = rest
        fix = (fix1_ref, fix2_ref)
    else:
        o_ref, carry_ref = rest
        fix = None
    first = (pl.program_id(0) % tiles_per_seq) == 0
    u = cc_ref[...] * cin_ref[...]
    o_c = cb_ref[...] * _dwconv3(u, wconv_ref[...], carry_ref, fix, t_seq, tm, first)
    branches = (oa_ref[...], ob_ref[...], o_c, od_ref[...])
    merged = jnp.zeros((tm, D_MODEL), F32)
    for bi, br in enumerate(branches):
        gate = jax.nn.sigmoid(gate_ref[:, bi * D_MODEL:(bi + 1) * D_MODEL])
        merged = merged + gate * _dot(br.astype(BF16), wb_ref[bi])
    o_ref[...] = x_ref[...] + g1_ref[0] * _dot(merged.astype(BF16), wo_ref[...])


def _merge(oa, ob, od, p32, x, g1, conv_c, wb, wo, fix, *, tm, t_seq):
    rows = x.shape[0]
    per_row = fix is not None
    tiles_per_seq = max(t_seq // tm, 1)
    rowblk = lambda width, colblk=0: pl.BlockSpec((tm, width), lambda i: (i, colblk))
    in_specs = [rowblk(MIX_W), rowblk(MIX_W), rowblk(MIX_W),
                rowblk(MIX_W, _OFF['c_in'] // MIX_W), rowblk(MIX_W, _OFF['c_b'] // MIX_W),
                rowblk(MIX_W, _OFF['c_c'] // MIX_W), rowblk(N_HEADS * D_MODEL, 0), rowblk(D_MODEL),
                _mod_spec(g1, tm, t_seq), _const_spec((CONV_W, MIX_W)), _const_spec((N_HEADS, MIX_W, D_MODEL)),
                _const_spec((D_MODEL, D_MODEL))]
    args = [oa, ob, od, p32, p32, p32, p32, x, g1, conv_c, wb, wo]
    if per_row:
        in_specs += [rowblk(MIX_W), rowblk(MIX_W)]
        args += list(fix)
    kern = functools.partial(_merge_kernel, tm=tm, t_seq=t_seq, tiles_per_seq=tiles_per_seq, per_row=per_row)
    return pl.pallas_call(
        kern, grid=(rows // tm,), in_specs=in_specs, out_specs=rowblk(D_MODEL),
        out_shape=jax.ShapeDtypeStruct((rows, D_MODEL), F32),
        scratch_shapes=[pltpu.VMEM((8, MIX_W), F32)],
        compiler_params=_cp(("arbitrary",)), name="merge",
    )(*args)


def _ffn_kernel(x_ref, gn_ref, sc_ref, sh_ref, g2_ref, wa_ref, wg_ref, wconv_ref, bf_ref, wout_ref, gf_ref, *rest,
                tm, tf, t_seq, tiles_per_seq, per_row):
    if per_row:
        fix1_ref, fix2_ref, o_ref, y_ref, a_ref, carry_ref = rest
    else:
        o_ref, y_ref, a_ref, carry_ref = rest
    first = (pl.program_id(0) % tiles_per_seq) == 0
    x = x_ref[...]
    h = _norm_mod(x, gn_ref[...], sc_ref[0], sh_ref[0]).astype(BF16)
    acc = jnp.zeros((tm, D_MODEL), F32)
    for f in range(D_FF // tf):
        fs = slice(f * tf, (f + 1) * tf)
        a = _dot(h, wa_ref[:, fs])
        g = _dot(h, wg_ref[:, fs])
        a_ref[:, fs] = a
        fix = (fix1_ref.at[:, fs], fix2_ref.at[:, fs]) if per_row else None
        conv = _dwconv3(a, wconv_ref[:, fs], carry_ref.at[:, fs], fix, t_seq, tm, first)
        pre = conv + bf_ref[:, fs]
        act = pre * jax.nn.sigmoid(pre) * g
        acc = acc + _dot(act.astype(BF16), wout_ref[fs, :])
    xn = x + g2_ref[0] * acc
    o_ref[...] = xn
    y_ref[...] = xn * lax.rsqrt(jnp.mean(xn * xn, axis=-1, keepdims=True) + EPS) * gf_ref[...]


def _ffn(x, gn, sc, sh, g2, wa, wg, conv_ffn, b_ffn, wout, g_final, fix, *, tm, t_seq):
    rows = x.shape[0]
    per_row = fix is not None
    tiles_per_seq = max(t_seq // tm, 1)
    tf = 256
    rowblk = lambda width: pl.BlockSpec((tm, width), lambda i: (i, 0))
    in_specs = [rowblk(D_MODEL), _const_spec((1, D_MODEL)), _mod_spec(sc, tm, t_seq), _mod_spec(sh, tm, t_seq),
                _mod_spec(g2, tm, t_seq), _const_spec((D_MODEL, D_FF)), _const_spec((D_MODEL, D_FF)),
                _const_spec((CONV_W, D_FF)), _const_spec((1, D_FF)), _const_spec((D_FF, D_MODEL)),
                _const_spec((1, D_MODEL))]
    args = [x, gn, sc, sh, g2, wa, wg, conv_ffn, b_ffn, wout, g_final]
    if per_row:
        in_specs += [rowblk(D_FF), rowblk(D_FF)]
        args += list(fix)
    kern = functools.partial(_ffn_kernel, tm=tm, tf=tf, t_seq=t_seq, tiles_per_seq=tiles_per_seq, per_row=per_row)
    return pl.pallas_call(
        kern, grid=(rows // tm,), in_specs=in_specs,
        out_specs=[rowblk(D_MODEL), rowblk(D_MODEL), rowblk(D_FF)],
        out_shape=[jax.ShapeDtypeStruct((rows, D_MODEL), F32), jax.ShapeDtypeStruct((rows, D_MODEL), F32),
                   jax.ShapeDtypeStruct((rows, D_FF), F32)],
        scratch_shapes=[pltpu.VMEM((8, D_FF), F32)],
        compiler_params=_cp(("arbitrary",)), name="ffn",
    )(*args)


def _permute_w_in(w):
    offs, o = {}, 0
    for name, n in _IN_SPLITS:
        offs[name] = (o, n)
        o += n

    def c(name):
        s, n = offs[name]
        return w[:, s:s + n]

    pieces = [c('gate'), c('a_k'), c('a_v'), c('a_q'), c('a_qi'), c('b_q'), c('b_cmp'), c('b_slc'), c('b_win'),
              c('a_ki'), c('a_wi'), c('b_g'), c('d_a'), jnp.zeros((w.shape[0], 32), w.dtype),
              c('c_in'), c('c_b'), c('c_c'), c('d_q'), c('d_k'), c('d_v'), c('d_r')]
    return jnp.concatenate(pieces, axis=1).astype(BF16)


def _cols(p, name, width):
    return p[:, _OFF[name]:_OFF[name] + width]


def _state_to_bd(s0):
    b = s0.shape[0]
    s0t = jnp.swapaxes(s0, 2, 3)
    eye = jnp.eye(N_HEADS, dtype=s0.dtype)[None, :, None, :, None]
    return (s0t[:, :, :, None, :] * eye).reshape(b, MIX_W, MIX_W)


def _bd_to_state(st):
    b = st.shape[0]
    st5 = st.reshape(b, N_HEADS, HEAD_DIM, N_HEADS, HEAD_DIM)
    return jnp.stack([jnp.swapaxes(st5[:, h, :, h, :], 1, 2) for h in range(N_HEADS)], axis=1)


def _conv_fix(state, t_seq):
    b, _, c = state.shape
    fix1 = jnp.concatenate([state[:, 1:2], jnp.zeros((b, t_seq - 1, c), state.dtype)], axis=1)
    fix2 = jnp.concatenate([state, jnp.zeros((b, t_seq - 2, c), state.dtype)], axis=1)
    return fix1.reshape(b * t_seq, c), fix2.reshape(b * t_seq, c)


def _layer_weights(l, w_in, w_alpha, g_gla, w_branch, w_out, w_ffn_in, w_ffn_out):
    wal_pad = jnp.zeros((128, MIX_W), F32).at[MISC_DA:MISC_DA + GATE_RANK].set(w_alpha[l])
    return dict(w_in=_permute_w_in(w_in[l]), wal=wal_pad, gg=jnp.tile(g_gla[l], N_HEADS).reshape(1, MIX_W),
                wb=w_branch[l].astype(BF16), wo=w_out[l].astype(BF16),
                wa=w_ffn_in[l][:, :D_FF].astype(BF16), wg=w_ffn_in[l][:, D_FF:].astype(BF16),
                wout=w_ffn_out[l].astype(BF16))


def kernel(x_prompt, x_sample, cache_a_kv, cache_a_idx, cache_b_cmp, cache_b_slc, state_b_win, state_c_conv, state_d_gla, state_ffn_conv, page_table, c_prompt, c_sample, w_ada, b_ada, g_norm1, w_in, w_cmp_pos, conv_c, w_alpha, b_alpha, g_gla, w_branch, w_out, g_norm2, w_ffn_in, conv_ffn, b_ffn, w_ffn_out, g_final):
    bp, seq, d = x_prompt.shape
    bs, tdec, _ = x_sample.shape
    depth = w_ada.shape[0]
    n_pages = page_table.shape[1]
    past = n_pages * PAGE
    w_buf = state_b_win.shape[2]
    assert bp == 1 and d == D_MODEL and seq % 1024 == 0 and tdec == 8 and w_buf == WINDOW

    n_c = bp + bs
    c_all = jnp.concatenate([c_prompt, c_sample, jnp.zeros((-n_c % 8, d), F32)], axis=0)
    mod = _ada(c_all, w_ada, b_ada)

    xp = x_prompt.reshape(bp * seq, d)
    xs = x_sample.reshape(bs * tdec, d)
    rs = bs * tdec
    tail_rows = KEY_TILE
    lps = past + tail_rows
    tkm_s = max(t for t in (1, 2, 3) if (lps // KEY_TILE) % t == 0)
    gfin = g_final.reshape(1, d)
    outs_p, outs_s = [], []
    y_p = y_s = None
    for l in range(depth):
        lw = _layer_weights(l, w_in, w_alpha, g_gla, w_branch, w_out, w_ffn_in, w_ffn_out)
        mp = [mod[l, :bp, k * d:(k + 1) * d].reshape(bp, 1, d) for k in range(6)]
        ms = [jnp.repeat(mod[l, bp:n_c, k * d:(k + 1) * d], tdec, axis=0).reshape(1, rs, d) for k in range(6)]
        gn1, gn2 = g_norm1[l].reshape(1, d), g_norm2[l].reshape(1, d)
        bffn = b_ffn[l].reshape(1, D_FF)

        p32, _ = _proj(xs, gn1, ms[1], ms[0], lw['w_in'], rs, rs)
        new = {n: _cols(p32, n, w).reshape(bs, tdec, w) for n, w in
               (('a_k', 2 * MIX_W), ('misc', IDX_DIM), ('b_cmp', 128), ('b_slc', 128), ('b_win', 128))}
        tails = [jnp.concatenate([new[n], jnp.zeros((bs, tail_rows - tdec, new[n].shape[-1]), F32)], axis=1)
                 for n in ('a_k', 'misc', 'b_cmp', 'b_slc')]
        kv_s, ki_s, cmp_s, slc_s = _gather_past(page_table, [cache_a_kv, cache_a_idx, cache_b_cmp, cache_b_slc],
                                                tails, l)
        seq_s = past + tdec
        bias_a = _dsa_select(p32, p32, ki_s, 0, IDX_DIM, n_seq=bs, t_seq=tdec, tq=tdec, qoff=past,
                             k_sel=min(A_TOPK, seq_s // 4), bias_dtype=F32)
        o_a = _flash(p32, _OFF['a_q'] // MIX_W, kv_s, 0, bias_a, n_seq=bs, t_seq=tdec, tq=tdec, tkm=tkm_s, qoff=past,
                     shared=False, out_dtype=F32)
        ncp = -(-(-(-seq_s // CMP_STRIDE) - 1) // 128) * 128
        kcvc = _compress(cmp_s, w_cmp_pos[l], seq_s, ncp)
        win_full = jnp.concatenate([state_b_win[l], new['b_win']], axis=1)
        win_pad = jnp.concatenate([win_full, jnp.zeros((bs, -(w_buf + tdec) % 16, 128), F32)], axis=1)
        ocw, bias_b = _nsa_select(p32, p32, kcvc, win_pad, 0, n_seq=bs, t_seq=tdec, tq=tdec, qoff=past, seq_len=seq_s,
                                  lp=lps, win_dyn=False, win_pos0=past - w_buf, bias_dtype=F32, out_dtype=F32)
        o_b = _flash(p32, _OFF['b_q'] // MIX_W, slc_s, 0, bias_b, n_seq=bs, t_seq=tdec, tq=tdec, tkm=tkm_s, qoff=past,
                     shared=True, out_dtype=F32, gate_args=(p32, ocw))
        o_d, st = _gla(p32, lw['wal'], b_alpha[l], lw['gg'], _state_to_bd(state_d_gla[l]), n_seq=bs, t_seq=tdec,
                       tt=tdec, out_dtype=F32)
        u_tail = (_cols(p32, 'c_c', MIX_W) * _cols(p32, 'c_in', MIX_W)).reshape(bs, tdec, MIX_W)[:, tdec - 2:]
        xs = _merge(o_a, o_b, o_d, p32, xs, ms[2], conv_c[l], lw['wb'], lw['wo'], _conv_fix(state_c_conv[l], tdec),
                    tm=rs, t_seq=tdec)
        xs, y_s, a_full = _ffn(xs, gn2, ms[4], ms[3], ms[5], lw['wa'], lw['wg'], conv_ffn[l], bffn, lw['wout'], gfin,
                               _conv_fix(state_ffn_conv[l], tdec), tm=rs, t_seq=tdec)
        outs_s.append((new['a_k'], new['misc'], new['b_cmp'], new['b_slc'], win_full[:, tdec:], u_tail,
                       _bd_to_state(st), a_full.reshape(bs, tdec, D_FF)[:, tdec - 2:]))

        p32, p16 = _proj(xp, gn1, mp[1], mp[0], lw['w_in'], 1024, seq)
        p16_3 = p16.reshape(bp, seq, NP)
        a_kv = _cols(p32, 'a_k', 2 * MIX_W).reshape(bp, seq, 2 * MIX_W)
        a_idx = _cols(p32, 'misc', IDX_DIM).reshape(bp, seq, IDX_DIM)
        b_cmp = _cols(p32, 'b_cmp', 128).reshape(bp, seq, 128)
        b_slc = _cols(p32, 'b_slc', 128).reshape(bp, seq, 128)
        b_win = _cols(p32, 'b_win', 128).reshape(bp, seq, 128)[:, seq - min(WINDOW, seq):]
        k_sel = min(A_TOPK, seq // 4)
        bias_a = _dsa_select(p16, p32, p16_3, _OFF['misc'] // 128, 128, n_seq=bp, t_seq=seq, tq=128, qoff=0,
                             k_sel=k_sel, bias_dtype=BF16)
        o_a = _flash(p16, _OFF['a_q'] // MIX_W, p16_3, _OFF['a_k'] // (2 * MIX_W), bias_a, n_seq=bp, t_seq=seq,
                     tq=512, tkm=2, qoff=0, shared=False, out_dtype=BF16)
        ncp = -(-(-(-seq // CMP_STRIDE) - 1) // 128) * 128
        kcvc = _compress(b_cmp, w_cmp_pos[l], seq, ncp)
        ocw, bias_b = _nsa_select(p16, p32, kcvc, p16_3, _OFF['b_win'] // 128, n_seq=bp, t_seq=seq, tq=128, qoff=0,
                                  seq_len=seq, lp=seq, win_dyn=True, win_pos0=0, bias_dtype=BF16, out_dtype=BF16)
        o_b = _flash(p16, _OFF['b_q'] // MIX_W, p16_3, _OFF['b_slc'] // 128, bias_b, n_seq=bp, t_seq=seq,
                     tq=512, tkm=2, qoff=0, shared=True, out_dtype=BF16, gate_args=(p32, ocw))
        o_d, st = _gla(p32, lw['wal'], b_alpha[l], lw['gg'], jnp.zeros((bp, MIX_W, MIX_W), F32), n_seq=bp, t_seq=seq,
                       tt=512, out_dtype=BF16)
        u_tail = (_cols(p32, 'c_c', MIX_W) * _cols(p32, 'c_in', MIX_W)).reshape(bp, seq, MIX_W)[:, seq - 2:]
        xp = _merge(o_a, o_b, o_d, p32, xp, mp[2], conv_c[l], lw['wb'], lw['wo'], None, tm=256, t_seq=seq)
        xp, y_p, a_full = _ffn(xp, gn2, mp[4], mp[3], mp[5], lw['wa'], lw['wg'], conv_ffn[l], bffn, lw['wout'], gfin,
                               None, tm=512, t_seq=seq)
        outs_p.append((a_kv, a_idx, b_cmp, b_slc, b_win, u_tail, _bd_to_state(st),
                       a_full.reshape(bp, seq, D_FF)[:, seq - 2:]))

    sp = [jnp.stack(z) for z in zip(*outs_p)]
    ss = [jnp.stack(z) for z in zip(*outs_s)]
    res = [y_p.reshape(bp, seq, d), y_s.reshape(bs, tdec, d)]
    for a, b in zip(sp, ss):
        res += [a, b]
    return tuple(res)
```

```python
import functools

import numpy as np
import jax
import jax.numpy as jnp
from jax import lax
from jax.experimental import pallas as pl
from jax.experimental.pallas import tpu as pltpu

F32 = jnp.float32
BF16 = jnp.bfloat16
HI = lax.Precision.HIGHEST

D_MODEL = 1024
PAGE = 128
HEAD_DIM = 64
MIX_W = 256
N_HEADS = 4
IDX_DIM = 64
A_TOPK = 256
CMP_LEN = 32
CMP_STRIDE = 16
SLC_BLOCK = 64
N_SLC = 16
WINDOW = 512
CONV_W = 3
GATE_RANK = 16
GATE_TAU = 16.0
GLA_CHUNK = 64
D_FF = 2816
EPS = 1e-6
NEG = -1e30
INT_MIN = -2 ** 31
KEY_TILE = 512

_IN_SPLITS = (('a_q', 256), ('a_k', 256), ('a_v', 256), ('a_qi', 256), ('a_ki', 64), ('a_wi', 4),
              ('b_q', 256), ('b_cmp', 128), ('b_slc', 128), ('b_win', 128), ('b_g', 12),
              ('c_in', 256), ('c_b', 256), ('c_c', 256),
              ('d_q', 256), ('d_k', 256), ('d_v', 256), ('d_r', 256), ('d_a', 16), ('gate', 4096))
_OFF = dict(gate=0, a_k=4096, a_v=4352, a_q=4608, a_qi=4864, b_q=5120, b_cmp=5376, b_slc=5504, b_win=5632,
            misc=5760, c_in=5888, c_b=6144, c_c=6400, d_q=6656, d_k=6912, d_v=7168, d_r=7424)
NP = 7680
MISC_KI, MISC_WI, MISC_G, MISC_DA = 0, 64, 68, 80
VMEM_LIMIT = 56 * 1024 * 1024


def _cp(sem):
    return pltpu.CompilerParams(dimension_semantics=sem, vmem_limit_bytes=VMEM_LIMIT)


def _dot(a, b):
    return jnp.dot(a, b, preferred_element_type=F32)


def _dot_nt(a, b):
    return lax.dot_general(a, b, (((1,), (1,)), ((), ())), preferred_element_type=F32)


def _dot_hi(a, b):
    return jnp.dot(a, b, preferred_element_type=F32, precision=HI)


def _const_spec(shape):
    nd = len(shape)
    return pl.BlockSpec(shape, lambda *a: (0,) * nd, pipeline_mode=pl.Buffered(1))


def _iota(shape, dim):
    return lax.broadcasted_iota(jnp.int32, shape, dim)


def _ada_kernel(c_ref, w_ref, b_ref, o_ref):
    o_ref[0] = _dot_hi(c_ref[...], w_ref[0]) + b_ref[0]


def _ada(c_all, w_ada, b_ada):
    depth, d, n6 = w_ada.shape
    rows = c_all.shape[0]
    tn = 1024
    return pl.pallas_call(
        _ada_kernel, grid=(depth, n6 // tn),
        in_specs=[pl.BlockSpec((rows, d), lambda l, j: (0, 0)),
                  pl.BlockSpec((1, d, tn), lambda l, j: (l, 0, j)),
                  pl.BlockSpec((1, 1, tn), lambda l, j: (l, 0, j))],
        out_specs=pl.BlockSpec((1, rows, tn), lambda l, j: (l, 0, j)),
        out_shape=jax.ShapeDtypeStruct((depth, rows, n6), F32),
        compiler_params=_cp(("arbitrary", "arbitrary")), name="ada",
    )(c_all, w_ada, b_ada.reshape(depth, 1, n6))


def _norm_mod(x, g, sc, sh):
    y = x * lax.rsqrt(jnp.mean(x * x, axis=-1, keepdims=True) + EPS) * g
    return y * (1.0 + sc) + sh


def _proj_kernel(x_ref, g_ref, sc_ref, sh_ref, w_ref, o32_ref, o16_ref, h_ref):
    @pl.when(pl.program_id(1) == 0)
    def _():
        h_ref[...] = _norm_mod(x_ref[...], g_ref[...], sc_ref[0], sh_ref[0]).astype(BF16)

    acc = _dot(h_ref[...], w_ref[...])
    o32_ref[...] = acc
    o16_ref[...] = acc.astype(BF16)


def _mod_spec(mod, tm, rows_per_group):
    mb = mod.shape[1]
    tiles = max(rows_per_group // tm, 1)
    return pl.BlockSpec((1, mb, D_MODEL), lambda i, *_: (i // tiles, 0, 0))


def _proj(x, g, sc, sh, w, tm, rows_per_group):
    rows = x.shape[0]
    tn = 768
    return pl.pallas_call(
        _proj_kernel, grid=(rows // tm, NP // tn),
        in_specs=[pl.BlockSpec((tm, D_MODEL), lambda i, j: (i, 0)),
                  pl.BlockSpec((1, D_MODEL), lambda i, j: (0, 0)),
                  _mod_spec(sc, tm, rows_per_group), _mod_spec(sh, tm, rows_per_group),
                  pl.BlockSpec((D_MODEL, tn), lambda i, j: (0, j))],
        out_specs=[pl.BlockSpec((tm, tn), lambda i, j: (i, j)), pl.BlockSpec((tm, tn), lambda i, j: (i, j))],
        out_shape=[jax.ShapeDtypeStruct((rows, NP), F32), jax.ShapeDtypeStruct((rows, NP), BF16)],
        scratch_shapes=[pltpu.VMEM((tm, D_MODEL), BF16)],
        compiler_params=_cp(("arbitrary", "arbitrary")), name="proj",
    )(x, g, sc, sh, w)


PAGES_PER_STEP = KEY_TILE // PAGE


def _gather_kernel(pt_ref, *refs, n_arr, n_chunks):
    ppc = PAGES_PER_STEP
    pages, tails, outs = refs[:n_arr * ppc], refs[n_arr * ppc:n_arr * (ppc + 1)], refs[n_arr * (ppc + 1):]
    c = pl.program_id(1)

    @pl.when(c < n_chunks - 1)
    def _():
        for k in range(n_arr):
            for r in range(ppc):
                outs[k][0, r * PAGE:(r + 1) * PAGE, :] = pages[k * ppc + r][0, 0].astype(outs[k].dtype)

    @pl.when(c == n_chunks - 1)
    def _():
        for k in range(n_arr):
            outs[k][0] = tails[k][0].astype(outs[k].dtype)


def _gather_past(page_table, pools, tails, out_dtypes, layer):
    n_seq, n_pages = page_table.shape
    n_arr, ppc = len(pools), PAGES_PER_STEP
    assert n_pages % ppc == 0 and tails[0].shape[1] == ppc * PAGE
    n_chunks = n_pages // ppc + 1
    lp = n_chunks * ppc * PAGE
    in_specs = []
    for p in pools:
        for r in range(ppc):
            in_specs.append(pl.BlockSpec(
                (1, 1, PAGE, p.shape[-1]),
                lambda b, c, pt, r=r: (layer, pt[b, jnp.minimum(c * ppc + r, n_pages - 1)], 0, 0)))
    in_specs += [pl.BlockSpec((1, ppc * PAGE, t.shape[-1]), lambda b, c, pt: (b, 0, 0)) for t in tails]
    args = [p for p in pools for _ in range(ppc)] + list(tails)
    kern = functools.partial(_gather_kernel, n_arr=n_arr, n_chunks=n_chunks)
    return pl.pallas_call(
        kern,
        grid_spec=pltpu.PrefetchScalarGridSpec(
            num_scalar_prefetch=1, grid=(n_seq, n_chunks), in_specs=in_specs,
            out_specs=[pl.BlockSpec((1, ppc * PAGE, p.shape[-1]), lambda b, c, pt: (b, c, 0)) for p in pools]),
        out_shape=[jax.ShapeDtypeStruct((n_seq, lp, p.shape[-1]), dt) for p, dt in zip(pools, out_dtypes)],
        compiler_params=_cp(("arbitrary", "arbitrary")), name="gather_past",
    )(page_table, *args)


def _stack_heads(q, width):
    t = q.shape[0]
    parts = []
    for h in range(N_HEADS):
        p = q[:, h * HEAD_DIM:(h + 1) * HEAD_DIM]
        if width > HEAD_DIM:
            p = jnp.concatenate([p, jnp.zeros((t, width - HEAD_DIM), q.dtype)], axis=1)
        parts.append(p)
    return jnp.concatenate(parts, axis=0)


def _dsa_sel_kernel(qi_ref, misc_ref, ki_ref, bias_ref, s_ref, *, tq, nk, k_sel, qoff, fk):
    tk = KEY_TILE
    i = pl.program_id(1)
    q0 = qoff + i * tq
    qpos = q0 + _iota((tq, 1), 0)
    nlim = jnp.minimum((q0 + tq - 1) // tk + 1, nk)
    qs = _stack_heads(qi_ref[...].astype(BF16), fk)
    wi = misc_ref[:, MISC_WI:MISC_WI + N_HEADS]

    def score_tile(j, c):
        kt = ki_ref[0, pl.ds(pl.multiple_of(j * tk, tk), tk), :].astype(BF16)
        d = _dot_nt(qs, kt).reshape(N_HEADS, tq, tk)
        sc = wi[:, 0:1] * jnp.maximum(d[0], 0.0)
        for h in range(1, N_HEADS):
            sc = sc + wi[:, h:h + 1] * jnp.maximum(d[h], 0.0)
        bits = lax.bitcast_convert_type(sc, jnp.int32)
        key = bits ^ ((bits >> 31) & 0x7FFFFFFF)
        key = jnp.where(key == -1, 0, key)
        kpos = j * tk + _iota((1, tk), 1)
        s_ref[j] = jnp.where(kpos <= qpos, key, INT_MIN)
        return c

    lax.fori_loop(0, nlim, score_tile, 0)

    def count(pred, level):
        level_b = jnp.broadcast_to(level, (tq, 128))

        def body(j, acc):
            for c in range(tk // 128):
                acc = acc + jnp.where(pred(s_ref[j, :, c * 128:(c + 1) * 128], level_b), 1.0, 0.0)
            return acc
        return jnp.sum(lax.fori_loop(0, nlim, body, jnp.zeros((tq, 128), F32)), axis=-1, keepdims=True)

    def bit_step(t, c):
        trial = c + lax.shift_left(jnp.int32(1), 31 - t)
        return jnp.where(count(lambda s, lv: s >= lv, trial) >= k_sel, trial, c)

    thr = lax.fori_loop(0, 32, bit_step, jnp.full((tq, 1), INT_MIN, jnp.int32))
    ties_allowed = k_sel - count(lambda s, lv: s > lv, thr)
    n_ties = count(lambda s, lv: s == lv, thr)
    need_rank = jnp.max(jnp.where(n_ties > ties_allowed, 1.0, 0.0)) > 0.5

    @pl.when(need_rank)
    def _():
        tri = jnp.where(_iota((tk, tk), 0) <= _iota((tk, tk), 1), 1.0, 0.0).astype(BF16)

        def out_tile(j, carry):
            key = s_ref[j]
            eq = jnp.where(key == thr, 1.0, 0.0)
            rank = _dot(eq.astype(BF16), tri) + carry
            kpos = j * tk + _iota((1, tk), 1)
            take = jnp.where(key > thr, 1.0, jnp.where(rank <= ties_allowed, eq, 0.0))
            take = jnp.where(kpos <= qpos, take, 0.0)
            bias_ref[j] = jnp.where(take > 0.5, 0.0, NEG).astype(bias_ref.dtype)
            return rank[:, tk - 1:tk]

        lax.fori_loop(0, nlim, out_tile, jnp.zeros((tq, 1), F32))

    @pl.when(jnp.logical_not(need_rank))
    def _():
        def out_tile(j, c):
            kpos = j * tk + _iota((1, tk), 1)
            take = jnp.where(kpos <= qpos, jnp.where(s_ref[j] >= thr, 1.0, 0.0), 0.0)
            bias_ref[j] = jnp.where(take > 0.5, 0.0, NEG).astype(bias_ref.dtype)
            return c

        lax.fori_loop(0, nlim, out_tile, 0)

    def fill(j, c):
        bias_ref[j] = jnp.full((tq, tk), NEG, bias_ref.dtype)
        return c

    lax.fori_loop(nlim, nk, fill, 0)


def _dsa_select(q_arr, p32, ki3, ki_col, fk, *, n_seq, t_seq, tq, qoff, k_sel, bias_dtype):
    lp = ki3.shape[1]
    nk, nq = lp // KEY_TILE, t_seq // tq
    rows = n_seq * t_seq
    kern = functools.partial(_dsa_sel_kernel, tq=tq, nk=nk, k_sel=k_sel, qoff=qoff, fk=fk)
    return pl.pallas_call(
        kern, grid=(n_seq, nq),
        in_specs=[pl.BlockSpec((tq, MIX_W), lambda b, i: (b * nq + i, _OFF['a_qi'] // MIX_W)),
                  pl.BlockSpec((tq, 128), lambda b, i: (b * nq + i, _OFF['misc'] // 128)),
                  pl.BlockSpec((1, lp, fk), lambda b, i: (b, 0, ki_col))],
        out_specs=pl.BlockSpec((nk, tq, KEY_TILE), lambda b, i: (0, b * nq + i, 0)),
        out_shape=jax.ShapeDtypeStruct((nk, rows, KEY_TILE), bias_dtype),
        scratch_shapes=[pltpu.VMEM((nk, tq, KEY_TILE), jnp.int32)],
        compiler_params=_cp(("arbitrary", "arbitrary")), name="dsa_select",
    )(q_arr, p32, ki3)


def _flash_row_chunk(tq):
    return min(32, tq)


def _flash_kernel(sb, sq, sk, sfl, q_ref, kv_ref, bias_ref, *rest, tq, tkm, shared, gated):
    if gated:
        misc_ref, add_ref, o_ref, m_ref, l_ref, acc_ref, s_ref, p_ref, qs_ref, a_ref = rest
    else:
        o_ref, m_ref, l_ref, acc_ref, s_ref, p_ref, qs_ref, a_ref = rest
    tk = KEY_TILE
    flags = sfl[pl.program_id(0)]
    dv = HEAD_DIM if shared else MIX_W
    lane_head = _iota((1, MIX_W), 1) // HEAD_DIM
    rc = _flash_row_chunk(tq)

    @pl.when((flags & 1) != 0)
    def _():
        m_ref[...] = jnp.full(m_ref.shape, NEG, F32)
        l_ref[...] = jnp.zeros(l_ref.shape, F32)
        acc_ref[...] = jnp.zeros(acc_ref.shape, F32)
        q = (q_ref[...] * (HEAD_DIM ** -0.5)).astype(BF16)
        if shared:
            qs_ref[...] = _stack_heads(q, HEAD_DIM)
        else:
            qs_ref[...] = jnp.concatenate([jnp.where(lane_head == h, q, jnp.zeros_like(q)) for h in range(N_HEADS)],
                                          axis=0)

    for u in range(tkm):
        kvt = kv_ref[0, u * tk:(u + 1) * tk, :]
        k = kvt[:, :dv].astype(BF16)
        v = kvt[:, dv:2 * dv].astype(BF16)
        half = N_HEADS * tq // 2
        s_ref[0:half, :] = _dot_nt(qs_ref[0:half, :], k)
        s_ref[half:, :] = _dot_nt(qs_ref[half:, :], k)
        n_chunks = N_HEADS * tq // rc

        def rows_of(c):
            return (pl.ds(pl.multiple_of(c * rc, rc), rc), pl.ds(pl.multiple_of((c % (tq // rc)) * rc, rc), rc))

        def pass_max(c, carry):
            rows, brow = rows_of(c)
            s = s_ref[rows, :] + bias_ref[u, brow, :].astype(F32)
            s_ref[rows, :] = s
            m_old = m_ref[rows, :]
            m_new = jnp.maximum(m_old, jnp.max(s, axis=-1, keepdims=True))
            a_ref[rows, :] = jnp.exp(m_old - m_new)
            m_ref[rows, :] = m_new
            return carry

        lax.fori_loop(0, n_chunks, pass_max, 0, unroll=min(8, n_chunks))

        def pass_exp(c, carry):
            rows, _ = rows_of(c)
            m = m_ref[rows, :]
            part = a_ref[rows, :] * l_ref[rows, :]
            for cc in range(tk // 128):
                p = jnp.exp(s_ref[rows, cc * 128:(cc + 1) * 128] - m)
                part = part + p
                p_ref[rows, cc * 128:(cc + 1) * 128] = p.astype(p_ref.dtype)
            l_ref[rows, :] = part
            return carry

        lax.fori_loop(0, n_chunks, pass_exp, 0, unroll=min(8, n_chunks))
        alpha = a_ref[...]
        alpha = alpha[:, :dv] if dv <= 128 else jnp.concatenate([alpha] * (dv // 128), axis=1)
        acc_ref[0:half, :] = alpha[0:half] * acc_ref[0:half, :] + _dot(p_ref[0:half, :].astype(BF16), v)
        acc_ref[half:, :] = alpha[half:] * acc_ref[half:, :] + _dot(p_ref[half:, :].astype(BF16), v)

    @pl.when((flags & 2) != 0)
    def _():
        l_sum = jnp.sum(l_ref[...], axis=-1, keepdims=True)
        o = (acc_ref[...] / l_sum).reshape(N_HEADS, tq, dv)
        if shared:
            out = jnp.concatenate([o[h] for h in range(N_HEADS)], axis=1)
        else:
            out = jnp.where(lane_head == 0, o[0], 0.0)
            for h in range(1, N_HEADS):
                out = out + jnp.where(lane_head == h, o[h], 0.0)
        if gated:
            g = jax.nn.sigmoid(misc_ref[...])
            gate = jnp.zeros((tq, MIX_W), F32)
            for h in range(N_HEADS):
                c = MISC_G + 3 * h + 1
                gate = gate + jnp.where(lane_head == h, g[:, c:c + 1], 0.0)
            out = add_ref[...].astype(F32) + gate * out
        o_ref[...] = out.astype(o_ref.dtype)


def _flash_steps(n_seq, t_seq, tq, tk, lp, qoff):
    sb, sq, sk, sfl = [], [], [], []
    for b in range(n_seq):
        for i in range(t_seq // tq):
            nlim = min((qoff + (i + 1) * tq - 1) // tk + 1, lp // tk)
            for j in range(nlim):
                sb.append(b), sq.append(i), sk.append(j)
                sfl.append((1 if j == 0 else 0) | (2 if j == nlim - 1 else 0))
    return [jnp.asarray(np.asarray(a, np.int32)) for a in (sb, sq, sk, sfl)]


def _flash(q_arr, q_col, kv3, kv_col, bias, *, n_seq, t_seq, tq, tkm, qoff, shared, out_dtype, gate_args=None):
    lp = kv3.shape[1]
    tk = tkm * KEY_TILE
    nq = t_seq // tq
    rows = n_seq * t_seq
    wblk = 2 * (HEAD_DIM if shared else MIX_W)
    dv = HEAD_DIM if shared else MIX_W
    steps = _flash_steps(n_seq, t_seq, tq, tk, lp, qoff)
    gated = gate_args is not None
    row = lambda s, sb, sq, sk, sfl: sb[s] * nq + sq[s]
    in_specs = [pl.BlockSpec((tq, MIX_W), lambda s, sb, sq, sk, sfl: (row(s, sb, sq, sk, sfl), q_col)),
                pl.BlockSpec((1, tk, wblk), lambda s, sb, sq, sk, sfl: (sb[s], sk[s], kv_col)),
                pl.BlockSpec((tkm, tq, KEY_TILE), lambda s, sb, sq, sk, sfl: (sk[s], row(s, sb, sq, sk, sfl), 0))]
    args = [q_arr, kv3, bias]
    if gated:
        in_specs += [pl.BlockSpec((tq, 128), lambda s, sb, sq, sk, sfl: (row(s, sb, sq, sk, sfl), _OFF['misc'] // 128)),
                     pl.BlockSpec((tq, MIX_W), lambda s, sb, sq, sk, sfl: (row(s, sb, sq, sk, sfl), 0))]
        args += list(gate_args)
    kern = functools.partial(_flash_kernel, tq=tq, tkm=tkm, shared=shared, gated=gated)
    return pl.pallas_call(
        kern,
        grid_spec=pltpu.PrefetchScalarGridSpec(
            num_scalar_prefetch=4, grid=(int(steps[0].shape[0]),), in_specs=in_specs,
            out_specs=pl.BlockSpec((tq, MIX_W), lambda s, sb, sq, sk, sfl: (row(s, sb, sq, sk, sfl), 0)),
            scratch_shapes=[pltpu.VMEM((N_HEADS * tq, 128), F32), pltpu.VMEM((N_HEADS * tq, 128), F32),
                            pltpu.VMEM((N_HEADS * tq, dv), F32), pltpu.VMEM((N_HEADS * tq, KEY_TILE), F32),
                            pltpu.VMEM((N_HEADS * tq, KEY_TILE), BF16 if _flash_row_chunk(tq) % 16 == 0 else F32),
                            pltpu.VMEM((N_HEADS * tq, dv), BF16), pltpu.VMEM((N_HEADS * tq, 128), F32)]),
        out_shape=jax.ShapeDtypeStruct((rows, MIX_W), out_dtype),
        compiler_params=_cp(("arbitrary",)), name="flash_shared" if shared else "flash_heads",
    )(*steps, *args)


def _compress_kernel(x_ref, w_ref, o_ref, *, n_blocks, ncp):
    w = w_ref[...]
    e = jnp.exp(w - jnp.max(w, axis=-1, keepdims=True))
    w = e / jnp.sum(e, axis=-1, keepdims=True)
    n16 = x_ref.shape[1]
    width = x_ref.shape[2] // CMP_STRIDE
    first = jnp.zeros((n16, width), F32)
    second = jnp.zeros((n16, width), F32)
    for j in range(CMP_STRIDE):
        xj = x_ref[0, :, j * width:(j + 1) * width]
        first = first + xj * w[:, j:j + 1]
        second = second + xj * w[:, CMP_STRIDE + j:CMP_STRIDE + j + 1]
    shifted = jnp.concatenate([second[1:], jnp.zeros((1, width), F32)], axis=0)
    out = first + shifted
    if n16 < ncp:
        out = jnp.concatenate([out, jnp.zeros((ncp - n16, width), F32)], axis=0)
    out = out[:ncp]
    o_ref[0] = jnp.where(_iota((ncp, 1), 0) < n_blocks, out, 0.0)


def _compress(rows3, w_pos, length, ncp):
    n_seq, lp, width = rows3.shape
    n16 = lp // CMP_STRIDE
    n_blocks = -(-length // CMP_STRIDE) - 1
    kern = functools.partial(_compress_kernel, n_blocks=n_blocks, ncp=ncp)
    return pl.pallas_call(
        kern, grid=(n_seq,),
        in_specs=[pl.BlockSpec((1, n16, CMP_STRIDE * width), lambda b: (b, 0, 0)),
                  pl.BlockSpec((1, CMP_LEN), lambda b: (0, 0))],
        out_specs=pl.BlockSpec((1, ncp, width), lambda b: (b, 0, 0)),
        out_shape=jax.ShapeDtypeStruct((n_seq, ncp, width), F32),
        compiler_params=_cp(("arbitrary",)), name="compress",
    )(rows3.reshape(n_seq, n16, CMP_STRIDE * width), w_pos.reshape(1, CMP_LEN))


def _masked_softmax(s, mask):
    s = jnp.where(mask, s, NEG)
    m = jnp.max(s, axis=-1, keepdims=True)
    e = jnp.where(mask, jnp.exp(s - m), 0.0)
    return e / jnp.maximum(jnp.sum(e, axis=-1, keepdims=True), 1e-30)


def _nsa_sel_kernel(q_ref, misc_ref, kc_ref, win_ref, ocw_ref, bias_ref, *, tq, nk, qoff, n_cmp, n_blk, nbp, n_top,
                    win_rows, win_dyn, win_pos0):
    tk = KEY_TILE
    i = pl.program_id(1)
    q0 = qoff + i * tq
    qpos = q0 + _iota((tq, 1), 0)
    nlim = jnp.minimum((q0 + tq - 1) // tk + 1, nk)
    qs = _stack_heads((q_ref[...] * (HEAD_DIM ** -0.5)).astype(BF16), HEAD_DIM)
    ncp = kc_ref.shape[1]

    kcv = kc_ref[0]
    kc = kcv[:, :HEAD_DIM].astype(BF16)
    vc = kcv[:, HEAD_DIM:].astype(BF16)
    cidx = _iota((1, ncp), 1)
    c_mask = ((cidx * CMP_STRIDE + CMP_LEN - 1) <= qpos) & (cidx < n_cmp)
    s_c = _dot_nt(qs, kc).reshape(N_HEADS, tq, ncp)
    p_c = _masked_softmax(s_c, c_mask[None])
    o_c = _dot(p_c.reshape(N_HEADS * tq, ncp).astype(BF16), vc).reshape(N_HEADS, tq, HEAD_DIM)

    c_start = _iota((ncp, 1), 0) * CMP_STRIDE
    s_start = _iota((1, nbp), 1) * SLC_BLOCK
    overlap = (c_start < s_start + SLC_BLOCK) & (c_start + CMP_LEN > s_start) & (_iota((ncp, 1), 0) < n_cmp)
    imp = _dot_hi(p_c[0] + p_c[1] + p_c[2] + p_c[3], jnp.where(overlap, 1.0, 0.0))
    blk = _iota((1, nbp), 1)
    cur = qpos // SLC_BLOCK
    forced = (blk == 0) | (blk == cur) | (blk == cur - 1)
    live = jnp.where(forced, jnp.inf, jnp.where((blk <= cur) & (blk < n_blk), imp, -jnp.inf))
    blk_f = blk.astype(F32)

    def pick(t, carry):
        live, sel = carry
        top = jnp.max(live, axis=-1, keepdims=True)
        first = jnp.min(jnp.where(live == top, blk_f, float(nbp)), axis=-1, keepdims=True)
        hit = blk_f == first
        return jnp.where(hit, -jnp.inf, live), jnp.where(hit, 1.0, sel)

    _, sel = lax.fori_loop(0, n_top, pick, (live, jnp.zeros((tq, nbp), F32)))
    sel = sel.astype(BF16)
    row_blk = _iota((nbp, 1), 0)

    def bias_tile(j, c):
        kpos = j * tk + _iota((1, tk), 1)
        expand = jnp.where(row_blk == kpos // SLC_BLOCK, 1.0, 0.0).astype(BF16)
        on = _dot(sel, expand)
        ok = (on > 0.5) & (kpos <= qpos)
        bias_ref[j] = jnp.where(ok, 0.0, NEG).astype(bias_ref.dtype)
        return c

    lax.fori_loop(0, nlim, bias_tile, 0)

    def fill(j, c):
        bias_ref[j] = jnp.full((tq, tk), NEG, bias_ref.dtype)
        return c

    lax.fori_loop(nlim, nk, fill, 0)

    if win_dyn:
        start = pl.multiple_of(jnp.maximum(q0 - WINDOW, 0), 8)
        wkv = win_ref[0, pl.ds(start, win_rows), :]
        kwpos = start + _iota((1, win_rows), 1)
    else:
        wkv = win_ref[0]
        kwpos = win_pos0 + _iota((1, win_rows), 1)
    kw = wkv[:, :HEAD_DIM].astype(BF16)
    vw = wkv[:, HEAD_DIM:].astype(BF16)
    rel = qpos - kwpos
    w_mask = (rel >= 0) & (rel < WINDOW) & (kwpos >= 0)
    s_w = _dot_nt(qs, kw).reshape(N_HEADS, tq, win_rows)
    p_w = _masked_softmax(s_w, w_mask[None])
    o_w = _dot(p_w.reshape(N_HEADS * tq, win_rows).astype(BF16), vw).reshape(N_HEADS, tq, HEAD_DIM)

    g = jax.nn.sigmoid(misc_ref[...])
    parts = []
    for h in range(N_HEADS):
        c = MISC_G + 3 * h
        parts.append(g[:, c:c + 1] * o_c[h] + g[:, c + 2:c + 3] * o_w[h])
    ocw_ref[...] = jnp.concatenate(parts, axis=1).astype(ocw_ref.dtype)


def _nsa_select(q_arr, p32, kcvc, win3, win_col, *, n_seq, t_seq, tq, qoff, seq_len, lp, win_dyn, win_pos0, bias_dtype,
                out_dtype):
    nk, nq = lp // KEY_TILE, t_seq // tq
    rows = n_seq * t_seq
    n_cmp = -(-seq_len // CMP_STRIDE) - 1
    n_blk = -(-seq_len // SLC_BLOCK)
    nbp = -(-n_blk // 128) * 128
    win_rows = (tq + WINDOW) if win_dyn else win3.shape[1]
    kern = functools.partial(_nsa_sel_kernel, tq=tq, nk=nk, qoff=qoff, n_cmp=n_cmp, n_blk=n_blk, nbp=nbp,
                             n_top=min(N_SLC, n_blk), win_rows=win_rows, win_dyn=win_dyn, win_pos0=win_pos0)
    return pl.pallas_call(
        kern, grid=(n_seq, nq),
        in_specs=[pl.BlockSpec((tq, MIX_W), lambda b, i: (b * nq + i, _OFF['b_q'] // MIX_W)),
                  pl.BlockSpec((tq, 128), lambda b, i: (b * nq + i, _OFF['misc'] // 128)),
                  pl.BlockSpec((1,) + kcvc.shape[1:], lambda b, i: (b, 0, 0)),
                  pl.BlockSpec((1, win3.shape[1], 128), lambda b, i: (b, 0, win_col))],
        out_specs=[pl.BlockSpec((tq, MIX_W), lambda b, i: (b * nq + i, 0)),
                   pl.BlockSpec((nk, tq, KEY_TILE), lambda b, i: (0, b * nq + i, 0))],
        out_shape=[jax.ShapeDtypeStruct((rows, MIX_W), out_dtype),
                   jax.ShapeDtypeStruct((nk, rows, KEY_TILE), bias_dtype)],
        compiler_params=_cp(("arbitrary", "arbitrary")), name="nsa_select",
    )(q_arr, p32, kcvc, win3)


def _gla_tables(c):
    levels = []
    b = c
    while b >= 2:
        levels.append(b)
        b //= 2
    r = np.arange(c)
    mats = [(r[None, :] <= r[:, None]), (r[None, :] > r[:, None])]
    qm, km = [], []
    for b in levels:
        mid = (r // b) * b + b // 2
        upper = r >= mid
        qm.append(upper[:, None] & (r[None, :] >= mid[:, None]) & (r[None, :] <= r[:, None]))
        km.append((~upper)[:, None] & (r[None, :] > r[:, None]) & (r[None, :] < mid[:, None]))
    return np.concatenate(mats + qm + km, axis=0).astype(np.float32), levels


def _gla_kernel(q_ref, k_ref, v_ref, r_ref, misc_ref, wal_ref, bal_ref, gg_ref, m_ref, s0_ref, o_ref, st_ref, s_scr,
                *, tt, c, levels, n_tiles):
    t_idx = pl.program_id(1)
    nl = len(levels)

    @pl.when(t_idx == 0)
    def _():
        s_scr[...] = s0_ref[0]

    lane_head = _iota((1, MIX_W), 1) // HEAD_DIM
    same_head = (_iota((MIX_W, 1), 0) // HEAD_DIM) == lane_head
    eye = jnp.where(_iota((MIX_W, MIX_W), 0) == _iota((MIX_W, MIX_W), 1), 1.0, 0.0).astype(BF16)
    ones_head = jnp.where(same_head, 1.0, 0.0)
    rr, cc = _iota((c, c), 0), _iota((c, c), 1)
    pair_masks = [((rr // b) == (cc // b)) & ((rr % b) >= b // 2) & ((cc % b) < b // 2) for b in levels]
    diag_mask = rr == cc

    z = _dot_hi(misc_ref[...], wal_ref[...]) + bal_ref[...]
    la = (jnp.minimum(z, 0.0) - jnp.log1p(jnp.exp(-jnp.abs(z)))) * (1.0 / GATE_TAU)
    m_all = m_ref[...]

    def bd(x):
        return jnp.concatenate([jnp.where(lane_head == h, x, jnp.zeros_like(x)) for h in range(N_HEADS)], axis=0)

    def unbd(x):
        out = jnp.where(lane_head == 0, x[0:c], 0.0)
        for h in range(1, N_HEADS):
            out = out + jnp.where(lane_head == h, x[h * c:(h + 1) * c], 0.0)
        return out

    for ci in range(tt // c):
        sl = slice(ci * c, (ci + 1) * c)
        e = jnp.exp(_dot_hi(m_all, la[sl]))
        q = q_ref[sl, :] * (HEAD_DIM ** -0.5)
        k = k_ref[sl, :]
        v = v_ref[sl, :].astype(BF16)
        e_cum, e_rest = e[0:c], e[c:2 * c]
        att = jnp.where(diag_mask[None], _dot_nt(bd(q.astype(BF16)), k.astype(BF16)).reshape(N_HEADS, c, c), 0.0)
        for li in range(nl):
            eq = e[(2 + li) * c:(3 + li) * c]
            ek = e[(2 + nl + li) * c:(3 + nl + li) * c]
            a = _dot_nt(bd((q * eq).astype(BF16)), (k * ek).astype(BF16)).reshape(N_HEADS, c, c)
            att = att + jnp.where(pair_masks[li][None], a, 0.0)
        o_intra = unbd(_dot(att.reshape(N_HEADS * c, c).astype(BF16), v))
        st = s_scr[...]
        o_inter = _dot_nt((q * e_cum).astype(BF16), st.astype(BF16))
        v_t = _dot_nt(eye, v).astype(BF16)
        upd = _dot(v_t, (k * e_rest).astype(BF16))
        s_scr[...] = st * e_cum[c - 1:c, :] + jnp.where(same_head, upd, 0.0)
        o = o_inter + o_intra
        ms = _dot_hi(o * o, ones_head) * (1.0 / HEAD_DIM)
        o = o * lax.rsqrt(ms + EPS) * gg_ref[...]
        r = r_ref[sl, :]
        o_ref[sl, :] = (o * (r * jax.nn.sigmoid(r))).astype(o_ref.dtype)

    @pl.when(t_idx == n_tiles - 1)
    def _():
        st_ref[0] = s_scr[...]


def _gla(p32, wal_pad, b_alpha, g_gla4, st0, *, n_seq, t_seq, tt, out_dtype):
    c = min(GLA_CHUNK, t_seq)
    m_all, levels = _gla_tables(c)
    n_tiles = t_seq // tt
    rows = n_seq * t_seq
    col = lambda name: pl.BlockSpec((tt, MIX_W), lambda b, t, o=_OFF[name] // MIX_W: (b * n_tiles + t, o))
    kern = functools.partial(_gla_kernel, tt=tt, c=c, levels=levels, n_tiles=n_tiles)
    return pl.pallas_call(
        kern, grid=(n_seq, n_tiles),
        in_specs=[col('d_q'), col('d_k'), col('d_v'), col('d_r'),
                  pl.BlockSpec((tt, 128), lambda b, t: (b * n_tiles + t, _OFF['misc'] // 128)),
                  pl.BlockSpec((128, MIX_W), lambda b, t: (0, 0)),
                  pl.BlockSpec((1, MIX_W), lambda b, t: (0, 0)),
                  pl.BlockSpec((1, MIX_W), lambda b, t: (0, 0)),
                  pl.BlockSpec(m_all.shape, lambda b, t: (0, 0)),
                  pl.BlockSpec((1, MIX_W, MIX_W), lambda b, t: (b, 0, 0))],
        out_specs=[pl.BlockSpec((tt, MIX_W), lambda b, t: (b * n_tiles + t, 0)),
                   pl.BlockSpec((1, MIX_W, MIX_W), lambda b, t: (b, 0, 0))],
        out_shape=[jax.ShapeDtypeStruct((rows, MIX_W), out_dtype),
                   jax.ShapeDtypeStruct((n_seq, MIX_W, MIX_W), F32)],
        scratch_shapes=[pltpu.VMEM((MIX_W, MIX_W), F32)],
        compiler_params=_cp(("arbitrary", "arbitrary")), name="gla",
    )(p32, p32, p32, p32, p32, wal_pad, b_alpha.reshape(1, MIX_W), g_gla4, jnp.asarray(m_all), st0)


def _dwconv3(u, w, carry_ref, fix_refs, t_seq, tm, first_tile):
    row = _iota((tm, 1), 0)
    u1 = pltpu.roll(u, 1, axis=0)
    u2 = pltpu.roll(u, 2, axis=0)
    if fix_refs is None:
        prev = jnp.where(first_tile, 0.0, carry_ref[0:2, :])
        u1 = jnp.where(row == 0, prev[1:2], u1)
        u2 = jnp.where(row == 0, prev[0:1], jnp.where(row == 1, prev[1:2], u2))
        carry_ref[0:2, :] = u[tm - 2:tm]
    else:
        pos = row % t_seq
        u1 = jnp.where(pos == 0, fix_refs[0][...], u1)
        u2 = jnp.where(pos < 2, fix_refs[1][...], u2)
    return w[0:1] * u2 + w[1:2] * u1 + w[2:3] * u


def _merge_kernel(oa_ref, ob_ref, od_ref, cin_ref, cb_ref, cc_ref, gate_ref, x_ref, g1_ref, wconv_ref, wb_ref,
                  wo_ref, *rest, tm, t_seq, tiles_per_seq, per_row):
    if per_row:
        fix1_ref, fix2_ref, o_ref, carry_ref = rest
        fix = (fix1_ref, fix2_ref)
    else:
        o_ref, carry_ref = rest
        fix = None
    first = (pl.program_id(0) % tiles_per_seq) == 0
    u = cc_ref[...] * cin_ref[...]
    o_c = cb_ref[...] * _dwconv3(u, wconv_ref[...], carry_ref, fix, t_seq, tm, first)
    branches = (oa_ref[...], ob_ref[...], o_c, od_ref[...])
    merged = jnp.zeros((tm, D_MODEL), F32)
    for bi, br in enumerate(branches):
        gate = jax.nn.sigmoid(gate_ref[:, bi * D_MODEL:(bi + 1) * D_MODEL])
        merged = merged + gate * _dot(br.astype(BF16), wb_ref[bi])
    o_ref[...] = x_ref[...] + g1_ref[0] * _dot(merged.astype(BF16), wo_ref[...])


def _merge(oa, ob, od, p32, x, g1, conv_c, wb, wo, fix, *, tm, t_seq):
    rows = x.shape[0]
    per_row = fix is not None
    tiles_per_seq = max(t_seq // tm, 1)
    rowblk = lambda width, colblk=0: pl.BlockSpec((tm, width), lambda i: (i, colblk))
    in_specs = [rowblk(MIX_W), rowblk(MIX_W), rowblk(MIX_W),
                rowblk(MIX_W, _OFF['c_in'] // MIX_W), rowblk(MIX_W, _OFF['c_b'] // MIX_W),
                rowblk(MIX_W, _OFF['c_c'] // MIX_W), rowblk(N_HEADS * D_MODEL, 0), rowblk(D_MODEL),
                _mod_spec(g1, tm, t_seq), _const_spec((CONV_W, MIX_W)), _const_spec((N_HEADS, MIX_W, D_MODEL)),
                _const_spec((D_MODEL, D_MODEL))]
    args = [oa, ob, od, p32, p32, p32, p32, x, g1, conv_c, wb, wo]
    if per_row:
        in_specs += [rowblk(MIX_W), rowblk(MIX_W)]
        args += list(fix)
    kern = functools.partial(_merge_kernel, tm=tm, t_seq=t_seq, tiles_per_seq=tiles_per_seq, per_row=per_row)
    return pl.pallas_call(
        kern, grid=(rows // tm,), in_specs=in_specs, out_specs=rowblk(D_MODEL),
        out_shape=jax.ShapeDtypeStruct((rows, D_MODEL), F32),
        scratch_shapes=[pltpu.VMEM((8, MIX_W), F32)],
        compiler_params=_cp(("arbitrary",)), name="merge",
    )(*args)


def _ffn_kernel(x_ref, gn_ref, sc_ref, sh_ref, g2_ref, wa_ref, wg_ref, wconv_ref, bf_ref, wout_ref, gf_ref, *rest,
                tm, tf, t_seq, tiles_per_seq, per_row):
    if per_row:
        fix1_ref, fix2_ref, o_ref, y_ref, a_ref, carry_ref = rest
    else:
        o_ref, y_ref, a_ref, carry_ref = rest
    first = (pl.program_id(0) % tiles_per_seq) == 0
    x = x_ref[...]
    h = _norm_mod(x, gn_ref[...], sc_ref[0], sh_ref[0]).astype(BF16)
    acc = jnp.zeros((tm, D_MODEL), F32)
    for f in range(D_FF // tf):
        fs = slice(f * tf, (f + 1) * tf)
        a = _dot(h, wa_ref[:, fs])
        g = _dot(h, wg_ref[:, fs])
        a_ref[:, fs] = a
        fix = (fix1_ref.at[:, fs], fix2_ref.at[:, fs]) if per_row else None
        conv = _dwconv3(a, wconv_ref[:, fs], carry_ref.at[:, fs], fix, t_seq, tm, first)
        pre = conv + bf_ref[:, fs]
        act = pre * jax.nn.sigmoid(pre) * g
        acc = acc + _dot(act.astype(BF16), wout_ref[fs, :])
    xn = x + g2_ref[0] * acc
    o_ref[...] = xn
    y_ref[...] = xn * lax.rsqrt(jnp.mean(xn * xn, axis=-1, keepdims=True) + EPS) * gf_ref[...]


def _ffn(x, gn, sc, sh, g2, wa, wg, conv_ffn, b_ffn, wout, g_final, fix, *, tm, t_seq):
    rows = x.shape[0]
    per_row = fix is not None
    tiles_per_seq = max(t_seq // tm, 1)
    tf = 256
    rowblk = lambda width: pl.BlockSpec((tm, width), lambda i: (i, 0))
    in_specs = [rowblk(D_MODEL), _const_spec((1, D_MODEL)), _mod_spec(sc, tm, t_seq), _mod_spec(sh, tm, t_seq),
                _mod_spec(g2, tm, t_seq), _const_spec((D_MODEL, D_FF)), _const_spec((D_MODEL, D_FF)),
                _const_spec((CONV_W, D_FF)), _const_spec((1, D_FF)), _const_spec((D_FF, D_MODEL)),
                _const_spec((1, D_MODEL))]
    args = [x, gn, sc, sh, g2, wa, wg, conv_ffn, b_ffn, wout, g_final]
    if per_row:
        in_specs += [rowblk(D_FF), rowblk(D_FF)]
        args += list(fix)
    kern = functools.partial(_ffn_kernel, tm=tm, tf=tf, t_seq=t_seq, tiles_per_seq=tiles_per_seq, per_row=per_row)
    return pl.pallas_call(
        kern, grid=(rows // tm,), in_specs=in_specs,
        out_specs=[rowblk(D_MODEL), rowblk(D_MODEL), rowblk(D_FF)],
        out_shape=[jax.ShapeDtypeStruct((rows, D_MODEL), F32), jax.ShapeDtypeStruct((rows, D_MODEL), F32),
                   jax.ShapeDtypeStruct((rows, D_FF), F32)],
        scratch_shapes=[pltpu.VMEM((8, D_FF), F32)],
        compiler_params=_cp(("arbitrary",)), name="ffn",
    )(*args)


def _permute_w_in(w):
    offs, o = {}, 0
    for name, n in _IN_SPLITS:
        offs[name] = (o, n)
        o += n

    def c(name):
        s, n = offs[name]
        return w[:, s:s + n]

    pieces = [c('gate'), c('a_k'), c('a_v'), c('a_q'), c('a_qi'), c('b_q'), c('b_cmp'), c('b_slc'), c('b_win'),
              c('a_ki'), c('a_wi'), c('b_g'), c('d_a'), jnp.zeros((w.shape[0], 32), w.dtype),
              c('c_in'), c('c_b'), c('c_c'), c('d_q'), c('d_k'), c('d_v'), c('d_r')]
    return jnp.concatenate(pieces, axis=1).astype(BF16)


def _cols(p, name, width):
    return p[:, _OFF[name]:_OFF[name] + width]


def _state_to_bd(s0):
    b = s0.shape[0]
    s0t = jnp.swapaxes(s0, 2, 3)
    eye = jnp.eye(N_HEADS, dtype=s0.dtype)[None, :, None, :, None]
    return (s0t[:, :, :, None, :] * eye).reshape(b, MIX_W, MIX_W)


def _bd_to_state(st):
    b = st.shape[0]
    st5 = st.reshape(b, N_HEADS, HEAD_DIM, N_HEADS, HEAD_DIM)
    return jnp.stack([jnp.swapaxes(st5[:, h, :, h, :], 1, 2) for h in range(N_HEADS)], axis=1)


def _conv_fix(state, t_seq):
    b, _, c = state.shape
    fix1 = jnp.concatenate([state[:, 1:2], jnp.zeros((b, t_seq - 1, c), state.dtype)], axis=1)
    fix2 = jnp.concatenate([state, jnp.zeros((b, t_seq - 2, c), state.dtype)], axis=1)
    return fix1.reshape(b * t_seq, c), fix2.reshape(b * t_seq, c)


def _layer_weights(l, w_in, w_alpha, g_gla, w_branch, w_out, w_ffn_in, w_ffn_out):
    wal_pad = jnp.zeros((128, MIX_W), F32).at[MISC_DA:MISC_DA + GATE_RANK].set(w_alpha[l])
    return dict(w_in=_permute_w_in(w_in[l]), wal=wal_pad, gg=jnp.tile(g_gla[l], N_HEADS).reshape(1, MIX_W),
                wb=w_branch[l].astype(BF16), wo=w_out[l].astype(BF16),
                wa=w_ffn_in[l][:, :D_FF].astype(BF16), wg=w_ffn_in[l][:, D_FF:].astype(BF16),
                wout=w_ffn_out[l].astype(BF16))


def kernel(x_prompt, x_sample, cache_a_kv, cache_a_idx, cache_b_cmp, cache_b_slc, state_b_win, state_c_conv, state_d_gla, state_ffn_conv, page_table, c_prompt, c_sample, w_ada, b_ada, g_norm1, w_in, w_cmp_pos, conv_c, w_alpha, b_alpha, g_gla, w_branch, w_out, g_norm2, w_ffn_in, conv_ffn, b_ffn, w_ffn_out, g_final):
    bp, seq, d = x_prompt.shape
    bs, tdec, _ = x_sample.shape
    depth = w_ada.shape[0]
    n_pages = page_table.shape[1]
    past = n_pages * PAGE
    w_buf = state_b_win.shape[2]
    assert bp == 1 and d == D_MODEL and seq % 1024 == 0 and tdec == 8 and w_buf == WINDOW

    n_c = bp + bs
    c_all = jnp.concatenate([c_prompt, c_sample, jnp.zeros((-n_c % 8, d), F32)], axis=0)
    mod = _ada(c_all, w_ada, b_ada)

    xp = x_prompt.reshape(bp * seq, d)
    xs = x_sample.reshape(bs * tdec, d)
    rs = bs * tdec
    tail_rows = KEY_TILE
    lps = past + tail_rows
    tkm_s = max(t for t in (1, 2, 3) if (lps // KEY_TILE) % t == 0)
    gfin = g_final.reshape(1, d)
    outs_p, outs_s = [], []
    y_p = y_s = None
    for l in range(depth):
        lw = _layer_weights(l, w_in, w_alpha, g_gla, w_branch, w_out, w_ffn_in, w_ffn_out)
        mp = [mod[l, :bp, k * d:(k + 1) * d].reshape(bp, 1, d) for k in range(6)]
        ms = [jnp.repeat(mod[l, bp:n_c, k * d:(k + 1) * d], tdec, axis=0).reshape(1, rs, d) for k in range(6)]
        gn1, gn2 = g_norm1[l].reshape(1, d), g_norm2[l].reshape(1, d)
        bffn = b_ffn[l].reshape(1, D_FF)

        p32, _ = _proj(xs, gn1, ms[1], ms[0], lw['w_in'], rs, rs)
        new = {n: _cols(p32, n, w).reshape(bs, tdec, w) for n, w in
               (('a_k', 2 * MIX_W), ('misc', IDX_DIM), ('b_cmp', 128), ('b_slc', 128), ('b_win', 128))}
        tails = [jnp.concatenate([new[n], jnp.zeros((bs, tail_rows - tdec, new[n].shape[-1]), F32)], axis=1)
                 for n in ('a_k', 'misc', 'b_cmp', 'b_slc')]
        kv_s, ki_s, cmp_s, slc_s = _gather_past(page_table, [cache_a_kv, cache_a_idx, cache_b_cmp, cache_b_slc],
                                                tails, (BF16, BF16, F32, BF16), l)
        seq_s = past + tdec
        bias_a = _dsa_select(p32, p32, ki_s, 0, IDX_DIM, n_seq=bs, t_seq=tdec, tq=tdec, qoff=past,
                             k_sel=min(A_TOPK, seq_s // 4), bias_dtype=F32)
        o_a = _flash(p32, _OFF['a_q'] // MIX_W, kv_s, 0, bias_a, n_seq=bs, t_seq=tdec, tq=tdec, tkm=tkm_s, qoff=past,
                     shared=False, out_dtype=F32)
        ncp = -(-(-(-seq_s // CMP_STRIDE) - 1) // 128) * 128
        kcvc = _compress(cmp_s, w_cmp_pos[l], seq_s, ncp)
        win_full = jnp.concatenate([state_b_win[l], new['b_win']], axis=1)
        win_pad = jnp.concatenate([win_full, jnp.zeros((bs, -(w_buf + tdec) % 16, 128), F32)], axis=1)
        ocw, bias_b = _nsa_select(p32, p32, kcvc, win_pad, 0, n_seq=bs, t_seq=tdec, tq=tdec, qoff=past, seq_len=seq_s,
                                  lp=lps, win_dyn=False, win_pos0=past - w_buf, bias_dtype=F32, out_dtype=F32)
        o_b = _flash(p32, _OFF['b_q'] // MIX_W, slc_s, 0, bias_b, n_seq=bs, t_seq=tdec, tq=tdec, tkm=tkm_s, qoff=past,
                     shared=True, out_dtype=F32, gate_args=(p32, ocw))
        o_d, st = _gla(p32, lw['wal'], b_alpha[l], lw['gg'], _state_to_bd(state_d_gla[l]), n_seq=bs, t_seq=tdec,
                       tt=tdec, out_dtype=F32)
        u_tail = (_cols(p32, 'c_c', MIX_W) * _cols(p32, 'c_in', MIX_W)).reshape(bs, tdec, MIX_W)[:, tdec - 2:]
        xs = _merge(o_a, o_b, o_d, p32, xs, ms[2], conv_c[l], lw['wb'], lw['wo'], _conv_fix(state_c_conv[l], tdec),
                    tm=rs, t_seq=tdec)
        xs, y_s, a_full = _ffn(xs, gn2, ms[4], ms[3], ms[5], lw['wa'], lw['wg'], conv_ffn[l], bffn, lw['wout'], gfin,
                               _conv_fix(state_ffn_conv[l], tdec), tm=rs, t_seq=tdec)
        outs_s.append((new['a_k'], new['misc'], new['b_cmp'], new['b_slc'], win_full[:, tdec:], u_tail,
                       _bd_to_state(st), a_full.reshape(bs, tdec, D_FF)[:, tdec - 2:]))

        p32, p16 = _proj(xp, gn1, mp[1], mp[0], lw['w_in'], 1024, seq)
        p16_3 = p16.reshape(bp, seq, NP)
        a_kv = _cols(p32, 'a_k', 2 * MIX_W).reshape(bp, seq, 2 * MIX_W)
        a_idx = _cols(p32, 'misc', IDX_DIM).reshape(bp, seq, IDX_DIM)
        b_cmp = _cols(p32, 'b_cmp', 128).reshape(bp, seq, 128)
        b_slc = _cols(p32, 'b_slc', 128).reshape(bp, seq, 128)
        b_win = _cols(p32, 'b_win', 128).reshape(bp, seq, 128)[:, seq - min(WINDOW, seq):]
        k_sel = min(A_TOPK, seq // 4)
        bias_a = _dsa_select(p16, p32, p16_3, _OFF['misc'] // 128, 128, n_seq=bp, t_seq=seq, tq=128, qoff=0,
                             k_sel=k_sel, bias_dtype=BF16)
        o_a = _flash(p16, _OFF['a_q'] // MIX_W, p16_3, _OFF['a_k'] // (2 * MIX_W), bias_a, n_seq=bp, t_seq=seq,
                     tq=512, tkm=2, qoff=0, shared=False, out_dtype=BF16)
        ncp = -(-(-(-seq // CMP_STRIDE) - 1) // 128) * 128
        kcvc = _compress(b_cmp, w_cmp_pos[l], seq, ncp)
        ocw, bias_b = _nsa_select(p16, p32, kcvc, p16_3, _OFF['b_win'] // 128, n_seq=bp, t_seq=seq, tq=128, qoff=0,
                                  seq_len=seq, lp=seq, win_dyn=True, win_pos0=0, bias_dtype=BF16, out_dtype=BF16)
        o_b = _flash(p16, _OFF['b_q'] // MIX_W, p16_3, _OFF['b_slc'] // 128, bias_b, n_seq=bp, t_seq=seq,
                     tq=512, tkm=2, qoff=0, shared=True, out_dtype=BF16, gate_args=(p32, ocw))
        o_d, st = _gla(p32, lw['wal'], b_alpha[l], lw['gg'], jnp.zeros((bp, MIX_W, MIX_W), F32), n_seq=bp, t_seq=seq,
                       tt=512, out_dtype=BF16)
        u_tail = (_cols(p32, 'c_c', MIX_W) * _cols(p32, 'c_in', MIX_W)).reshape(bp, seq, MIX_W)[:, seq - 2:]
        xp = _merge(o_a, o_b, o_d, p32, xp, mp[2], conv_c[l], lw['wb'], lw['wo'], None, tm=256, t_seq=seq)
        xp, y_p, a_full = _ffn(xp, gn2, mp[4], mp[3], mp[5], lw['wa'], lw['wg'], conv_ffn[l], bffn, lw['wout'], gfin,
                               None, tm=512, t_seq=seq)
        outs_p.append((a_kv, a_idx, b_cmp, b_slc, b_win, u_tail, _bd_to_state(st),
                       a_full.reshape(bp, seq, D_FF)[:, seq - 2:]))

    sp = [jnp.stack(z) for z in zip(*outs_p)]
    ss = [jnp.stack(z) for z in zip(*outs_s)]
    res = [y_p.reshape(bp, seq, d), y_s.reshape(bs, tdec, d)]
    for a, b in zip(sp, ss):
        res += [a, b]
    return tuple(res)
```

```python
import functools

import numpy as np
import jax
import jax.numpy as jnp
from jax import lax
from jax.experimental import pallas as pl
from jax.experimental.pallas import tpu as pltpu

F32 = jnp.float32
BF16 = jnp.bfloat16
HI = lax.Precision.HIGHEST

D_MODEL = 1024
PAGE = 128
HEAD_DIM = 64
MIX_W = 256
N_HEADS = 4
IDX_DIM = 64
A_TOPK = 256
CMP_LEN = 32
CMP_STRIDE = 16
SLC_BLOCK = 64
N_SLC = 16
WINDOW = 512
CONV_W = 3
GATE_RANK = 16
GATE_TAU = 16.0
GLA_CHUNK = 64
D_FF = 2816
EPS = 1e-6
NEG = -1e30
INT_MIN = -2 ** 31
KEY_TILE = 512

_IN_SPLITS = (('a_q', 256), ('a_k', 256), ('a_v', 256), ('a_qi', 256), ('a_ki', 64), ('a_wi', 4),
              ('b_q', 256), ('b_cmp', 128), ('b_slc', 128), ('b_win', 128), ('b_g', 12),
              ('c_in', 256), ('c_b', 256), ('c_c', 256),
              ('d_q', 256), ('d_k', 256), ('d_v', 256), ('d_r', 256), ('d_a', 16), ('gate', 4096))
_OFF = dict(gate=0, a_k=4096, a_v=4352, a_q=4608, a_qi=4864, b_q=5120, b_cmp=5376, b_slc=5504, b_win=5632,
            misc=5760, c_in=5888, c_b=6144, c_c=6400, d_q=6656, d_k=6912, d_v=7168, d_r=7424)
NP = 7680
MISC_KI, MISC_WI, MISC_G, MISC_DA = 0, 64, 68, 80
VMEM_LIMIT = 56 * 1024 * 1024


def _cp(sem):
    return pltpu.CompilerParams(dimension_semantics=sem, vmem_limit_bytes=VMEM_LIMIT)


def _dot(a, b):
    return jnp.dot(a, b, preferred_element_type=F32)


def _dot_nt(a, b):
    return lax.dot_general(a, b, (((1,), (1,)), ((), ())), preferred_element_type=F32)


def _dot_hi(a, b):
    return jnp.dot(a, b, preferred_element_type=F32, precision=HI)


def _const_spec(shape):
    nd = len(shape)
    return pl.BlockSpec(shape, lambda *a: (0,) * nd, pipeline_mode=pl.Buffered(1))


def _iota(shape, dim):
    return lax.broadcasted_iota(jnp.int32, shape, dim)


def _ada_kernel(c_ref, w_ref, b_ref, o_ref):
    o_ref[0] = _dot_hi(c_ref[...], w_ref[0]) + b_ref[0]


def _ada(c_all, w_ada, b_ada):
    depth, d, n6 = w_ada.shape
    rows = c_all.shape[0]
    tn = 1024
    return pl.pallas_call(
        _ada_kernel, grid=(depth, n6 // tn),
        in_specs=[pl.BlockSpec((rows, d), lambda l, j: (0, 0)),
                  pl.BlockSpec((1, d, tn), lambda l, j: (l, 0, j)),
                  pl.BlockSpec((1, 1, tn), lambda l, j: (l, 0, j))],
        out_specs=pl.BlockSpec((1, rows, tn), lambda l, j: (l, 0, j)),
        out_shape=jax.ShapeDtypeStruct((depth, rows, n6), F32),
        compiler_params=_cp(("arbitrary", "arbitrary")), name="ada",
    )(c_all, w_ada, b_ada.reshape(depth, 1, n6))


def _norm_mod(x, g, sc, sh):
    y = x * lax.rsqrt(jnp.mean(x * x, axis=-1, keepdims=True) + EPS) * g
    return y * (1.0 + sc) + sh


def _proj_kernel(x_ref, g_ref, sc_ref, sh_ref, w_ref, o32_ref, o16_ref, h_ref):
    @pl.when(pl.program_id(1) == 0)
    def _():
        h_ref[...] = _norm_mod(x_ref[...], g_ref[...], sc_ref[0], sh_ref[0]).astype(BF16)

    acc = _dot(h_ref[...], w_ref[...])
    o32_ref[...] = acc
    o16_ref[...] = acc.astype(BF16)


def _mod_spec(mod, tm, rows_per_group):
    mb = mod.shape[1]
    tiles = max(rows_per_group // tm, 1)
    return pl.BlockSpec((1, mb, D_MODEL), lambda i, *_: (i // tiles, 0, 0))


def _proj(x, g, sc, sh, w, tm, rows_per_group):
    rows = x.shape[0]
    tn = 768
    return pl.pallas_call(
        _proj_kernel, grid=(rows // tm, NP // tn),
        in_specs=[pl.BlockSpec((tm, D_MODEL), lambda i, j: (i, 0)),
                  pl.BlockSpec((1, D_MODEL), lambda i, j: (0, 0)),
                  _mod_spec(sc, tm, rows_per_group), _mod_spec(sh, tm, rows_per_group),
                  pl.BlockSpec((D_MODEL, tn), lambda i, j: (0, j))],
        out_specs=[pl.BlockSpec((tm, tn), lambda i, j: (i, j)), pl.BlockSpec((tm, tn), lambda i, j: (i, j))],
        out_shape=[jax.ShapeDtypeStruct((rows, NP), F32), jax.ShapeDtypeStruct((rows, NP), BF16)],
        scratch_shapes=[pltpu.VMEM((tm, D_MODEL), BF16)],
        compiler_params=_cp(("arbitrary", "arbitrary")), name="proj",
    )(x, g, sc, sh, w)


PAGES_PER_STEP = KEY_TILE // PAGE


def _gather_kernel(pt_ref, *refs, n_arr, n_chunks):
    ppc = PAGES_PER_STEP
    pages, tails, outs = refs[:n_arr * ppc], refs[n_arr * ppc:n_arr * (ppc + 1)], refs[n_arr * (ppc + 1):]
    c = pl.program_id(1)

    @pl.when(c < n_chunks - 1)
    def _():
        for k in range(n_arr):
            for r in range(ppc):
                outs[k][0, r * PAGE:(r + 1) * PAGE, :] = pages[k * ppc + r][0, 0].astype(outs[k].dtype)

    @pl.when(c == n_chunks - 1)
    def _():
        for k in range(n_arr):
            outs[k][0] = tails[k][0].astype(outs[k].dtype)


def _gather_past(page_table, pools, tails, out_dtypes, layer):
    n_seq, n_pages = page_table.shape
    n_arr, ppc = len(pools), PAGES_PER_STEP
    assert n_pages % ppc == 0 and tails[0].shape[1] == ppc * PAGE
    n_chunks = n_pages // ppc + 1
    lp = n_chunks * ppc * PAGE
    in_specs = []
    for p in pools:
        for r in range(ppc):
            in_specs.append(pl.BlockSpec(
                (1, 1, PAGE, p.shape[-1]),
                lambda b, c, pt, r=r: (layer, pt[b, jnp.minimum(c * ppc + r, n_pages - 1)], 0, 0)))
    in_specs += [pl.BlockSpec((1, ppc * PAGE, t.shape[-1]), lambda b, c, pt: (b, 0, 0)) for t in tails]
    args = [p for p in pools for _ in range(ppc)] + list(tails)
    kern = functools.partial(_gather_kernel, n_arr=n_arr, n_chunks=n_chunks)
    return pl.pallas_call(
        kern,
        grid_spec=pltpu.PrefetchScalarGridSpec(
            num_scalar_prefetch=1, grid=(n_seq, n_chunks), in_specs=in_specs,
            out_specs=[pl.BlockSpec((1, ppc * PAGE, p.shape[-1]), lambda b, c, pt: (b, c, 0)) for p in pools]),
        out_shape=[jax.ShapeDtypeStruct((n_seq, lp, p.shape[-1]), dt) for p, dt in zip(pools, out_dtypes)],
        compiler_params=_cp(("arbitrary", "arbitrary")), name="gather_past",
    )(page_table, *args)


def _stack_heads(q, width):
    t = q.shape[0]
    parts = []
    for h in range(N_HEADS):
        p = q[:, h * HEAD_DIM:(h + 1) * HEAD_DIM]
        if width > HEAD_DIM:
            p = jnp.concatenate([p, jnp.zeros((t, width - HEAD_DIM), q.dtype)], axis=1)
        parts.append(p)
    return jnp.concatenate(parts, axis=0)


def _dsa_sel_kernel(qi_ref, misc_ref, ki_ref, bias_ref, s_ref, *halves, tq, nk, k_sel, qoff, fk):
    tk = KEY_TILE
    packed = len(halves) == 2
    if packed:
        hi_ref, lo_ref = halves
    i = pl.program_id(1)
    q0 = qoff + i * tq
    qpos = q0 + _iota((tq, 1), 0)
    nlim = jnp.minimum((q0 + tq - 1) // tk + 1, nk)
    qs = _stack_heads(qi_ref[...].astype(BF16), fk)
    wi = misc_ref[:, MISC_WI:MISC_WI + N_HEADS]

    def score_tile(j, c):
        kt = ki_ref[0, pl.ds(pl.multiple_of(j * tk, tk), tk), :].astype(BF16)
        d = _dot_nt(qs, kt).reshape(N_HEADS, tq, tk)
        sc = wi[:, 0:1] * jnp.maximum(d[0], 0.0)
        for h in range(1, N_HEADS):
            sc = sc + wi[:, h:h + 1] * jnp.maximum(d[h], 0.0)
        bits = lax.bitcast_convert_type(sc, jnp.int32)
        key = bits ^ ((bits >> 31) & 0x7FFFFFFF)
        key = jnp.where(key == -1, 0, key)
        kpos = j * tk + _iota((1, tk), 1)
        key = jnp.where(kpos <= qpos, key, INT_MIN)
        s_ref[j] = key
        if packed:
            hi_ref[j] = (key >> 16).astype(jnp.int16)
            lo_ref[j] = ((key & 0xFFFF) - 32768).astype(jnp.int16)
        return c

    lax.fori_loop(0, nlim, score_tile, 0)

    def count(ref, pred, level):
        dt = ref.dtype
        level_b = jnp.broadcast_to(level, (tq, 128)).astype(dt)
        one, zero = jnp.ones((), dt), jnp.zeros((), dt)

        def body(j, acc):
            for c in range(tk // 128):
                acc = acc + jnp.where(pred(ref[j, :, c * 128:(c + 1) * 128], level_b), one, zero)
            return acc
        acc = lax.fori_loop(0, nlim, body, jnp.zeros((tq, 128), dt))
        return jnp.sum(acc.astype(F32), axis=-1, keepdims=True)

    def kth_largest(ref, n_bits, want):
        def bit_step(t, c):
            trial = c + lax.shift_left(jnp.int32(1), n_bits - 1 - t)
            return jnp.where(count(ref, lambda s, lv: s >= lv, trial) >= want, trial, c)
        return lax.fori_loop(0, n_bits, bit_step, jnp.full((tq, 1), -2 ** (n_bits - 1), jnp.int32))

    ge, gt, eq = (lambda s, lv: s >= lv), (lambda s, lv: s > lv), (lambda s, lv: s == lv)
    if packed:
        thr_hi = kth_largest(hi_ref, 16, float(k_sel))
        above = count(hi_ref, gt, thr_hi)
        thr_hi16 = jnp.broadcast_to(thr_hi, (tq, tk)).astype(jnp.int16)

        def mask_low(j, c):
            lo_ref[j] = jnp.where(hi_ref[j] == thr_hi16, lo_ref[j], jnp.int16(-32768))
            return c

        lax.fori_loop(0, nlim, mask_low, 0)
        thr_lo = kth_largest(lo_ref, 16, k_sel - above)
        thr = thr_hi * 65536 + (thr_lo + 32768)
        ties_allowed = k_sel - above - count(lo_ref, gt, thr_lo)
    else:
        thr = kth_largest(s_ref, 32, float(k_sel))
        ties_allowed = k_sel - count(s_ref, gt, thr)
    n_ties = count(s_ref, eq, thr)
    need_rank = jnp.max(jnp.where(n_ties > ties_allowed, 1.0, 0.0)) > 0.5

    @pl.when(need_rank)
    def _():
        tri = jnp.where(_iota((tk, tk), 0) <= _iota((tk, tk), 1), 1.0, 0.0).astype(BF16)

        def out_tile(j, carry):
            key = s_ref[j]
            eq = jnp.where(key == thr, 1.0, 0.0)
            rank = _dot(eq.astype(BF16), tri) + carry
            kpos = j * tk + _iota((1, tk), 1)
            take = jnp.where(key > thr, 1.0, jnp.where(rank <= ties_allowed, eq, 0.0))
            take = jnp.where(kpos <= qpos, take, 0.0)
            bias_ref[j] = jnp.where(take > 0.5, 0.0, NEG).astype(bias_ref.dtype)
            return rank[:, tk - 1:tk]

        lax.fori_loop(0, nlim, out_tile, jnp.zeros((tq, 1), F32))

    @pl.when(jnp.logical_not(need_rank))
    def _():
        def out_tile(j, c):
            kpos = j * tk + _iota((1, tk), 1)
            take = jnp.where(kpos <= qpos, jnp.where(s_ref[j] >= thr, 1.0, 0.0), 0.0)
            bias_ref[j] = jnp.where(take > 0.5, 0.0, NEG).astype(bias_ref.dtype)
            return c

        lax.fori_loop(0, nlim, out_tile, 0)

    def fill(j, c):
        bias_ref[j] = jnp.full((tq, tk), NEG, bias_ref.dtype)
        return c

    lax.fori_loop(nlim, nk, fill, 0)


def _dsa_select(q_arr, p32, ki3, ki_col, fk, *, n_seq, t_seq, tq, qoff, k_sel, bias_dtype):
    lp = ki3.shape[1]
    nk, nq = lp // KEY_TILE, t_seq // tq
    rows = n_seq * t_seq
    kern = functools.partial(_dsa_sel_kernel, tq=tq, nk=nk, k_sel=k_sel, qoff=qoff, fk=fk)
    return pl.pallas_call(
        kern, grid=(n_seq, nq),
        in_specs=[pl.BlockSpec((tq, MIX_W), lambda b, i: (b * nq + i, _OFF['a_qi'] // MIX_W)),
                  pl.BlockSpec((tq, 128), lambda b, i: (b * nq + i, _OFF['misc'] // 128)),
                  pl.BlockSpec((1, lp, fk), lambda b, i: (b, 0, ki_col))],
        out_specs=pl.BlockSpec((nk, tq, KEY_TILE), lambda b, i: (0, b * nq + i, 0)),
        out_shape=jax.ShapeDtypeStruct((nk, rows, KEY_TILE), bias_dtype),
        scratch_shapes=[pltpu.VMEM((nk, tq, KEY_TILE), jnp.int32)]
        + ([pltpu.VMEM((nk, tq, KEY_TILE), jnp.int16)] * 2 if tq % 16 == 0 and 4 * nk < 2 ** 15 else []),
        compiler_params=_cp(("arbitrary", "arbitrary")), name="dsa_select",
    )(q_arr, p32, ki3)


def _flash_row_chunk(tq):
    return min(32, tq)


def _flash_kernel(sb, sq, sk, sfl, q_ref, kv_ref, bias_ref, *rest, tq, tkm, shared, gated):
    if gated:
        misc_ref, add_ref, o_ref, m_ref, l_ref, acc_ref, s_ref, p_ref, qs_ref, a_ref = rest
    else:
        o_ref, m_ref, l_ref, acc_ref, s_ref, p_ref, qs_ref, a_ref = rest
    tk = KEY_TILE
    flags = sfl[pl.program_id(0)]
    dv = HEAD_DIM if shared else MIX_W
    lane_head = _iota((1, MIX_W), 1) // HEAD_DIM
    rc = _flash_row_chunk(tq)

    @pl.when((flags & 1) != 0)
    def _():
        m_ref[...] = jnp.full(m_ref.shape, NEG, F32)
        l_ref[...] = jnp.zeros(l_ref.shape, F32)
        acc_ref[...] = jnp.zeros(acc_ref.shape, F32)
        q = (q_ref[...] * (HEAD_DIM ** -0.5)).astype(BF16)
        if shared:
            qs_ref[...] = _stack_heads(q, HEAD_DIM)
        else:
            qs_ref[...] = jnp.concatenate([jnp.where(lane_head == h, q, jnp.zeros_like(q)) for h in range(N_HEADS)],
                                          axis=0)

    for u in range(tkm):
        kvt = kv_ref[0, u * tk:(u + 1) * tk, :]
        k = kvt[:, :dv].astype(BF16)
        v = kvt[:, dv:2 * dv].astype(BF16)
        half = N_HEADS * tq // 2
        s_ref[0:half, :] = _dot_nt(qs_ref[0:half, :], k)
        s_ref[half:, :] = _dot_nt(qs_ref[half:, :], k)
        n_chunks = N_HEADS * tq // rc

        def rows_of(c):
            return (pl.ds(pl.multiple_of(c * rc, rc), rc), pl.ds(pl.multiple_of((c % (tq // rc)) * rc, rc), rc))

        def pass_max(c, carry):
            rows, brow = rows_of(c)
            s = s_ref[rows, :] + bias_ref[u, brow, :].astype(F32)
            s_ref[rows, :] = s
            m_old = m_ref[rows, :]
            m_new = jnp.maximum(m_old, jnp.max(s, axis=-1, keepdims=True))
            a_ref[rows, :] = jnp.exp(m_old - m_new)
            m_ref[rows, :] = m_new
            return carry

        lax.fori_loop(0, n_chunks, pass_max, 0, unroll=min(8, n_chunks))

        def pass_exp(c, carry):
            rows, _ = rows_of(c)
            m = m_ref[rows, :]
            part = a_ref[rows, :] * l_ref[rows, :]
            for cc in range(tk // 128):
                p = jnp.exp(s_ref[rows, cc * 128:(cc + 1) * 128] - m)
                part = part + p
                p_ref[rows, cc * 128:(cc + 1) * 128] = p.astype(p_ref.dtype)
            l_ref[rows, :] = part
            return carry

        lax.fori_loop(0, n_chunks, pass_exp, 0, unroll=min(8, n_chunks))
        alpha = a_ref[...]
        alpha = alpha[:, :dv] if dv <= 128 else jnp.concatenate([alpha] * (dv // 128), axis=1)
        acc_ref[0:half, :] = alpha[0:half] * acc_ref[0:half, :] + _dot(p_ref[0:half, :].astype(BF16), v)
        acc_ref[half:, :] = alpha[half:] * acc_ref[half:, :] + _dot(p_ref[half:, :].astype(BF16), v)

    @pl.when((flags & 2) != 0)
    def _():
        l_sum = jnp.sum(l_ref[...], axis=-1, keepdims=True)
        o = (acc_ref[...] / l_sum).reshape(N_HEADS, tq, dv)
        if shared:
            out = jnp.concatenate([o[h] for h in range(N_HEADS)], axis=1)
        else:
            out = jnp.where(lane_head == 0, o[0], 0.0)
            for h in range(1, N_HEADS):
                out = out + jnp.where(lane_head == h, o[h], 0.0)
        if gated:
            g = jax.nn.sigmoid(misc_ref[...])
            gate = jnp.zeros((tq, MIX_W), F32)
            for h in range(N_HEADS):
                c = MISC_G + 3 * h + 1
                gate = gate + jnp.where(lane_head == h, g[:, c:c + 1], 0.0)
            out = add_ref[...].astype(F32) + gate * out
        o_ref[...] = out.astype(o_ref.dtype)


def _flash_steps(n_seq, t_seq, tq, tk, lp, qoff):
    sb, sq, sk, sfl = [], [], [], []
    for b in range(n_seq):
        for i in range(t_seq // tq):
            nlim = min((qoff + (i + 1) * tq - 1) // tk + 1, lp // tk)
            for j in range(nlim):
                sb.append(b), sq.append(i), sk.append(j)
                sfl.append((1 if j == 0 else 0) | (2 if j == nlim - 1 else 0))
    return [jnp.asarray(np.asarray(a, np.int32)) for a in (sb, sq, sk, sfl)]


def _flash(q_arr, q_col, kv3, kv_col, bias, *, n_seq, t_seq, tq, tkm, qoff, shared, out_dtype, gate_args=None):
    lp = kv3.shape[1]
    tk = tkm * KEY_TILE
    nq = t_seq // tq
    rows = n_seq * t_seq
    wblk = 2 * (HEAD_DIM if shared else MIX_W)
    dv = HEAD_DIM if shared else MIX_W
    steps = _flash_steps(n_seq, t_seq, tq, tk, lp, qoff)
    gated = gate_args is not None
    row = lambda s, sb, sq, sk, sfl: sb[s] * nq + sq[s]
    in_specs = [pl.BlockSpec((tq, MIX_W), lambda s, sb, sq, sk, sfl: (row(s, sb, sq, sk, sfl), q_col)),
                pl.BlockSpec((1, tk, wblk), lambda s, sb, sq, sk, sfl: (sb[s], sk[s], kv_col)),
                pl.BlockSpec((tkm, tq, KEY_TILE), lambda s, sb, sq, sk, sfl: (sk[s], row(s, sb, sq, sk, sfl), 0))]
    args = [q_arr, kv3, bias]
    if gated:
        in_specs += [pl.BlockSpec((tq, 128), lambda s, sb, sq, sk, sfl: (row(s, sb, sq, sk, sfl), _OFF['misc'] // 128)),
                     pl.BlockSpec((tq, MIX_W), lambda s, sb, sq, sk, sfl: (row(s, sb, sq, sk, sfl), 0))]
        args += list(gate_args)
    kern = functools.partial(_flash_kernel, tq=tq, tkm=tkm, shared=shared, gated=gated)
    return pl.pallas_call(
        kern,
        grid_spec=pltpu.PrefetchScalarGridSpec(
            num_scalar_prefetch=4, grid=(int(steps[0].shape[0]),), in_specs=in_specs,
            out_specs=pl.BlockSpec((tq, MIX_W), lambda s, sb, sq, sk, sfl: (row(s, sb, sq, sk, sfl), 0)),
            scratch_shapes=[pltpu.VMEM((N_HEADS * tq, 128), F32), pltpu.VMEM((N_HEADS * tq, 128), F32),
                            pltpu.VMEM((N_HEADS * tq, dv), F32), pltpu.VMEM((N_HEADS * tq, KEY_TILE), F32),
                            pltpu.VMEM((N_HEADS * tq, KEY_TILE), BF16 if _flash_row_chunk(tq) % 16 == 0 else F32),
                            pltpu.VMEM((N_HEADS * tq, dv), BF16), pltpu.VMEM((N_HEADS * tq, 128), F32)]),
        out_shape=jax.ShapeDtypeStruct((rows, MIX_W), out_dtype),
        compiler_params=_cp(("arbitrary",)), name="flash_shared" if shared else "flash_heads",
    )(*steps, *args)


def _flash_t_kernel(sq, sk, sfl, qt_ref, k_ref, vt_ref, bias_ref, *rest, tq, tkm, shared, gated):
    if gated:
        misct_ref, addt_ref, o_ref, m_ref, l_ref, acc_ref = rest
    else:
        o_ref, m_ref, l_ref, acc_ref = rest
    tk = KEY_TILE
    flags = sfl[pl.program_id(0)]
    row_head = _iota((MIX_W, 1), 0) // HEAD_DIM

    @pl.when((flags & 1) != 0)
    def _():
        m_ref[...] = jnp.full(m_ref.shape, NEG, F32)
        l_ref[...] = jnp.zeros(l_ref.shape, F32)
        acc_ref[...] = jnp.zeros(acc_ref.shape, F32)

    qt = qt_ref[...] * (HEAD_DIM ** -0.5)
    for u in range(tkm):
        k = k_ref[u * tk:(u + 1) * tk, :]
        k = k[:, :HEAD_DIM] if shared else k
        bias_t = bias_ref[u].astype(F32).T
        for h in range(N_HEADS):
            hs = slice(h * HEAD_DIM, (h + 1) * HEAD_DIM)
            q_h = qt[hs, :] if shared else jnp.where(row_head == h, qt, jnp.zeros_like(qt))
            s = _dot(k, q_h) + bias_t
            m_old = m_ref[h]
            m_new = jnp.maximum(m_old, jnp.max(s, axis=0, keepdims=True))
            alpha = jnp.exp(m_old - m_new)
            p = jnp.exp(s - m_new)
            l_ref[h] = alpha * l_ref[h] + jnp.sum(p, axis=0, keepdims=True)
            vt_h = vt_ref[0:HEAD_DIM, u * tk:(u + 1) * tk] if shared else vt_ref[hs, u * tk:(u + 1) * tk]
            acc_ref[h] = alpha * acc_ref[h] + _dot(vt_h, p.astype(BF16))
            m_ref[h] = m_new

    @pl.when((flags & 2) != 0)
    def _():
        for h in range(N_HEADS):
            hs = slice(h * HEAD_DIM, (h + 1) * HEAD_DIM)
            out = acc_ref[h] / l_ref[h]
            if gated:
                c = MISC_G + 3 * h + 1
                out = addt_ref[hs, :].astype(F32) + jax.nn.sigmoid(misct_ref[c:c + 1, :]) * out
            o_ref[hs, :] = out.astype(o_ref.dtype)


def _flash_t(qt, k_arr, k_col, k_width, vt, bias, *, t_seq, tq, tkm, shared, gate_args=None):
    tk = tkm * KEY_TILE
    steps = _flash_steps(1, t_seq, tq, tk, t_seq, 0)[1:]
    gated = gate_args is not None
    vrows = vt.shape[0]
    in_specs = [pl.BlockSpec((MIX_W, tq), lambda s, sq, sk, sfl: (0, sq[s])),
                pl.BlockSpec((tk, k_width), lambda s, sq, sk, sfl: (sk[s], k_col)),
                pl.BlockSpec((vrows, tk), lambda s, sq, sk, sfl: (0, sk[s])),
                pl.BlockSpec((tkm, tq, KEY_TILE), lambda s, sq, sk, sfl: (sk[s], sq[s], 0))]
    args = [qt, k_arr, vt, bias]
    if gated:
        in_specs += [pl.BlockSpec((128, tq), lambda s, sq, sk, sfl: (0, sq[s])),
                     pl.BlockSpec((MIX_W, tq), lambda s, sq, sk, sfl: (0, sq[s]))]
        args += list(gate_args)
    kern = functools.partial(_flash_t_kernel, tq=tq, tkm=tkm, shared=shared, gated=gated)
    return pl.pallas_call(
        kern,
        grid_spec=pltpu.PrefetchScalarGridSpec(
            num_scalar_prefetch=3, grid=(int(steps[0].shape[0]),), in_specs=in_specs,
            out_specs=pl.BlockSpec((MIX_W, tq), lambda s, sq, sk, sfl: (0, sq[s])),
            scratch_shapes=[pltpu.VMEM((N_HEADS, 1, tq), F32), pltpu.VMEM((N_HEADS, 1, tq), F32),
                            pltpu.VMEM((N_HEADS, HEAD_DIM, tq), F32)]),
        out_shape=jax.ShapeDtypeStruct((MIX_W, t_seq), BF16),
        compiler_params=_cp(("arbitrary",)), name="flash_t_shared" if shared else "flash_t_heads",
    )(*steps, *args)


def _compress_kernel(x_ref, w_ref, o_ref, *, n_blocks, ncp):
    w = w_ref[...]
    e = jnp.exp(w - jnp.max(w, axis=-1, keepdims=True))
    w = e / jnp.sum(e, axis=-1, keepdims=True)
    n16 = x_ref.shape[1] // CMP_STRIDE
    width = x_ref.shape[2]
    first = jnp.zeros((n16, width), F32)
    second = jnp.zeros((n16, width), F32)
    for j in range(CMP_STRIDE):
        xj = x_ref[0, pl.ds(j, n16, stride=CMP_STRIDE), :]
        first = first + xj * w[:, j:j + 1]
        second = second + xj * w[:, CMP_STRIDE + j:CMP_STRIDE + j + 1]
    shifted = jnp.concatenate([second[1:], jnp.zeros((1, width), F32)], axis=0)
    out = first + shifted
    if n16 < ncp:
        out = jnp.concatenate([out, jnp.zeros((ncp - n16, width), F32)], axis=0)
    out = out[:ncp]
    o_ref[0] = jnp.where(_iota((ncp, 1), 0) < n_blocks, out, 0.0)


def _compress(rows3, w_pos, length, ncp):
    n_seq, lp, width = rows3.shape
    assert lp % CMP_STRIDE == 0
    n_blocks = -(-length // CMP_STRIDE) - 1
    kern = functools.partial(_compress_kernel, n_blocks=n_blocks, ncp=ncp)
    return pl.pallas_call(
        kern, grid=(n_seq,),
        in_specs=[pl.BlockSpec((1, lp, width), lambda b: (b, 0, 0)),
                  pl.BlockSpec((1, CMP_LEN), lambda b: (0, 0))],
        out_specs=pl.BlockSpec((1, ncp, width), lambda b: (b, 0, 0)),
        out_shape=jax.ShapeDtypeStruct((n_seq, ncp, width), F32),
        compiler_params=_cp(("arbitrary",)), name="compress",
    )(rows3, w_pos.reshape(1, CMP_LEN))


def _masked_softmax(s, mask):
    s = jnp.where(mask, s, NEG)
    m = jnp.max(s, axis=-1, keepdims=True)
    e = jnp.where(mask, jnp.exp(s - m), 0.0)
    return e / jnp.maximum(jnp.sum(e, axis=-1, keepdims=True), 1e-30)


def _nsa_sel_kernel(q_ref, misc_ref, kc_ref, win_ref, ocw_ref, bias_ref, *, tq, nk, qoff, n_cmp, n_blk, nbp, n_top,
                    win_rows, win_dyn, win_pos0):
    tk = KEY_TILE
    i = pl.program_id(1)
    q0 = qoff + i * tq
    qpos = q0 + _iota((tq, 1), 0)
    nlim = jnp.minimum((q0 + tq - 1) // tk + 1, nk)
    qs = _stack_heads((q_ref[...] * (HEAD_DIM ** -0.5)).astype(BF16), HEAD_DIM)
    ncp = kc_ref.shape[1]

    kcv = kc_ref[0]
    kc = kcv[:, :HEAD_DIM].astype(BF16)
    vc = kcv[:, HEAD_DIM:].astype(BF16)
    cidx = _iota((1, ncp), 1)
    c_mask = ((cidx * CMP_STRIDE + CMP_LEN - 1) <= qpos) & (cidx < n_cmp)
    s_c = _dot_nt(qs, kc).reshape(N_HEADS, tq, ncp)
    p_c = _masked_softmax(s_c, c_mask[None])
    o_c = _dot(p_c.reshape(N_HEADS * tq, ncp).astype(BF16), vc).reshape(N_HEADS, tq, HEAD_DIM)

    c_start = _iota((ncp, 1), 0) * CMP_STRIDE
    s_start = _iota((1, nbp), 1) * SLC_BLOCK
    overlap = (c_start < s_start + SLC_BLOCK) & (c_start + CMP_LEN > s_start) & (_iota((ncp, 1), 0) < n_cmp)
    imp = _dot_hi(p_c[0] + p_c[1] + p_c[2] + p_c[3], jnp.where(overlap, 1.0, 0.0))
    blk = _iota((1, nbp), 1)
    cur = qpos // SLC_BLOCK
    forced = (blk == 0) | (blk == cur) | (blk == cur - 1)
    live = jnp.where(forced, jnp.inf, jnp.where((blk <= cur) & (blk < n_blk), imp, -jnp.inf))
    blk_f = blk.astype(F32)

    def pick(t, carry):
        live, sel = carry
        top = jnp.max(live, axis=-1, keepdims=True)
        first = jnp.min(jnp.where(live == top, blk_f, float(nbp)), axis=-1, keepdims=True)
        hit = blk_f == first
        return jnp.where(hit, -jnp.inf, live), jnp.where(hit, 1.0, sel)

    _, sel = lax.fori_loop(0, n_top, pick, (live, jnp.zeros((tq, nbp), F32)))
    sel = sel.astype(BF16)
    row_blk = _iota((nbp, 1), 0)

    def bias_tile(j, c):
        kpos = j * tk + _iota((1, tk), 1)
        expand = jnp.where(row_blk == kpos // SLC_BLOCK, 1.0, 0.0).astype(BF16)
        on = _dot(sel, expand)
        ok = (on > 0.5) & (kpos <= qpos)
        bias_ref[j] = jnp.where(ok, 0.0, NEG).astype(bias_ref.dtype)
        return c

    lax.fori_loop(0, nlim, bias_tile, 0)

    def fill(j, c):
        bias_ref[j] = jnp.full((tq, tk), NEG, bias_ref.dtype)
        return c

    lax.fori_loop(nlim, nk, fill, 0)

    if win_dyn:
        start = pl.multiple_of(jnp.maximum(q0 - WINDOW, 0), 8)
        wkv = win_ref[0, pl.ds(start, win_rows), :]
        kwpos = start + _iota((1, win_rows), 1)
    else:
        wkv = win_ref[0]
        kwpos = win_pos0 + _iota((1, win_rows), 1)
    kw = wkv[:, :HEAD_DIM].astype(BF16)
    vw = wkv[:, HEAD_DIM:].astype(BF16)
    rel = qpos - kwpos
    w_mask = (rel >= 0) & (rel < WINDOW) & (kwpos >= 0)
    s_w = _dot_nt(qs, kw).reshape(N_HEADS, tq, win_rows)
    p_w = _masked_softmax(s_w, w_mask[None])
    o_w = _dot(p_w.reshape(N_HEADS * tq, win_rows).astype(BF16), vw).reshape(N_HEADS, tq, HEAD_DIM)

    g = jax.nn.sigmoid(misc_ref[...])
    parts = []
    for h in range(N_HEADS):
        c = MISC_G + 3 * h
        parts.append(g[:, c:c + 1] * o_c[h] + g[:, c + 2:c + 3] * o_w[h])
    ocw_ref[...] = jnp.concatenate(parts, axis=1).astype(ocw_ref.dtype)


def _nsa_select(q_arr, p32, kcvc, win3, win_col, *, n_seq, t_seq, tq, qoff, seq_len, lp, win_dyn, win_pos0, bias_dtype,
                out_dtype):
    nk, nq = lp // KEY_TILE, t_seq // tq
    rows = n_seq * t_seq
    n_cmp = -(-seq_len // CMP_STRIDE) - 1
    n_blk = -(-seq_len // SLC_BLOCK)
    nbp = -(-n_blk // 128) * 128
    win_rows = (tq + WINDOW) if win_dyn else win3.shape[1]
    kern = functools.partial(_nsa_sel_kernel, tq=tq, nk=nk, qoff=qoff, n_cmp=n_cmp, n_blk=n_blk, nbp=nbp,
                             n_top=min(N_SLC, n_blk), win_rows=win_rows, win_dyn=win_dyn, win_pos0=win_pos0)
    return pl.pallas_call(
        kern, grid=(n_seq, nq),
        in_specs=[pl.BlockSpec((tq, MIX_W), lambda b, i: (b * nq + i, _OFF['b_q'] // MIX_W)),
                  pl.BlockSpec((tq, 128), lambda b, i: (b * nq + i, _OFF['misc'] // 128)),
                  pl.BlockSpec((1,) + kcvc.shape[1:], lambda b, i: (b, 0, 0)),
                  pl.BlockSpec((1, win3.shape[1], 128), lambda b, i: (b, 0, win_col))],
        out_specs=[pl.BlockSpec((tq, MIX_W), lambda b, i: (b * nq + i, 0)),
                   pl.BlockSpec((nk, tq, KEY_TILE), lambda b, i: (0, b * nq + i, 0))],
        out_shape=[jax.ShapeDtypeStruct((rows, MIX_W), out_dtype),
                   jax.ShapeDtypeStruct((nk, rows, KEY_TILE), bias_dtype)],
        compiler_params=_cp(("arbitrary", "arbitrary")), name="nsa_select",
    )(q_arr, p32, kcvc, win3)


def _gla_tables(c):
    levels = []
    b = c
    while b >= 2:
        levels.append(b)
        b //= 2
    r = np.arange(c)
    mats = [(r[None, :] <= r[:, None]), (r[None, :] > r[:, None])]
    qm, km = [], []
    for b in levels:
        mid = (r // b) * b + b // 2
        upper = r >= mid
        qm.append(upper[:, None] & (r[None, :] >= mid[:, None]) & (r[None, :] <= r[:, None]))
        km.append((~upper)[:, None] & (r[None, :] > r[:, None]) & (r[None, :] < mid[:, None]))
    return np.concatenate(mats + qm + km, axis=0).astype(np.float32), levels


def _gla_kernel(q_ref, k_ref, v_ref, r_ref, misc_ref, wal_ref, bal_ref, gg_ref, m_ref, s0_ref, o_ref, st_ref, s_scr,
                *, tt, c, levels, n_tiles):
    t_idx = pl.program_id(1)
    nl = len(levels)

    @pl.when(t_idx == 0)
    def _():
        s_scr[...] = s0_ref[0]

    lane_head = _iota((1, MIX_W), 1) // HEAD_DIM
    same_head = (_iota((MIX_W, 1), 0) // HEAD_DIM) == lane_head
    eye = jnp.where(_iota((MIX_W, MIX_W), 0) == _iota((MIX_W, MIX_W), 1), 1.0, 0.0).astype(BF16)
    ones_head = jnp.where(same_head, 1.0, 0.0)
    rr, cc = _iota((c, c), 0), _iota((c, c), 1)
    pair_masks = [((rr // b) == (cc // b)) & ((rr % b) >= b // 2) & ((cc % b) < b // 2) for b in levels]
    diag_mask = rr == cc

    z = _dot_hi(misc_ref[...], wal_ref[...]) + bal_ref[...]
    la = (jnp.minimum(z, 0.0) - jnp.log1p(jnp.exp(-jnp.abs(z)))) * (1.0 / GATE_TAU)
    m_all = m_ref[...]

    def bd(x):
        return jnp.concatenate([jnp.where(lane_head == h, x, jnp.zeros_like(x)) for h in range(N_HEADS)], axis=0)

    def unbd(x):
        out = jnp.where(lane_head == 0, x[0:c], 0.0)
        for h in range(1, N_HEADS):
            out = out + jnp.where(lane_head == h, x[h * c:(h + 1) * c], 0.0)
        return out

    for ci in range(tt // c):
        sl = slice(ci * c, (ci + 1) * c)
        e = jnp.exp(_dot_hi(m_all, la[sl]))
        q = q_ref[sl, :] * (HEAD_DIM ** -0.5)
        k = k_ref[sl, :]
        v = v_ref[sl, :].astype(BF16)
        e_cum, e_rest = e[0:c], e[c:2 * c]
        att = jnp.where(diag_mask[None], _dot_nt(bd(q.astype(BF16)), k.astype(BF16)).reshape(N_HEADS, c, c), 0.0)
        for li in range(nl):
            eq = e[(2 + li) * c:(3 + li) * c]
            ek = e[(2 + nl + li) * c:(3 + nl + li) * c]
            a = _dot_nt(bd((q * eq).astype(BF16)), (k * ek).astype(BF16)).reshape(N_HEADS, c, c)
            att = att + jnp.where(pair_masks[li][None], a, 0.0)
        o_intra = unbd(_dot(att.reshape(N_HEADS * c, c).astype(BF16), v))
        st = s_scr[...]
        o_inter = _dot_nt((q * e_cum).astype(BF16), st.astype(BF16))
        v_t = _dot_nt(eye, v).astype(BF16)
        upd = _dot(v_t, (k * e_rest).astype(BF16))
        s_scr[...] = st * e_cum[c - 1:c, :] + jnp.where(same_head, upd, 0.0)
        o = o_inter + o_intra
        ms = _dot_hi(o * o, ones_head) * (1.0 / HEAD_DIM)
        o = o * lax.rsqrt(ms + EPS) * gg_ref[...]
        r = r_ref[sl, :]
        o_ref[sl, :] = (o * (r * jax.nn.sigmoid(r))).astype(o_ref.dtype)

    @pl.when(t_idx == n_tiles - 1)
    def _():
        st_ref[0] = s_scr[...]


def _gla(p32, wal_pad, b_alpha, g_gla4, st0, *, n_seq, t_seq, tt, out_dtype):
    c = min(GLA_CHUNK, t_seq)
    m_all, levels = _gla_tables(c)
    n_tiles = t_seq // tt
    rows = n_seq * t_seq
    col = lambda name: pl.BlockSpec((tt, MIX_W), lambda b, t, o=_OFF[name] // MIX_W: (b * n_tiles + t, o))
    kern = functools.partial(_gla_kernel, tt=tt, c=c, levels=levels, n_tiles=n_tiles)
    return pl.pallas_call(
        kern, grid=(n_seq, n_tiles),
        in_specs=[col('d_q'), col('d_k'), col('d_v'), col('d_r'),
                  pl.BlockSpec((tt, 128), lambda b, t: (b * n_tiles + t, _OFF['misc'] // 128)),
                  pl.BlockSpec((128, MIX_W), lambda b, t: (0, 0)),
                  pl.BlockSpec((1, MIX_W), lambda b, t: (0, 0)),
                  pl.BlockSpec((1, MIX_W), lambda b, t: (0, 0)),
                  pl.BlockSpec(m_all.shape, lambda b, t: (0, 0)),
                  pl.BlockSpec((1, MIX_W, MIX_W), lambda b, t: (b, 0, 0))],
        out_specs=[pl.BlockSpec((tt, MIX_W), lambda b, t: (b * n_tiles + t, 0)),
                   pl.BlockSpec((1, MIX_W, MIX_W), lambda b, t: (b, 0, 0))],
        out_shape=[jax.ShapeDtypeStruct((rows, MIX_W), out_dtype),
                   jax.ShapeDtypeStruct((n_seq, MIX_W, MIX_W), F32)],
        scratch_shapes=[pltpu.VMEM((MIX_W, MIX_W), F32)],
        compiler_params=_cp(("arbitrary", "arbitrary")), name="gla",
    )(p32, p32, p32, p32, p32, wal_pad, b_alpha.reshape(1, MIX_W), g_gla4, jnp.asarray(m_all), st0)


def _dwconv3(u, w, carry_ref, fix_refs, t_seq, tm, first_tile):
    row = _iota((tm, 1), 0)
    u1 = pltpu.roll(u, 1, axis=0)
    u2 = pltpu.roll(u, 2, axis=0)
    if fix_refs is None:
        prev = jnp.where(first_tile, 0.0, carry_ref[0:2, :])
        u1 = jnp.where(row == 0, prev[1:2], u1)
        u2 = jnp.where(row == 0, prev[0:1], jnp.where(row == 1, prev[1:2], u2))
        carry_ref[0:2, :] = u[tm - 2:tm]
    else:
        pos = row % t_seq
        u1 = jnp.where(pos == 0, fix_refs[0][...], u1)
        u2 = jnp.where(pos < 2, fix_refs[1][...], u2)
    return w[0:1] * u2 + w[1:2] * u1 + w[2:3] * u


def _merge_kernel(oa_ref, ob_ref, od_ref, cin_ref, cb_ref, cc_ref, gate_ref, x_ref, g1_ref, wconv_ref, wb_ref,
                  wo_ref, *rest, tm, t_seq, tiles_per_seq, per_row):
    if per_row:
        fix1_ref, fix2_ref, o_ref, carry_ref = rest
        fix = (fix1_ref, fix2_ref)
    else:
        o_ref, carry_ref = rest
        fix = None
    first = (pl.program_id(0) % tiles_per_seq) == 0
    u = cc_ref[...] * cin_ref[...]
    o_c = cb_ref[...] * _dwconv3(u, wconv_ref[...], carry_ref, fix, t_seq, tm, first)
    branches = (oa_ref[...], ob_ref[...], o_c, od_ref[...])
    merged = jnp.zeros((tm, D_MODEL), F32)
    for bi, br in enumerate(branches):
        gate = jax.nn.sigmoid(gate_ref[:, bi * D_MODEL:(bi + 1) * D_MODEL])
        merged = merged + gate * _dot(br.astype(BF16), wb_ref[bi])
    o_ref[...] = x_ref[...] + g1_ref[0] * _dot(merged.astype(BF16), wo_ref[...])


def _merge(oa, ob, od, p32, x, g1, conv_c, wb, wo, fix, *, tm, t_seq):
    rows = x.shape[0]
    per_row = fix is not None
    tiles_per_seq = max(t_seq // tm, 1)
    rowblk = lambda width, colblk=0: pl.BlockSpec((tm, width), lambda i: (i, colblk))
    in_specs = [rowblk(MIX_W), rowblk(MIX_W), rowblk(MIX_W),
                rowblk(MIX_W, _OFF['c_in'] // MIX_W), rowblk(MIX_W, _OFF['c_b'] // MIX_W),
                rowblk(MIX_W, _OFF['c_c'] // MIX_W), rowblk(N_HEADS * D_MODEL, 0), rowblk(D_MODEL),
                _mod_spec(g1, tm, t_seq), _const_spec((CONV_W, MIX_W)), _const_spec((N_HEADS, MIX_W, D_MODEL)),
                _const_spec((D_MODEL, D_MODEL))]
    args = [oa, ob, od, p32, p32, p32, p32, x, g1, conv_c, wb, wo]
    if per_row:
        in_specs += [rowblk(MIX_W), rowblk(MIX_W)]
        args += list(fix)
    kern = functools.partial(_merge_kernel, tm=tm, t_seq=t_seq, tiles_per_seq=tiles_per_seq, per_row=per_row)
    return pl.pallas_call(
        kern, grid=(rows // tm,), in_specs=in_specs, out_specs=rowblk(D_MODEL),
        out_shape=jax.ShapeDtypeStruct((rows, D_MODEL), F32),
        scratch_shapes=[pltpu.VMEM((8, MIX_W), F32)],
        compiler_params=_cp(("arbitrary",)), name="merge",
    )(*args)


def _ffn_kernel(x_ref, gn_ref, sc_ref, sh_ref, g2_ref, wa_ref, wg_ref, wconv_ref, bf_ref, wout_ref, gf_ref, *rest,
                tm, tf, t_seq, tiles_per_seq, per_row):
    if per_row:
        fix1_ref, fix2_ref, o_ref, y_ref, a_ref, carry_ref = rest
    else:
        o_ref, y_ref, a_ref, carry_ref = rest
    first = (pl.program_id(0) % tiles_per_seq) == 0
    x = x_ref[...]
    h = _norm_mod(x, gn_ref[...], sc_ref[0], sh_ref[0]).astype(BF16)
    acc = jnp.zeros((tm, D_MODEL), F32)
    for f in range(D_FF // tf):
        fs = slice(f * tf, (f + 1) * tf)
        a = _dot(h, wa_ref[:, fs])
        g = _dot(h, wg_ref[:, fs])
        a_ref[:, fs] = a
        fix = (fix1_ref.at[:, fs], fix2_ref.at[:, fs]) if per_row else None
        conv = _dwconv3(a, wconv_ref[:, fs], carry_ref.at[:, fs], fix, t_seq, tm, first)
        pre = conv + bf_ref[:, fs]
        act = pre * jax.nn.sigmoid(pre) * g
        acc = acc + _dot(act.astype(BF16), wout_ref[fs, :])
    xn = x + g2_ref[0] * acc
    o_ref[...] = xn
    y_ref[...] = xn * lax.rsqrt(jnp.mean(xn * xn, axis=-1, keepdims=True) + EPS) * gf_ref[...]


def _ffn(x, gn, sc, sh, g2, wa, wg, conv_ffn, b_ffn, wout, g_final, fix, *, tm, t_seq):
    rows = x.shape[0]
    per_row = fix is not None
    tiles_per_seq = max(t_seq // tm, 1)
    tf = 256
    rowblk = lambda width: pl.BlockSpec((tm, width), lambda i: (i, 0))
    in_specs = [rowblk(D_MODEL), _const_spec((1, D_MODEL)), _mod_spec(sc, tm, t_seq), _mod_spec(sh, tm, t_seq),
                _mod_spec(g2, tm, t_seq), _const_spec((D_MODEL, D_FF)), _const_spec((D_MODEL, D_FF)),
                _const_spec((CONV_W, D_FF)), _const_spec((1, D_FF)), _const_spec((D_FF, D_MODEL)),
                _const_spec((1, D_MODEL))]
    args = [x, gn, sc, sh, g2, wa, wg, conv_ffn, b_ffn, wout, g_final]
    if per_row:
        in_specs += [rowblk(D_FF), rowblk(D_FF)]
        args += list(fix)
    kern = functools.partial(_ffn_kernel, tm=tm, tf=tf, t_seq=t_seq, tiles_per_seq=tiles_per_seq, per_row=per_row)
    return pl.pallas_call(
        kern, grid=(rows // tm,), in_specs=in_specs,
        out_specs=[rowblk(D_MODEL), rowblk(D_MODEL), rowblk(D_FF)],
        out_shape=[jax.ShapeDtypeStruct((rows, D_MODEL), F32), jax.ShapeDtypeStruct((rows, D_MODEL), F32),
                   jax.ShapeDtypeStruct((rows, D_FF), F32)],
        scratch_shapes=[pltpu.VMEM((8, D_FF), F32)],
        compiler_params=_cp(("arbitrary",)), name="ffn",
    )(*args)


def _permute_w_in(w):
    offs, o = {}, 0
    for name, n in _IN_SPLITS:
        offs[name] = (o, n)
        o += n

    def c(name):
        s, n = offs[name]
        return w[:, s:s + n]

    pieces = [c('gate'), c('a_k'), c('a_v'), c('a_q'), c('a_qi'), c('b_q'), c('b_cmp'), c('b_slc'), c('b_win'),
              c('a_ki'), c('a_wi'), c('b_g'), c('d_a'), jnp.zeros((w.shape[0], 32), w.dtype),
              c('c_in'), c('c_b'), c('c_c'), c('d_q'), c('d_k'), c('d_v'), c('d_r')]
    return jnp.concatenate(pieces, axis=1).astype(BF16)


def _cols(p, name, width):
    return p[:, _OFF[name]:_OFF[name] + width]


def _state_to_bd(s0):
    b = s0.shape[0]
    s0t = jnp.swapaxes(s0, 2, 3)
    eye = jnp.eye(N_HEADS, dtype=s0.dtype)[None, :, None, :, None]
    return (s0t[:, :, :, None, :] * eye).reshape(b, MIX_W, MIX_W)


def _bd_to_state(st):
    b = st.shape[0]
    st5 = st.reshape(b, N_HEADS, HEAD_DIM, N_HEADS, HEAD_DIM)
    return jnp.stack([jnp.swapaxes(st5[:, h, :, h, :], 1, 2) for h in range(N_HEADS)], axis=1)


def _conv_fix(state, t_seq):
    b, _, c = state.shape
    fix1 = jnp.concatenate([state[:, 1:2], jnp.zeros((b, t_seq - 1, c), state.dtype)], axis=1)
    fix2 = jnp.concatenate([state, jnp.zeros((b, t_seq - 2, c), state.dtype)], axis=1)
    return fix1.reshape(b * t_seq, c), fix2.reshape(b * t_seq, c)


def _layer_weights(l, w_in, w_alpha, g_gla, w_branch, w_out, w_ffn_in, w_ffn_out):
    wal_pad = jnp.zeros((128, MIX_W), F32).at[MISC_DA:MISC_DA + GATE_RANK].set(w_alpha[l])
    return dict(w_in=_permute_w_in(w_in[l]), wal=wal_pad, gg=jnp.tile(g_gla[l], N_HEADS).reshape(1, MIX_W),
                wb=w_branch[l].astype(BF16), wo=w_out[l].astype(BF16),
                wa=w_ffn_in[l][:, :D_FF].astype(BF16), wg=w_ffn_in[l][:, D_FF:].astype(BF16),
                wout=w_ffn_out[l].astype(BF16))


def kernel(x_prompt, x_sample, cache_a_kv, cache_a_idx, cache_b_cmp, cache_b_slc, state_b_win, state_c_conv, state_d_gla, state_ffn_conv, page_table, c_prompt, c_sample, w_ada, b_ada, g_norm1, w_in, w_cmp_pos, conv_c, w_alpha, b_alpha, g_gla, w_branch, w_out, g_norm2, w_ffn_in, conv_ffn, b_ffn, w_ffn_out, g_final):
    bp, seq, d = x_prompt.shape
    bs, tdec, _ = x_sample.shape
    depth = w_ada.shape[0]
    n_pages = page_table.shape[1]
    past = n_pages * PAGE
    w_buf = state_b_win.shape[2]
    assert bp == 1 and d == D_MODEL and seq % 1024 == 0 and tdec == 8 and w_buf == WINDOW

    n_c = bp + bs
    c_all = jnp.concatenate([c_prompt, c_sample, jnp.zeros((-n_c % 8, d), F32)], axis=0)
    mod = _ada(c_all, w_ada, b_ada)

    xp = x_prompt.reshape(bp * seq, d)
    xs = x_sample.reshape(bs * tdec, d)
    rs = bs * tdec
    tail_rows = KEY_TILE
    lps = past + tail_rows
    tkm_s = max(t for t in (1, 2, 3) if (lps // KEY_TILE) % t == 0)
    gfin = g_final.reshape(1, d)
    outs_p, outs_s = [], []
    y_p = y_s = None
    for l in range(depth):
        lw = _layer_weights(l, w_in, w_alpha, g_gla, w_branch, w_out, w_ffn_in, w_ffn_out)
        mp = [mod[l, :bp, k * d:(k + 1) * d].reshape(bp, 1, d) for k in range(6)]
        ms = [jnp.repeat(mod[l, bp:n_c, k * d:(k + 1) * d], tdec, axis=0).reshape(1, rs, d) for k in range(6)]
        gn1, gn2 = g_norm1[l].reshape(1, d), g_norm2[l].reshape(1, d)
        bffn = b_ffn[l].reshape(1, D_FF)

        p32, _ = _proj(xs, gn1, ms[1], ms[0], lw['w_in'], rs, rs)
        new = {n: _cols(p32, n, w).reshape(bs, tdec, w) for n, w in
               (('a_k', 2 * MIX_W), ('misc', IDX_DIM), ('b_cmp', 128), ('b_slc', 128), ('b_win', 128))}
        tails = [jnp.concatenate([new[n], jnp.zeros((bs, tail_rows - tdec, new[n].shape[-1]), F32)], axis=1)
                 for n in ('a_k', 'misc', 'b_cmp', 'b_slc')]
        kv_s, ki_s, cmp_s, slc_s = _gather_past(page_table, [cache_a_kv, cache_a_idx, cache_b_cmp, cache_b_slc],
                                                tails, (BF16, BF16, F32, BF16), l)
        seq_s = past + tdec
        bias_a = _dsa_select(p32, p32, ki_s, 0, IDX_DIM, n_seq=bs, t_seq=tdec, tq=tdec, qoff=past,
                             k_sel=min(A_TOPK, seq_s // 4), bias_dtype=F32)
        o_a = _flash(p32, _OFF['a_q'] // MIX_W, kv_s, 0, bias_a, n_seq=bs, t_seq=tdec, tq=tdec, tkm=tkm_s, qoff=past,
                     shared=False, out_dtype=F32)
        ncp = -(-(-(-seq_s // CMP_STRIDE) - 1) // 128) * 128
        kcvc = _compress(cmp_s, w_cmp_pos[l], seq_s, ncp)
        win_full = jnp.concatenate([state_b_win[l], new['b_win']], axis=1)
        win_pad = jnp.concatenate([win_full, jnp.zeros((bs, -(w_buf + tdec) % 16, 128), F32)], axis=1)
        ocw, bias_b = _nsa_select(p32, p32, kcvc, win_pad, 0, n_seq=bs, t_seq=tdec, tq=tdec, qoff=past, seq_len=seq_s,
                                  lp=lps, win_dyn=False, win_pos0=past - w_buf, bias_dtype=F32, out_dtype=F32)
        o_b = _flash(p32, _OFF['b_q'] // MIX_W, slc_s, 0, bias_b, n_seq=bs, t_seq=tdec, tq=tdec, tkm=tkm_s, qoff=past,
                     shared=True, out_dtype=F32, gate_args=(p32, ocw))
        o_d, st = _gla(p32, lw['wal'], b_alpha[l], lw['gg'], _state_to_bd(state_d_gla[l]), n_seq=bs, t_seq=tdec,
                       tt=tdec, out_dtype=F32)
        u_tail = (_cols(p32, 'c_c', MIX_W) * _cols(p32, 'c_in', MIX_W)).reshape(bs, tdec, MIX_W)[:, tdec - 2:]
        xs = _merge(o_a, o_b, o_d, p32, xs, ms[2], conv_c[l], lw['wb'], lw['wo'], _conv_fix(state_c_conv[l], tdec),
                    tm=rs, t_seq=tdec)
        xs, y_s, a_full = _ffn(xs, gn2, ms[4], ms[3], ms[5], lw['wa'], lw['wg'], conv_ffn[l], bffn, lw['wout'], gfin,
                               _conv_fix(state_ffn_conv[l], tdec), tm=rs, t_seq=tdec)
        outs_s.append((new['a_k'], new['misc'], new['b_cmp'], new['b_slc'], win_full[:, tdec:], u_tail,
                       _bd_to_state(st), a_full.reshape(bs, tdec, D_FF)[:, tdec - 2:]))

        p32, p16 = _proj(xp, gn1, mp[1], mp[0], lw['w_in'], 1024, seq)
        p16_3 = p16.reshape(bp, seq, NP)
        a_kv = _cols(p32, 'a_k', 2 * MIX_W).reshape(bp, seq, 2 * MIX_W)
        a_idx = _cols(p32, 'misc', IDX_DIM).reshape(bp, seq, IDX_DIM)
        b_cmp = _cols(p32, 'b_cmp', 128).reshape(bp, seq, 128)
        b_slc = _cols(p32, 'b_slc', 128).reshape(bp, seq, 128)
        b_win = _cols(p32, 'b_win', 128).reshape(bp, seq, 128)[:, seq - min(WINDOW, seq):]
        k_sel = min(A_TOPK, seq // 4)
        bias_a = _dsa_select(p16, p32, p16_3, _OFF['misc'] // 128, 128, n_seq=bp, t_seq=seq, tq=128, qoff=0,
                             k_sel=k_sel, bias_dtype=BF16)
        o_a = jnp.transpose(_flash_t(jnp.transpose(_cols(p16, 'a_q', MIX_W)), p16, _OFF['a_k'] // MIX_W, MIX_W,
                                     jnp.transpose(_cols(p16, 'a_v', MIX_W)), bias_a, t_seq=seq, tq=512, tkm=2,
                                     shared=False))
        ncp = -(-(-(-seq // CMP_STRIDE) - 1) // 128) * 128
        kcvc = _compress(b_cmp, w_cmp_pos[l], seq, ncp)
        ocw, bias_b = _nsa_select(p16, p32, kcvc, p16_3, _OFF['b_win'] // 128, n_seq=bp, t_seq=seq, tq=128, qoff=0,
                                  seq_len=seq, lp=seq, win_dyn=True, win_pos0=0, bias_dtype=BF16, out_dtype=BF16)
        vt_b = jnp.transpose(p16[:, _OFF['b_slc'] + HEAD_DIM:_OFF['b_slc'] + 2 * HEAD_DIM])
        o_b = jnp.transpose(_flash_t(jnp.transpose(_cols(p16, 'b_q', MIX_W)), p16, _OFF['b_slc'] // 128, 128, vt_b,
                                     bias_b, t_seq=seq, tq=512, tkm=2, shared=True,
                                     gate_args=(jnp.transpose(_cols(p32, 'misc', 128)), jnp.transpose(ocw))))
        o_d, st = _gla(p32, lw['wal'], b_alpha[l], lw['gg'], jnp.zeros((bp, MIX_W, MIX_W), F32), n_seq=bp, t_seq=seq,
                       tt=512, out_dtype=BF16)
        u_tail = (_cols(p32, 'c_c', MIX_W) * _cols(p32, 'c_in', MIX_W)).reshape(bp, seq, MIX_W)[:, seq - 2:]
        xp = _merge(o_a, o_b, o_d, p32, xp, mp[2], conv_c[l], lw['wb'], lw['wo'], None, tm=256, t_seq=seq)
        xp, y_p, a_full = _ffn(xp, gn2, mp[4], mp[3], mp[5], lw['wa'], lw['wg'], conv_ffn[l], bffn, lw['wout'], gfin,
                               None, tm=512, t_seq=seq)
        outs_p.append((a_kv, a_idx, b_cmp, b_slc, b_win, u_tail, _bd_to_state(st),
                       a_full.reshape(bp, seq, D_FF)[:, seq - 2:]))

    sp = [jnp.stack(z) for z in zip(*outs_p)]
    ss = [jnp.stack(z) for z in zip(*outs_s)]
    res = [y_p.reshape(bp, seq, d), y_s.reshape(bs, tdec, d)]
    for a, b in zip(sp, ss):
        res += [a, b]
    return tuple(res)
```

```python
import functools

import numpy as np
import jax
import jax.numpy as jnp
from jax import lax
from jax.experimental import pallas as pl
from jax.experimental.pallas import tpu as pltpu

F32 = jnp.float32
BF16 = jnp.bfloat16
HI = lax.Precision.HIGHEST

D_MODEL = 1024
PAGE = 128
HEAD_DIM = 64
MIX_W = 256
N_HEADS = 4
IDX_DIM = 64
A_TOPK = 256
CMP_LEN = 32
CMP_STRIDE = 16
SLC_BLOCK = 64
N_SLC = 16
WINDOW = 512
CONV_W = 3
GATE_RANK = 16
GATE_TAU = 16.0
GLA_CHUNK = 64
D_FF = 2816
EPS = 1e-6
NEG = -1e30
INT_MIN = -2 ** 31
KEY_TILE = 512

_IN_SPLITS = (('a_q', 256), ('a_k', 256), ('a_v', 256), ('a_qi', 256), ('a_ki', 64), ('a_wi', 4),
              ('b_q', 256), ('b_cmp', 128), ('b_slc', 128), ('b_win', 128), ('b_g', 12),
              ('c_in', 256), ('c_b', 256), ('c_c', 256),
              ('d_q', 256), ('d_k', 256), ('d_v', 256), ('d_r', 256), ('d_a', 16), ('gate', 4096))
_OFF = dict(gate=0, a_k=4096, a_v=4352, a_q=4608, a_qi=4864, b_q=5120, b_cmp=5376, b_slc=5504, b_win=5632,
            misc=5760, c_in=5888, c_b=6144, c_c=6400, d_q=6656, d_k=6912, d_v=7168, d_r=7424)
NP = 7680
MISC_KI, MISC_WI, MISC_G, MISC_DA = 0, 64, 68, 80
VMEM_LIMIT = 56 * 1024 * 1024


def _cp(sem):
    return pltpu.CompilerParams(dimension_semantics=sem, vmem_limit_bytes=VMEM_LIMIT)


def _dot(a, b):
    return jnp.dot(a, b, preferred_element_type=F32)


def _dot_nt(a, b):
    return lax.dot_general(a, b, (((1,), (1,)), ((), ())), preferred_element_type=F32)


def _dot_hi(a, b):
    return jnp.dot(a, b, preferred_element_type=F32, precision=HI)


def _const_spec(shape):
    nd = len(shape)
    return pl.BlockSpec(shape, lambda *a: (0,) * nd, pipeline_mode=pl.Buffered(1))


def _iota(shape, dim):
    return lax.broadcasted_iota(jnp.int32, shape, dim)


def _ada_kernel(c_ref, w_ref, b_ref, o_ref):
    o_ref[0] = _dot_hi(c_ref[...], w_ref[0]) + b_ref[0]


def _ada(c_all, w_ada, b_ada):
    depth, d, n6 = w_ada.shape
    rows = c_all.shape[0]
    tn = 1024
    return pl.pallas_call(
        _ada_kernel, grid=(depth, n6 // tn),
        in_specs=[pl.BlockSpec((rows, d), lambda l, j: (0, 0)),
                  pl.BlockSpec((1, d, tn), lambda l, j: (l, 0, j)),
                  pl.BlockSpec((1, 1, tn), lambda l, j: (l, 0, j))],
        out_specs=pl.BlockSpec((1, rows, tn), lambda l, j: (l, 0, j)),
        out_shape=jax.ShapeDtypeStruct((depth, rows, n6), F32),
        compiler_params=_cp(("arbitrary", "arbitrary")), name="ada",
    )(c_all, w_ada, b_ada.reshape(depth, 1, n6))


def _norm_mod(x, g, sc, sh):
    y = x * lax.rsqrt(jnp.mean(x * x, axis=-1, keepdims=True) + EPS) * g
    return y * (1.0 + sc) + sh


def _proj_kernel(x_ref, g_ref, sc_ref, sh_ref, w_ref, o32_ref, o16_ref, h_ref):
    @pl.when(pl.program_id(1) == 0)
    def _():
        h_ref[...] = _norm_mod(x_ref[...], g_ref[...], sc_ref[0], sh_ref[0]).astype(BF16)

    acc = _dot(h_ref[...], w_ref[...])
    o32_ref[...] = acc
    o16_ref[...] = acc.astype(BF16)


def _mod_spec(mod, tm, rows_per_group):
    mb = mod.shape[1]
    tiles = max(rows_per_group // tm, 1)
    return pl.BlockSpec((1, mb, D_MODEL), lambda i, *_: (i // tiles, 0, 0))


def _proj(x, g, sc, sh, w, tm, rows_per_group):
    rows = x.shape[0]
    tn = 768
    return pl.pallas_call(
        _proj_kernel, grid=(rows // tm, NP // tn),
        in_specs=[pl.BlockSpec((tm, D_MODEL), lambda i, j: (i, 0)),
                  pl.BlockSpec((1, D_MODEL), lambda i, j: (0, 0)),
                  _mod_spec(sc, tm, rows_per_group), _mod_spec(sh, tm, rows_per_group),
                  pl.BlockSpec((D_MODEL, tn), lambda i, j: (0, j))],
        out_specs=[pl.BlockSpec((tm, tn), lambda i, j: (i, j)), pl.BlockSpec((tm, tn), lambda i, j: (i, j))],
        out_shape=[jax.ShapeDtypeStruct((rows, NP), F32), jax.ShapeDtypeStruct((rows, NP), BF16)],
        scratch_shapes=[pltpu.VMEM((tm, D_MODEL), BF16)],
        compiler_params=_cp(("arbitrary", "arbitrary")), name="proj",
    )(x, g, sc, sh, w)


PAGES_PER_STEP = KEY_TILE // PAGE


def _gather_kernel(pt_ref, *refs, n_arr, n_chunks):
    ppc = PAGES_PER_STEP
    pages, tails, outs = refs[:n_arr * ppc], refs[n_arr * ppc:n_arr * (ppc + 1)], refs[n_arr * (ppc + 1):]
    c = pl.program_id(1)

    @pl.when(c < n_chunks - 1)
    def _():
        for k in range(n_arr):
            for r in range(ppc):
                outs[k][0, r * PAGE:(r + 1) * PAGE, :] = pages[k * ppc + r][0, 0].astype(outs[k].dtype)

    @pl.when(c == n_chunks - 1)
    def _():
        for k in range(n_arr):
            outs[k][0] = tails[k][0].astype(outs[k].dtype)


def _gather_past(page_table, pools, tails, out_dtypes, layer):
    n_seq, n_pages = page_table.shape
    n_arr, ppc = len(pools), PAGES_PER_STEP
    assert n_pages % ppc == 0 and tails[0].shape[1] == ppc * PAGE
    n_chunks = n_pages // ppc + 1
    lp = n_chunks * ppc * PAGE
    in_specs = []
    for p in pools:
        for r in range(ppc):
            in_specs.append(pl.BlockSpec(
                (1, 1, PAGE, p.shape[-1]),
                lambda b, c, pt, r=r: (layer, pt[b, jnp.minimum(c * ppc + r, n_pages - 1)], 0, 0)))
    in_specs += [pl.BlockSpec((1, ppc * PAGE, t.shape[-1]), lambda b, c, pt: (b, 0, 0)) for t in tails]
    args = [p for p in pools for _ in range(ppc)] + list(tails)
    kern = functools.partial(_gather_kernel, n_arr=n_arr, n_chunks=n_chunks)
    return pl.pallas_call(
        kern,
        grid_spec=pltpu.PrefetchScalarGridSpec(
            num_scalar_prefetch=1, grid=(n_seq, n_chunks), in_specs=in_specs,
            out_specs=[pl.BlockSpec((1, ppc * PAGE, p.shape[-1]), lambda b, c, pt: (b, c, 0)) for p in pools]),
        out_shape=[jax.ShapeDtypeStruct((n_seq, lp, p.shape[-1]), dt) for p, dt in zip(pools, out_dtypes)],
        compiler_params=_cp(("arbitrary", "arbitrary")), name="gather_past",
    )(page_table, *args)


def _stack_heads(q, width):
    t = q.shape[0]
    parts = []
    for h in range(N_HEADS):
        p = q[:, h * HEAD_DIM:(h + 1) * HEAD_DIM]
        if width > HEAD_DIM:
            p = jnp.concatenate([p, jnp.zeros((t, width - HEAD_DIM), q.dtype)], axis=1)
        parts.append(p)
    return jnp.concatenate(parts, axis=0)


def _dsa_sel_kernel(qi_ref, misc_ref, ki_ref, bias_ref, s_ref, *planes, tq, nk, k_sel, qoff, fk):
    tk = KEY_TILE
    i = pl.program_id(1)
    q0 = qoff + i * tq
    qpos = q0 + _iota((tq, 1), 0)
    nlim = jnp.minimum((q0 + tq - 1) // tk + 1, nk)
    qs = _stack_heads(qi_ref[...].astype(BF16), fk)
    wi = misc_ref[:, MISC_WI:MISC_WI + N_HEADS]

    def score_tile(j, c):
        kt = ki_ref[0, pl.ds(pl.multiple_of(j * tk, tk), tk), :].astype(BF16)
        d = _dot_nt(qs, kt).reshape(N_HEADS, tq, tk)
        sc = wi[:, 0:1] * jnp.maximum(d[0], 0.0)
        for h in range(1, N_HEADS):
            sc = sc + wi[:, h:h + 1] * jnp.maximum(d[h], 0.0)
        bits = lax.bitcast_convert_type(sc, jnp.int32)
        key = bits ^ ((bits >> 31) & 0x7FFFFFFF)
        key = jnp.where(key == -1, 0, key)
        kpos = j * tk + _iota((1, tk), 1)
        s_ref[j] = jnp.where(kpos <= qpos, key, INT_MIN)
        return c

    if nk % 2 == 0:
        def score_pair(j2, c):
            score_tile(2 * j2, c)
            return score_tile(2 * j2 + 1, c)

        lax.fori_loop(0, (nlim + 1) // 2, score_pair, 0)
    else:
        lax.fori_loop(0, nlim, score_tile, 0)

    if planes:
        p_ref, cand_ref = planes

        def unused_tile(j, c):
            s_ref[j] = jnp.full((tq, tk), INT_MIN, jnp.int32)
            return c

        lax.fori_loop(nlim, nk, unused_tile, 0)

        def transpose_bits(rg, c):
            rows = pl.ds(pl.multiple_of(rg * 8, 8), 8)
            for lc in range(tk // 128):
                lanes = slice(lc * 128, (lc + 1) * 128)
                x = [s_ref[j, rows, lanes] ^ INT_MIN if j < nk else jnp.zeros((8, 128), jnp.int32) for j in range(32)]
                s, m = 16, 0x0000FFFF
                while s >= 1:
                    for a in range(32):
                        if (a & s) == 0:
                            t = (lax.shift_right_logical(x[a], s) ^ x[a + s]) & m
                            x[a + s] = x[a + s] ^ t
                            x[a] = x[a] ^ lax.shift_left(t, s)
                    s //= 2
                    if s:
                        m = (m ^ (m << s)) & 0xFFFFFFFF
                        m = m - (1 << 32) if m >= (1 << 31) else m
                for b in range(32):
                    p_ref[b, rows, lanes] = x[b]
            return c

        lax.fori_loop(0, tq // 8, transpose_bits, 0)
        cand_ref[...] = jnp.full((tq, tk), -1, jnp.int32)

        def lane_total(words):
            pc = lax.population_count(words)
            part = pc[:, 0:128]
            for c in range(1, tk // 128):
                part = part + pc[:, c * 128:(c + 1) * 128]
            return jnp.sum(part.astype(F32), axis=-1, keepdims=True)

        def bit_step(t, carry):
            need, thr_u = carry
            b = 31 - t
            ones = cand_ref[...] & p_ref[b]
            cnt = lane_total(ones)
            take = cnt >= need
            cand_ref[...] = jnp.where(take, ones, cand_ref[...] & ~p_ref[b])
            return jnp.where(take, need, need - cnt), jnp.where(take, thr_u | lax.shift_left(jnp.int32(1), b), thr_u)

        ties_allowed, thr_u = lax.fori_loop(
            0, 32, bit_step, (jnp.full((tq, 1), float(k_sel), F32), jnp.zeros((tq, 1), jnp.int32)))
        thr = thr_u ^ INT_MIN
        n_ties = lane_total(cand_ref[...])
    else:
        def count(pred, level):
            level_b = jnp.broadcast_to(level, (tq, 128))

            def body(j, acc):
                for c in range(tk // 128):
                    acc = acc + jnp.where(pred(s_ref[j, :, c * 128:(c + 1) * 128], level_b), 1.0, 0.0)
                return acc
            return jnp.sum(lax.fori_loop(0, nlim, body, jnp.zeros((tq, 128), F32)), axis=-1, keepdims=True)

        def bit_step(t, c):
            trial = c + lax.shift_left(jnp.int32(1), 31 - t)
            return jnp.where(count(lambda s, lv: s >= lv, trial) >= k_sel, trial, c)

        thr = lax.fori_loop(0, 32, bit_step, jnp.full((tq, 1), INT_MIN, jnp.int32))
        ties_allowed = k_sel - count(lambda s, lv: s > lv, thr)
        n_ties = count(lambda s, lv: s == lv, thr)
    need_rank = jnp.max(jnp.where(n_ties > ties_allowed, 1.0, 0.0)) > 0.5

    @pl.when(need_rank)
    def _():
        tri = jnp.where(_iota((tk, tk), 0) <= _iota((tk, tk), 1), 1.0, 0.0).astype(BF16)

        def out_tile(j, carry):
            key = s_ref[j]
            eq = jnp.where(key == thr, 1.0, 0.0)
            rank = _dot(eq.astype(BF16), tri) + carry
            kpos = j * tk + _iota((1, tk), 1)
            take = jnp.where(key > thr, 1.0, jnp.where(rank <= ties_allowed, eq, 0.0))
            take = jnp.where(kpos <= qpos, take, 0.0)
            bias_ref[j] = jnp.where(take > 0.5, 0.0, NEG).astype(bias_ref.dtype)
            return rank[:, tk - 1:tk]

        lax.fori_loop(0, nlim, out_tile, jnp.zeros((tq, 1), F32))

    @pl.when(jnp.logical_not(need_rank))
    def _():
        def out_tile(j, c):
            kpos = j * tk + _iota((1, tk), 1)
            take = jnp.where(kpos <= qpos, jnp.where(s_ref[j] >= thr, 1.0, 0.0), 0.0)
            bias_ref[j] = jnp.where(take > 0.5, 0.0, NEG).astype(bias_ref.dtype)
            return c

        lax.fori_loop(0, nlim, out_tile, 0)

    def fill(j, c):
        bias_ref[j] = jnp.full((tq, tk), NEG, bias_ref.dtype)
        return c

    lax.fori_loop(nlim, nk, fill, 0)


def _dsa_select(q_arr, p32, ki3, ki_col, fk, *, n_seq, t_seq, tq, qoff, k_sel, bias_dtype):
    lp = ki3.shape[1]
    nk, nq = lp // KEY_TILE, t_seq // tq
    rows = n_seq * t_seq
    kern = functools.partial(_dsa_sel_kernel, tq=tq, nk=nk, k_sel=k_sel, qoff=qoff, fk=fk)
    return pl.pallas_call(
        kern, grid=(n_seq, nq),
        in_specs=[pl.BlockSpec((tq, MIX_W), lambda b, i: (b * nq + i, _OFF['a_qi'] // MIX_W)),
                  pl.BlockSpec((tq, 128), lambda b, i: (b * nq + i, _OFF['misc'] // 128)),
                  pl.BlockSpec((1, lp, fk), lambda b, i: (b, 0, ki_col))],
        out_specs=pl.BlockSpec((nk, tq, KEY_TILE), lambda b, i: (0, b * nq + i, 0)),
        out_shape=jax.ShapeDtypeStruct((nk, rows, KEY_TILE), bias_dtype),
        scratch_shapes=[pltpu.VMEM((nk, tq, KEY_TILE), jnp.int32)]
        + ([pltpu.VMEM((32, tq, KEY_TILE), jnp.int32), pltpu.VMEM((tq, KEY_TILE), jnp.int32)] if nk <= 32 else []),
        compiler_params=_cp(("arbitrary", "arbitrary")), name="dsa_select",
    )(q_arr, p32, ki3)


def _flash_row_chunk(tq):
    return min(32, tq)


def _flash_kernel(sb, sq, sk, sfl, q_ref, kv_ref, bias_ref, *rest, tq, tkm, shared, gated):
    if gated:
        misc_ref, add_ref, o_ref, m_ref, l_ref, acc_ref, s_ref, p_ref, qs_ref, a_ref = rest
    else:
        o_ref, m_ref, l_ref, acc_ref, s_ref, p_ref, qs_ref, a_ref = rest
    tk = KEY_TILE
    flags = sfl[pl.program_id(0)]
    dv = HEAD_DIM if shared else MIX_W
    lane_head = _iota((1, MIX_W), 1) // HEAD_DIM
    rc = _flash_row_chunk(tq)

    @pl.when((flags & 1) != 0)
    def _():
        m_ref[...] = jnp.full(m_ref.shape, NEG, F32)
        l_ref[...] = jnp.zeros(l_ref.shape, F32)
        acc_ref[...] = jnp.zeros(acc_ref.shape, F32)
        q = (q_ref[...] * (HEAD_DIM ** -0.5)).astype(BF16)
        if shared:
            qs_ref[...] = _stack_heads(q, HEAD_DIM)
        else:
            qs_ref[...] = jnp.concatenate([jnp.where(lane_head == h, q, jnp.zeros_like(q)) for h in range(N_HEADS)],
                                          axis=0)

    for u in range(tkm):
        kvt = kv_ref[0, u * tk:(u + 1) * tk, :]
        k = kvt[:, :dv].astype(BF16)
        v = kvt[:, dv:2 * dv].astype(BF16)
        half = N_HEADS * tq // 2
        s_ref[0:half, :] = _dot_nt(qs_ref[0:half, :], k)
        s_ref[half:, :] = _dot_nt(qs_ref[half:, :], k)
        n_chunks = N_HEADS * tq // rc

        def rows_of(c):
            return (pl.ds(pl.multiple_of(c * rc, rc), rc), pl.ds(pl.multiple_of((c % (tq // rc)) * rc, rc), rc))

        def pass_max(c, carry):
            rows, brow = rows_of(c)
            s = s_ref[rows, :] + bias_ref[u, brow, :].astype(F32)
            s_ref[rows, :] = s
            m_old = m_ref[rows, :]
            m_new = jnp.maximum(m_old, jnp.max(s, axis=-1, keepdims=True))
            a_ref[rows, :] = jnp.exp(m_old - m_new)
            m_ref[rows, :] = m_new
            return carry

        lax.fori_loop(0, n_chunks, pass_max, 0, unroll=min(8, n_chunks))

        def pass_exp(c, carry):
            rows, _ = rows_of(c)
            m = m_ref[rows, :]
            part = a_ref[rows, :] * l_ref[rows, :]
            for cc in range(tk // 128):
                p = jnp.exp(s_ref[rows, cc * 128:(cc + 1) * 128] - m)
                part = part + p
                p_ref[rows, cc * 128:(cc + 1) * 128] = p.astype(p_ref.dtype)
            l_ref[rows, :] = part
            return carry

        lax.fori_loop(0, n_chunks, pass_exp, 0, unroll=min(8, n_chunks))
        alpha = a_ref[...]
        alpha = alpha[:, :dv] if dv <= 128 else jnp.concatenate([alpha] * (dv // 128), axis=1)
        acc_ref[0:half, :] = alpha[0:half] * acc_ref[0:half, :] + _dot(p_ref[0:half, :].astype(BF16), v)
        acc_ref[half:, :] = alpha[half:] * acc_ref[half:, :] + _dot(p_ref[half:, :].astype(BF16), v)

    @pl.when((flags & 2) != 0)
    def _():
        l_sum = jnp.sum(l_ref[...], axis=-1, keepdims=True)
        o = (acc_ref[...] / l_sum).reshape(N_HEADS, tq, dv)
        if shared:
            out = jnp.concatenate([o[h] for h in range(N_HEADS)], axis=1)
        else:
            out = jnp.where(lane_head == 0, o[0], 0.0)
            for h in range(1, N_HEADS):
                out = out + jnp.where(lane_head == h, o[h], 0.0)
        if gated:
            g = jax.nn.sigmoid(misc_ref[...])
            gate = jnp.zeros((tq, MIX_W), F32)
            for h in range(N_HEADS):
                c = MISC_G + 3 * h + 1
                gate = gate + jnp.where(lane_head == h, g[:, c:c + 1], 0.0)
            out = add_ref[...].astype(F32) + gate * out
        o_ref[...] = out.astype(o_ref.dtype)


def _flash_steps(n_seq, t_seq, tq, tk, lp, qoff):
    sb, sq, sk, sfl = [], [], [], []
    for b in range(n_seq):
        for i in range(t_seq // tq):
            nlim = min((qoff + (i + 1) * tq - 1) // tk + 1, lp // tk)
            for j in range(nlim):
                sb.append(b), sq.append(i), sk.append(j)
                sfl.append((1 if j == 0 else 0) | (2 if j == nlim - 1 else 0))
    return [jnp.asarray(np.asarray(a, np.int32)) for a in (sb, sq, sk, sfl)]


def _flash(q_arr, q_col, kv3, kv_col, bias, *, n_seq, t_seq, tq, tkm, qoff, shared, out_dtype, gate_args=None):
    lp = kv3.shape[1]
    tk = tkm * KEY_TILE
    nq = t_seq // tq
    rows = n_seq * t_seq
    wblk = 2 * (HEAD_DIM if shared else MIX_W)
    dv = HEAD_DIM if shared else MIX_W
    steps = _flash_steps(n_seq, t_seq, tq, tk, lp, qoff)
    gated = gate_args is not None
    row = lambda s, sb, sq, sk, sfl: sb[s] * nq + sq[s]
    in_specs = [pl.BlockSpec((tq, MIX_W), lambda s, sb, sq, sk, sfl: (row(s, sb, sq, sk, sfl), q_col)),
                pl.BlockSpec((1, tk, wblk), lambda s, sb, sq, sk, sfl: (sb[s], sk[s], kv_col)),
                pl.BlockSpec((tkm, tq, KEY_TILE), lambda s, sb, sq, sk, sfl: (sk[s], row(s, sb, sq, sk, sfl), 0))]
    args = [q_arr, kv3, bias]
    if gated:
        in_specs += [pl.BlockSpec((tq, 128), lambda s, sb, sq, sk, sfl: (row(s, sb, sq, sk, sfl), _OFF['misc'] // 128)),
                     pl.BlockSpec((tq, MIX_W), lambda s, sb, sq, sk, sfl: (row(s, sb, sq, sk, sfl), 0))]
        args += list(gate_args)
    kern = functools.partial(_flash_kernel, tq=tq, tkm=tkm, shared=shared, gated=gated)
    return pl.pallas_call(
        kern,
        grid_spec=pltpu.PrefetchScalarGridSpec(
            num_scalar_prefetch=4, grid=(int(steps[0].shape[0]),), in_specs=in_specs,
            out_specs=pl.BlockSpec((tq, MIX_W), lambda s, sb, sq, sk, sfl: (row(s, sb, sq, sk, sfl), 0)),
            scratch_shapes=[pltpu.VMEM((N_HEADS * tq, 128), F32), pltpu.VMEM((N_HEADS * tq, 128), F32),
                            pltpu.VMEM((N_HEADS * tq, dv), F32), pltpu.VMEM((N_HEADS * tq, KEY_TILE), F32),
                            pltpu.VMEM((N_HEADS * tq, KEY_TILE), BF16 if _flash_row_chunk(tq) % 16 == 0 else F32),
                            pltpu.VMEM((N_HEADS * tq, dv), BF16), pltpu.VMEM((N_HEADS * tq, 128), F32)]),
        out_shape=jax.ShapeDtypeStruct((rows, MIX_W), out_dtype),
        compiler_params=_cp(("arbitrary",)), name="flash_shared" if shared else "flash_heads",
    )(*steps, *args)


def _flash_t_kernel(sq, sk, sfl, qt_ref, k_ref, vt_ref, bias_ref, *rest, tq, tkm, shared, gated):
    if gated:
        misct_ref, addt_ref, o_ref, m_ref, l_ref, acc_ref = rest
    else:
        o_ref, m_ref, l_ref, acc_ref = rest
    tk = KEY_TILE
    flags = sfl[pl.program_id(0)]
    row_head = _iota((MIX_W, 1), 0) // HEAD_DIM

    @pl.when((flags & 1) != 0)
    def _():
        m_ref[...] = jnp.full(m_ref.shape, NEG, F32)
        l_ref[...] = jnp.zeros(l_ref.shape, F32)
        acc_ref[...] = jnp.zeros(acc_ref.shape, F32)

    qt = qt_ref[...]
    for u in range(tkm):
        k = k_ref[u * tk:(u + 1) * tk, :]
        k = k[:, :HEAD_DIM] if shared else k
        bias_t = bias_ref[u].astype(F32).T
        for h in range(N_HEADS):
            hs = slice(h * HEAD_DIM, (h + 1) * HEAD_DIM)
            q_h = qt[hs, :] if shared else jnp.where(row_head == h, qt, jnp.zeros_like(qt))
            s = _dot(k, q_h) + bias_t
            m_old = m_ref[h]
            m_new = jnp.maximum(m_old, jnp.max(s, axis=0, keepdims=True))
            alpha = jnp.exp2(m_old - m_new)
            p = jnp.exp2(s - m_new)
            l_ref[h] = alpha * l_ref[h] + jnp.sum(p, axis=0, keepdims=True)
            vt_h = vt_ref[0:HEAD_DIM, u * tk:(u + 1) * tk] if shared else vt_ref[hs, u * tk:(u + 1) * tk]
            acc_ref[h] = alpha * acc_ref[h] + _dot(vt_h, p.astype(BF16))
            m_ref[h] = m_new

    @pl.when((flags & 2) != 0)
    def _():
        for h in range(N_HEADS):
            hs = slice(h * HEAD_DIM, (h + 1) * HEAD_DIM)
            out = acc_ref[h] / l_ref[h]
            if gated:
                c = MISC_G + 3 * h + 1
                out = addt_ref[hs, :].astype(F32) + jax.nn.sigmoid(misct_ref[c:c + 1, :]) * out
            o_ref[hs, :] = out.astype(o_ref.dtype)


def _flash_t(qt, k_arr, k_col, k_width, vt, bias, *, t_seq, tq, tkm, shared, gate_args=None):
    tk = tkm * KEY_TILE
    steps = _flash_steps(1, t_seq, tq, tk, t_seq, 0)[1:]
    gated = gate_args is not None
    vrows = vt.shape[0]
    in_specs = [pl.BlockSpec((MIX_W, tq), lambda s, sq, sk, sfl: (0, sq[s])),
                pl.BlockSpec((tk, k_width), lambda s, sq, sk, sfl: (sk[s], k_col)),
                pl.BlockSpec((vrows, tk), lambda s, sq, sk, sfl: (0, sk[s])),
                pl.BlockSpec((tkm, tq, KEY_TILE), lambda s, sq, sk, sfl: (sk[s], sq[s], 0))]
    args = [qt, k_arr, vt, bias]
    if gated:
        in_specs += [pl.BlockSpec((128, tq), lambda s, sq, sk, sfl: (0, sq[s])),
                     pl.BlockSpec((MIX_W, tq), lambda s, sq, sk, sfl: (0, sq[s]))]
        args += list(gate_args)
    kern = functools.partial(_flash_t_kernel, tq=tq, tkm=tkm, shared=shared, gated=gated)
    return pl.pallas_call(
        kern,
        grid_spec=pltpu.PrefetchScalarGridSpec(
            num_scalar_prefetch=3, grid=(int(steps[0].shape[0]),), in_specs=in_specs,
            out_specs=pl.BlockSpec((MIX_W, tq), lambda s, sq, sk, sfl: (0, sq[s])),
            scratch_shapes=[pltpu.VMEM((N_HEADS, 1, tq), F32), pltpu.VMEM((N_HEADS, 1, tq), F32),
                            pltpu.VMEM((N_HEADS, HEAD_DIM, tq), F32)]),
        out_shape=jax.ShapeDtypeStruct((MIX_W, t_seq), BF16),
        compiler_params=_cp(("arbitrary",)), name="flash_t_shared" if shared else "flash_t_heads",
    )(*steps, *args)


def _compress_kernel(x_ref, w_ref, o_ref, *, n_blocks, ncp):
    w = w_ref[...]
    e = jnp.exp(w - jnp.max(w, axis=-1, keepdims=True))
    w = e / jnp.sum(e, axis=-1, keepdims=True)
    n16 = x_ref.shape[1] // CMP_STRIDE
    width = x_ref.shape[2]
    first = jnp.zeros((n16, width), F32)
    second = jnp.zeros((n16, width), F32)
    for j in range(CMP_STRIDE):
        xj = x_ref[0, pl.ds(j, n16, stride=CMP_STRIDE), :]
        first = first + xj * w[:, j:j + 1]
        second = second + xj * w[:, CMP_STRIDE + j:CMP_STRIDE + j + 1]
    shifted = jnp.concatenate([second[1:], jnp.zeros((1, width), F32)], axis=0)
    out = first + shifted
    if n16 < ncp:
        out = jnp.concatenate([out, jnp.zeros((ncp - n16, width), F32)], axis=0)
    out = out[:ncp]
    o_ref[0] = jnp.where(_iota((ncp, 1), 0) < n_blocks, out, 0.0)


def _compress(rows3, w_pos, length, ncp):
    n_seq, lp, width = rows3.shape
    assert lp % CMP_STRIDE == 0
    n_blocks = -(-length // CMP_STRIDE) - 1
    kern = functools.partial(_compress_kernel, n_blocks=n_blocks, ncp=ncp)
    return pl.pallas_call(
        kern, grid=(n_seq,),
        in_specs=[pl.BlockSpec((1, lp, width), lambda b: (b, 0, 0)),
                  pl.BlockSpec((1, CMP_LEN), lambda b: (0, 0))],
        out_specs=pl.BlockSpec((1, ncp, width), lambda b: (b, 0, 0)),
        out_shape=jax.ShapeDtypeStruct((n_seq, ncp, width), F32),
        compiler_params=_cp(("arbitrary",)), name="compress",
    )(rows3, w_pos.reshape(1, CMP_LEN))


def _masked_softmax(s, mask):
    s = jnp.where(mask, s, NEG)
    m = jnp.max(s, axis=-1, keepdims=True)
    e = jnp.where(mask, jnp.exp(s - m), 0.0)
    return e / jnp.maximum(jnp.sum(e, axis=-1, keepdims=True), 1e-30)


def _nsa_sel_kernel(q_ref, misc_ref, kc_ref, win_ref, ocw_ref, bias_ref, *, tq, nk, qoff, n_cmp, n_blk, nbp, n_top,
                    win_rows, win_dyn, win_pos0):
    tk = KEY_TILE
    i = pl.program_id(1)
    q0 = qoff + i * tq
    qpos = q0 + _iota((tq, 1), 0)
    nlim = jnp.minimum((q0 + tq - 1) // tk + 1, nk)
    qs = _stack_heads((q_ref[...] * (HEAD_DIM ** -0.5)).astype(BF16), HEAD_DIM)
    ncp = kc_ref.shape[1]

    kcv = kc_ref[0]
    kc = kcv[:, :HEAD_DIM].astype(BF16)
    vc = kcv[:, HEAD_DIM:].astype(BF16)
    cidx = _iota((1, ncp), 1)
    c_mask = ((cidx * CMP_STRIDE + CMP_LEN - 1) <= qpos) & (cidx < n_cmp)
    s_c = _dot_nt(qs, kc).reshape(N_HEADS, tq, ncp)
    p_c = _masked_softmax(s_c, c_mask[None])
    o_c = _dot(p_c.reshape(N_HEADS * tq, ncp).astype(BF16), vc).reshape(N_HEADS, tq, HEAD_DIM)

    c_start = _iota((ncp, 1), 0) * CMP_STRIDE
    s_start = _iota((1, nbp), 1) * SLC_BLOCK
    overlap = (c_start < s_start + SLC_BLOCK) & (c_start + CMP_LEN > s_start) & (_iota((ncp, 1), 0) < n_cmp)
    imp = _dot_hi(p_c[0] + p_c[1] + p_c[2] + p_c[3], jnp.where(overlap, 1.0, 0.0))
    blk = _iota((1, nbp), 1)
    cur = qpos // SLC_BLOCK
    forced = (blk == 0) | (blk == cur) | (blk == cur - 1)
    live = jnp.where(forced, jnp.inf, jnp.where((blk <= cur) & (blk < n_blk), imp, -jnp.inf))
    blk_f = blk.astype(F32)

    def pick(t, carry):
        live, sel = carry
        top = jnp.max(live, axis=-1, keepdims=True)
        first = jnp.min(jnp.where(live == top, blk_f, float(nbp)), axis=-1, keepdims=True)
        hit = blk_f == first
        return jnp.where(hit, -jnp.inf, live), jnp.where(hit, 1.0, sel)

    _, sel = lax.fori_loop(0, n_top, pick, (live, jnp.zeros((tq, nbp), F32)))
    sel = sel.astype(BF16)
    row_blk = _iota((nbp, 1), 0)

    def bias_tile(j, c):
        kpos = j * tk + _iota((1, tk), 1)
        expand = jnp.where(row_blk == kpos // SLC_BLOCK, 1.0, 0.0).astype(BF16)
        on = _dot(sel, expand)
        ok = (on > 0.5) & (kpos <= qpos)
        bias_ref[j] = jnp.where(ok, 0.0, NEG).astype(bias_ref.dtype)
        return c

    lax.fori_loop(0, nlim, bias_tile, 0)

    def fill(j, c):
        bias_ref[j] = jnp.full((tq, tk), NEG, bias_ref.dtype)
        return c

    lax.fori_loop(nlim, nk, fill, 0)

    if win_dyn:
        start = pl.multiple_of(jnp.maximum(q0 - WINDOW, 0), 8)
        wkv = win_ref[0, pl.ds(start, win_rows), :]
        kwpos = start + _iota((1, win_rows), 1)
    else:
        wkv = win_ref[0]
        kwpos = win_pos0 + _iota((1, win_rows), 1)
    kw = wkv[:, :HEAD_DIM].astype(BF16)
    vw = wkv[:, HEAD_DIM:].astype(BF16)
    rel = qpos - kwpos
    w_mask = (rel >= 0) & (rel < WINDOW) & (kwpos >= 0)
    s_w = _dot_nt(qs, kw).reshape(N_HEADS, tq, win_rows)
    p_w = _masked_softmax(s_w, w_mask[None])
    o_w = _dot(p_w.reshape(N_HEADS * tq, win_rows).astype(BF16), vw).reshape(N_HEADS, tq, HEAD_DIM)

    g = jax.nn.sigmoid(misc_ref[...])
    parts = []
    for h in range(N_HEADS):
        c = MISC_G + 3 * h
        parts.append(g[:, c:c + 1] * o_c[h] + g[:, c + 2:c + 3] * o_w[h])
    ocw_ref[...] = jnp.concatenate(parts, axis=1).astype(ocw_ref.dtype)


def _nsa_select(q_arr, p32, kcvc, win3, win_col, *, n_seq, t_seq, tq, qoff, seq_len, lp, win_dyn, win_pos0, bias_dtype,
                out_dtype):
    nk, nq = lp // KEY_TILE, t_seq // tq
    rows = n_seq * t_seq
    n_cmp = -(-seq_len // CMP_STRIDE) - 1
    n_blk = -(-seq_len // SLC_BLOCK)
    nbp = -(-n_blk // 128) * 128
    win_rows = (tq + WINDOW) if win_dyn else win3.shape[1]
    kern = functools.partial(_nsa_sel_kernel, tq=tq, nk=nk, qoff=qoff, n_cmp=n_cmp, n_blk=n_blk, nbp=nbp,
                             n_top=min(N_SLC, n_blk), win_rows=win_rows, win_dyn=win_dyn, win_pos0=win_pos0)
    return pl.pallas_call(
        kern, grid=(n_seq, nq),
        in_specs=[pl.BlockSpec((tq, MIX_W), lambda b, i: (b * nq + i, _OFF['b_q'] // MIX_W)),
                  pl.BlockSpec((tq, 128), lambda b, i: (b * nq + i, _OFF['misc'] // 128)),
                  pl.BlockSpec((1,) + kcvc.shape[1:], lambda b, i: (b, 0, 0)),
                  pl.BlockSpec((1, win3.shape[1], 128), lambda b, i: (b, 0, win_col))],
        out_specs=[pl.BlockSpec((tq, MIX_W), lambda b, i: (b * nq + i, 0)),
                   pl.BlockSpec((nk, tq, KEY_TILE), lambda b, i: (0, b * nq + i, 0))],
        out_shape=[jax.ShapeDtypeStruct((rows, MIX_W), out_dtype),
                   jax.ShapeDtypeStruct((nk, rows, KEY_TILE), bias_dtype)],
        compiler_params=_cp(("arbitrary", "arbitrary")), name="nsa_select",
    )(q_arr, p32, kcvc, win3)


def _gla_tables(c):
    levels = []
    b = c
    while b >= 2:
        levels.append(b)
        b //= 2
    r = np.arange(c)
    mats = [(r[None, :] <= r[:, None]), (r[None, :] > r[:, None])]
    qm, km = [], []
    for b in levels:
        mid = (r // b) * b + b // 2
        upper = r >= mid
        qm.append(upper[:, None] & (r[None, :] >= mid[:, None]) & (r[None, :] <= r[:, None]))
        km.append((~upper)[:, None] & (r[None, :] > r[:, None]) & (r[None, :] < mid[:, None]))
    return np.concatenate(mats + qm + km, axis=0).astype(np.float32), levels


def _gla_kernel(q_ref, k_ref, v_ref, r_ref, misc_ref, wal_ref, bal_ref, gg_ref, m_ref, s0_ref, o_ref, st_ref, s_scr,
                *, tt, c, levels, n_tiles):
    t_idx = pl.program_id(1)
    nl = len(levels)

    @pl.when(t_idx == 0)
    def _():
        s_scr[...] = s0_ref[0]

    lane_head = _iota((1, MIX_W), 1) // HEAD_DIM
    same_head = (_iota((MIX_W, 1), 0) // HEAD_DIM) == lane_head
    eye = jnp.where(_iota((MIX_W, MIX_W), 0) == _iota((MIX_W, MIX_W), 1), 1.0, 0.0).astype(BF16)
    ones_head = jnp.where(same_head, 1.0, 0.0)
    rr, cc = _iota((c, c), 0), _iota((c, c), 1)
    pair_masks = [((rr // b) == (cc // b)) & ((rr % b) >= b // 2) & ((cc % b) < b // 2) for b in levels]
    diag_mask = rr == cc

    z = _dot_hi(misc_ref[...], wal_ref[...]) + bal_ref[...]
    la = (jnp.minimum(z, 0.0) - jnp.log1p(jnp.exp(-jnp.abs(z)))) * (1.0 / GATE_TAU)
    m_all = m_ref[...]

    def bd(x):
        return jnp.concatenate([jnp.where(lane_head == h, x, jnp.zeros_like(x)) for h in range(N_HEADS)], axis=0)

    def unbd(x):
        out = jnp.where(lane_head == 0, x[0:c], 0.0)
        for h in range(1, N_HEADS):
            out = out + jnp.where(lane_head == h, x[h * c:(h + 1) * c], 0.0)
        return out

    for ci in range(tt // c):
        sl = slice(ci * c, (ci + 1) * c)
        e = jnp.exp(_dot_hi(m_all, la[sl]))
        q = q_ref[sl, :] * (HEAD_DIM ** -0.5)
        k = k_ref[sl, :]
        v = v_ref[sl, :].astype(BF16)
        e_cum, e_rest = e[0:c], e[c:2 * c]
        att = jnp.where(diag_mask[None], _dot_nt(bd(q.astype(BF16)), k.astype(BF16)).reshape(N_HEADS, c, c), 0.0)
        for li in range(nl):
            eq = e[(2 + li) * c:(3 + li) * c]
            ek = e[(2 + nl + li) * c:(3 + nl + li) * c]
            a = _dot_nt(bd((q * eq).astype(BF16)), (k * ek).astype(BF16)).reshape(N_HEADS, c, c)
            att = att + jnp.where(pair_masks[li][None], a, 0.0)
        o_intra = unbd(_dot(att.reshape(N_HEADS * c, c).astype(BF16), v))
        st = s_scr[...]
        o_inter = _dot_nt((q * e_cum).astype(BF16), st.astype(BF16))
        v_t = _dot_nt(eye, v).astype(BF16)
        upd = _dot(v_t, (k * e_rest).astype(BF16))
        s_scr[...] = st * e_cum[c - 1:c, :] + jnp.where(same_head, upd, 0.0)
        o = o_inter + o_intra
        ms = _dot_hi(o * o, ones_head) * (1.0 / HEAD_DIM)
        o = o * lax.rsqrt(ms + EPS) * gg_ref[...]
        r = r_ref[sl, :]
        o_ref[sl, :] = (o * (r * jax.nn.sigmoid(r))).astype(o_ref.dtype)

    @pl.when(t_idx == n_tiles - 1)
    def _():
        st_ref[0] = s_scr[...]


def _gla(p32, wal_pad, b_alpha, g_gla4, st0, *, n_seq, t_seq, tt, out_dtype):
    c = min(GLA_CHUNK, t_seq)
    m_all, levels = _gla_tables(c)
    n_tiles = t_seq // tt
    rows = n_seq * t_seq
    col = lambda name: pl.BlockSpec((tt, MIX_W), lambda b, t, o=_OFF[name] // MIX_W: (b * n_tiles + t, o))
    kern = functools.partial(_gla_kernel, tt=tt, c=c, levels=levels, n_tiles=n_tiles)
    return pl.pallas_call(
        kern, grid=(n_seq, n_tiles),
        in_specs=[col('d_q'), col('d_k'), col('d_v'), col('d_r'),
                  pl.BlockSpec((tt, 128), lambda b, t: (b * n_tiles + t, _OFF['misc'] // 128)),
                  pl.BlockSpec((128, MIX_W), lambda b, t: (0, 0)),
                  pl.BlockSpec((1, MIX_W), lambda b, t: (0, 0)),
                  pl.BlockSpec((1, MIX_W), lambda b, t: (0, 0)),
                  pl.BlockSpec(m_all.shape, lambda b, t: (0, 0)),
                  pl.BlockSpec((1, MIX_W, MIX_W), lambda b, t: (b, 0, 0))],
        out_specs=[pl.BlockSpec((tt, MIX_W), lambda b, t: (b * n_tiles + t, 0)),
                   pl.BlockSpec((1, MIX_W, MIX_W), lambda b, t: (b, 0, 0))],
        out_shape=[jax.ShapeDtypeStruct((rows, MIX_W), out_dtype),
                   jax.ShapeDtypeStruct((n_seq, MIX_W, MIX_W), F32)],
        scratch_shapes=[pltpu.VMEM((MIX_W, MIX_W), F32)],
        compiler_params=_cp(("arbitrary", "arbitrary")), name="gla",
    )(p32, p32, p32, p32, p32, wal_pad, b_alpha.reshape(1, MIX_W), g_gla4, jnp.asarray(m_all), st0)


def _dwconv3(u, w, carry_ref, fix_refs, t_seq, tm, first_tile):
    row = _iota((tm, 1), 0)
    u1 = pltpu.roll(u, 1, axis=0)
    u2 = pltpu.roll(u, 2, axis=0)
    if fix_refs is None:
        prev = jnp.where(first_tile, 0.0, carry_ref[0:2, :])
        u1 = jnp.where(row == 0, prev[1:2], u1)
        u2 = jnp.where(row == 0, prev[0:1], jnp.where(row == 1, prev[1:2], u2))
        carry_ref[0:2, :] = u[tm - 2:tm]
    else:
        pos = row % t_seq
        u1 = jnp.where(pos == 0, fix_refs[0][...], u1)
        u2 = jnp.where(pos < 2, fix_refs[1][...], u2)
    return w[0:1] * u2 + w[1:2] * u1 + w[2:3] * u


def _merge_kernel(oa_ref, ob_ref, od_ref, cin_ref, cb_ref, cc_ref, gate_ref, x_ref, g1_ref, wconv_ref, wb_ref,
                  wo_ref, *rest, tm, t_seq, tiles_per_seq, per_row):
    if per_row:
        fix1_ref, fix2_ref, o_ref, carry_ref = rest
        fix = (fix1_ref, fix2_ref)
    else:
        o_ref, carry_ref = rest
        fix = None
    first = (pl.program_id(0) % tiles_per_seq) == 0
    u = cc_ref[...] * cin_ref[...]
    o_c = cb_ref[...] * _dwconv3(u, wconv_ref[...], carry_ref, fix, t_seq, tm, first)
    branches = (oa_ref[...], ob_ref[...], o_c, od_ref[...])
    merged = jnp.zeros((tm, D_MODEL), F32)
    for bi, br in enumerate(branches):
        gate = jax.nn.sigmoid(gate_ref[:, bi * D_MODEL:(bi + 1) * D_MODEL])
        merged = merged + gate * _dot(br.astype(BF16), wb_ref[bi])
    o_ref[...] = x_ref[...] + g1_ref[0] * _dot(merged.astype(BF16), wo_ref[...])


def _merge(oa, ob, od, p32, x, g1, conv_c, wb, wo, fix, *, tm, t_seq):
    rows = x.shape[0]
    per_row = fix is not None
    tiles_per_seq = max(t_seq // tm, 1)
    rowblk = lambda width, colblk=0: pl.BlockSpec((tm, width), lambda i: (i, colblk))
    in_specs = [rowblk(MIX_W), rowblk(MIX_W), rowblk(MIX_W),
                rowblk(MIX_W, _OFF['c_in'] // MIX_W), rowblk(MIX_W, _OFF['c_b'] // MIX_W),
                rowblk(MIX_W, _OFF['c_c'] // MIX_W), rowblk(N_HEADS * D_MODEL, 0), rowblk(D_MODEL),
                _mod_spec(g1, tm, t_seq), _const_spec((CONV_W, MIX_W)), _const_spec((N_HEADS, MIX_W, D_MODEL)),
                _const_spec((D_MODEL, D_MODEL))]
    args = [oa, ob, od, p32, p32, p32, p32, x, g1, conv_c, wb, wo]
    if per_row:
        in_specs += [rowblk(MIX_W), rowblk(MIX_W)]
        args += list(fix)
    kern = functools.partial(_merge_kernel, tm=tm, t_seq=t_seq, tiles_per_seq=tiles_per_seq, per_row=per_row)
    return pl.pallas_call(
        kern, grid=(rows // tm,), in_specs=in_specs, out_specs=rowblk(D_MODEL),
        out_shape=jax.ShapeDtypeStruct((rows, D_MODEL), F32),
        scratch_shapes=[pltpu.VMEM((8, MIX_W), F32)],
        compiler_params=_cp(("arbitrary",)), name="merge",
    )(*args)


def _ffn_kernel(x_ref, gn_ref, sc_ref, sh_ref, g2_ref, wa_ref, wg_ref, wconv_ref, bf_ref, wout_ref, gf_ref, *rest,
                tm, tf, t_seq, tiles_per_seq, per_row):
    if per_row:
        fix1_ref, fix2_ref, o_ref, y_ref, a_ref, carry_ref = rest
    else:
        o_ref, y_ref, a_ref, carry_ref = rest
    first = (pl.program_id(0) % tiles_per_seq) == 0
    x = x_ref[...]
    h = _norm_mod(x, gn_ref[...], sc_ref[0], sh_ref[0]).astype(BF16)
    acc = jnp.zeros((tm, D_MODEL), F32)
    for f in range(D_FF // tf):
        fs = slice(f * tf, (f + 1) * tf)
        a = _dot(h, wa_ref[:, fs])
        g = _dot(h, wg_ref[:, fs])
        a_ref[:, fs] = a
        fix = (fix1_ref.at[:, fs], fix2_ref.at[:, fs]) if per_row else None
        conv = _dwconv3(a, wconv_ref[:, fs], carry_ref.at[:, fs], fix, t_seq, tm, first)
        pre = conv + bf_ref[:, fs]
        act = pre * jax.nn.sigmoid(pre) * g
        acc = acc + _dot(act.astype(BF16), wout_ref[fs, :])
    xn = x + g2_ref[0] * acc
    o_ref[...] = xn
    y_ref[...] = xn * lax.rsqrt(jnp.mean(xn * xn, axis=-1, keepdims=True) + EPS) * gf_ref[...]


def _ffn(x, gn, sc, sh, g2, wa, wg, conv_ffn, b_ffn, wout, g_final, fix, *, tm, t_seq):
    rows = x.shape[0]
    per_row = fix is not None
    tiles_per_seq = max(t_seq // tm, 1)
    tf = 256
    rowblk = lambda width: pl.BlockSpec((tm, width), lambda i: (i, 0))
    in_specs = [rowblk(D_MODEL), _const_spec((1, D_MODEL)), _mod_spec(sc, tm, t_seq), _mod_spec(sh, tm, t_seq),
                _mod_spec(g2, tm, t_seq), _const_spec((D_MODEL, D_FF)), _const_spec((D_MODEL, D_FF)),
                _const_spec((CONV_W, D_FF)), _const_spec((1, D_FF)), _const_spec((D_FF, D_MODEL)),
                _const_spec((1, D_MODEL))]
    args = [x, gn, sc, sh, g2, wa, wg, conv_ffn, b_ffn, wout, g_final]
    if per_row:
        in_specs += [rowblk(D_FF), rowblk(D_FF)]
        args += list(fix)
    kern = functools.partial(_ffn_kernel, tm=tm, tf=tf, t_seq=t_seq, tiles_per_seq=tiles_per_seq, per_row=per_row)
    return pl.pallas_call(
        kern, grid=(rows // tm,), in_specs=in_specs,
        out_specs=[rowblk(D_MODEL), rowblk(D_MODEL), rowblk(D_FF)],
        out_shape=[jax.ShapeDtypeStruct((rows, D_MODEL), F32), jax.ShapeDtypeStruct((rows, D_MODEL), F32),
                   jax.ShapeDtypeStruct((rows, D_FF), F32)],
        scratch_shapes=[pltpu.VMEM((8, D_FF), F32)],
        compiler_params=_cp(("arbitrary",)), name="ffn",
    )(*args)


def _permute_w_in(w):
    offs, o = {}, 0
    for name, n in _IN_SPLITS:
        offs[name] = (o, n)
        o += n

    def c(name):
        s, n = offs[name]
        return w[:, s:s + n]

    pieces = [c('gate'), c('a_k'), c('a_v'), c('a_q'), c('a_qi'), c('b_q'), c('b_cmp'), c('b_slc'), c('b_win'),
              c('a_ki'), c('a_wi'), c('b_g'), c('d_a'), jnp.zeros((w.shape[0], 32), w.dtype),
              c('c_in'), c('c_b'), c('c_c'), c('d_q'), c('d_k'), c('d_v'), c('d_r')]
    return jnp.concatenate(pieces, axis=1).astype(BF16)


def _cols(p, name, width):
    return p[:, _OFF[name]:_OFF[name] + width]


def _state_to_bd(s0):
    b = s0.shape[0]
    s0t = jnp.swapaxes(s0, 2, 3)
    eye = jnp.eye(N_HEADS, dtype=s0.dtype)[None, :, None, :, None]
    return (s0t[:, :, :, None, :] * eye).reshape(b, MIX_W, MIX_W)


def _bd_to_state(st):
    b = st.shape[0]
    st5 = st.reshape(b, N_HEADS, HEAD_DIM, N_HEADS, HEAD_DIM)
    return jnp.stack([jnp.swapaxes(st5[:, h, :, h, :], 1, 2) for h in range(N_HEADS)], axis=1)


def _conv_fix(state, t_seq):
    b, _, c = state.shape
    fix1 = jnp.concatenate([state[:, 1:2], jnp.zeros((b, t_seq - 1, c), state.dtype)], axis=1)
    fix2 = jnp.concatenate([state, jnp.zeros((b, t_seq - 2, c), state.dtype)], axis=1)
    return fix1.reshape(b * t_seq, c), fix2.reshape(b * t_seq, c)


def _layer_weights(l, w_in, w_alpha, g_gla, w_branch, w_out, w_ffn_in, w_ffn_out):
    wal_pad = jnp.zeros((128, MIX_W), F32).at[MISC_DA:MISC_DA + GATE_RANK].set(w_alpha[l])
    return dict(w_in=_permute_w_in(w_in[l]), wal=wal_pad, gg=jnp.tile(g_gla[l], N_HEADS).reshape(1, MIX_W),
                wb=w_branch[l].astype(BF16), wo=w_out[l].astype(BF16),
                wa=w_ffn_in[l][:, :D_FF].astype(BF16), wg=w_ffn_in[l][:, D_FF:].astype(BF16),
                wout=w_ffn_out[l].astype(BF16))


def kernel(x_prompt, x_sample, cache_a_kv, cache_a_idx, cache_b_cmp, cache_b_slc, state_b_win, state_c_conv, state_d_gla, state_ffn_conv, page_table, c_prompt, c_sample, w_ada, b_ada, g_norm1, w_in, w_cmp_pos, conv_c, w_alpha, b_alpha, g_gla, w_branch, w_out, g_norm2, w_ffn_in, conv_ffn, b_ffn, w_ffn_out, g_final):
    bp, seq, d = x_prompt.shape
    bs, tdec, _ = x_sample.shape
    depth = w_ada.shape[0]
    n_pages = page_table.shape[1]
    past = n_pages * PAGE
    w_buf = state_b_win.shape[2]
    assert bp == 1 and d == D_MODEL and seq % 1024 == 0 and tdec == 8 and w_buf == WINDOW

    n_c = bp + bs
    c_all = jnp.concatenate([c_prompt, c_sample, jnp.zeros((-n_c % 8, d), F32)], axis=0)
    mod = _ada(c_all, w_ada, b_ada)

    xp = x_prompt.reshape(bp * seq, d)
    xs = x_sample.reshape(bs * tdec, d)
    rs = bs * tdec
    tail_rows = KEY_TILE
    lps = past + tail_rows
    tkm_s = max(t for t in (1, 2, 3) if (lps // KEY_TILE) % t == 0)
    gfin = g_final.reshape(1, d)
    outs_p, outs_s = [], []
    y_p = y_s = None
    for l in range(depth):
        lw = _layer_weights(l, w_in, w_alpha, g_gla, w_branch, w_out, w_ffn_in, w_ffn_out)
        mp = [mod[l, :bp, k * d:(k + 1) * d].reshape(bp, 1, d) for k in range(6)]
        ms = [jnp.repeat(mod[l, bp:n_c, k * d:(k + 1) * d], tdec, axis=0).reshape(1, rs, d) for k in range(6)]
        gn1, gn2 = g_norm1[l].reshape(1, d), g_norm2[l].reshape(1, d)
        bffn = b_ffn[l].reshape(1, D_FF)

        p32, _ = _proj(xs, gn1, ms[1], ms[0], lw['w_in'], rs, rs)
        new = {n: _cols(p32, n, w).reshape(bs, tdec, w) for n, w in
               (('a_k', 2 * MIX_W), ('misc', IDX_DIM), ('b_cmp', 128), ('b_slc', 128), ('b_win', 128))}
        tails = [jnp.concatenate([new[n], jnp.zeros((bs, tail_rows - tdec, new[n].shape[-1]), F32)], axis=1)
                 for n in ('a_k', 'misc', 'b_cmp', 'b_slc')]
        kv_s, ki_s, cmp_s, slc_s = _gather_past(page_table, [cache_a_kv, cache_a_idx, cache_b_cmp, cache_b_slc],
                                                tails, (BF16, BF16, F32, BF16), l)
        seq_s = past + tdec
        bias_a = _dsa_select(p32, p32, ki_s, 0, IDX_DIM, n_seq=bs, t_seq=tdec, tq=tdec, qoff=past,
                             k_sel=min(A_TOPK, seq_s // 4), bias_dtype=F32)
        o_a = _flash(p32, _OFF['a_q'] // MIX_W, kv_s, 0, bias_a, n_seq=bs, t_seq=tdec, tq=tdec, tkm=tkm_s, qoff=past,
                     shared=False, out_dtype=F32)
        ncp = -(-(-(-seq_s // CMP_STRIDE) - 1) // 128) * 128
        kcvc = _compress(cmp_s, w_cmp_pos[l], seq_s, ncp)
        win_full = jnp.concatenate([state_b_win[l], new['b_win']], axis=1)
        win_pad = jnp.concatenate([win_full, jnp.zeros((bs, -(w_buf + tdec) % 16, 128), F32)], axis=1)
        ocw, bias_b = _nsa_select(p32, p32, kcvc, win_pad, 0, n_seq=bs, t_seq=tdec, tq=tdec, qoff=past, seq_len=seq_s,
                                  lp=lps, win_dyn=False, win_pos0=past - w_buf, bias_dtype=F32, out_dtype=F32)
        o_b = _flash(p32, _OFF['b_q'] // MIX_W, slc_s, 0, bias_b, n_seq=bs, t_seq=tdec, tq=tdec, tkm=tkm_s, qoff=past,
                     shared=True, out_dtype=F32, gate_args=(p32, ocw))
        o_d, st = _gla(p32, lw['wal'], b_alpha[l], lw['gg'], _state_to_bd(state_d_gla[l]), n_seq=bs, t_seq=tdec,
                       tt=tdec, out_dtype=F32)
        u_tail = (_cols(p32, 'c_c', MIX_W) * _cols(p32, 'c_in', MIX_W)).reshape(bs, tdec, MIX_W)[:, tdec - 2:]
        xs = _merge(o_a, o_b, o_d, p32, xs, ms[2], conv_c[l], lw['wb'], lw['wo'], _conv_fix(state_c_conv[l], tdec),
                    tm=rs, t_seq=tdec)
        xs, y_s, a_full = _ffn(xs, gn2, ms[4], ms[3], ms[5], lw['wa'], lw['wg'], conv_ffn[l], bffn, lw['wout'], gfin,
                               _conv_fix(state_ffn_conv[l], tdec), tm=rs, t_seq=tdec)
        outs_s.append((new['a_k'], new['misc'], new['b_cmp'], new['b_slc'], win_full[:, tdec:], u_tail,
                       _bd_to_state(st), a_full.reshape(bs, tdec, D_FF)[:, tdec - 2:]))

        p32, p16 = _proj(xp, gn1, mp[1], mp[0], lw['w_in'], 1024, seq)
        p16_3 = p16.reshape(bp, seq, NP)
        a_kv = _cols(p32, 'a_k', 2 * MIX_W).reshape(bp, seq, 2 * MIX_W)
        a_idx = _cols(p32, 'misc', IDX_DIM).reshape(bp, seq, IDX_DIM)
        b_cmp = _cols(p32, 'b_cmp', 128).reshape(bp, seq, 128)
        b_slc = _cols(p32, 'b_slc', 128).reshape(bp, seq, 128)
        b_win = _cols(p32, 'b_win', 128).reshape(bp, seq, 128)[:, seq - min(WINDOW, seq):]
        k_sel = min(A_TOPK, seq // 4)
        bias_a = _dsa_select(p16, p32, p16_3, _OFF['misc'] // 128, 128, n_seq=bp, t_seq=seq, tq=128, qoff=0,
                             k_sel=k_sel, bias_dtype=BF16)
        q_scale = HEAD_DIM ** -0.5 * float(np.log2(np.e))
        o_a = jnp.transpose(_flash_t(jnp.transpose(_cols(p32, 'a_q', MIX_W) * q_scale).astype(BF16), p16,
                                     _OFF['a_k'] // MIX_W, MIX_W,
                                     jnp.transpose(_cols(p16, 'a_v', MIX_W)), bias_a, t_seq=seq, tq=512, tkm=2,
                                     shared=False))
        ncp = -(-(-(-seq // CMP_STRIDE) - 1) // 128) * 128
        kcvc = _compress(b_cmp, w_cmp_pos[l], seq, ncp)
        ocw, bias_b = _nsa_select(p16, p32, kcvc, p16_3, _OFF['b_win'] // 128, n_seq=bp, t_seq=seq, tq=128, qoff=0,
                                  seq_len=seq, lp=seq, win_dyn=True, win_pos0=0, bias_dtype=BF16, out_dtype=BF16)
        vt_b = jnp.transpose(p16[:, _OFF['b_slc'] + HEAD_DIM:_OFF['b_slc'] + 2 * HEAD_DIM])
        o_b = jnp.transpose(_flash_t(jnp.transpose(_cols(p32, 'b_q', MIX_W) * q_scale).astype(BF16), p16,
                                     _OFF['b_slc'] // 128, 128, vt_b,
                                     bias_b, t_seq=seq, tq=512, tkm=2, shared=True,
                                     gate_args=(jnp.transpose(_cols(p32, 'misc', 128)), jnp.transpose(ocw))))
        o_d, st = _gla(p32, lw['wal'], b_alpha[l], lw['gg'], jnp.zeros((bp, MIX_W, MIX_W), F32), n_seq=bp, t_seq=seq,
                       tt=512, out_dtype=BF16)
        u_tail = (_cols(p32, 'c_c', MIX_W) * _cols(p32, 'c_in', MIX_W)).reshape(bp, seq, MIX_W)[:, seq - 2:]
        xp = _merge(o_a, o_b, o_d, p32, xp, mp[2], conv_c[l], lw['wb'], lw['wo'], None, tm=256, t_seq=seq)
        xp, y_p, a_full = _ffn(xp, gn2, mp[4], mp[3], mp[5], lw['wa'], lw['wg'], conv_ffn[l], bffn, lw['wout'], gfin,
                               None, tm=512, t_seq=seq)
        outs_p.append((a_kv, a_idx, b_cmp, b_slc, b_win, u_tail, _bd_to_state(st),
                       a_full.reshape(bp, seq, D_FF)[:, seq - 2:]))

    sp = [jnp.stack(z) for z in zip(*outs_p)]
    ss = [jnp.stack(z) for z in zip(*outs_s)]
    res = [y_p.reshape(bp, seq, d), y_s.reshape(bs, tdec, d)]
    for a, b in zip(sp, ss):
        res += [a, b]
    return tuple(res)
```

```python
import functools

import numpy as np
import jax
import jax.numpy as jnp
from jax import lax
from jax.experimental import pallas as pl
from jax.experimental.pallas import tpu as pltpu

F32 = jnp.float32
BF16 = jnp.bfloat16
HI = lax.Precision.HIGHEST

D_MODEL = 1024
PAGE = 128
HEAD_DIM = 64
MIX_W = 256
N_HEADS = 4
IDX_DIM = 64
A_TOPK = 256
CMP_LEN = 32
CMP_STRIDE = 16
SLC_BLOCK = 64
N_SLC = 16
WINDOW = 512
CONV_W = 3
GATE_RANK = 16
GATE_TAU = 16.0
GLA_CHUNK = 64
D_FF = 2816
EPS = 1e-6
NEG = -1e30
INT_MIN = -2 ** 31
KEY_TILE = 512

_IN_SPLITS = (('a_q', 256), ('a_k', 256), ('a_v', 256), ('a_qi', 256), ('a_ki', 64), ('a_wi', 4),
              ('b_q', 256), ('b_cmp', 128), ('b_slc', 128), ('b_win', 128), ('b_g', 12),
              ('c_in', 256), ('c_b', 256), ('c_c', 256),
              ('d_q', 256), ('d_k', 256), ('d_v', 256), ('d_r', 256), ('d_a', 16), ('gate', 4096))
_OFF = dict(gate=0, a_k=4096, a_v=4352, a_q=4608, a_qi=4864, b_q=5120, b_cmp=5376, b_slc=5504, b_win=5632,
            misc=5760, c_in=5888, c_b=6144, c_c=6400, d_q=6656, d_k=6912, d_v=7168, d_r=7424)
NP = 7680
MISC_KI, MISC_WI, MISC_G, MISC_DA = 0, 64, 68, 80
VMEM_LIMIT = 56 * 1024 * 1024


def _cp(sem):
    return pltpu.CompilerParams(dimension_semantics=sem, vmem_limit_bytes=VMEM_LIMIT)


def _dot(a, b):
    return jnp.dot(a, b, preferred_element_type=F32)


def _dot_nt(a, b):
    return lax.dot_general(a, b, (((1,), (1,)), ((), ())), preferred_element_type=F32)


def _dot_hi(a, b):
    return jnp.dot(a, b, preferred_element_type=F32, precision=HI)


def _dot_exact01(a01, x):
    hi = x.astype(BF16)
    lo = (x - hi.astype(F32)).astype(BF16)
    return _dot(a01, hi) + _dot(a01, lo)


def _dot_x_exact01(x, b01):
    hi = x.astype(BF16)
    lo = (x - hi.astype(F32)).astype(BF16)
    return _dot(hi, b01) + _dot(lo, b01)


def _const_spec(shape):
    nd = len(shape)
    return pl.BlockSpec(shape, lambda *a: (0,) * nd, pipeline_mode=pl.Buffered(1))


def _iota(shape, dim):
    return lax.broadcasted_iota(jnp.int32, shape, dim)


def _ada_kernel(c_ref, w_ref, b_ref, o_ref):
    o_ref[0] = _dot_hi(c_ref[...], w_ref[0]) + b_ref[0]


def _ada(c_all, w_ada, b_ada):
    depth, d, n6 = w_ada.shape
    rows = c_all.shape[0]
    tn = 1024
    return pl.pallas_call(
        _ada_kernel, grid=(depth, n6 // tn),
        in_specs=[pl.BlockSpec((rows, d), lambda l, j: (0, 0)),
                  pl.BlockSpec((1, d, tn), lambda l, j: (l, 0, j)),
                  pl.BlockSpec((1, 1, tn), lambda l, j: (l, 0, j))],
        out_specs=pl.BlockSpec((1, rows, tn), lambda l, j: (l, 0, j)),
        out_shape=jax.ShapeDtypeStruct((depth, rows, n6), F32),
        compiler_params=_cp(("arbitrary", "arbitrary")), name="ada",
    )(c_all, w_ada, b_ada.reshape(depth, 1, n6))


def _norm_mod(x, g, sc, sh):
    y = x * lax.rsqrt(jnp.mean(x * x, axis=-1, keepdims=True) + EPS) * g
    return y * (1.0 + sc) + sh


def _proj_kernel(x_ref, g_ref, sc_ref, sh_ref, w_ref, o32_ref, o16_ref, h_ref):
    @pl.when(pl.program_id(1) == 0)
    def _():
        h_ref[...] = _norm_mod(x_ref[...], g_ref[...], sc_ref[0], sh_ref[0]).astype(BF16)

    acc = _dot(h_ref[...], w_ref[...])
    o32_ref[...] = acc
    o16_ref[...] = acc.astype(BF16)


def _mod_spec(mod, tm, rows_per_group):
    mb = mod.shape[1]
    tiles = max(rows_per_group // tm, 1)
    return pl.BlockSpec((1, mb, D_MODEL), lambda i, *_: (i // tiles, 0, 0))


def _proj(x, g, sc, sh, w, tm, rows_per_group):
    rows = x.shape[0]
    tn = 768
    return pl.pallas_call(
        _proj_kernel, grid=(rows // tm, NP // tn),
        in_specs=[pl.BlockSpec((tm, D_MODEL), lambda i, j: (i, 0)),
                  pl.BlockSpec((1, D_MODEL), lambda i, j: (0, 0)),
                  _mod_spec(sc, tm, rows_per_group), _mod_spec(sh, tm, rows_per_group),
                  pl.BlockSpec((D_MODEL, tn), lambda i, j: (0, j))],
        out_specs=[pl.BlockSpec((tm, tn), lambda i, j: (i, j)), pl.BlockSpec((tm, tn), lambda i, j: (i, j))],
        out_shape=[jax.ShapeDtypeStruct((rows, NP), F32), jax.ShapeDtypeStruct((rows, NP), BF16)],
        scratch_shapes=[pltpu.VMEM((tm, D_MODEL), BF16)],
        compiler_params=_cp(("arbitrary", "arbitrary")), name="proj",
    )(x, g, sc, sh, w)


PAGES_PER_STEP = KEY_TILE // PAGE


def _gather_kernel(pt_ref, *refs, n_arr, n_chunks):
    ppc = PAGES_PER_STEP
    pages, tails, outs = refs[:n_arr * ppc], refs[n_arr * ppc:n_arr * (ppc + 1)], refs[n_arr * (ppc + 1):]
    c = pl.program_id(1)

    @pl.when(c < n_chunks - 1)
    def _():
        for k in range(n_arr):
            for r in range(ppc):
                outs[k][0, r * PAGE:(r + 1) * PAGE, :] = pages[k * ppc + r][0, 0].astype(outs[k].dtype)

    @pl.when(c == n_chunks - 1)
    def _():
        for k in range(n_arr):
            outs[k][0] = tails[k][0].astype(outs[k].dtype)


def _gather_past(page_table, pools, tails, out_dtypes, layer):
    n_seq, n_pages = page_table.shape
    n_arr, ppc = len(pools), PAGES_PER_STEP
    assert n_pages % ppc == 0 and tails[0].shape[1] == ppc * PAGE
    n_chunks = n_pages // ppc + 1
    lp = n_chunks * ppc * PAGE
    in_specs = []
    for p in pools:
        for r in range(ppc):
            in_specs.append(pl.BlockSpec(
                (1, 1, PAGE, p.shape[-1]),
                lambda b, c, pt, r=r: (layer, pt[b, jnp.minimum(c * ppc + r, n_pages - 1)], 0, 0)))
    in_specs += [pl.BlockSpec((1, ppc * PAGE, t.shape[-1]), lambda b, c, pt: (b, 0, 0)) for t in tails]
    args = [p for p in pools for _ in range(ppc)] + list(tails)
    kern = functools.partial(_gather_kernel, n_arr=n_arr, n_chunks=n_chunks)
    return pl.pallas_call(
        kern,
        grid_spec=pltpu.PrefetchScalarGridSpec(
            num_scalar_prefetch=1, grid=(n_seq, n_chunks), in_specs=in_specs,
            out_specs=[pl.BlockSpec((1, ppc * PAGE, p.shape[-1]), lambda b, c, pt: (b, c, 0)) for p in pools]),
        out_shape=[jax.ShapeDtypeStruct((n_seq, lp, p.shape[-1]), dt) for p, dt in zip(pools, out_dtypes)],
        compiler_params=_cp(("arbitrary", "arbitrary")), name="gather_past",
    )(page_table, *args)


def _stack_heads(q, width):
    t = q.shape[0]
    parts = []
    for h in range(N_HEADS):
        p = q[:, h * HEAD_DIM:(h + 1) * HEAD_DIM]
        if width > HEAD_DIM:
            p = jnp.concatenate([p, jnp.zeros((t, width - HEAD_DIM), q.dtype)], axis=1)
        parts.append(p)
    return jnp.concatenate(parts, axis=0)


def _dsa_sel_kernel(qi_ref, misc_ref, ki_ref, bias_ref, s_ref, *planes, tq, nk, k_sel, qoff, fk):
    tk = KEY_TILE
    i = pl.program_id(1)
    q0 = qoff + i * tq
    qpos = q0 + _iota((tq, 1), 0)
    nlim = jnp.minimum((q0 + tq - 1) // tk + 1, nk)
    qs = _stack_heads(qi_ref[...].astype(BF16), fk)
    wi = misc_ref[:, MISC_WI:MISC_WI + N_HEADS]

    def score_tile(j, c):
        kt = ki_ref[0, pl.ds(pl.multiple_of(j * tk, tk), tk), :].astype(BF16)
        d = _dot_nt(qs, kt).reshape(N_HEADS, tq, tk)
        sc = wi[:, 0:1] * jnp.maximum(d[0], 0.0)
        for h in range(1, N_HEADS):
            sc = sc + wi[:, h:h + 1] * jnp.maximum(d[h], 0.0)
        bits = lax.bitcast_convert_type(sc, jnp.int32)
        key = bits ^ ((bits >> 31) & 0x7FFFFFFF)
        key = jnp.where(key == -1, 0, key)
        kpos = j * tk + _iota((1, tk), 1)
        s_ref[j] = jnp.where(kpos <= qpos, key, INT_MIN)
        return c

    if nk % 2 == 0:
        def score_pair(j2, c):
            score_tile(2 * j2, c)
            return score_tile(2 * j2 + 1, c)

        lax.fori_loop(0, (nlim + 1) // 2, score_pair, 0)
    else:
        lax.fori_loop(0, nlim, score_tile, 0)

    if planes:
        p_ref, cand_ref = planes

        def unused_tile(j, c):
            s_ref[j] = jnp.full((tq, tk), INT_MIN, jnp.int32)
            return c

        lax.fori_loop(nlim, nk, unused_tile, 0)

        def transpose_bits(rg, c):
            rows = pl.ds(pl.multiple_of(rg * 8, 8), 8)
            for lc in range(tk // 128):
                lanes = slice(lc * 128, (lc + 1) * 128)
                x = [s_ref[j, rows, lanes] ^ INT_MIN if j < nk else jnp.zeros((8, 128), jnp.int32) for j in range(32)]
                s, m = 16, 0x0000FFFF
                while s >= 1:
                    for a in range(32):
                        if (a & s) == 0:
                            t = (lax.shift_right_logical(x[a], s) ^ x[a + s]) & m
                            x[a + s] = x[a + s] ^ t
                            x[a] = x[a] ^ lax.shift_left(t, s)
                    s //= 2
                    if s:
                        m = (m ^ (m << s)) & 0xFFFFFFFF
                        m = m - (1 << 32) if m >= (1 << 31) else m
                for b in range(32):
                    p_ref[b, rows, lanes] = x[b]
            return c

        lax.fori_loop(0, tq // 8, transpose_bits, 0)
        cand_ref[...] = jnp.full((tq, tk), -1, jnp.int32)

        def lane_total(words):
            pc = lax.population_count(words)
            part = pc[:, 0:128]
            for c in range(1, tk // 128):
                part = part + pc[:, c * 128:(c + 1) * 128]
            return jnp.sum(part.astype(F32), axis=-1, keepdims=True)

        def bit_step(t, carry):
            need, thr_u = carry
            b = 31 - t
            ones = cand_ref[...] & p_ref[b]
            cnt = lane_total(ones)
            take = cnt >= need
            cand_ref[...] = jnp.where(take, ones, cand_ref[...] & ~p_ref[b])
            return jnp.where(take, need, need - cnt), jnp.where(take, thr_u | lax.shift_left(jnp.int32(1), b), thr_u)

        ties_allowed, thr_u = lax.fori_loop(
            0, 32, bit_step, (jnp.full((tq, 1), float(k_sel), F32), jnp.zeros((tq, 1), jnp.int32)))
        thr = thr_u ^ INT_MIN
        n_ties = lane_total(cand_ref[...])
    else:
        def count(pred, level):
            level_b = jnp.broadcast_to(level, (tq, 128))

            def body(j, acc):
                for c in range(tk // 128):
                    acc = acc + jnp.where(pred(s_ref[j, :, c * 128:(c + 1) * 128], level_b), 1.0, 0.0)
                return acc
            return jnp.sum(lax.fori_loop(0, nlim, body, jnp.zeros((tq, 128), F32)), axis=-1, keepdims=True)

        def bit_step(t, c):
            trial = c + lax.shift_left(jnp.int32(1), 31 - t)
            return jnp.where(count(lambda s, lv: s >= lv, trial) >= k_sel, trial, c)

        thr = lax.fori_loop(0, 32, bit_step, jnp.full((tq, 1), INT_MIN, jnp.int32))
        ties_allowed = k_sel - count(lambda s, lv: s > lv, thr)
        n_ties = count(lambda s, lv: s == lv, thr)
    need_rank = jnp.max(jnp.where(n_ties > ties_allowed, 1.0, 0.0)) > 0.5

    @pl.when(need_rank)
    def _():
        tri = jnp.where(_iota((tk, tk), 0) <= _iota((tk, tk), 1), 1.0, 0.0).astype(BF16)

        def out_tile(j, carry):
            key = s_ref[j]
            eq = jnp.where(key == thr, 1.0, 0.0)
            rank = _dot(eq.astype(BF16), tri) + carry
            kpos = j * tk + _iota((1, tk), 1)
            take = jnp.where(key > thr, 1.0, jnp.where(rank <= ties_allowed, eq, 0.0))
            take = jnp.where(kpos <= qpos, take, 0.0)
            bias_ref[j] = jnp.where(take > 0.5, 0.0, NEG).astype(bias_ref.dtype)
            return rank[:, tk - 1:tk]

        lax.fori_loop(0, nlim, out_tile, jnp.zeros((tq, 1), F32))

    @pl.when(jnp.logical_not(need_rank))
    def _():
        def out_tile(j, c):
            kpos = j * tk + _iota((1, tk), 1)
            take = jnp.where(kpos <= qpos, jnp.where(s_ref[j] >= thr, 1.0, 0.0), 0.0)
            bias_ref[j] = jnp.where(take > 0.5, 0.0, NEG).astype(bias_ref.dtype)
            return c

        lax.fori_loop(0, nlim, out_tile, 0)

    def fill(j, c):
        bias_ref[j] = jnp.full((tq, tk), NEG, bias_ref.dtype)
        return c

    lax.fori_loop(nlim, nk, fill, 0)


def _dsa_select(q_arr, p32, ki3, ki_col, fk, *, n_seq, t_seq, tq, qoff, k_sel, bias_dtype):
    lp = ki3.shape[1]
    nk, nq = lp // KEY_TILE, t_seq // tq
    rows = n_seq * t_seq
    kern = functools.partial(_dsa_sel_kernel, tq=tq, nk=nk, k_sel=k_sel, qoff=qoff, fk=fk)
    return pl.pallas_call(
        kern, grid=(n_seq, nq),
        in_specs=[pl.BlockSpec((tq, MIX_W), lambda b, i: (b * nq + i, _OFF['a_qi'] // MIX_W)),
                  pl.BlockSpec((tq, 128), lambda b, i: (b * nq + i, _OFF['misc'] // 128)),
                  pl.BlockSpec((1, lp, fk), lambda b, i: (b, 0, ki_col))],
        out_specs=pl.BlockSpec((nk, tq, KEY_TILE), lambda b, i: (0, b * nq + i, 0)),
        out_shape=jax.ShapeDtypeStruct((nk, rows, KEY_TILE), bias_dtype),
        scratch_shapes=[pltpu.VMEM((nk, tq, KEY_TILE), jnp.int32)]
        + ([pltpu.VMEM((32, tq, KEY_TILE), jnp.int32), pltpu.VMEM((tq, KEY_TILE), jnp.int32)] if nk <= 32 else []),
        compiler_params=_cp(("arbitrary", "arbitrary")), name="dsa_select",
    )(q_arr, p32, ki3)


def _flash_row_chunk(tq):
    return min(32, tq)


def _flash_kernel(sb, sq, sk, sfl, q_ref, kv_ref, bias_ref, *rest, tq, tkm, shared, gated):
    if gated:
        misc_ref, add_ref, o_ref, m_ref, l_ref, acc_ref, s_ref, p_ref, qs_ref, a_ref = rest
    else:
        o_ref, m_ref, l_ref, acc_ref, s_ref, p_ref, qs_ref, a_ref = rest
    tk = KEY_TILE
    flags = sfl[pl.program_id(0)]
    dv = HEAD_DIM if shared else MIX_W
    lane_head = _iota((1, MIX_W), 1) // HEAD_DIM
    rc = _flash_row_chunk(tq)

    @pl.when((flags & 1) != 0)
    def _():
        m_ref[...] = jnp.full(m_ref.shape, NEG, F32)
        l_ref[...] = jnp.zeros(l_ref.shape, F32)
        acc_ref[...] = jnp.zeros(acc_ref.shape, F32)
        q = (q_ref[...] * (HEAD_DIM ** -0.5)).astype(BF16)
        if shared:
            qs_ref[...] = _stack_heads(q, HEAD_DIM)
        else:
            qs_ref[...] = jnp.concatenate([jnp.where(lane_head == h, q, jnp.zeros_like(q)) for h in range(N_HEADS)],
                                          axis=0)

    for u in range(tkm):
        kvt = kv_ref[0, u * tk:(u + 1) * tk, :]
        k = kvt[:, :dv].astype(BF16)
        v = kvt[:, dv:2 * dv].astype(BF16)
        half = N_HEADS * tq // 2
        s_ref[0:half, :] = _dot_nt(qs_ref[0:half, :], k)
        s_ref[half:, :] = _dot_nt(qs_ref[half:, :], k)
        n_chunks = N_HEADS * tq // rc

        def rows_of(c):
            return (pl.ds(pl.multiple_of(c * rc, rc), rc), pl.ds(pl.multiple_of((c % (tq // rc)) * rc, rc), rc))

        def pass_max(c, carry):
            rows, brow = rows_of(c)
            s = s_ref[rows, :] + bias_ref[u, brow, :].astype(F32)
            s_ref[rows, :] = s
            m_old = m_ref[rows, :]
            m_new = jnp.maximum(m_old, jnp.max(s, axis=-1, keepdims=True))
            a_ref[rows, :] = jnp.exp(m_old - m_new)
            m_ref[rows, :] = m_new
            return carry

        lax.fori_loop(0, n_chunks, pass_max, 0, unroll=min(8, n_chunks))

        def pass_exp(c, carry):
            rows, _ = rows_of(c)
            m = m_ref[rows, :]
            part = a_ref[rows, :] * l_ref[rows, :]
            for cc in range(tk // 128):
                p = jnp.exp(s_ref[rows, cc * 128:(cc + 1) * 128] - m)
                part = part + p
                p_ref[rows, cc * 128:(cc + 1) * 128] = p.astype(p_ref.dtype)
            l_ref[rows, :] = part
            return carry

        lax.fori_loop(0, n_chunks, pass_exp, 0, unroll=min(8, n_chunks))
        alpha = a_ref[...]
        alpha = alpha[:, :dv] if dv <= 128 else jnp.concatenate([alpha] * (dv // 128), axis=1)
        acc_ref[0:half, :] = alpha[0:half] * acc_ref[0:half, :] + _dot(p_ref[0:half, :].astype(BF16), v)
        acc_ref[half:, :] = alpha[half:] * acc_ref[half:, :] + _dot(p_ref[half:, :].astype(BF16), v)

    @pl.when((flags & 2) != 0)
    def _():
        l_sum = jnp.sum(l_ref[...], axis=-1, keepdims=True)
        o = (acc_ref[...] / l_sum).reshape(N_HEADS, tq, dv)
        if shared:
            out = jnp.concatenate([o[h] for h in range(N_HEADS)], axis=1)
        else:
            out = jnp.where(lane_head == 0, o[0], 0.0)
            for h in range(1, N_HEADS):
                out = out + jnp.where(lane_head == h, o[h], 0.0)
        if gated:
            g = jax.nn.sigmoid(misc_ref[...])
            gate = jnp.zeros((tq, MIX_W), F32)
            for h in range(N_HEADS):
                c = MISC_G + 3 * h + 1
                gate = gate + jnp.where(lane_head == h, g[:, c:c + 1], 0.0)
            out = add_ref[...].astype(F32) + gate * out
        o_ref[...] = out.astype(o_ref.dtype)


def _flash_steps(n_seq, t_seq, tq, tk, lp, qoff):
    sb, sq, sk, sfl = [], [], [], []
    for b in range(n_seq):
        for i in range(t_seq // tq):
            nlim = min((qoff + (i + 1) * tq - 1) // tk + 1, lp // tk)
            for j in range(nlim):
                sb.append(b), sq.append(i), sk.append(j)
                sfl.append((1 if j == 0 else 0) | (2 if j == nlim - 1 else 0))
    return [jnp.asarray(np.asarray(a, np.int32)) for a in (sb, sq, sk, sfl)]


def _flash(q_arr, q_col, kv3, kv_col, bias, *, n_seq, t_seq, tq, tkm, qoff, shared, out_dtype, gate_args=None):
    lp = kv3.shape[1]
    tk = tkm * KEY_TILE
    nq = t_seq // tq
    rows = n_seq * t_seq
    wblk = 2 * (HEAD_DIM if shared else MIX_W)
    dv = HEAD_DIM if shared else MIX_W
    steps = _flash_steps(n_seq, t_seq, tq, tk, lp, qoff)
    gated = gate_args is not None
    row = lambda s, sb, sq, sk, sfl: sb[s] * nq + sq[s]
    in_specs = [pl.BlockSpec((tq, MIX_W), lambda s, sb, sq, sk, sfl: (row(s, sb, sq, sk, sfl), q_col)),
                pl.BlockSpec((1, tk, wblk), lambda s, sb, sq, sk, sfl: (sb[s], sk[s], kv_col)),
                pl.BlockSpec((tkm, tq, KEY_TILE), lambda s, sb, sq, sk, sfl: (sk[s], row(s, sb, sq, sk, sfl), 0))]
    args = [q_arr, kv3, bias]
    if gated:
        in_specs += [pl.BlockSpec((tq, 128), lambda s, sb, sq, sk, sfl: (row(s, sb, sq, sk, sfl), _OFF['misc'] // 128)),
                     pl.BlockSpec((tq, MIX_W), lambda s, sb, sq, sk, sfl: (row(s, sb, sq, sk, sfl), 0))]
        args += list(gate_args)
    kern = functools.partial(_flash_kernel, tq=tq, tkm=tkm, shared=shared, gated=gated)
    return pl.pallas_call(
        kern,
        grid_spec=pltpu.PrefetchScalarGridSpec(
            num_scalar_prefetch=4, grid=(int(steps[0].shape[0]),), in_specs=in_specs,
            out_specs=pl.BlockSpec((tq, MIX_W), lambda s, sb, sq, sk, sfl: (row(s, sb, sq, sk, sfl), 0)),
            scratch_shapes=[pltpu.VMEM((N_HEADS * tq, 128), F32), pltpu.VMEM((N_HEADS * tq, 128), F32),
                            pltpu.VMEM((N_HEADS * tq, dv), F32), pltpu.VMEM((N_HEADS * tq, KEY_TILE), F32),
                            pltpu.VMEM((N_HEADS * tq, KEY_TILE), BF16 if _flash_row_chunk(tq) % 16 == 0 else F32),
                            pltpu.VMEM((N_HEADS * tq, dv), BF16), pltpu.VMEM((N_HEADS * tq, 128), F32)]),
        out_shape=jax.ShapeDtypeStruct((rows, MIX_W), out_dtype),
        compiler_params=_cp(("arbitrary",)), name="flash_shared" if shared else "flash_heads",
    )(*steps, *args)


def _flash_t_kernel(sq, sk, sfl, qt_ref, k_ref, vt_ref, bias_ref, *rest, tq, tkm, shared, gated, block_sel):
    if gated:
        misct_ref, addt_ref, o_ref, m_ref, l_ref, acc_ref = rest
    else:
        o_ref, m_ref, l_ref, acc_ref = rest
    tk = KEY_TILE
    flags = sfl[pl.program_id(0)]
    row_head = _iota((MIX_W, 1), 0) // HEAD_DIM

    @pl.when((flags & 1) != 0)
    def _():
        m_ref[...] = jnp.full(m_ref.shape, NEG, F32)
        l_ref[...] = jnp.zeros(l_ref.shape, F32)
        acc_ref[...] = jnp.zeros(acc_ref.shape, F32)

    qt = qt_ref[...]
    for u in range(tkm):
        k = k_ref[u * tk:(u + 1) * tk, :]
        k = k[:, :HEAD_DIM] if shared else k
        if block_sel:
            nb = tk // SLC_BLOCK
            kt = sk[pl.program_id(0)] * tkm + u
            blocks = bias_ref[pl.ds(pl.multiple_of(kt * nb, nb), nb), :]
            on = jnp.concatenate([jnp.broadcast_to(blocks[r:r + 1, :], (SLC_BLOCK, tq)) for r in range(nb)], axis=0)
            kpos = kt * tk + _iota((tk, 1), 0)
            qpos = sq[pl.program_id(0)] * tq + _iota((1, tq), 1)
            bias_t = jnp.where((on > 0.5) & (kpos <= qpos), 0.0, NEG)
        else:
            bias_t = bias_ref[u].astype(F32).T
        for h in range(N_HEADS):
            hs = slice(h * HEAD_DIM, (h + 1) * HEAD_DIM)
            q_h = qt[hs, :] if shared else jnp.where(row_head == h, qt, jnp.zeros_like(qt))
            s = _dot(k, q_h) + bias_t
            m_old = m_ref[h]
            m_new = jnp.maximum(m_old, jnp.max(s, axis=0, keepdims=True))
            alpha = jnp.exp2(m_old - m_new)
            p = jnp.exp2(s - m_new)
            l_ref[h] = alpha * l_ref[h] + jnp.sum(p, axis=0, keepdims=True)
            vt_h = vt_ref[0:HEAD_DIM, u * tk:(u + 1) * tk] if shared else vt_ref[hs, u * tk:(u + 1) * tk]
            acc_ref[h] = alpha * acc_ref[h] + _dot(vt_h, p.astype(BF16))
            m_ref[h] = m_new

    @pl.when((flags & 2) != 0)
    def _():
        for h in range(N_HEADS):
            hs = slice(h * HEAD_DIM, (h + 1) * HEAD_DIM)
            out = acc_ref[h] / l_ref[h]
            if gated:
                c = MISC_G + 3 * h + 1
                out = addt_ref[hs, :].astype(F32) + jax.nn.sigmoid(misct_ref[c:c + 1, :]) * out
            o_ref[hs, :] = out.astype(o_ref.dtype)


def _flash_t(qt, k_arr, k_col, k_width, vt, bias, *, t_seq, tq, tkm, shared, gate_args=None):
    tk = tkm * KEY_TILE
    steps = _flash_steps(1, t_seq, tq, tk, t_seq, 0)[1:]
    gated = gate_args is not None
    vrows = vt.shape[0]
    block_sel = bias.ndim == 2
    if block_sel:
        bias_spec = pl.BlockSpec((bias.shape[0], tq), lambda s, sq, sk, sfl: (0, sq[s]))
    else:
        bias_spec = pl.BlockSpec((tkm, tq, KEY_TILE), lambda s, sq, sk, sfl: (sk[s], sq[s], 0))
    in_specs = [pl.BlockSpec((MIX_W, tq), lambda s, sq, sk, sfl: (0, sq[s])),
                pl.BlockSpec((tk, k_width), lambda s, sq, sk, sfl: (sk[s], k_col)),
                pl.BlockSpec((vrows, tk), lambda s, sq, sk, sfl: (0, sk[s])), bias_spec]
    args = [qt, k_arr, vt, bias]
    if gated:
        in_specs += [pl.BlockSpec((128, tq), lambda s, sq, sk, sfl: (0, sq[s])),
                     pl.BlockSpec((MIX_W, tq), lambda s, sq, sk, sfl: (0, sq[s]))]
        args += list(gate_args)
    kern = functools.partial(_flash_t_kernel, tq=tq, tkm=tkm, shared=shared, gated=gated, block_sel=block_sel)
    return pl.pallas_call(
        kern,
        grid_spec=pltpu.PrefetchScalarGridSpec(
            num_scalar_prefetch=3, grid=(int(steps[0].shape[0]),), in_specs=in_specs,
            out_specs=pl.BlockSpec((MIX_W, tq), lambda s, sq, sk, sfl: (0, sq[s])),
            scratch_shapes=[pltpu.VMEM((N_HEADS, 1, tq), F32), pltpu.VMEM((N_HEADS, 1, tq), F32),
                            pltpu.VMEM((N_HEADS, HEAD_DIM, tq), F32)]),
        out_shape=jax.ShapeDtypeStruct((MIX_W, t_seq), BF16),
        compiler_params=_cp(("arbitrary",)), name="flash_t_shared" if shared else "flash_t_heads",
    )(*steps, *args)


def _compress_kernel(x_ref, w_ref, o_ref, *, n_blocks, ncp):
    w = w_ref[...]
    e = jnp.exp(w - jnp.max(w, axis=-1, keepdims=True))
    w = e / jnp.sum(e, axis=-1, keepdims=True)
    n16 = x_ref.shape[1] // CMP_STRIDE
    width = x_ref.shape[2]
    first = jnp.zeros((n16, width), F32)
    second = jnp.zeros((n16, width), F32)
    for j in range(CMP_STRIDE):
        xj = x_ref[0, pl.ds(j, n16, stride=CMP_STRIDE), :]
        first = first + xj * w[:, j:j + 1]
        second = second + xj * w[:, CMP_STRIDE + j:CMP_STRIDE + j + 1]
    shifted = jnp.concatenate([second[1:], jnp.zeros((1, width), F32)], axis=0)
    out = first + shifted
    if n16 < ncp:
        out = jnp.concatenate([out, jnp.zeros((ncp - n16, width), F32)], axis=0)
    out = out[:ncp]
    o_ref[0] = jnp.where(_iota((ncp, 1), 0) < n_blocks, out, 0.0)


def _compress(rows3, w_pos, length, ncp):
    n_seq, lp, width = rows3.shape
    assert lp % CMP_STRIDE == 0
    n_blocks = -(-length // CMP_STRIDE) - 1
    kern = functools.partial(_compress_kernel, n_blocks=n_blocks, ncp=ncp)
    return pl.pallas_call(
        kern, grid=(n_seq,),
        in_specs=[pl.BlockSpec((1, lp, width), lambda b: (b, 0, 0)),
                  pl.BlockSpec((1, CMP_LEN), lambda b: (0, 0))],
        out_specs=pl.BlockSpec((1, ncp, width), lambda b: (b, 0, 0)),
        out_shape=jax.ShapeDtypeStruct((n_seq, ncp, width), F32),
        compiler_params=_cp(("arbitrary",)), name="compress",
    )(rows3, w_pos.reshape(1, CMP_LEN))


def _masked_softmax(s, mask):
    s = jnp.where(mask, s, NEG)
    m = jnp.max(s, axis=-1, keepdims=True)
    e = jnp.where(mask, jnp.exp(s - m), 0.0)
    return e / jnp.maximum(jnp.sum(e, axis=-1, keepdims=True), 1e-30)


def _nsa_sel_kernel(q_ref, misc_ref, kc_ref, win_ref, ocw_ref, bias_ref, *, tq, nk, qoff, n_cmp, n_blk, nbp, n_top,
                    win_rows, win_dyn, win_pos0, block_out):
    tk = KEY_TILE
    i = pl.program_id(1)
    q0 = qoff + i * tq
    qpos = q0 + _iota((tq, 1), 0)
    nlim = jnp.minimum((q0 + tq - 1) // tk + 1, nk)
    qs = _stack_heads((q_ref[...] * (HEAD_DIM ** -0.5)).astype(BF16), HEAD_DIM)
    ncp = kc_ref.shape[1]

    kcv = kc_ref[0]
    kc = kcv[:, :HEAD_DIM].astype(BF16)
    vc = kcv[:, HEAD_DIM:].astype(BF16)
    cidx = _iota((1, ncp), 1)
    c_mask = ((cidx * CMP_STRIDE + CMP_LEN - 1) <= qpos) & (cidx < n_cmp)
    s_c = _dot_nt(qs, kc).reshape(N_HEADS, tq, ncp)
    p_c = _masked_softmax(s_c, c_mask[None])
    o_c = _dot(p_c.reshape(N_HEADS * tq, ncp).astype(BF16), vc).reshape(N_HEADS, tq, HEAD_DIM)

    c_start = _iota((ncp, 1), 0) * CMP_STRIDE
    s_start = _iota((1, nbp), 1) * SLC_BLOCK
    overlap = (c_start < s_start + SLC_BLOCK) & (c_start + CMP_LEN > s_start) & (_iota((ncp, 1), 0) < n_cmp)
    imp = _dot_x_exact01(p_c[0] + p_c[1] + p_c[2] + p_c[3], jnp.where(overlap, 1.0, 0.0).astype(BF16))
    blk = _iota((1, nbp), 1)
    cur = qpos // SLC_BLOCK
    forced = (blk == 0) | (blk == cur) | (blk == cur - 1)
    live = jnp.where(forced, jnp.inf, jnp.where((blk <= cur) & (blk < n_blk), imp, -jnp.inf))
    blk_f = blk.astype(F32)

    def pick(t, carry):
        live, sel = carry
        top = jnp.max(live, axis=-1, keepdims=True)
        first = jnp.min(jnp.where(live == top, blk_f, float(nbp)), axis=-1, keepdims=True)
        hit = blk_f == first
        return jnp.where(hit, -jnp.inf, live), jnp.where(hit, 1.0, sel)

    if block_out:
        blk_t = _iota((nbp, 1), 0).astype(F32)

        def pick_t(t, carry):
            live_t, sel_t = carry
            top = jnp.max(live_t, axis=0, keepdims=True)
            first = jnp.min(jnp.where(live_t == top, blk_t, float(nbp)), axis=0, keepdims=True)
            hit = blk_t == first
            return jnp.where(hit, -jnp.inf, live_t), jnp.where(hit, 1.0, sel_t)

        _, sel_t = lax.fori_loop(0, n_top, pick_t, (live.T, jnp.zeros((nbp, tq), F32)))
        bias_ref[...] = sel_t
    else:
        _, sel = lax.fori_loop(0, n_top, pick, (live, jnp.zeros((tq, nbp), F32)))
        sel = sel.astype(BF16)
        row_blk = _iota((nbp, 1), 0)

        def bias_tile(j, c):
            kpos = j * tk + _iota((1, tk), 1)
            expand = jnp.where(row_blk == kpos // SLC_BLOCK, 1.0, 0.0).astype(BF16)
            on = _dot(sel, expand)
            ok = (on > 0.5) & (kpos <= qpos)
            bias_ref[j] = jnp.where(ok, 0.0, NEG).astype(bias_ref.dtype)
            return c

        lax.fori_loop(0, nlim, bias_tile, 0)

        def fill(j, c):
            bias_ref[j] = jnp.full((tq, tk), NEG, bias_ref.dtype)
            return c

        lax.fori_loop(nlim, nk, fill, 0)

    if win_dyn:
        start = pl.multiple_of(jnp.maximum(q0 - WINDOW, 0), 8)
        wkv = win_ref[0, pl.ds(start, win_rows), :]
        kwpos = start + _iota((1, win_rows), 1)
    else:
        wkv = win_ref[0]
        kwpos = win_pos0 + _iota((1, win_rows), 1)
    kw = wkv[:, :HEAD_DIM].astype(BF16)
    vw = wkv[:, HEAD_DIM:].astype(BF16)
    rel = qpos - kwpos
    w_mask = (rel >= 0) & (rel < WINDOW) & (kwpos >= 0)
    s_w = _dot_nt(qs, kw).reshape(N_HEADS, tq, win_rows)
    p_w = _masked_softmax(s_w, w_mask[None])
    o_w = _dot(p_w.reshape(N_HEADS * tq, win_rows).astype(BF16), vw).reshape(N_HEADS, tq, HEAD_DIM)

    g = jax.nn.sigmoid(misc_ref[...])
    parts = []
    for h in range(N_HEADS):
        c = MISC_G + 3 * h
        parts.append(g[:, c:c + 1] * o_c[h] + g[:, c + 2:c + 3] * o_w[h])
    ocw_ref[...] = jnp.concatenate(parts, axis=1).astype(ocw_ref.dtype)


def _nsa_select(q_arr, p32, kcvc, win3, win_col, *, n_seq, t_seq, tq, qoff, seq_len, lp, win_dyn, win_pos0, bias_dtype,
                out_dtype):
    nk, nq = lp // KEY_TILE, t_seq // tq
    rows = n_seq * t_seq
    n_cmp = -(-seq_len // CMP_STRIDE) - 1
    n_blk = -(-seq_len // SLC_BLOCK)
    nbp = -(-n_blk // 128) * 128
    win_rows = (tq + WINDOW) if win_dyn else win3.shape[1]
    block_out = bias_dtype is None
    kern = functools.partial(_nsa_sel_kernel, tq=tq, nk=nk, qoff=qoff, n_cmp=n_cmp, n_blk=n_blk, nbp=nbp,
                             n_top=min(N_SLC, n_blk), win_rows=win_rows, win_dyn=win_dyn, win_pos0=win_pos0,
                             block_out=block_out)
    if block_out:
        sel_spec = pl.BlockSpec((nbp, tq), lambda b, i: (0, b * nq + i))
        sel_shape = jax.ShapeDtypeStruct((nbp, rows), F32)
    else:
        sel_spec = pl.BlockSpec((nk, tq, KEY_TILE), lambda b, i: (0, b * nq + i, 0))
        sel_shape = jax.ShapeDtypeStruct((nk, rows, KEY_TILE), bias_dtype)
    return pl.pallas_call(
        kern, grid=(n_seq, nq),
        in_specs=[pl.BlockSpec((tq, MIX_W), lambda b, i: (b * nq + i, _OFF['b_q'] // MIX_W)),
                  pl.BlockSpec((tq, 128), lambda b, i: (b * nq + i, _OFF['misc'] // 128)),
                  pl.BlockSpec((1,) + kcvc.shape[1:], lambda b, i: (b, 0, 0)),
                  pl.BlockSpec((1, win3.shape[1], 128), lambda b, i: (b, 0, win_col))],
        out_specs=[pl.BlockSpec((tq, MIX_W), lambda b, i: (b * nq + i, 0)), sel_spec],
        out_shape=[jax.ShapeDtypeStruct((rows, MIX_W), out_dtype), sel_shape],
        compiler_params=_cp(("arbitrary", "arbitrary")), name="nsa_select",
    )(q_arr, p32, kcvc, win3)


def _gla_tables(c):
    levels = []
    b = c
    while b >= 2:
        levels.append(b)
        b //= 2
    r = np.arange(c)
    mats = [(r[None, :] <= r[:, None]), (r[None, :] > r[:, None])]
    qm, km = [], []
    for b in levels:
        mid = (r // b) * b + b // 2
        upper = r >= mid
        qm.append(upper[:, None] & (r[None, :] >= mid[:, None]) & (r[None, :] <= r[:, None]))
        km.append((~upper)[:, None] & (r[None, :] > r[:, None]) & (r[None, :] < mid[:, None]))
    return np.concatenate(mats + qm + km, axis=0).astype(np.float32), levels


def _gla_kernel(q_ref, k_ref, v_ref, r_ref, misc_ref, wal_ref, bal_ref, gg_ref, m_ref, s0_ref, o_ref, st_ref, s_scr,
                *, tt, c, levels, n_tiles):
    t_idx = pl.program_id(1)
    nl = len(levels)

    @pl.when(t_idx == 0)
    def _():
        s_scr[...] = s0_ref[0]

    lane_head = _iota((1, MIX_W), 1) // HEAD_DIM
    same_head = (_iota((MIX_W, 1), 0) // HEAD_DIM) == lane_head
    eye = jnp.where(_iota((MIX_W, MIX_W), 0) == _iota((MIX_W, MIX_W), 1), 1.0, 0.0).astype(BF16)
    ones_head = jnp.where(same_head, 1.0, 0.0)
    rr, cc = _iota((c, c), 0), _iota((c, c), 1)
    pair_masks = [((rr // b) == (cc // b)) & ((rr % b) >= b // 2) & ((cc % b) < b // 2) for b in levels]
    diag_mask = rr == cc

    z = _dot_hi(misc_ref[...], wal_ref[...]) + bal_ref[...]
    la = (jnp.minimum(z, 0.0) - jnp.log1p(jnp.exp(-jnp.abs(z)))) * (1.0 / GATE_TAU)
    m_all = m_ref[...]

    def bd(x):
        return jnp.concatenate([jnp.where(lane_head == h, x, jnp.zeros_like(x)) for h in range(N_HEADS)], axis=0)

    def unbd(x):
        out = jnp.where(lane_head == 0, x[0:c], 0.0)
        for h in range(1, N_HEADS):
            out = out + jnp.where(lane_head == h, x[h * c:(h + 1) * c], 0.0)
        return out

    for ci in range(tt // c):
        sl = slice(ci * c, (ci + 1) * c)
        e = jnp.exp(_dot_exact01(m_all, la[sl]))
        q = q_ref[sl, :] * (HEAD_DIM ** -0.5)
        k = k_ref[sl, :]
        v = v_ref[sl, :].astype(BF16)
        e_cum, e_rest = e[0:c], e[c:2 * c]
        att = jnp.where(diag_mask[None], _dot_nt(bd(q.astype(BF16)), k.astype(BF16)).reshape(N_HEADS, c, c), 0.0)
        for li in range(nl):
            eq = e[(2 + li) * c:(3 + li) * c]
            ek = e[(2 + nl + li) * c:(3 + nl + li) * c]
            a = _dot_nt(bd((q * eq).astype(BF16)), (k * ek).astype(BF16)).reshape(N_HEADS, c, c)
            att = att + jnp.where(pair_masks[li][None], a, 0.0)
        o_intra = unbd(_dot(att.reshape(N_HEADS * c, c).astype(BF16), v))
        st = s_scr[...]
        o_inter = _dot_nt((q * e_cum).astype(BF16), st.astype(BF16))
        v_t = _dot_nt(eye, v).astype(BF16)
        upd = _dot(v_t, (k * e_rest).astype(BF16))
        s_scr[...] = st * e_cum[c - 1:c, :] + jnp.where(same_head, upd, 0.0)
        o = o_inter + o_intra
        ms = _dot_hi(o * o, ones_head) * (1.0 / HEAD_DIM)
        o = o * lax.rsqrt(ms + EPS) * gg_ref[...]
        r = r_ref[sl, :]
        o_ref[sl, :] = (o * (r * jax.nn.sigmoid(r))).astype(o_ref.dtype)

    @pl.when(t_idx == n_tiles - 1)
    def _():
        st_ref[0] = s_scr[...]


def _gla(p32, wal_pad, b_alpha, g_gla4, st0, *, n_seq, t_seq, tt, out_dtype):
    c = min(GLA_CHUNK, t_seq)
    m_all, levels = _gla_tables(c)
    n_tiles = t_seq // tt
    rows = n_seq * t_seq
    col = lambda name: pl.BlockSpec((tt, MIX_W), lambda b, t, o=_OFF[name] // MIX_W: (b * n_tiles + t, o))
    kern = functools.partial(_gla_kernel, tt=tt, c=c, levels=levels, n_tiles=n_tiles)
    return pl.pallas_call(
        kern, grid=(n_seq, n_tiles),
        in_specs=[col('d_q'), col('d_k'), col('d_v'), col('d_r'),
                  pl.BlockSpec((tt, 128), lambda b, t: (b * n_tiles + t, _OFF['misc'] // 128)),
                  pl.BlockSpec((128, MIX_W), lambda b, t: (0, 0)),
                  pl.BlockSpec((1, MIX_W), lambda b, t: (0, 0)),
                  pl.BlockSpec((1, MIX_W), lambda b, t: (0, 0)),
                  pl.BlockSpec(m_all.shape, lambda b, t: (0, 0)),
                  pl.BlockSpec((1, MIX_W, MIX_W), lambda b, t: (b, 0, 0))],
        out_specs=[pl.BlockSpec((tt, MIX_W), lambda b, t: (b * n_tiles + t, 0)),
                   pl.BlockSpec((1, MIX_W, MIX_W), lambda b, t: (b, 0, 0))],
        out_shape=[jax.ShapeDtypeStruct((rows, MIX_W), out_dtype),
                   jax.ShapeDtypeStruct((n_seq, MIX_W, MIX_W), F32)],
        scratch_shapes=[pltpu.VMEM((MIX_W, MIX_W), F32)],
        compiler_params=_cp(("arbitrary", "arbitrary")), name="gla",
    )(p32, p32, p32, p32, p32, wal_pad, b_alpha.reshape(1, MIX_W), g_gla4, jnp.asarray(m_all, dtype=BF16), st0)


def _dwconv3(u, w, carry_ref, fix_refs, t_seq, tm, first_tile):
    row = _iota((tm, 1), 0)
    u1 = pltpu.roll(u, 1, axis=0)
    u2 = pltpu.roll(u, 2, axis=0)
    if fix_refs is None:
        prev = jnp.where(first_tile, 0.0, carry_ref[0:2, :])
        u1 = jnp.where(row == 0, prev[1:2], u1)
        u2 = jnp.where(row == 0, prev[0:1], jnp.where(row == 1, prev[1:2], u2))
        carry_ref[0:2, :] = u[tm - 2:tm]
    else:
        pos = row % t_seq
        u1 = jnp.where(pos == 0, fix_refs[0][...], u1)
        u2 = jnp.where(pos < 2, fix_refs[1][...], u2)
    return w[0:1] * u2 + w[1:2] * u1 + w[2:3] * u


def _merge_kernel(oa_ref, ob_ref, od_ref, cin_ref, cb_ref, cc_ref, gate_ref, x_ref, g1_ref, wconv_ref, wb_ref,
                  wo_ref, *rest, tm, t_seq, tiles_per_seq, per_row):
    if per_row:
        fix1_ref, fix2_ref, o_ref, carry_ref = rest
        fix = (fix1_ref, fix2_ref)
    else:
        o_ref, carry_ref = rest
        fix = None
    first = (pl.program_id(0) % tiles_per_seq) == 0
    u = cc_ref[...] * cin_ref[...]
    o_c = cb_ref[...] * _dwconv3(u, wconv_ref[...], carry_ref, fix, t_seq, tm, first)
    branches = (oa_ref[...], ob_ref[...], o_c, od_ref[...])
    merged = jnp.zeros((tm, D_MODEL), F32)
    for bi, br in enumerate(branches):
        gate = jax.nn.sigmoid(gate_ref[:, bi * D_MODEL:(bi + 1) * D_MODEL])
        merged = merged + gate * _dot(br.astype(BF16), wb_ref[bi])
    o_ref[...] = x_ref[...] + g1_ref[0] * _dot(merged.astype(BF16), wo_ref[...])


def _merge(oa, ob, od, p32, x, g1, conv_c, wb, wo, fix, *, tm, t_seq):
    rows = x.shape[0]
    per_row = fix is not None
    tiles_per_seq = max(t_seq // tm, 1)
    rowblk = lambda width, colblk=0: pl.BlockSpec((tm, width), lambda i: (i, colblk))
    in_specs = [rowblk(MIX_W), rowblk(MIX_W), rowblk(MIX_W),
                rowblk(MIX_W, _OFF['c_in'] // MIX_W), rowblk(MIX_W, _OFF['c_b'] // MIX_W),
                rowblk(MIX_W, _OFF['c_c'] // MIX_W), rowblk(N_HEADS * D_MODEL, 0), rowblk(D_MODEL),
                _mod_spec(g1, tm, t_seq), _const_spec((CONV_W, MIX_W)), _const_spec((N_HEADS, MIX_W, D_MODEL)),
                _const_spec((D_MODEL, D_MODEL))]
    args = [oa, ob, od, p32, p32, p32, p32, x, g1, conv_c, wb, wo]
    if per_row:
        in_specs += [rowblk(MIX_W), rowblk(MIX_W)]
        args += list(fix)
    kern = functools.partial(_merge_kernel, tm=tm, t_seq=t_seq, tiles_per_seq=tiles_per_seq, per_row=per_row)
    return pl.pallas_call(
        kern, grid=(rows // tm,), in_specs=in_specs, out_specs=rowblk(D_MODEL),
        out_shape=jax.ShapeDtypeStruct((rows, D_MODEL), F32),
        scratch_shapes=[pltpu.VMEM((8, MIX_W), F32)],
        compiler_params=_cp(("arbitrary",)), name="merge",
    )(*args)


def _ffn_kernel(x_ref, gn_ref, sc_ref, sh_ref, g2_ref, wa_ref, wg_ref, wconv_ref, bf_ref, wout_ref, gf_ref, *rest,
                tm, tf, t_seq, tiles_per_seq, per_row):
    if per_row:
        fix1_ref, fix2_ref, o_ref, y_ref, a_ref, carry_ref = rest
    else:
        o_ref, y_ref, a_ref, carry_ref = rest
    first = (pl.program_id(0) % tiles_per_seq) == 0
    x = x_ref[...]
    h = _norm_mod(x, gn_ref[...], sc_ref[0], sh_ref[0]).astype(BF16)
    acc = jnp.zeros((tm, D_MODEL), F32)
    for f in range(D_FF // tf):
        fs = slice(f * tf, (f + 1) * tf)
        a = _dot(h, wa_ref[:, fs])
        g = _dot(h, wg_ref[:, fs])
        a_ref[:, fs] = a
        fix = (fix1_ref.at[:, fs], fix2_ref.at[:, fs]) if per_row else None
        conv = _dwconv3(a, wconv_ref[:, fs], carry_ref.at[:, fs], fix, t_seq, tm, first)
        pre = conv + bf_ref[:, fs]
        act = pre * jax.nn.sigmoid(pre) * g
        acc = acc + _dot(act.astype(BF16), wout_ref[fs, :])
    xn = x + g2_ref[0] * acc
    o_ref[...] = xn
    y_ref[...] = xn * lax.rsqrt(jnp.mean(xn * xn, axis=-1, keepdims=True) + EPS) * gf_ref[...]


def _ffn(x, gn, sc, sh, g2, wa, wg, conv_ffn, b_ffn, wout, g_final, fix, *, tm, t_seq):
    rows = x.shape[0]
    per_row = fix is not None
    tiles_per_seq = max(t_seq // tm, 1)
    tf = 256
    rowblk = lambda width: pl.BlockSpec((tm, width), lambda i: (i, 0))
    in_specs = [rowblk(D_MODEL), _const_spec((1, D_MODEL)), _mod_spec(sc, tm, t_seq), _mod_spec(sh, tm, t_seq),
                _mod_spec(g2, tm, t_seq), _const_spec((D_MODEL, D_FF)), _const_spec((D_MODEL, D_FF)),
                _const_spec((CONV_W, D_FF)), _const_spec((1, D_FF)), _const_spec((D_FF, D_MODEL)),
                _const_spec((1, D_MODEL))]
    args = [x, gn, sc, sh, g2, wa, wg, conv_ffn, b_ffn, wout, g_final]
    if per_row:
        in_specs += [rowblk(D_FF), rowblk(D_FF)]
        args += list(fix)
    kern = functools.partial(_ffn_kernel, tm=tm, tf=tf, t_seq=t_seq, tiles_per_seq=tiles_per_seq, per_row=per_row)
    return pl.pallas_call(
        kern, grid=(rows // tm,), in_specs=in_specs,
        out_specs=[rowblk(D_MODEL), rowblk(D_MODEL), rowblk(D_FF)],
        out_shape=[jax.ShapeDtypeStruct((rows, D_MODEL), F32), jax.ShapeDtypeStruct((rows, D_MODEL), F32),
                   jax.ShapeDtypeStruct((rows, D_FF), F32)],
        scratch_shapes=[pltpu.VMEM((8, D_FF), F32)],
        compiler_params=_cp(("arbitrary",)), name="ffn",
    )(*args)


def _permute_w_in(w):
    offs, o = {}, 0
    for name, n in _IN_SPLITS:
        offs[name] = (o, n)
        o += n

    def c(name):
        s, n = offs[name]
        return w[:, s:s + n]

    pieces = [c('gate'), c('a_k'), c('a_v'), c('a_q'), c('a_qi'), c('b_q'), c('b_cmp'), c('b_slc'), c('b_win'),
              c('a_ki'), c('a_wi'), c('b_g'), c('d_a'), jnp.zeros((w.shape[0], 32), w.dtype),
              c('c_in'), c('c_b'), c('c_c'), c('d_q'), c('d_k'), c('d_v'), c('d_r')]
    return jnp.concatenate(pieces, axis=1).astype(BF16)


def _cols(p, name, width):
    return p[:, _OFF[name]:_OFF[name] + width]


def _state_to_bd(s0):
    b = s0.shape[0]
    s0t = jnp.swapaxes(s0, 2, 3)
    eye = jnp.eye(N_HEADS, dtype=s0.dtype)[None, :, None, :, None]
    return (s0t[:, :, :, None, :] * eye).reshape(b, MIX_W, MIX_W)


def _bd_to_state(st):
    b = st.shape[0]
    st5 = st.reshape(b, N_HEADS, HEAD_DIM, N_HEADS, HEAD_DIM)
    return jnp.stack([jnp.swapaxes(st5[:, h, :, h, :], 1, 2) for h in range(N_HEADS)], axis=1)


def _conv_fix(state, t_seq):
    b, _, c = state.shape
    fix1 = jnp.concatenate([state[:, 1:2], jnp.zeros((b, t_seq - 1, c), state.dtype)], axis=1)
    fix2 = jnp.concatenate([state, jnp.zeros((b, t_seq - 2, c), state.dtype)], axis=1)
    return fix1.reshape(b * t_seq, c), fix2.reshape(b * t_seq, c)


def _layer_weights(l, w_in, w_alpha, g_gla, w_branch, w_out, w_ffn_in, w_ffn_out):
    wal_pad = jnp.zeros((128, MIX_W), F32).at[MISC_DA:MISC_DA + GATE_RANK].set(w_alpha[l])
    return dict(w_in=_permute_w_in(w_in[l]), wal=wal_pad, gg=jnp.tile(g_gla[l], N_HEADS).reshape(1, MIX_W),
                wb=w_branch[l].astype(BF16), wo=w_out[l].astype(BF16),
                wa=w_ffn_in[l][:, :D_FF].astype(BF16), wg=w_ffn_in[l][:, D_FF:].astype(BF16),
                wout=w_ffn_out[l].astype(BF16))


def kernel(x_prompt, x_sample, cache_a_kv, cache_a_idx, cache_b_cmp, cache_b_slc, state_b_win, state_c_conv, state_d_gla, state_ffn_conv, page_table, c_prompt, c_sample, w_ada, b_ada, g_norm1, w_in, w_cmp_pos, conv_c, w_alpha, b_alpha, g_gla, w_branch, w_out, g_norm2, w_ffn_in, conv_ffn, b_ffn, w_ffn_out, g_final):
    bp, seq, d = x_prompt.shape
    bs, tdec, _ = x_sample.shape
    depth = w_ada.shape[0]
    n_pages = page_table.shape[1]
    past = n_pages * PAGE
    w_buf = state_b_win.shape[2]
    assert bp == 1 and d == D_MODEL and seq % 1024 == 0 and tdec == 8 and w_buf == WINDOW

    n_c = bp + bs
    c_all = jnp.concatenate([c_prompt, c_sample, jnp.zeros((-n_c % 8, d), F32)], axis=0)
    mod = _ada(c_all, w_ada, b_ada)

    xp = x_prompt.reshape(bp * seq, d)
    xs = x_sample.reshape(bs * tdec, d)
    rs = bs * tdec
    tail_rows = KEY_TILE
    lps = past + tail_rows
    tkm_s = max(t for t in (1, 2, 3, 4, 6, 8, 11) if (lps // KEY_TILE) % t == 0)
    gfin = g_final.reshape(1, d)
    outs_p, outs_s = [], []
    y_p = y_s = None
    for l in range(depth):
        lw = _layer_weights(l, w_in, w_alpha, g_gla, w_branch, w_out, w_ffn_in, w_ffn_out)
        mp = [mod[l, :bp, k * d:(k + 1) * d].reshape(bp, 1, d) for k in range(6)]
        ms = [jnp.repeat(mod[l, bp:n_c, k * d:(k + 1) * d], tdec, axis=0).reshape(1, rs, d) for k in range(6)]
        gn1, gn2 = g_norm1[l].reshape(1, d), g_norm2[l].reshape(1, d)
        bffn = b_ffn[l].reshape(1, D_FF)

        p32, _ = _proj(xs, gn1, ms[1], ms[0], lw['w_in'], rs, rs)
        new = {n: _cols(p32, n, w).reshape(bs, tdec, w) for n, w in
               (('a_k', 2 * MIX_W), ('misc', IDX_DIM), ('b_cmp', 128), ('b_slc', 128), ('b_win', 128))}
        tails = [jnp.concatenate([new[n], jnp.zeros((bs, tail_rows - tdec, new[n].shape[-1]), F32)], axis=1)
                 for n in ('a_k', 'misc', 'b_cmp', 'b_slc')]
        kv_s, ki_s, cmp_s, slc_s = _gather_past(page_table, [cache_a_kv, cache_a_idx, cache_b_cmp, cache_b_slc],
                                                tails, (BF16, BF16, F32, BF16), l)
        seq_s = past + tdec
        bias_a = _dsa_select(p32, p32, ki_s, 0, IDX_DIM, n_seq=bs, t_seq=tdec, tq=tdec, qoff=past,
                             k_sel=min(A_TOPK, seq_s // 4), bias_dtype=F32)
        o_a = _flash(p32, _OFF['a_q'] // MIX_W, kv_s, 0, bias_a, n_seq=bs, t_seq=tdec, tq=tdec, tkm=tkm_s, qoff=past,
                     shared=False, out_dtype=F32)
        ncp = -(-(-(-seq_s // CMP_STRIDE) - 1) // 128) * 128
        kcvc = _compress(cmp_s, w_cmp_pos[l], seq_s, ncp)
        win_full = jnp.concatenate([state_b_win[l], new['b_win']], axis=1)
        win_pad = jnp.concatenate([win_full, jnp.zeros((bs, -(w_buf + tdec) % 16, 128), F32)], axis=1)
        ocw, bias_b = _nsa_select(p32, p32, kcvc, win_pad, 0, n_seq=bs, t_seq=tdec, tq=tdec, qoff=past, seq_len=seq_s,
                                  lp=lps, win_dyn=False, win_pos0=past - w_buf, bias_dtype=F32, out_dtype=F32)
        o_b = _flash(p32, _OFF['b_q'] // MIX_W, slc_s, 0, bias_b, n_seq=bs, t_seq=tdec, tq=tdec, tkm=tkm_s, qoff=past,
                     shared=True, out_dtype=F32, gate_args=(p32, ocw))
        o_d, st = _gla(p32, lw['wal'], b_alpha[l], lw['gg'], _state_to_bd(state_d_gla[l]), n_seq=bs, t_seq=tdec,
                       tt=tdec, out_dtype=F32)
        u_tail = (_cols(p32, 'c_c', MIX_W) * _cols(p32, 'c_in', MIX_W)).reshape(bs, tdec, MIX_W)[:, tdec - 2:]
        xs = _merge(o_a, o_b, o_d, p32, xs, ms[2], conv_c[l], lw['wb'], lw['wo'], _conv_fix(state_c_conv[l], tdec),
                    tm=rs, t_seq=tdec)
        xs, y_s, a_full = _ffn(xs, gn2, ms[4], ms[3], ms[5], lw['wa'], lw['wg'], conv_ffn[l], bffn, lw['wout'], gfin,
                               _conv_fix(state_ffn_conv[l], tdec), tm=rs, t_seq=tdec)
        outs_s.append((new['a_k'], new['misc'], new['b_cmp'], new['b_slc'], win_full[:, tdec:], u_tail,
                       _bd_to_state(st), a_full.reshape(bs, tdec, D_FF)[:, tdec - 2:]))

        p32, p16 = _proj(xp, gn1, mp[1], mp[0], lw['w_in'], 1024, seq)
        p16_3 = p16.reshape(bp, seq, NP)
        a_kv = _cols(p32, 'a_k', 2 * MIX_W).reshape(bp, seq, 2 * MIX_W)
        a_idx = _cols(p32, 'misc', IDX_DIM).reshape(bp, seq, IDX_DIM)
        b_cmp = _cols(p32, 'b_cmp', 128).reshape(bp, seq, 128)
        b_slc = _cols(p32, 'b_slc', 128).reshape(bp, seq, 128)
        b_win = _cols(p32, 'b_win', 128).reshape(bp, seq, 128)[:, seq - min(WINDOW, seq):]
        k_sel = min(A_TOPK, seq // 4)
        bias_a = _dsa_select(p16, p32, p16_3, _OFF['misc'] // 128, 128, n_seq=bp, t_seq=seq, tq=128, qoff=0,
                             k_sel=k_sel, bias_dtype=BF16)
        q_scale = HEAD_DIM ** -0.5 * float(np.log2(np.e))
        o_a = jnp.transpose(_flash_t(jnp.transpose(_cols(p32, 'a_q', MIX_W) * q_scale).astype(BF16), p16,
                                     _OFF['a_k'] // MIX_W, MIX_W,
                                     jnp.transpose(_cols(p16, 'a_v', MIX_W)), bias_a, t_seq=seq, tq=512, tkm=2,
                                     shared=False))
        ncp = -(-(-(-seq // CMP_STRIDE) - 1) // 128) * 128
        kcvc = _compress(b_cmp, w_cmp_pos[l], seq, ncp)
        ocw, sel_b = _nsa_select(p16, p32, kcvc, p16_3, _OFF['b_win'] // 128, n_seq=bp, t_seq=seq, tq=128, qoff=0,
                                 seq_len=seq, lp=seq, win_dyn=True, win_pos0=0, bias_dtype=None, out_dtype=BF16)
        vt_b = jnp.transpose(p16[:, _OFF['b_slc'] + HEAD_DIM:_OFF['b_slc'] + 2 * HEAD_DIM])
        o_b = jnp.transpose(_flash_t(jnp.transpose(_cols(p32, 'b_q', MIX_W) * q_scale).astype(BF16), p16,
                                     _OFF['b_slc'] // 128, 128, vt_b,
                                     sel_b, t_seq=seq, tq=512, tkm=2, shared=True,
                                     gate_args=(jnp.transpose(_cols(p32, 'misc', 128)), jnp.transpose(ocw))))
        o_d, st = _gla(p32, lw['wal'], b_alpha[l], lw['gg'], jnp.zeros((bp, MIX_W, MIX_W), F32), n_seq=bp, t_seq=seq,
                       tt=512, out_dtype=BF16)
        u_tail = (_cols(p32, 'c_c', MIX_W) * _cols(p32, 'c_in', MIX_W)).reshape(bp, seq, MIX_W)[:, seq - 2:]
        xp = _merge(o_a, o_b, o_d, p32, xp, mp[2], conv_c[l], lw['wb'], lw['wo'], None, tm=256, t_seq=seq)
        xp, y_p, a_full = _ffn(xp, gn2, mp[4], mp[3], mp[5], lw['wa'], lw['wg'], conv_ffn[l], bffn, lw['wout'], gfin,
                               None, tm=512, t_seq=seq)
        outs_p.append((a_kv, a_idx, b_cmp, b_slc, b_win, u_tail, _bd_to_state(st),
                       a_full.reshape(bp, seq, D_FF)[:, seq - 2:]))

    sp = [jnp.stack(z) for z in zip(*outs_p)]
    ss = [jnp.stack(z) for z in zip(*outs_s)]
    res = [y_p.reshape(bp, seq, d), y_s.reshape(bs, tdec, d)]
    for a, b in zip(sp, ss):
        res += [a, b]
    return tuple(res)
```

```python
import functools

import numpy as np
import jax
import jax.numpy as jnp
from jax import lax
from jax.experimental import pallas as pl
from jax.experimental.pallas import tpu as pltpu

F32 = jnp.float32
BF16 = jnp.bfloat16
HI = lax.Precision.HIGHEST

D_MODEL = 1024
PAGE = 128
HEAD_DIM = 64
MIX_W = 256
N_HEADS = 4
IDX_DIM = 64
A_TOPK = 256
CMP_LEN = 32
CMP_STRIDE = 16
SLC_BLOCK = 64
N_SLC = 16
WINDOW = 512
CONV_W = 3
GATE_RANK = 16
GATE_TAU = 16.0
GLA_CHUNK = 64
D_FF = 2816
EPS = 1e-6
NEG = -1e30
INT_MIN = -2 ** 31
KEY_TILE = 512

_IN_SPLITS = (('a_q', 256), ('a_k', 256), ('a_v', 256), ('a_qi', 256), ('a_ki', 64), ('a_wi', 4),
              ('b_q', 256), ('b_cmp', 128), ('b_slc', 128), ('b_win', 128), ('b_g', 12),
              ('c_in', 256), ('c_b', 256), ('c_c', 256),
              ('d_q', 256), ('d_k', 256), ('d_v', 256), ('d_r', 256), ('d_a', 16), ('gate', 4096))
_OFF = dict(gate=0, a_k=4096, a_v=4352, a_q=4608, a_qi=4864, b_q=5120, b_cmp=5376, b_slc=5504, b_win=5632,
            misc=5760, c_in=5888, c_b=6144, c_c=6400, d_q=6656, d_k=6912, d_v=7168, d_r=7424)
NP = 7680
MISC_KI, MISC_WI, MISC_G, MISC_DA = 0, 64, 68, 80
VMEM_LIMIT = 56 * 1024 * 1024


def _cp(sem):
    return pltpu.CompilerParams(dimension_semantics=sem, vmem_limit_bytes=VMEM_LIMIT)


def _dot(a, b):
    return jnp.dot(a, b, preferred_element_type=F32)


def _dot_nt(a, b):
    return lax.dot_general(a, b, (((1,), (1,)), ((), ())), preferred_element_type=F32)


def _dot_hi(a, b):
    return jnp.dot(a, b, preferred_element_type=F32, precision=HI)


def _dot_exact01(a01, x):
    hi = x.astype(BF16)
    lo = (x - hi.astype(F32)).astype(BF16)
    return _dot(a01, hi) + _dot(a01, lo)


def _dot_x_exact01(x, b01):
    hi = x.astype(BF16)
    lo = (x - hi.astype(F32)).astype(BF16)
    return _dot(hi, b01) + _dot(lo, b01)


def _const_spec(shape):
    nd = len(shape)
    return pl.BlockSpec(shape, lambda *a: (0,) * nd, pipeline_mode=pl.Buffered(1))


def _iota(shape, dim):
    return lax.broadcasted_iota(jnp.int32, shape, dim)


def _ada_kernel(c_ref, w_ref, b_ref, o_ref):
    o_ref[0] = _dot_hi(c_ref[...], w_ref[0]) + b_ref[0]


def _ada(c_all, w_ada, b_ada):
    depth, d, n6 = w_ada.shape
    rows = c_all.shape[0]
    tn = 1024
    return pl.pallas_call(
        _ada_kernel, grid=(depth, n6 // tn),
        in_specs=[pl.BlockSpec((rows, d), lambda l, j: (0, 0)),
                  pl.BlockSpec((1, d, tn), lambda l, j: (l, 0, j)),
                  pl.BlockSpec((1, 1, tn), lambda l, j: (l, 0, j))],
        out_specs=pl.BlockSpec((1, rows, tn), lambda l, j: (l, 0, j)),
        out_shape=jax.ShapeDtypeStruct((depth, rows, n6), F32),
        compiler_params=_cp(("arbitrary", "arbitrary")), name="ada",
    )(c_all, w_ada, b_ada.reshape(depth, 1, n6))


def _norm_mod(x, g, sc, sh):
    y = x * lax.rsqrt(jnp.mean(x * x, axis=-1, keepdims=True) + EPS) * g
    return y * (1.0 + sc) + sh


def _proj_kernel(x_ref, g_ref, sc_ref, sh_ref, w_ref, o32_ref, o16_ref, h_ref):
    @pl.when(pl.program_id(1) == 0)
    def _():
        h_ref[...] = _norm_mod(x_ref[...], g_ref[...], sc_ref[0], sh_ref[0]).astype(BF16)

    acc = _dot(h_ref[...], w_ref[...])
    o32_ref[...] = acc
    o16_ref[...] = acc.astype(BF16)


def _mod_spec(mod, tm, rows_per_group):
    mb = mod.shape[1]
    tiles = max(rows_per_group // tm, 1)
    return pl.BlockSpec((1, mb, D_MODEL), lambda i, *_: (i // tiles, 0, 0))


def _proj(x, g, sc, sh, w, tm, rows_per_group):
    rows = x.shape[0]
    tn = 768
    return pl.pallas_call(
        _proj_kernel, grid=(rows // tm, NP // tn),
        in_specs=[pl.BlockSpec((tm, D_MODEL), lambda i, j: (i, 0)),
                  pl.BlockSpec((1, D_MODEL), lambda i, j: (0, 0)),
                  _mod_spec(sc, tm, rows_per_group), _mod_spec(sh, tm, rows_per_group),
                  pl.BlockSpec((D_MODEL, tn), lambda i, j: (0, j))],
        out_specs=[pl.BlockSpec((tm, tn), lambda i, j: (i, j)), pl.BlockSpec((tm, tn), lambda i, j: (i, j))],
        out_shape=[jax.ShapeDtypeStruct((rows, NP), F32), jax.ShapeDtypeStruct((rows, NP), BF16)],
        scratch_shapes=[pltpu.VMEM((tm, D_MODEL), BF16)],
        compiler_params=_cp(("arbitrary", "arbitrary")), name="proj",
    )(x, g, sc, sh, w)


PAGES_PER_STEP = KEY_TILE // PAGE


def _gather_kernel(pt_ref, *refs, n_arr, n_chunks):
    ppc = PAGES_PER_STEP
    pages, tails, outs = refs[:n_arr * ppc], refs[n_arr * ppc:n_arr * (ppc + 1)], refs[n_arr * (ppc + 1):]
    c = pl.program_id(1)

    @pl.when(c < n_chunks - 1)
    def _():
        for k in range(n_arr):
            for r in range(ppc):
                outs[k][0, r * PAGE:(r + 1) * PAGE, :] = pages[k * ppc + r][0, 0].astype(outs[k].dtype)

    @pl.when(c == n_chunks - 1)
    def _():
        for k in range(n_arr):
            outs[k][0] = tails[k][0].astype(outs[k].dtype)


def _gather_past(page_table, pools, tails, out_dtypes, layer):
    n_seq, n_pages = page_table.shape
    n_arr, ppc = len(pools), PAGES_PER_STEP
    assert n_pages % ppc == 0 and tails[0].shape[1] == ppc * PAGE
    n_chunks = n_pages // ppc + 1
    lp = n_chunks * ppc * PAGE
    in_specs = []
    for p in pools:
        for r in range(ppc):
            in_specs.append(pl.BlockSpec(
                (1, 1, PAGE, p.shape[-1]),
                lambda b, c, pt, r=r: (layer, pt[b, jnp.minimum(c * ppc + r, n_pages - 1)], 0, 0)))
    in_specs += [pl.BlockSpec((1, ppc * PAGE, t.shape[-1]), lambda b, c, pt: (b, 0, 0)) for t in tails]
    args = [p for p in pools for _ in range(ppc)] + list(tails)
    kern = functools.partial(_gather_kernel, n_arr=n_arr, n_chunks=n_chunks)
    return pl.pallas_call(
        kern,
        grid_spec=pltpu.PrefetchScalarGridSpec(
            num_scalar_prefetch=1, grid=(n_seq, n_chunks), in_specs=in_specs,
            out_specs=[pl.BlockSpec((1, ppc * PAGE, p.shape[-1]), lambda b, c, pt: (b, c, 0)) for p in pools]),
        out_shape=[jax.ShapeDtypeStruct((n_seq, lp, p.shape[-1]), dt) for p, dt in zip(pools, out_dtypes)],
        compiler_params=_cp(("arbitrary", "arbitrary")), name="gather_past",
    )(page_table, *args)


def _stack_heads(q, width):
    t = q.shape[0]
    parts = []
    for h in range(N_HEADS):
        p = q[:, h * HEAD_DIM:(h + 1) * HEAD_DIM]
        if width > HEAD_DIM:
            p = jnp.concatenate([p, jnp.zeros((t, width - HEAD_DIM), q.dtype)], axis=1)
        parts.append(p)
    return jnp.concatenate(parts, axis=0)


def _dsa_sel_kernel(qi_ref, misc_ref, ki_ref, bias_ref, s_ref, *planes, tq, nk, k_sel, qoff, fk):
    tk = KEY_TILE
    i = pl.program_id(1)
    q0 = qoff + i * tq
    qpos = q0 + _iota((tq, 1), 0)
    nlim = jnp.minimum((q0 + tq - 1) // tk + 1, nk)
    qs = _stack_heads(qi_ref[...].astype(BF16), fk)
    wi = misc_ref[:, MISC_WI:MISC_WI + N_HEADS]

    def score_tile(j, c):
        kt = ki_ref[0, pl.ds(pl.multiple_of(j * tk, tk), tk), :].astype(BF16)
        d = _dot_nt(qs, kt).reshape(N_HEADS, tq, tk)
        sc = wi[:, 0:1] * jnp.maximum(d[0], 0.0)
        for h in range(1, N_HEADS):
            sc = sc + wi[:, h:h + 1] * jnp.maximum(d[h], 0.0)
        bits = lax.bitcast_convert_type(sc, jnp.int32)
        key = bits ^ ((bits >> 31) & 0x7FFFFFFF)
        key = jnp.where(key == -1, 0, key)
        kpos = j * tk + _iota((1, tk), 1)
        s_ref[j] = jnp.where(kpos <= qpos, key, INT_MIN)
        return c

    if nk % 2 == 0:
        def score_pair(j2, c):
            score_tile(2 * j2, c)
            return score_tile(2 * j2 + 1, c)

        lax.fori_loop(0, (nlim + 1) // 2, score_pair, 0)
    else:
        lax.fori_loop(0, nlim, score_tile, 0)

    if planes:
        p_ref, cand_ref = planes

        def unused_tile(j, c):
            s_ref[j] = jnp.full((tq, tk), INT_MIN, jnp.int32)
            return c

        lax.fori_loop(nlim, nk, unused_tile, 0)

        def transpose_bits(rg, c):
            rows = pl.ds(pl.multiple_of(rg * 8, 8), 8)
            for lc in range(tk // 128):
                lanes = slice(lc * 128, (lc + 1) * 128)
                x = [s_ref[j, rows, lanes] ^ INT_MIN if j < nk else jnp.zeros((8, 128), jnp.int32) for j in range(32)]
                s, m = 16, 0x0000FFFF
                while s >= 1:
                    for a in range(32):
                        if (a & s) == 0:
                            t = (lax.shift_right_logical(x[a], s) ^ x[a + s]) & m
                            x[a + s] = x[a + s] ^ t
                            x[a] = x[a] ^ lax.shift_left(t, s)
                    s //= 2
                    if s:
                        m = (m ^ (m << s)) & 0xFFFFFFFF
                        m = m - (1 << 32) if m >= (1 << 31) else m
                for b in range(32):
                    p_ref[b, rows, lanes] = x[b]
            return c

        lax.fori_loop(0, tq // 8, transpose_bits, 0)
        cand_ref[...] = jnp.full((tq, tk), -1, jnp.int32)

        def lane_total(words):
            pc = lax.population_count(words)
            part = pc[:, 0:128]
            for c in range(1, tk // 128):
                part = part + pc[:, c * 128:(c + 1) * 128]
            return jnp.sum(part.astype(F32), axis=-1, keepdims=True)

        def bit_step(t, carry):
            need, thr_u = carry
            b = 31 - t
            ones = cand_ref[...] & p_ref[b]
            cnt = lane_total(ones)
            take = cnt >= need
            cand_ref[...] = jnp.where(take, ones, cand_ref[...] & ~p_ref[b])
            return jnp.where(take, need, need - cnt), jnp.where(take, thr_u | lax.shift_left(jnp.int32(1), b), thr_u)

        ties_allowed, thr_u = lax.fori_loop(
            0, 32, bit_step, (jnp.full((tq, 1), float(k_sel), F32), jnp.zeros((tq, 1), jnp.int32)))
        thr = thr_u ^ INT_MIN
        need_rank = jnp.max(jnp.where(lane_total(cand_ref[...]) > ties_allowed, 1.0, 0.0)) > 0.5
        ties_allowed_fn = lambda: ties_allowed
    else:
        def count(pred, level):
            level_b = jnp.broadcast_to(level, (tq, 128))

            def body(j, acc):
                for c in range(tk // 128):
                    acc = acc + jnp.where(pred(s_ref[j, :, c * 128:(c + 1) * 128], level_b), 1.0, 0.0)
                return acc
            return jnp.sum(lax.fori_loop(0, nlim, body, jnp.zeros((tq, 128), F32)), axis=-1, keepdims=True)

        def bit_step(t, carry):
            c, n_ge = carry
            trial = c + lax.shift_left(jnp.int32(1), 31 - t)
            cnt = count(lambda s, lv: s >= lv, trial)
            ok = cnt >= k_sel
            return jnp.where(ok, trial, c), jnp.where(ok, cnt, n_ge)

        thr, n_ge = lax.fori_loop(0, 32, bit_step, (jnp.full((tq, 1), INT_MIN, jnp.int32),
                                                    jnp.full((tq, 1), float(nk * tk), F32)))
        need_rank = jnp.max(jnp.where(n_ge > k_sel, 1.0, 0.0)) > 0.5
        ties_allowed_fn = lambda: k_sel - count(lambda s, lv: s > lv, thr)

    @pl.when(need_rank)
    def _():
        ties_allowed = ties_allowed_fn()
        tri = jnp.where(_iota((tk, tk), 0) <= _iota((tk, tk), 1), 1.0, 0.0).astype(BF16)

        def out_tile(j, carry):
            key = s_ref[j]
            eq = jnp.where(key == thr, 1.0, 0.0)
            rank = _dot(eq.astype(BF16), tri) + carry
            kpos = j * tk + _iota((1, tk), 1)
            take = jnp.where(key > thr, 1.0, jnp.where(rank <= ties_allowed, eq, 0.0))
            take = jnp.where(kpos <= qpos, take, 0.0)
            bias_ref[j] = jnp.where(take > 0.5, 0.0, NEG).astype(bias_ref.dtype)
            return rank[:, tk - 1:tk]

        lax.fori_loop(0, nlim, out_tile, jnp.zeros((tq, 1), F32))

    @pl.when(jnp.logical_not(need_rank))
    def _():
        def out_tile(j, c):
            kpos = j * tk + _iota((1, tk), 1)
            take = jnp.where(kpos <= qpos, jnp.where(s_ref[j] >= thr, 1.0, 0.0), 0.0)
            bias_ref[j] = jnp.where(take > 0.5, 0.0, NEG).astype(bias_ref.dtype)
            return c

        lax.fori_loop(0, nlim, out_tile, 0)

    def fill(j, c):
        bias_ref[j] = jnp.full((tq, tk), NEG, bias_ref.dtype)
        return c

    lax.fori_loop(nlim, nk, fill, 0)


def _dsa_select(q_arr, p32, ki3, ki_col, fk, *, n_seq, t_seq, tq, qoff, k_sel, bias_dtype):
    lp = ki3.shape[1]
    nk, nq = lp // KEY_TILE, t_seq // tq
    rows = n_seq * t_seq
    kern = functools.partial(_dsa_sel_kernel, tq=tq, nk=nk, k_sel=k_sel, qoff=qoff, fk=fk)
    return pl.pallas_call(
        kern, grid=(n_seq, nq),
        in_specs=[pl.BlockSpec((tq, MIX_W), lambda b, i: (b * nq + i, _OFF['a_qi'] // MIX_W)),
                  pl.BlockSpec((tq, 128), lambda b, i: (b * nq + i, _OFF['misc'] // 128)),
                  pl.BlockSpec((1, lp, fk), lambda b, i: (b, 0, ki_col))],
        out_specs=pl.BlockSpec((nk, tq, KEY_TILE), lambda b, i: (0, b * nq + i, 0)),
        out_shape=jax.ShapeDtypeStruct((nk, rows, KEY_TILE), bias_dtype),
        scratch_shapes=[pltpu.VMEM((nk, tq, KEY_TILE), jnp.int32)]
        + ([pltpu.VMEM((32, tq, KEY_TILE), jnp.int32), pltpu.VMEM((tq, KEY_TILE), jnp.int32)] if nk <= 32 else []),
        compiler_params=_cp(("arbitrary", "arbitrary")), name="dsa_select",
    )(q_arr, p32, ki3)


def _flash_kernel(sb, sq, sk, sfl, q_ref, kv_ref, bias_ref, *rest, tq, tkm, shared, gated):
    if gated:
        misc_ref, add_ref, o_ref, m_ref, l_ref, acc_ref, qs_ref = rest
    else:
        o_ref, m_ref, l_ref, acc_ref, qs_ref = rest
    tk = KEY_TILE
    flags = sfl[pl.program_id(0)]
    dv = HEAD_DIM if shared else MIX_W
    lane_head = _iota((1, MIX_W), 1) // HEAD_DIM

    @pl.when((flags & 1) != 0)
    def _():
        m_ref[...] = jnp.full(m_ref.shape, NEG, F32)
        l_ref[...] = jnp.zeros(l_ref.shape, F32)
        acc_ref[...] = jnp.zeros(acc_ref.shape, F32)
        q = (q_ref[...] * (HEAD_DIM ** -0.5)).astype(BF16)
        if shared:
            qs_ref[...] = _stack_heads(q, HEAD_DIM)
        else:
            qs_ref[...] = jnp.concatenate([jnp.where(lane_head == h, q, jnp.zeros_like(q)) for h in range(N_HEADS)],
                                          axis=0)

    tkk = tkm * tk
    kv = kv_ref[0]
    k = kv[:, :dv].astype(BF16)
    v = kv[:, dv:2 * dv].astype(BF16)
    bias = jnp.concatenate([bias_ref[u].astype(F32) for u in range(tkm)], axis=1)
    s = _dot_nt(qs_ref[...], k).reshape(N_HEADS, tq, tkk) + bias[None]
    m_old = m_ref[...]
    m_new = jnp.maximum(m_old, jnp.max(s, axis=-1, keepdims=True))
    alpha = jnp.exp(m_old - m_new)
    p = jnp.exp(s - m_new)
    l_ref[...] = alpha * l_ref[...] + jnp.sum(p, axis=-1, keepdims=True)
    pv = _dot(p.reshape(N_HEADS * tq, tkk).astype(BF16), v).reshape(N_HEADS, tq, dv)
    acc_ref[...] = alpha * acc_ref[...] + pv
    m_ref[...] = m_new

    @pl.when((flags & 2) != 0)
    def _():
        o = acc_ref[...] / l_ref[...]
        if shared:
            out = jnp.concatenate([o[h] for h in range(N_HEADS)], axis=1)
        else:
            out = jnp.where(lane_head == 0, o[0], 0.0)
            for h in range(1, N_HEADS):
                out = out + jnp.where(lane_head == h, o[h], 0.0)
        if gated:
            g = jax.nn.sigmoid(misc_ref[...])
            gate = jnp.zeros((tq, MIX_W), F32)
            for h in range(N_HEADS):
                c = MISC_G + 3 * h + 1
                gate = gate + jnp.where(lane_head == h, g[:, c:c + 1], 0.0)
            out = add_ref[...].astype(F32) + gate * out
        o_ref[...] = out.astype(o_ref.dtype)


def _flash_steps(n_seq, t_seq, tq, tk, lp, qoff):
    sb, sq, sk, sfl = [], [], [], []
    for b in range(n_seq):
        for i in range(t_seq // tq):
            nlim = min((qoff + (i + 1) * tq - 1) // tk + 1, lp // tk)
            for j in range(nlim):
                sb.append(b), sq.append(i), sk.append(j)
                sfl.append((1 if j == 0 else 0) | (2 if j == nlim - 1 else 0))
    return [jnp.asarray(np.asarray(a, np.int32)) for a in (sb, sq, sk, sfl)]


def _flash(q_arr, q_col, kv3, kv_col, bias, *, n_seq, t_seq, tq, tkm, qoff, shared, out_dtype, gate_args=None):
    lp = kv3.shape[1]
    tk = tkm * KEY_TILE
    nq = t_seq // tq
    rows = n_seq * t_seq
    wblk = 2 * (HEAD_DIM if shared else MIX_W)
    dv = HEAD_DIM if shared else MIX_W
    steps = _flash_steps(n_seq, t_seq, tq, tk, lp, qoff)
    gated = gate_args is not None
    row = lambda s, sb, sq, sk, sfl: sb[s] * nq + sq[s]
    in_specs = [pl.BlockSpec((tq, MIX_W), lambda s, sb, sq, sk, sfl: (row(s, sb, sq, sk, sfl), q_col)),
                pl.BlockSpec((1, tk, wblk), lambda s, sb, sq, sk, sfl: (sb[s], sk[s], kv_col)),
                pl.BlockSpec((tkm, tq, KEY_TILE), lambda s, sb, sq, sk, sfl: (sk[s], row(s, sb, sq, sk, sfl), 0))]
    args = [q_arr, kv3, bias]
    if gated:
        in_specs += [pl.BlockSpec((tq, 128), lambda s, sb, sq, sk, sfl: (row(s, sb, sq, sk, sfl), _OFF['misc'] // 128)),
                     pl.BlockSpec((tq, MIX_W), lambda s, sb, sq, sk, sfl: (row(s, sb, sq, sk, sfl), 0))]
        args += list(gate_args)
    kern = functools.partial(_flash_kernel, tq=tq, tkm=tkm, shared=shared, gated=gated)
    return pl.pallas_call(
        kern,
        grid_spec=pltpu.PrefetchScalarGridSpec(
            num_scalar_prefetch=4, grid=(int(steps[0].shape[0]),), in_specs=in_specs,
            out_specs=pl.BlockSpec((tq, MIX_W), lambda s, sb, sq, sk, sfl: (row(s, sb, sq, sk, sfl), 0)),
            scratch_shapes=[pltpu.VMEM((N_HEADS, tq, 1), F32), pltpu.VMEM((N_HEADS, tq, 1), F32),
                            pltpu.VMEM((N_HEADS, tq, dv), F32), pltpu.VMEM((N_HEADS * tq, dv), BF16)]),
        out_shape=jax.ShapeDtypeStruct((rows, MIX_W), out_dtype),
        compiler_params=_cp(("arbitrary",)), name="flash_shared" if shared else "flash_heads",
    )(*steps, *args)


def _flash_t_kernel(sq, sk, sfl, qt_ref, k_ref, vt_ref, bias_ref, *rest, tq, tkm, shared, gated, block_sel):
    if gated:
        misct_ref, addt_ref, o_ref, m_ref, acc_ref = rest
    else:
        o_ref, m_ref, acc_ref = rest
    tk = KEY_TILE
    flags = sfl[pl.program_id(0)]
    row_head = _iota((MIX_W, 1), 0) // HEAD_DIM
    ones_row = jnp.where(_iota((8, tk), 0) == 0, 1.0, 0.0).astype(BF16)

    @pl.when((flags & 1) != 0)
    def _():
        m_ref[...] = jnp.full(m_ref.shape, NEG, F32)
        acc_ref[...] = jnp.zeros(acc_ref.shape, F32)

    qt = qt_ref[...]
    for u in range(tkm):
        k = k_ref[u * tk:(u + 1) * tk, :]
        k = k[:, :HEAD_DIM] if shared else k
        if block_sel:
            nb = tk // SLC_BLOCK
            kt = sk[pl.program_id(0)] * tkm + u
            blocks = bias_ref[pl.ds(pl.multiple_of(kt * nb, nb), nb), :]
            on = jnp.concatenate([jnp.broadcast_to(blocks[r:r + 1, :], (SLC_BLOCK, tq)) for r in range(nb)], axis=0)
            kpos = kt * tk + _iota((tk, 1), 0)
            qpos = sq[pl.program_id(0)] * tq + _iota((1, tq), 1)
            bias_t = jnp.where((on > 0.5) & (kpos <= qpos), 0.0, NEG)
        else:
            bias_t = bias_ref[u].astype(F32).T
        for h in range(N_HEADS):
            hs = slice(h * HEAD_DIM, (h + 1) * HEAD_DIM)
            q_h = qt[hs, :] if shared else jnp.where(row_head == h, qt, jnp.zeros_like(qt))
            s = _dot(k, q_h) + bias_t
            m_old = m_ref[h]
            m_new = jnp.maximum(m_old, jnp.max(s, axis=0, keepdims=True))
            alpha = jnp.exp2(m_old - m_new)
            p = jnp.exp2(s - m_new)
            vt_h = vt_ref[0:HEAD_DIM, u * tk:(u + 1) * tk] if shared else vt_ref[hs, u * tk:(u + 1) * tk]
            acc_ref[h] = alpha * acc_ref[h] + _dot(jnp.concatenate([vt_h, ones_row], axis=0), p.astype(BF16))
            m_ref[h] = m_new

    @pl.when((flags & 2) != 0)
    def _():
        for h in range(N_HEADS):
            hs = slice(h * HEAD_DIM, (h + 1) * HEAD_DIM)
            acc = acc_ref[h]
            out = acc[0:HEAD_DIM] / acc[HEAD_DIM:HEAD_DIM + 1]
            if gated:
                c = MISC_G + 3 * h + 1
                out = addt_ref[hs, :].astype(F32) + jax.nn.sigmoid(misct_ref[c:c + 1, :]) * out
            o_ref[hs, :] = out.astype(o_ref.dtype)


def _flash_t(qt, k_arr, k_col, k_width, vt, bias, *, t_seq, tq, tkm, shared, gate_args=None):
    tk = tkm * KEY_TILE
    steps = _flash_steps(1, t_seq, tq, tk, t_seq, 0)[1:]
    gated = gate_args is not None
    vrows = vt.shape[0]
    block_sel = bias.ndim == 2
    if block_sel:
        bias_spec = pl.BlockSpec((bias.shape[0], tq), lambda s, sq, sk, sfl: (0, sq[s]))
    else:
        bias_spec = pl.BlockSpec((tkm, tq, KEY_TILE), lambda s, sq, sk, sfl: (sk[s], sq[s], 0))
    in_specs = [pl.BlockSpec((MIX_W, tq), lambda s, sq, sk, sfl: (0, sq[s])),
                pl.BlockSpec((tk, k_width), lambda s, sq, sk, sfl: (sk[s], k_col)),
                pl.BlockSpec((vrows, tk), lambda s, sq, sk, sfl: (0, sk[s])), bias_spec]
    args = [qt, k_arr, vt, bias]
    if gated:
        in_specs += [pl.BlockSpec((128, tq), lambda s, sq, sk, sfl: (0, sq[s])),
                     pl.BlockSpec((MIX_W, tq), lambda s, sq, sk, sfl: (0, sq[s]))]
        args += list(gate_args)
    kern = functools.partial(_flash_t_kernel, tq=tq, tkm=tkm, shared=shared, gated=gated, block_sel=block_sel)
    return pl.pallas_call(
        kern,
        grid_spec=pltpu.PrefetchScalarGridSpec(
            num_scalar_prefetch=3, grid=(int(steps[0].shape[0]),), in_specs=in_specs,
            out_specs=pl.BlockSpec((MIX_W, tq), lambda s, sq, sk, sfl: (0, sq[s])),
            scratch_shapes=[pltpu.VMEM((N_HEADS, 1, tq), F32), pltpu.VMEM((N_HEADS, HEAD_DIM + 8, tq), F32)]),
        out_shape=jax.ShapeDtypeStruct((MIX_W, t_seq), BF16),
        compiler_params=_cp(("arbitrary",)), name="flash_t_shared" if shared else "flash_t_heads",
    )(*steps, *args)


def _compress_kernel(x_ref, w_ref, o_ref, *, n_blocks, ncp):
    w = w_ref[...]
    e = jnp.exp(w - jnp.max(w, axis=-1, keepdims=True))
    w = e / jnp.sum(e, axis=-1, keepdims=True)
    n16 = x_ref.shape[1] // CMP_STRIDE
    width = x_ref.shape[2]
    first = jnp.zeros((n16, width), F32)
    second = jnp.zeros((n16, width), F32)
    for j in range(CMP_STRIDE):
        xj = x_ref[0, pl.ds(j, n16, stride=CMP_STRIDE), :]
        first = first + xj * w[:, j:j + 1]
        second = second + xj * w[:, CMP_STRIDE + j:CMP_STRIDE + j + 1]
    shifted = jnp.concatenate([second[1:], jnp.zeros((1, width), F32)], axis=0)
    out = first + shifted
    if n16 < ncp:
        out = jnp.concatenate([out, jnp.zeros((ncp - n16, width), F32)], axis=0)
    out = out[:ncp]
    o_ref[0] = jnp.where(_iota((ncp, 1), 0) < n_blocks, out, 0.0)


def _compress(rows3, w_pos, length, ncp):
    n_seq, lp, width = rows3.shape
    assert lp % CMP_STRIDE == 0
    n_blocks = -(-length // CMP_STRIDE) - 1
    kern = functools.partial(_compress_kernel, n_blocks=n_blocks, ncp=ncp)
    return pl.pallas_call(
        kern, grid=(n_seq,),
        in_specs=[pl.BlockSpec((1, lp, width), lambda b: (b, 0, 0)),
                  pl.BlockSpec((1, CMP_LEN), lambda b: (0, 0))],
        out_specs=pl.BlockSpec((1, ncp, width), lambda b: (b, 0, 0)),
        out_shape=jax.ShapeDtypeStruct((n_seq, ncp, width), F32),
        compiler_params=_cp(("arbitrary",)), name="compress",
    )(rows3, w_pos.reshape(1, CMP_LEN))


def _masked_softmax(s, mask):
    s = jnp.where(mask, s, NEG)
    m = jnp.max(s, axis=-1, keepdims=True)
    e = jnp.where(mask, jnp.exp(s - m), 0.0)
    return e / jnp.maximum(jnp.sum(e, axis=-1, keepdims=True), 1e-30)


def _nsa_sel_kernel(q_ref, misc_ref, kc_ref, win_ref, ocw_ref, bias_ref, *, tq, nk, qoff, n_cmp, n_blk, nbp, n_top,
                    win_rows, win_dyn, win_pos0, block_out):
    tk = KEY_TILE
    i = pl.program_id(1)
    q0 = qoff + i * tq
    qpos = q0 + _iota((tq, 1), 0)
    nlim = jnp.minimum((q0 + tq - 1) // tk + 1, nk)
    qs = _stack_heads((q_ref[...] * (HEAD_DIM ** -0.5)).astype(BF16), HEAD_DIM)
    ncp = kc_ref.shape[1]

    kcv = kc_ref[0]
    kc = kcv[:, :HEAD_DIM].astype(BF16)
    vc = kcv[:, HEAD_DIM:].astype(BF16)
    cidx = _iota((1, ncp), 1)
    c_mask = ((cidx * CMP_STRIDE + CMP_LEN - 1) <= qpos) & (cidx < n_cmp)
    s_c = _dot_nt(qs, kc).reshape(N_HEADS, tq, ncp)
    p_c = _masked_softmax(s_c, c_mask[None])
    o_c = _dot(p_c.reshape(N_HEADS * tq, ncp).astype(BF16), vc).reshape(N_HEADS, tq, HEAD_DIM)

    c_start = _iota((ncp, 1), 0) * CMP_STRIDE
    s_start = _iota((1, nbp), 1) * SLC_BLOCK
    overlap = (c_start < s_start + SLC_BLOCK) & (c_start + CMP_LEN > s_start) & (_iota((ncp, 1), 0) < n_cmp)
    imp = _dot_x_exact01(p_c[0] + p_c[1] + p_c[2] + p_c[3], jnp.where(overlap, 1.0, 0.0).astype(BF16))
    blk = _iota((1, nbp), 1)
    cur = qpos // SLC_BLOCK
    forced = (blk == 0) | (blk == cur) | (blk == cur - 1)
    live = jnp.where(forced, jnp.inf, jnp.where((blk <= cur) & (blk < n_blk), imp, -jnp.inf))
    blk_f = blk.astype(F32)

    def pick(t, carry):
        live, sel = carry
        top = jnp.max(live, axis=-1, keepdims=True)
        first = jnp.min(jnp.where(live == top, blk_f, float(nbp)), axis=-1, keepdims=True)
        hit = blk_f == first
        return jnp.where(hit, -jnp.inf, live), jnp.where(hit, 1.0, sel)

    if block_out:
        blk_t = _iota((nbp, 1), 0).astype(F32)

        def pick_t(t, carry):
            live_t, sel_t = carry
            top = jnp.max(live_t, axis=0, keepdims=True)
            first = jnp.min(jnp.where(live_t == top, blk_t, float(nbp)), axis=0, keepdims=True)
            hit = blk_t == first
            return jnp.where(hit, -jnp.inf, live_t), jnp.where(hit, 1.0, sel_t)

        _, sel_t = lax.fori_loop(0, n_top, pick_t, (live.T, jnp.zeros((nbp, tq), F32)))
        bias_ref[...] = sel_t
    else:
        _, sel = lax.fori_loop(0, n_top, pick, (live, jnp.zeros((tq, nbp), F32)))
        sel = sel.astype(BF16)
        row_blk = _iota((nbp, 1), 0)

        def bias_tile(j, c):
            kpos = j * tk + _iota((1, tk), 1)
            expand = jnp.where(row_blk == kpos // SLC_BLOCK, 1.0, 0.0).astype(BF16)
            on = _dot(sel, expand)
            ok = (on > 0.5) & (kpos <= qpos)
            bias_ref[j] = jnp.where(ok, 0.0, NEG).astype(bias_ref.dtype)
            return c

        lax.fori_loop(0, nlim, bias_tile, 0)

        def fill(j, c):
            bias_ref[j] = jnp.full((tq, tk), NEG, bias_ref.dtype)
            return c

        lax.fori_loop(nlim, nk, fill, 0)

    if win_dyn:
        start = pl.multiple_of(jnp.maximum(q0 - WINDOW, 0), 8)
        wkv = win_ref[0, pl.ds(start, win_rows), :]
        kwpos = start + _iota((1, win_rows), 1)
    else:
        wkv = win_ref[0]
        kwpos = win_pos0 + _iota((1, win_rows), 1)
    kw = wkv[:, :HEAD_DIM].astype(BF16)
    vw = wkv[:, HEAD_DIM:].astype(BF16)
    rel = qpos - kwpos
    w_mask = (rel >= 0) & (rel < WINDOW) & (kwpos >= 0)
    s_w = _dot_nt(qs, kw).reshape(N_HEADS, tq, win_rows)
    p_w = _masked_softmax(s_w, w_mask[None])
    o_w = _dot(p_w.reshape(N_HEADS * tq, win_rows).astype(BF16), vw).reshape(N_HEADS, tq, HEAD_DIM)

    g = jax.nn.sigmoid(misc_ref[...])
    parts = []
    for h in range(N_HEADS):
        c = MISC_G + 3 * h
        parts.append(g[:, c:c + 1] * o_c[h] + g[:, c + 2:c + 3] * o_w[h])
    ocw_ref[...] = jnp.concatenate(parts, axis=1).astype(ocw_ref.dtype)


def _nsa_select(q_arr, p32, kcvc, win3, win_col, *, n_seq, t_seq, tq, qoff, seq_len, lp, win_dyn, win_pos0, bias_dtype,
                out_dtype):
    nk, nq = lp // KEY_TILE, t_seq // tq
    rows = n_seq * t_seq
    n_cmp = -(-seq_len // CMP_STRIDE) - 1
    n_blk = -(-seq_len // SLC_BLOCK)
    nbp = -(-n_blk // 128) * 128
    win_rows = (tq + WINDOW) if win_dyn else win3.shape[1]
    block_out = bias_dtype is None
    kern = functools.partial(_nsa_sel_kernel, tq=tq, nk=nk, qoff=qoff, n_cmp=n_cmp, n_blk=n_blk, nbp=nbp,
                             n_top=min(N_SLC, n_blk), win_rows=win_rows, win_dyn=win_dyn, win_pos0=win_pos0,
                             block_out=block_out)
    if block_out:
        sel_spec = pl.BlockSpec((nbp, tq), lambda b, i: (0, b * nq + i))
        sel_shape = jax.ShapeDtypeStruct((nbp, rows), F32)
    else:
        sel_spec = pl.BlockSpec((nk, tq, KEY_TILE), lambda b, i: (0, b * nq + i, 0))
        sel_shape = jax.ShapeDtypeStruct((nk, rows, KEY_TILE), bias_dtype)
    return pl.pallas_call(
        kern, grid=(n_seq, nq),
        in_specs=[pl.BlockSpec((tq, MIX_W), lambda b, i: (b * nq + i, _OFF['b_q'] // MIX_W)),
                  pl.BlockSpec((tq, 128), lambda b, i: (b * nq + i, _OFF['misc'] // 128)),
                  pl.BlockSpec((1,) + kcvc.shape[1:], lambda b, i: (b, 0, 0)),
                  pl.BlockSpec((1, win3.shape[1], 128), lambda b, i: (b, 0, win_col))],
        out_specs=[pl.BlockSpec((tq, MIX_W), lambda b, i: (b * nq + i, 0)), sel_spec],
        out_shape=[jax.ShapeDtypeStruct((rows, MIX_W), out_dtype), sel_shape],
        compiler_params=_cp(("arbitrary", "arbitrary")), name="nsa_select",
    )(q_arr, p32, kcvc, win3)


def _gla_tables(c):
    levels = []
    b = c
    while b >= 2:
        levels.append(b)
        b //= 2
    r = np.arange(c)
    mats = [(r[None, :] <= r[:, None]), (r[None, :] > r[:, None])]
    qm, km = [], []
    for b in levels:
        mid = (r // b) * b + b // 2
        upper = r >= mid
        qm.append(upper[:, None] & (r[None, :] >= mid[:, None]) & (r[None, :] <= r[:, None]))
        km.append((~upper)[:, None] & (r[None, :] > r[:, None]) & (r[None, :] < mid[:, None]))
    return np.concatenate(mats + qm + km, axis=0).astype(np.float32), levels


def _gla_kernel(q_ref, k_ref, v_ref, r_ref, misc_ref, wal_ref, bal_ref, gg_ref, m_ref, s0_ref, o_ref, st_ref, s_scr,
                *, tt, c, levels, n_tiles):
    t_idx = pl.program_id(1)
    nl = len(levels)

    @pl.when(t_idx == 0)
    def _():
        s_scr[...] = s0_ref[0]

    lane_head = _iota((1, MIX_W), 1) // HEAD_DIM
    same_head = (_iota((MIX_W, 1), 0) // HEAD_DIM) == lane_head
    eye = jnp.where(_iota((MIX_W, MIX_W), 0) == _iota((MIX_W, MIX_W), 1), 1.0, 0.0).astype(BF16)
    ones_head = jnp.where(same_head, 1.0, 0.0)
    rr, cc = _iota((c, c), 0), _iota((c, c), 1)
    pair_masks = [((rr // b) == (cc // b)) & ((rr % b) >= b // 2) & ((cc % b) < b // 2) for b in levels]
    diag_mask = rr == cc

    z = _dot_hi(misc_ref[...], wal_ref[...]) + bal_ref[...]
    la = (jnp.minimum(z, 0.0) - jnp.log1p(jnp.exp(-jnp.abs(z)))) * (1.0 / GATE_TAU)
    m_all = m_ref[...]

    def bd(x):
        return jnp.concatenate([jnp.where(lane_head == h, x, jnp.zeros_like(x)) for h in range(N_HEADS)], axis=0)

    def unbd(x):
        out = jnp.where(lane_head == 0, x[0:c], 0.0)
        for h in range(1, N_HEADS):
            out = out + jnp.where(lane_head == h, x[h * c:(h + 1) * c], 0.0)
        return out

    for ci in range(tt // c):
        sl = slice(ci * c, (ci + 1) * c)
        e = jnp.exp(_dot_exact01(m_all, la[sl]))
        q = q_ref[sl, :] * (HEAD_DIM ** -0.5)
        k = k_ref[sl, :]
        v = v_ref[sl, :].astype(BF16)
        e_cum, e_rest = e[0:c], e[c:2 * c]
        att = jnp.where(diag_mask[None], _dot_nt(bd(q.astype(BF16)), k.astype(BF16)).reshape(N_HEADS, c, c), 0.0)
        for li in range(nl):
            eq = e[(2 + li) * c:(3 + li) * c]
            ek = e[(2 + nl + li) * c:(3 + nl + li) * c]
            a = _dot_nt(bd((q * eq).astype(BF16)), (k * ek).astype(BF16)).reshape(N_HEADS, c, c)
            att = att + jnp.where(pair_masks[li][None], a, 0.0)
        o_intra = unbd(_dot(att.reshape(N_HEADS * c, c).astype(BF16), v))
        st = s_scr[...]
        o_inter = _dot_nt((q * e_cum).astype(BF16), st.astype(BF16))
        v_t = _dot_nt(eye, v).astype(BF16)
        upd = _dot(v_t, (k * e_rest).astype(BF16))
        s_scr[...] = st * e_cum[c - 1:c, :] + jnp.where(same_head, upd, 0.0)
        o = o_inter + o_intra
        ms = _dot_hi(o * o, ones_head) * (1.0 / HEAD_DIM)
        o = o * lax.rsqrt(ms + EPS) * gg_ref[...]
        r = r_ref[sl, :]
        o_ref[sl, :] = (o * (r * jax.nn.sigmoid(r))).astype(o_ref.dtype)

    @pl.when(t_idx == n_tiles - 1)
    def _():
        st_ref[0] = s_scr[...]


def _gla(p32, wal_pad, b_alpha, g_gla4, st0, *, n_seq, t_seq, tt, out_dtype):
    c = min(GLA_CHUNK, t_seq)
    m_all, levels = _gla_tables(c)
    n_tiles = t_seq // tt
    rows = n_seq * t_seq
    col = lambda name: pl.BlockSpec((tt, MIX_W), lambda b, t, o=_OFF[name] // MIX_W: (b * n_tiles + t, o))
    kern = functools.partial(_gla_kernel, tt=tt, c=c, levels=levels, n_tiles=n_tiles)
    return pl.pallas_call(
        kern, grid=(n_seq, n_tiles),
        in_specs=[col('d_q'), col('d_k'), col('d_v'), col('d_r'),
                  pl.BlockSpec((tt, 128), lambda b, t: (b * n_tiles + t, _OFF['misc'] // 128)),
                  pl.BlockSpec((128, MIX_W), lambda b, t: (0, 0)),
                  pl.BlockSpec((1, MIX_W), lambda b, t: (0, 0)),
                  pl.BlockSpec((1, MIX_W), lambda b, t: (0, 0)),
                  pl.BlockSpec(m_all.shape, lambda b, t: (0, 0)),
                  pl.BlockSpec((1, MIX_W, MIX_W), lambda b, t: (b, 0, 0))],
        out_specs=[pl.BlockSpec((tt, MIX_W), lambda b, t: (b * n_tiles + t, 0)),
                   pl.BlockSpec((1, MIX_W, MIX_W), lambda b, t: (b, 0, 0))],
        out_shape=[jax.ShapeDtypeStruct((rows, MIX_W), out_dtype),
                   jax.ShapeDtypeStruct((n_seq, MIX_W, MIX_W), F32)],
        scratch_shapes=[pltpu.VMEM((MIX_W, MIX_W), F32)],
        compiler_params=_cp(("arbitrary", "arbitrary")), name="gla",
    )(p32, p32, p32, p32, p32, wal_pad, b_alpha.reshape(1, MIX_W), g_gla4, jnp.asarray(m_all, dtype=BF16), st0)


def _dwconv3(u, w, carry_ref, fix_refs, t_seq, tm, first_tile):
    row = _iota((tm, 1), 0)
    u1 = pltpu.roll(u, 1, axis=0)
    u2 = pltpu.roll(u, 2, axis=0)
    if fix_refs is None:
        prev = jnp.where(first_tile, 0.0, carry_ref[0:2, :])
        u1 = jnp.where(row == 0, prev[1:2], u1)
        u2 = jnp.where(row == 0, prev[0:1], jnp.where(row == 1, prev[1:2], u2))
        carry_ref[0:2, :] = u[tm - 2:tm]
    else:
        pos = row % t_seq
        u1 = jnp.where(pos == 0, fix_refs[0][...], u1)
        u2 = jnp.where(pos < 2, fix_refs[1][...], u2)
    return w[0:1] * u2 + w[1:2] * u1 + w[2:3] * u


def _merge_kernel(oa_ref, ob_ref, od_ref, cin_ref, cb_ref, cc_ref, gate_ref, x_ref, g1_ref, wconv_ref, wb_ref,
                  wo_ref, *rest, tm, t_seq, tiles_per_seq, per_row):
    if per_row:
        fix1_ref, fix2_ref, o_ref, carry_ref = rest
        fix = (fix1_ref, fix2_ref)
    else:
        o_ref, carry_ref = rest
        fix = None
    first = (pl.program_id(0) % tiles_per_seq) == 0
    u = cc_ref[...] * cin_ref[...]
    o_c = cb_ref[...] * _dwconv3(u, wconv_ref[...], carry_ref, fix, t_seq, tm, first)
    branches = (oa_ref[...], ob_ref[...], o_c, od_ref[...])
    merged = jnp.zeros((tm, D_MODEL), F32)
    for bi, br in enumerate(branches):
        gate = jax.nn.sigmoid(gate_ref[:, bi * D_MODEL:(bi + 1) * D_MODEL])
        merged = merged + gate * _dot(br.astype(BF16), wb_ref[bi])
    o_ref[...] = x_ref[...] + g1_ref[0] * _dot(merged.astype(BF16), wo_ref[...])


def _merge(oa, ob, od, p32, x, g1, conv_c, wb, wo, fix, *, tm, t_seq):
    rows = x.shape[0]
    per_row = fix is not None
    tiles_per_seq = max(t_seq // tm, 1)
    rowblk = lambda width, colblk=0: pl.BlockSpec((tm, width), lambda i: (i, colblk))
    in_specs = [rowblk(MIX_W), rowblk(MIX_W), rowblk(MIX_W),
                rowblk(MIX_W, _OFF['c_in'] // MIX_W), rowblk(MIX_W, _OFF['c_b'] // MIX_W),
                rowblk(MIX_W, _OFF['c_c'] // MIX_W), rowblk(N_HEADS * D_MODEL, 0), rowblk(D_MODEL),
                _mod_spec(g1, tm, t_seq), _const_spec((CONV_W, MIX_W)), _const_spec((N_HEADS, MIX_W, D_MODEL)),
                _const_spec((D_MODEL, D_MODEL))]
    args = [oa, ob, od, p32, p32, p32, p32, x, g1, conv_c, wb, wo]
    if per_row:
        in_specs += [rowblk(MIX_W), rowblk(MIX_W)]
        args += list(fix)
    kern = functools.partial(_merge_kernel, tm=tm, t_seq=t_seq, tiles_per_seq=tiles_per_seq, per_row=per_row)
    return pl.pallas_call(
        kern, grid=(rows // tm,), in_specs=in_specs, out_specs=rowblk(D_MODEL),
        out_shape=jax.ShapeDtypeStruct((rows, D_MODEL), F32),
        scratch_shapes=[pltpu.VMEM((8, MIX_W), F32)],
        compiler_params=_cp(("arbitrary",)), name="merge",
    )(*args)


def _ffn_kernel(x_ref, gn_ref, sc_ref, sh_ref, g2_ref, wa_ref, wg_ref, wconv_ref, bf_ref, wout_ref, gf_ref, *rest,
                tm, tf, t_seq, tiles_per_seq, per_row):
    if per_row:
        fix1_ref, fix2_ref, o_ref, y_ref, a_ref, carry_ref = rest
    else:
        o_ref, y_ref, a_ref, carry_ref = rest
    first = (pl.program_id(0) % tiles_per_seq) == 0
    x = x_ref[...]
    h = _norm_mod(x, gn_ref[...], sc_ref[0], sh_ref[0]).astype(BF16)
    acc = jnp.zeros((tm, D_MODEL), F32)
    for f in range(D_FF // tf):
        fs = slice(f * tf, (f + 1) * tf)
        a = _dot(h, wa_ref[:, fs])
        g = _dot(h, wg_ref[:, fs])
        a_ref[:, fs] = a
        fix = (fix1_ref.at[:, fs], fix2_ref.at[:, fs]) if per_row else None
        conv = _dwconv3(a, wconv_ref[:, fs], carry_ref.at[:, fs], fix, t_seq, tm, first)
        pre = conv + bf_ref[:, fs]
        act = pre * jax.nn.sigmoid(pre) * g
        acc = acc + _dot(act.astype(BF16), wout_ref[fs, :])
    xn = x + g2_ref[0] * acc
    o_ref[...] = xn
    y_ref[...] = xn * lax.rsqrt(jnp.mean(xn * xn, axis=-1, keepdims=True) + EPS) * gf_ref[...]


def _ffn(x, gn, sc, sh, g2, wa, wg, conv_ffn, b_ffn, wout, g_final, fix, *, tm, t_seq):
    rows = x.shape[0]
    per_row = fix is not None
    tiles_per_seq = max(t_seq // tm, 1)
    tf = 256
    rowblk = lambda width: pl.BlockSpec((tm, width), lambda i: (i, 0))
    in_specs = [rowblk(D_MODEL), _const_spec((1, D_MODEL)), _mod_spec(sc, tm, t_seq), _mod_spec(sh, tm, t_seq),
                _mod_spec(g2, tm, t_seq), _const_spec((D_MODEL, D_FF)), _const_spec((D_MODEL, D_FF)),
                _const_spec((CONV_W, D_FF)), _const_spec((1, D_FF)), _const_spec((D_FF, D_MODEL)),
                _const_spec((1, D_MODEL))]
    args = [x, gn, sc, sh, g2, wa, wg, conv_ffn, b_ffn, wout, g_final]
    if per_row:
        in_specs += [rowblk(D_FF), rowblk(D_FF)]
        args += list(fix)
    kern = functools.partial(_ffn_kernel, tm=tm, tf=tf, t_seq=t_seq, tiles_per_seq=tiles_per_seq, per_row=per_row)
    return pl.pallas_call(
        kern, grid=(rows // tm,), in_specs=in_specs,
        out_specs=[rowblk(D_MODEL), rowblk(D_MODEL), rowblk(D_FF)],
        out_shape=[jax.ShapeDtypeStruct((rows, D_MODEL), F32), jax.ShapeDtypeStruct((rows, D_MODEL), F32),
                   jax.ShapeDtypeStruct((rows, D_FF), F32)],
        scratch_shapes=[pltpu.VMEM((8, D_FF), F32)],
        compiler_params=_cp(("arbitrary",)), name="ffn",
    )(*args)


def _permute_w_in(w):
    offs, o = {}, 0
    for name, n in _IN_SPLITS:
        offs[name] = (o, n)
        o += n

    def c(name):
        s, n = offs[name]
        return w[:, s:s + n]

    pieces = [c('gate'), c('a_k'), c('a_v'), c('a_q'), c('a_qi'), c('b_q'), c('b_cmp'), c('b_slc'), c('b_win'),
              c('a_ki'), c('a_wi'), c('b_g'), c('d_a'), jnp.zeros((w.shape[0], 32), w.dtype),
              c('c_in'), c('c_b'), c('c_c'), c('d_q'), c('d_k'), c('d_v'), c('d_r')]
    return jnp.concatenate(pieces, axis=1).astype(BF16)


def _cols(p, name, width):
    return p[:, _OFF[name]:_OFF[name] + width]


def _state_to_bd(s0):
    b = s0.shape[0]
    s0t = jnp.swapaxes(s0, 2, 3)
    eye = jnp.eye(N_HEADS, dtype=s0.dtype)[None, :, None, :, None]
    return (s0t[:, :, :, None, :] * eye).reshape(b, MIX_W, MIX_W)


def _bd_to_state(st):
    b = st.shape[0]
    st5 = st.reshape(b, N_HEADS, HEAD_DIM, N_HEADS, HEAD_DIM)
    return jnp.stack([jnp.swapaxes(st5[:, h, :, h, :], 1, 2) for h in range(N_HEADS)], axis=1)


def _conv_fix(state, t_seq):
    b, _, c = state.shape
    fix1 = jnp.concatenate([state[:, 1:2], jnp.zeros((b, t_seq - 1, c), state.dtype)], axis=1)
    fix2 = jnp.concatenate([state, jnp.zeros((b, t_seq - 2, c), state.dtype)], axis=1)
    return fix1.reshape(b * t_seq, c), fix2.reshape(b * t_seq, c)


def _layer_weights(l, w_in, w_alpha, g_gla, w_branch, w_out, w_ffn_in, w_ffn_out):
    wal_pad = jnp.zeros((128, MIX_W), F32).at[MISC_DA:MISC_DA + GATE_RANK].set(w_alpha[l])
    return dict(w_in=_permute_w_in(w_in[l]), wal=wal_pad, gg=jnp.tile(g_gla[l], N_HEADS).reshape(1, MIX_W),
                wb=w_branch[l].astype(BF16), wo=w_out[l].astype(BF16),
                wa=w_ffn_in[l][:, :D_FF].astype(BF16), wg=w_ffn_in[l][:, D_FF:].astype(BF16),
                wout=w_ffn_out[l].astype(BF16))


def kernel(x_prompt, x_sample, cache_a_kv, cache_a_idx, cache_b_cmp, cache_b_slc, state_b_win, state_c_conv, state_d_gla, state_ffn_conv, page_table, c_prompt, c_sample, w_ada, b_ada, g_norm1, w_in, w_cmp_pos, conv_c, w_alpha, b_alpha, g_gla, w_branch, w_out, g_norm2, w_ffn_in, conv_ffn, b_ffn, w_ffn_out, g_final):
    bp, seq, d = x_prompt.shape
    bs, tdec, _ = x_sample.shape
    depth = w_ada.shape[0]
    n_pages = page_table.shape[1]
    past = n_pages * PAGE
    w_buf = state_b_win.shape[2]
    assert bp == 1 and d == D_MODEL and seq % 1024 == 0 and tdec == 8 and w_buf == WINDOW

    n_c = bp + bs
    c_all = jnp.concatenate([c_prompt, c_sample, jnp.zeros((-n_c % 8, d), F32)], axis=0)
    mod = _ada(c_all, w_ada, b_ada)

    xp = x_prompt.reshape(bp * seq, d)
    xs = x_sample.reshape(bs * tdec, d)
    rs = bs * tdec
    tail_rows = KEY_TILE
    lps = past + tail_rows
    tkm_s = max(t for t in (1, 2, 3, 4, 6, 8, 11) if (lps // KEY_TILE) % t == 0)
    gfin = g_final.reshape(1, d)
    outs_p, outs_s = [], []
    y_p = y_s = None
    for l in range(depth):
        lw = _layer_weights(l, w_in, w_alpha, g_gla, w_branch, w_out, w_ffn_in, w_ffn_out)
        mp = [mod[l, :bp, k * d:(k + 1) * d].reshape(bp, 1, d) for k in range(6)]
        ms = [jnp.repeat(mod[l, bp:n_c, k * d:(k + 1) * d], tdec, axis=0).reshape(1, rs, d) for k in range(6)]
        gn1, gn2 = g_norm1[l].reshape(1, d), g_norm2[l].reshape(1, d)
        bffn = b_ffn[l].reshape(1, D_FF)

        p32, _ = _proj(xs, gn1, ms[1], ms[0], lw['w_in'], rs, rs)
        new = {n: _cols(p32, n, w).reshape(bs, tdec, w) for n, w in
               (('a_k', 2 * MIX_W), ('misc', IDX_DIM), ('b_cmp', 128), ('b_slc', 128), ('b_win', 128))}
        tails = [jnp.concatenate([new[n], jnp.zeros((bs, tail_rows - tdec, new[n].shape[-1]), F32)], axis=1)
                 for n in ('a_k', 'misc', 'b_cmp', 'b_slc')]
        kv_s, ki_s, cmp_s, slc_s = _gather_past(page_table, [cache_a_kv, cache_a_idx, cache_b_cmp, cache_b_slc],
                                                tails, (BF16, BF16, F32, BF16), l)
        seq_s = past + tdec
        bias_a = _dsa_select(p32, p32, ki_s, 0, IDX_DIM, n_seq=bs, t_seq=tdec, tq=tdec, qoff=past,
                             k_sel=min(A_TOPK, seq_s // 4), bias_dtype=F32)
        o_a = _flash(p32, _OFF['a_q'] // MIX_W, kv_s, 0, bias_a, n_seq=bs, t_seq=tdec, tq=tdec, tkm=tkm_s, qoff=past,
                     shared=False, out_dtype=F32)
        ncp = -(-(-(-seq_s // CMP_STRIDE) - 1) // 128) * 128
        kcvc = _compress(cmp_s, w_cmp_pos[l], seq_s, ncp)
        win_full = jnp.concatenate([state_b_win[l], new['b_win']], axis=1)
        win_pad = jnp.concatenate([win_full, jnp.zeros((bs, -(w_buf + tdec) % 16, 128), F32)], axis=1)
        ocw, bias_b = _nsa_select(p32, p32, kcvc, win_pad, 0, n_seq=bs, t_seq=tdec, tq=tdec, qoff=past, seq_len=seq_s,
                                  lp=lps, win_dyn=False, win_pos0=past - w_buf, bias_dtype=F32, out_dtype=F32)
        o_b = _flash(p32, _OFF['b_q'] // MIX_W, slc_s, 0, bias_b, n_seq=bs, t_seq=tdec, tq=tdec, tkm=tkm_s, qoff=past,
                     shared=True, out_dtype=F32, gate_args=(p32, ocw))
        o_d, st = _gla(p32, lw['wal'], b_alpha[l], lw['gg'], _state_to_bd(state_d_gla[l]), n_seq=bs, t_seq=tdec,
                       tt=tdec, out_dtype=F32)
        u_tail = (_cols(p32, 'c_c', MIX_W) * _cols(p32, 'c_in', MIX_W)).reshape(bs, tdec, MIX_W)[:, tdec - 2:]
        xs = _merge(o_a, o_b, o_d, p32, xs, ms[2], conv_c[l], lw['wb'], lw['wo'], _conv_fix(state_c_conv[l], tdec),
                    tm=rs, t_seq=tdec)
        xs, y_s, a_full = _ffn(xs, gn2, ms[4], ms[3], ms[5], lw['wa'], lw['wg'], conv_ffn[l], bffn, lw['wout'], gfin,
                               _conv_fix(state_ffn_conv[l], tdec), tm=rs, t_seq=tdec)
        outs_s.append((new['a_k'], new['misc'], new['b_cmp'], new['b_slc'], win_full[:, tdec:], u_tail,
                       _bd_to_state(st), a_full.reshape(bs, tdec, D_FF)[:, tdec - 2:]))

        p32, p16 = _proj(xp, gn1, mp[1], mp[0], lw['w_in'], 1024, seq)
        p16_3 = p16.reshape(bp, seq, NP)
        a_kv = _cols(p32, 'a_k', 2 * MIX_W).reshape(bp, seq, 2 * MIX_W)
        a_idx = _cols(p32, 'misc', IDX_DIM).reshape(bp, seq, IDX_DIM)
        b_cmp = _cols(p32, 'b_cmp', 128).reshape(bp, seq, 128)
        b_slc = _cols(p32, 'b_slc', 128).reshape(bp, seq, 128)
        b_win = _cols(p32, 'b_win', 128).reshape(bp, seq, 128)[:, seq - min(WINDOW, seq):]
        k_sel = min(A_TOPK, seq // 4)
        bias_a = _dsa_select(p16, p32, p16_3, _OFF['misc'] // 128, 128, n_seq=bp, t_seq=seq, tq=128, qoff=0,
                             k_sel=k_sel, bias_dtype=BF16)
        q_scale = HEAD_DIM ** -0.5 * float(np.log2(np.e))
        o_a = jnp.transpose(_flash_t(jnp.transpose(_cols(p32, 'a_q', MIX_W) * q_scale).astype(BF16), p16,
                                     _OFF['a_k'] // MIX_W, MIX_W,
                                     jnp.transpose(_cols(p16, 'a_v', MIX_W)), bias_a, t_seq=seq, tq=512, tkm=2,
                                     shared=False))
        ncp = -(-(-(-seq // CMP_STRIDE) - 1) // 128) * 128
        kcvc = _compress(b_cmp, w_cmp_pos[l], seq, ncp)
        ocw, sel_b = _nsa_select(p16, p32, kcvc, p16_3, _OFF['b_win'] // 128, n_seq=bp, t_seq=seq, tq=128, qoff=0,
                                 seq_len=seq, lp=seq, win_dyn=True, win_pos0=0, bias_dtype=None, out_dtype=BF16)
        vt_b = jnp.transpose(p16[:, _OFF['b_slc'] + HEAD_DIM:_OFF['b_slc'] + 2 * HEAD_DIM])
        o_b = jnp.transpose(_flash_t(jnp.transpose(_cols(p32, 'b_q', MIX_W) * q_scale).astype(BF16), p16,
                                     _OFF['b_slc'] // 128, 128, vt_b,
                                     sel_b, t_seq=seq, tq=512, tkm=2, shared=True,
                                     gate_args=(jnp.transpose(_cols(p32, 'misc', 128)), jnp.transpose(ocw))))
        o_d, st = _gla(p32, lw['wal'], b_alpha[l], lw['gg'], jnp.zeros((bp, MIX_W, MIX_W), F32), n_seq=bp, t_seq=seq,
                       tt=512, out_dtype=BF16)
        u_tail = (_cols(p32, 'c_c', MIX_W) * _cols(p32, 'c_in', MIX_W)).reshape(bp, seq, MIX_W)[:, seq - 2:]
        xp = _merge(o_a, o_b, o_d, p32, xp, mp[2], conv_c[l], lw['wb'], lw['wo'], None, tm=256, t_seq=seq)
        xp, y_p, a_full = _ffn(xp, gn2, mp[4], mp[3], mp[5], lw['wa'], lw['wg'], conv_ffn[l], bffn, lw['wout'], gfin,
                               None, tm=512, t_seq=seq)
        outs_p.append((a_kv, a_idx, b_cmp, b_slc, b_win, u_tail, _bd_to_state(st),
                       a_full.reshape(bp, seq, D_FF)[:, seq - 2:]))

    sp = [jnp.stack(z) for z in zip(*outs_p)]
    ss = [jnp.stack(z) for z in zip(*outs_s)]
    res = [y_p.reshape(bp, seq, d), y_s.reshape(bs, tdec, d)]
    for a, b in zip(sp, ss):
        res += [a, b]
    return tuple(res)
```

```python
import functools

import numpy as np
import jax
import jax.numpy as jnp
from jax import lax
from jax.experimental import pallas as pl
from jax.experimental.pallas import tpu as pltpu

F32 = jnp.float32
BF16 = jnp.bfloat16
HI = lax.Precision.HIGHEST

D_MODEL = 1024
PAGE = 128
HEAD_DIM = 64
MIX_W = 256
N_HEADS = 4
IDX_DIM = 64
A_TOPK = 256
CMP_LEN = 32
CMP_STRIDE = 16
SLC_BLOCK = 64
N_SLC = 16
WINDOW = 512
CONV_W = 3
GATE_RANK = 16
GATE_TAU = 16.0
GLA_CHUNK = 64
D_FF = 2816
EPS = 1e-6
NEG = -1e30
INT_MIN = -2 ** 31
KEY_TILE = 512

_IN_SPLITS = (('a_q', 256), ('a_k', 256), ('a_v', 256), ('a_qi', 256), ('a_ki', 64), ('a_wi', 4),
              ('b_q', 256), ('b_cmp', 128), ('b_slc', 128), ('b_win', 128), ('b_g', 12),
              ('c_in', 256), ('c_b', 256), ('c_c', 256),
              ('d_q', 256), ('d_k', 256), ('d_v', 256), ('d_r', 256), ('d_a', 16), ('gate', 4096))
_OFF = dict(gate=0, a_k=4096, a_v=4352, a_q=4608, a_qi=4864, b_q=5120, b_cmp=5376, b_slc=5504, b_win=5632,
            misc=5760, c_in=5888, c_b=6144, c_c=6400, d_q=6656, d_k=6912, d_v=7168, d_r=7424)
NP = 7680
MISC_KI, MISC_WI, MISC_G, MISC_DA = 0, 64, 68, 80
VMEM_LIMIT = 56 * 1024 * 1024


def _cp(sem):
    return pltpu.CompilerParams(dimension_semantics=sem, vmem_limit_bytes=VMEM_LIMIT)


def _dot(a, b):
    return jnp.dot(a, b, preferred_element_type=F32)


def _dot_nt(a, b):
    return lax.dot_general(a, b, (((1,), (1,)), ((), ())), preferred_element_type=F32)


def _dot_hi(a, b):
    return jnp.dot(a, b, preferred_element_type=F32, precision=HI)


def _dot_exact01(a01, x):
    hi = x.astype(BF16)
    lo = (x - hi.astype(F32)).astype(BF16)
    return _dot(a01, hi) + _dot(a01, lo)


def _dot_x_exact01(x, b01):
    hi = x.astype(BF16)
    lo = (x - hi.astype(F32)).astype(BF16)
    return _dot(hi, b01) + _dot(lo, b01)


def _const_spec(shape):
    nd = len(shape)
    return pl.BlockSpec(shape, lambda *a: (0,) * nd, pipeline_mode=pl.Buffered(1))


def _iota(shape, dim):
    return lax.broadcasted_iota(jnp.int32, shape, dim)


def _ada_kernel(c_ref, w_ref, b_ref, o_ref):
    o_ref[0] = _dot_hi(c_ref[...], w_ref[0]) + b_ref[0]


def _ada(c_all, w_ada, b_ada):
    depth, d, n6 = w_ada.shape
    rows = c_all.shape[0]
    tn = 1024
    return pl.pallas_call(
        _ada_kernel, grid=(depth, n6 // tn),
        in_specs=[pl.BlockSpec((rows, d), lambda l, j: (0, 0)),
                  pl.BlockSpec((1, d, tn), lambda l, j: (l, 0, j)),
                  pl.BlockSpec((1, 1, tn), lambda l, j: (l, 0, j))],
        out_specs=pl.BlockSpec((1, rows, tn), lambda l, j: (l, 0, j)),
        out_shape=jax.ShapeDtypeStruct((depth, rows, n6), F32),
        compiler_params=_cp(("arbitrary", "arbitrary")), name="ada",
    )(c_all, w_ada, b_ada.reshape(depth, 1, n6))


def _norm_mod(x, g, sc, sh):
    y = x * lax.rsqrt(jnp.mean(x * x, axis=-1, keepdims=True) + EPS) * g
    return y * (1.0 + sc) + sh


def _proj_kernel(x_ref, g_ref, sc_ref, sh_ref, w_ref, o32_ref, o16_ref, h_ref):
    @pl.when(pl.program_id(1) == 0)
    def _():
        h_ref[...] = _norm_mod(x_ref[...], g_ref[...], sc_ref[0], sh_ref[0]).astype(BF16)

    acc = _dot(h_ref[...], w_ref[...])
    o32_ref[...] = acc
    o16_ref[...] = acc.astype(BF16)


def _mod_spec(mod, tm, rows_per_group):
    mb = mod.shape[1]
    tiles = max(rows_per_group // tm, 1)
    return pl.BlockSpec((1, mb, D_MODEL), lambda i, *_: (i // tiles, 0, 0))


def _proj(x, g, sc, sh, w, tm, rows_per_group):
    rows = x.shape[0]
    tn = 768
    return pl.pallas_call(
        _proj_kernel, grid=(rows // tm, NP // tn),
        in_specs=[pl.BlockSpec((tm, D_MODEL), lambda i, j: (i, 0)),
                  pl.BlockSpec((1, D_MODEL), lambda i, j: (0, 0)),
                  _mod_spec(sc, tm, rows_per_group), _mod_spec(sh, tm, rows_per_group),
                  pl.BlockSpec((D_MODEL, tn), lambda i, j: (0, j))],
        out_specs=[pl.BlockSpec((tm, tn), lambda i, j: (i, j)), pl.BlockSpec((tm, tn), lambda i, j: (i, j))],
        out_shape=[jax.ShapeDtypeStruct((rows, NP), F32), jax.ShapeDtypeStruct((rows, NP), BF16)],
        scratch_shapes=[pltpu.VMEM((tm, D_MODEL), BF16)],
        compiler_params=_cp(("arbitrary", "arbitrary")), name="proj",
    )(x, g, sc, sh, w)


PAGES_PER_STEP = KEY_TILE // PAGE


def _gather_kernel(pt_ref, *refs, n_arr, n_chunks):
    ppc = PAGES_PER_STEP
    pages, tails, outs = refs[:n_arr * ppc], refs[n_arr * ppc:n_arr * (ppc + 1)], refs[n_arr * (ppc + 1):]
    c = pl.program_id(1)

    @pl.when(c < n_chunks - 1)
    def _():
        for k in range(n_arr):
            for r in range(ppc):
                outs[k][0, r * PAGE:(r + 1) * PAGE, :] = pages[k * ppc + r][0, 0].astype(outs[k].dtype)

    @pl.when(c == n_chunks - 1)
    def _():
        for k in range(n_arr):
            outs[k][0] = tails[k][0].astype(outs[k].dtype)


def _gather_past(page_table, pools, tails, out_dtypes, layer):
    n_seq, n_pages = page_table.shape
    n_arr, ppc = len(pools), PAGES_PER_STEP
    assert n_pages % ppc == 0 and tails[0].shape[1] == ppc * PAGE
    n_chunks = n_pages // ppc + 1
    lp = n_chunks * ppc * PAGE
    in_specs = []
    for p in pools:
        for r in range(ppc):
            in_specs.append(pl.BlockSpec(
                (1, 1, PAGE, p.shape[-1]),
                lambda b, c, pt, r=r: (layer, pt[b, jnp.minimum(c * ppc + r, n_pages - 1)], 0, 0)))
    in_specs += [pl.BlockSpec((1, ppc * PAGE, t.shape[-1]), lambda b, c, pt: (b, 0, 0)) for t in tails]
    args = [p for p in pools for _ in range(ppc)] + list(tails)
    kern = functools.partial(_gather_kernel, n_arr=n_arr, n_chunks=n_chunks)
    return pl.pallas_call(
        kern,
        grid_spec=pltpu.PrefetchScalarGridSpec(
            num_scalar_prefetch=1, grid=(n_seq, n_chunks), in_specs=in_specs,
            out_specs=[pl.BlockSpec((1, ppc * PAGE, p.shape[-1]), lambda b, c, pt: (b, c, 0)) for p in pools]),
        out_shape=[jax.ShapeDtypeStruct((n_seq, lp, p.shape[-1]), dt) for p, dt in zip(pools, out_dtypes)],
        compiler_params=_cp(("arbitrary", "arbitrary")), name="gather_past",
    )(page_table, *args)


def _stack_heads(q, width):
    t = q.shape[0]
    parts = []
    for h in range(N_HEADS):
        p = q[:, h * HEAD_DIM:(h + 1) * HEAD_DIM]
        if width > HEAD_DIM:
            p = jnp.concatenate([p, jnp.zeros((t, width - HEAD_DIM), q.dtype)], axis=1)
        parts.append(p)
    return jnp.concatenate(parts, axis=0)


def _dsa_sel_kernel(qi_ref, misc_ref, ki_ref, bias_ref, s_ref, *planes, tq, nk, k_sel, qoff, fk):
    tk = KEY_TILE
    i = pl.program_id(1)
    q0 = qoff + i * tq
    qpos = q0 + _iota((tq, 1), 0)
    nlim = jnp.minimum((q0 + tq - 1) // tk + 1, nk)
    qs = _stack_heads(qi_ref[...].astype(BF16), fk)
    wi = misc_ref[:, MISC_WI:MISC_WI + N_HEADS]

    def score_tile(j, c):
        kt = ki_ref[0, pl.ds(pl.multiple_of(j * tk, tk), tk), :].astype(BF16)
        d = _dot_nt(qs, kt).reshape(N_HEADS, tq, tk)
        sc = wi[:, 0:1] * jnp.maximum(d[0], 0.0)
        for h in range(1, N_HEADS):
            sc = sc + wi[:, h:h + 1] * jnp.maximum(d[h], 0.0)
        bits = lax.bitcast_convert_type(sc, jnp.int32)
        key = bits ^ ((bits >> 31) & 0x7FFFFFFF)
        key = jnp.where(key == -1, 0, key)
        kpos = j * tk + _iota((1, tk), 1)
        s_ref[j] = jnp.where(kpos <= qpos, key, INT_MIN)
        return c

    def score_pair(j2, c):
        score_tile(2 * j2, c)
        return score_tile(2 * j2 + 1, c)

    lax.fori_loop(0, nlim // 2, score_pair, 0)

    @pl.when(nlim % 2 == 1)
    def _():
        score_tile(nlim - 1, 0)

    if planes:
        p_ref, cand_ref, candx_ref = planes
        n_main, n_extra = min(nk, 32), max(nk - 32, 0)

        def unused_tile(j, c):
            s_ref[j] = jnp.full((tq, tk), INT_MIN, jnp.int32)
            return c

        lax.fori_loop(nlim, nk, unused_tile, 0)

        def transpose_bits(rg, c):
            rows = pl.ds(pl.multiple_of(rg * 8, 8), 8)
            for lc in range(tk // 128):
                lanes = slice(lc * 128, (lc + 1) * 128)
                x = [s_ref[j, rows, lanes] ^ INT_MIN if j < n_main else jnp.zeros((8, 128), jnp.int32)
                     for j in range(32)]
                s, m = 16, 0x0000FFFF
                while s >= 1:
                    for a in range(32):
                        if (a & s) == 0:
                            t = (lax.shift_right_logical(x[a], s) ^ x[a + s]) & m
                            x[a + s] = x[a + s] ^ t
                            x[a] = x[a] ^ lax.shift_left(t, s)
                    s //= 2
                    if s:
                        m = (m ^ (m << s)) & 0xFFFFFFFF
                        m = m - (1 << 32) if m >= (1 << 31) else m
                for b in range(32):
                    p_ref[b, rows, lanes] = x[b]
            return c

        lax.fori_loop(0, tq // 8, transpose_bits, 0)
        cand_ref[...] = jnp.full((tq, tk), -1, jnp.int32)
        for e in range(n_extra):
            candx_ref[e] = jnp.ones((tq, tk), jnp.int32)

        def lane_sum(counts):
            part = counts[:, 0:128]
            for c in range(1, tk // 128):
                part = part + counts[:, c * 128:(c + 1) * 128]
            return jnp.sum(part.astype(F32), axis=-1, keepdims=True)

        def bit_step(t, carry):
            need, thr_u = carry
            b = 31 - t
            ones = cand_ref[...] & p_ref[b]
            cnt = lane_sum(lax.population_count(ones))
            ones_x = []
            for e in range(n_extra):
                bit = lax.shift_right_logical(s_ref[32 + e] ^ INT_MIN, b) & 1
                ones_x.append(candx_ref[e] & bit)
                cnt = cnt + lane_sum(ones_x[e])
            take = cnt >= need
            cand_ref[...] = jnp.where(take, ones, cand_ref[...] & ~p_ref[b])
            for e in range(n_extra):
                candx_ref[e] = jnp.where(take, ones_x[e], candx_ref[e] ^ ones_x[e])
            return jnp.where(take, need, need - cnt), jnp.where(take, thr_u | lax.shift_left(jnp.int32(1), b), thr_u)

        ties_allowed, thr_u = lax.fori_loop(
            0, 32, bit_step, (jnp.full((tq, 1), float(k_sel), F32), jnp.zeros((tq, 1), jnp.int32)))
        thr = thr_u ^ INT_MIN
        n_ties = lane_sum(lax.population_count(cand_ref[...]))
        for e in range(n_extra):
            n_ties = n_ties + lane_sum(candx_ref[e])
        need_rank = jnp.max(jnp.where(n_ties > ties_allowed, 1.0, 0.0)) > 0.5
        ties_allowed_fn = lambda: ties_allowed
    else:
        def count(pred, level):
            level_b = jnp.broadcast_to(level, (tq, 128))

            def body(j, acc):
                for c in range(tk // 128):
                    acc = acc + jnp.where(pred(s_ref[j, :, c * 128:(c + 1) * 128], level_b), 1.0, 0.0)
                return acc
            return jnp.sum(lax.fori_loop(0, nlim, body, jnp.zeros((tq, 128), F32)), axis=-1, keepdims=True)

        def bit_step(t, carry):
            c, n_ge = carry
            trial = c + lax.shift_left(jnp.int32(1), 31 - t)
            cnt = count(lambda s, lv: s >= lv, trial)
            ok = cnt >= k_sel
            return jnp.where(ok, trial, c), jnp.where(ok, cnt, n_ge)

        thr, n_ge = lax.fori_loop(0, 32, bit_step, (jnp.full((tq, 1), INT_MIN, jnp.int32),
                                                    jnp.full((tq, 1), float(nk * tk), F32)))
        need_rank = jnp.max(jnp.where(n_ge > k_sel, 1.0, 0.0)) > 0.5
        ties_allowed_fn = lambda: k_sel - count(lambda s, lv: s > lv, thr)

    @pl.when(need_rank)
    def _():
        ties_allowed = ties_allowed_fn()
        tri = jnp.where(_iota((tk, tk), 0) <= _iota((tk, tk), 1), 1.0, 0.0).astype(BF16)

        def out_tile(j, carry):
            key = s_ref[j]
            eq = jnp.where(key == thr, 1.0, 0.0)
            rank = _dot(eq.astype(BF16), tri) + carry
            kpos = j * tk + _iota((1, tk), 1)
            take = jnp.where(key > thr, 1.0, jnp.where(rank <= ties_allowed, eq, 0.0))
            take = jnp.where(kpos <= qpos, take, 0.0)
            bias_ref[j] = jnp.where(take > 0.5, 0.0, NEG).astype(bias_ref.dtype)
            return rank[:, tk - 1:tk]

        lax.fori_loop(0, nlim, out_tile, jnp.zeros((tq, 1), F32))

    @pl.when(jnp.logical_not(need_rank))
    def _():
        def out_tile(j, c):
            kpos = j * tk + _iota((1, tk), 1)
            take = jnp.where(kpos <= qpos, jnp.where(s_ref[j] >= thr, 1.0, 0.0), 0.0)
            bias_ref[j] = jnp.where(take > 0.5, 0.0, NEG).astype(bias_ref.dtype)
            return c

        lax.fori_loop(0, nlim, out_tile, 0)

    def fill(j, c):
        bias_ref[j] = jnp.full((tq, tk), NEG, bias_ref.dtype)
        return c

    lax.fori_loop(nlim, nk, fill, 0)


def _dsa_select(q_arr, p32, ki3, ki_col, fk, *, n_seq, t_seq, tq, qoff, k_sel, bias_dtype):
    lp = ki3.shape[1]
    nk, nq = lp // KEY_TILE, t_seq // tq
    rows = n_seq * t_seq
    kern = functools.partial(_dsa_sel_kernel, tq=tq, nk=nk, k_sel=k_sel, qoff=qoff, fk=fk)
    return pl.pallas_call(
        kern, grid=(n_seq, nq),
        in_specs=[pl.BlockSpec((tq, MIX_W), lambda b, i: (b * nq + i, _OFF['a_qi'] // MIX_W)),
                  pl.BlockSpec((tq, 128), lambda b, i: (b * nq + i, _OFF['misc'] // 128)),
                  pl.BlockSpec((1, lp, fk), lambda b, i: (b, 0, ki_col))],
        out_specs=pl.BlockSpec((nk, tq, KEY_TILE), lambda b, i: (0, b * nq + i, 0)),
        out_shape=jax.ShapeDtypeStruct((nk, rows, KEY_TILE), bias_dtype),
        scratch_shapes=[pltpu.VMEM((nk, tq, KEY_TILE), jnp.int32)]
        + ([pltpu.VMEM((32, tq, KEY_TILE), jnp.int32), pltpu.VMEM((tq, KEY_TILE), jnp.int32),
            pltpu.VMEM((max(nk - 32, 1), tq, KEY_TILE), jnp.int32)] if nk <= 36 else []),
        compiler_params=_cp(("arbitrary", "arbitrary")), name="dsa_select",
    )(q_arr, p32, ki3)


def _flash_kernel(sb, sq, sk, sfl, q_ref, kv_ref, bias_ref, *rest, tq, tkm, shared, gated):
    if gated:
        misc_ref, add_ref, o_ref, m_ref, l_ref, acc_ref, qs_ref = rest
    else:
        o_ref, m_ref, l_ref, acc_ref, qs_ref = rest
    tk = KEY_TILE
    flags = sfl[pl.program_id(0)]
    dv = HEAD_DIM if shared else MIX_W
    lane_head = _iota((1, MIX_W), 1) // HEAD_DIM

    @pl.when((flags & 1) != 0)
    def _():
        m_ref[...] = jnp.full(m_ref.shape, NEG, F32)
        l_ref[...] = jnp.zeros(l_ref.shape, F32)
        acc_ref[...] = jnp.zeros(acc_ref.shape, F32)
        q = (q_ref[...] * (HEAD_DIM ** -0.5)).astype(BF16)
        if shared:
            qs_ref[...] = _stack_heads(q, HEAD_DIM)
        else:
            qs_ref[...] = jnp.concatenate([jnp.where(lane_head == h, q, jnp.zeros_like(q)) for h in range(N_HEADS)],
                                          axis=0)

    tkk = tkm * tk
    kv = kv_ref[0]
    k = kv[:, :dv].astype(BF16)
    v = kv[:, dv:2 * dv].astype(BF16)
    bias = jnp.concatenate([bias_ref[u].astype(F32) for u in range(tkm)], axis=1)
    s = _dot_nt(qs_ref[...], k).reshape(N_HEADS, tq, tkk) + bias[None]
    m_old = m_ref[...]
    m_new = jnp.maximum(m_old, jnp.max(s, axis=-1, keepdims=True))
    alpha = jnp.exp(m_old - m_new)
    p = jnp.exp(s - m_new)
    l_ref[...] = alpha * l_ref[...] + jnp.sum(p, axis=-1, keepdims=True)
    pv = _dot(p.reshape(N_HEADS * tq, tkk).astype(BF16), v).reshape(N_HEADS, tq, dv)
    acc_ref[...] = alpha * acc_ref[...] + pv
    m_ref[...] = m_new

    @pl.when((flags & 2) != 0)
    def _():
        o = acc_ref[...] / l_ref[...]
        if shared:
            out = jnp.concatenate([o[h] for h in range(N_HEADS)], axis=1)
        else:
            out = jnp.where(lane_head == 0, o[0], 0.0)
            for h in range(1, N_HEADS):
                out = out + jnp.where(lane_head == h, o[h], 0.0)
        if gated:
            g = jax.nn.sigmoid(misc_ref[...])
            gate = jnp.zeros((tq, MIX_W), F32)
            for h in range(N_HEADS):
                c = MISC_G + 3 * h + 1
                gate = gate + jnp.where(lane_head == h, g[:, c:c + 1], 0.0)
            out = add_ref[...].astype(F32) + gate * out
        o_ref[...] = out.astype(o_ref.dtype)


def _flash_steps(n_seq, t_seq, tq, tk, lp, qoff):
    sb, sq, sk, sfl = [], [], [], []
    for b in range(n_seq):
        for i in range(t_seq // tq):
            nlim = min((qoff + (i + 1) * tq - 1) // tk + 1, lp // tk)
            for j in range(nlim):
                sb.append(b), sq.append(i), sk.append(j)
                sfl.append((1 if j == 0 else 0) | (2 if j == nlim - 1 else 0))
    return [jnp.asarray(np.asarray(a, np.int32)) for a in (sb, sq, sk, sfl)]


def _flash(q_arr, q_col, kv3, kv_col, bias, *, n_seq, t_seq, tq, tkm, qoff, shared, out_dtype, gate_args=None):
    lp = kv3.shape[1]
    tk = tkm * KEY_TILE
    nq = t_seq // tq
    rows = n_seq * t_seq
    wblk = 2 * (HEAD_DIM if shared else MIX_W)
    dv = HEAD_DIM if shared else MIX_W
    steps = _flash_steps(n_seq, t_seq, tq, tk, lp, qoff)
    gated = gate_args is not None
    row = lambda s, sb, sq, sk, sfl: sb[s] * nq + sq[s]
    in_specs = [pl.BlockSpec((tq, MIX_W), lambda s, sb, sq, sk, sfl: (row(s, sb, sq, sk, sfl), q_col)),
                pl.BlockSpec((1, tk, wblk), lambda s, sb, sq, sk, sfl: (sb[s], sk[s], kv_col)),
                pl.BlockSpec((tkm, tq, KEY_TILE), lambda s, sb, sq, sk, sfl: (sk[s], row(s, sb, sq, sk, sfl), 0))]
    args = [q_arr, kv3, bias]
    if gated:
        in_specs += [pl.BlockSpec((tq, 128), lambda s, sb, sq, sk, sfl: (row(s, sb, sq, sk, sfl), _OFF['misc'] // 128)),
                     pl.BlockSpec((tq, MIX_W), lambda s, sb, sq, sk, sfl: (row(s, sb, sq, sk, sfl), 0))]
        args += list(gate_args)
    kern = functools.partial(_flash_kernel, tq=tq, tkm=tkm, shared=shared, gated=gated)
    return pl.pallas_call(
        kern,
        grid_spec=pltpu.PrefetchScalarGridSpec(
            num_scalar_prefetch=4, grid=(int(steps[0].shape[0]),), in_specs=in_specs,
            out_specs=pl.BlockSpec((tq, MIX_W), lambda s, sb, sq, sk, sfl: (row(s, sb, sq, sk, sfl), 0)),
            scratch_shapes=[pltpu.VMEM((N_HEADS, tq, 1), F32), pltpu.VMEM((N_HEADS, tq, 1), F32),
                            pltpu.VMEM((N_HEADS, tq, dv), F32), pltpu.VMEM((N_HEADS * tq, dv), BF16)]),
        out_shape=jax.ShapeDtypeStruct((rows, MIX_W), out_dtype),
        compiler_params=_cp(("arbitrary",)), name="flash_shared" if shared else "flash_heads",
    )(*steps, *args)


def _flash_t_kernel(sq, sk, sfl, qt_ref, k_ref, vt_ref, bias_ref, *rest, tq, tkm, shared, gated, block_sel):
    if gated:
        misct_ref, addt_ref, o_ref, m_ref, acc_ref = rest
    else:
        o_ref, m_ref, acc_ref = rest
    tk = KEY_TILE
    flags = sfl[pl.program_id(0)]
    row_head = _iota((MIX_W, 1), 0) // HEAD_DIM
    ones_row = jnp.where(_iota((8, tk), 0) == 0, 1.0, 0.0).astype(BF16)

    @pl.when((flags & 1) != 0)
    def _():
        m_ref[...] = jnp.full(m_ref.shape, NEG, F32)
        acc_ref[...] = jnp.zeros(acc_ref.shape, F32)

    qt = qt_ref[...]
    for u in range(tkm):
        k = k_ref[u * tk:(u + 1) * tk, :]
        k = k[:, :HEAD_DIM] if shared else k
        if block_sel:
            nb = tk // SLC_BLOCK
            kt = sk[pl.program_id(0)] * tkm + u
            blocks = bias_ref[pl.ds(pl.multiple_of(kt * nb, nb), nb), :]
            on = jnp.concatenate([jnp.broadcast_to(blocks[r:r + 1, :], (SLC_BLOCK, tq)) for r in range(nb)], axis=0)
            kpos = kt * tk + _iota((tk, 1), 0)
            qpos = sq[pl.program_id(0)] * tq + _iota((1, tq), 1)
            bias_t = jnp.where((on > 0.5) & (kpos <= qpos), 0.0, NEG)
        else:
            bias_t = bias_ref[u].astype(F32).T
        for h in range(N_HEADS):
            hs = slice(h * HEAD_DIM, (h + 1) * HEAD_DIM)
            q_h = qt[hs, :] if shared else jnp.where(row_head == h, qt, jnp.zeros_like(qt))
            s = _dot(k, q_h) + bias_t
            m_old = m_ref[h]
            m_new = jnp.maximum(m_old, jnp.max(s, axis=0, keepdims=True))
            alpha = jnp.exp2(m_old - m_new)
            p = jnp.exp2(s - m_new)
            vt_h = vt_ref[0:HEAD_DIM, u * tk:(u + 1) * tk] if shared else vt_ref[hs, u * tk:(u + 1) * tk]
            acc_ref[h] = alpha * acc_ref[h] + _dot(jnp.concatenate([vt_h, ones_row], axis=0), p.astype(BF16))
            m_ref[h] = m_new

    @pl.when((flags & 2) != 0)
    def _():
        for h in range(N_HEADS):
            hs = slice(h * HEAD_DIM, (h + 1) * HEAD_DIM)
            acc = acc_ref[h]
            out = acc[0:HEAD_DIM] / acc[HEAD_DIM:HEAD_DIM + 1]
            if gated:
                c = MISC_G + 3 * h + 1
                out = addt_ref[hs, :].astype(F32) + jax.nn.sigmoid(misct_ref[c:c + 1, :]) * out
            o_ref[hs, :] = out.astype(o_ref.dtype)


def _flash_t(qt, k_arr, k_col, k_width, vt, bias, *, t_seq, tq, tkm, shared, gate_args=None):
    tk = tkm * KEY_TILE
    steps = _flash_steps(1, t_seq, tq, tk, t_seq, 0)[1:]
    gated = gate_args is not None
    vrows = vt.shape[0]
    block_sel = bias.ndim == 2
    if block_sel:
        bias_spec = pl.BlockSpec((bias.shape[0], tq), lambda s, sq, sk, sfl: (0, sq[s]))
    else:
        bias_spec = pl.BlockSpec((tkm, tq, KEY_TILE), lambda s, sq, sk, sfl: (sk[s], sq[s], 0))
    in_specs = [pl.BlockSpec((MIX_W, tq), lambda s, sq, sk, sfl: (0, sq[s])),
                pl.BlockSpec((tk, k_width), lambda s, sq, sk, sfl: (sk[s], k_col)),
                pl.BlockSpec((vrows, tk), lambda s, sq, sk, sfl: (0, sk[s])), bias_spec]
    args = [qt, k_arr, vt, bias]
    if gated:
        in_specs += [pl.BlockSpec((128, tq), lambda s, sq, sk, sfl: (0, sq[s])),
                     pl.BlockSpec((MIX_W, tq), lambda s, sq, sk, sfl: (0, sq[s]))]
        args += list(gate_args)
    kern = functools.partial(_flash_t_kernel, tq=tq, tkm=tkm, shared=shared, gated=gated, block_sel=block_sel)
    return pl.pallas_call(
        kern,
        grid_spec=pltpu.PrefetchScalarGridSpec(
            num_scalar_prefetch=3, grid=(int(steps[0].shape[0]),), in_specs=in_specs,
            out_specs=pl.BlockSpec((MIX_W, tq), lambda s, sq, sk, sfl: (0, sq[s])),
            scratch_shapes=[pltpu.VMEM((N_HEADS, 1, tq), F32), pltpu.VMEM((N_HEADS, HEAD_DIM + 8, tq), F32)]),
        out_shape=jax.ShapeDtypeStruct((MIX_W, t_seq), BF16),
        compiler_params=_cp(("arbitrary",)), name="flash_t_shared" if shared else "flash_t_heads",
    )(*steps, *args)


def _compress_kernel(x_ref, w_ref, o_ref, *, n_blocks, ncp):
    w = w_ref[...]
    e = jnp.exp(w - jnp.max(w, axis=-1, keepdims=True))
    w = e / jnp.sum(e, axis=-1, keepdims=True)
    n16 = x_ref.shape[1] // CMP_STRIDE
    width = x_ref.shape[2]
    first = jnp.zeros((n16, width), F32)
    second = jnp.zeros((n16, width), F32)
    for j in range(CMP_STRIDE):
        xj = x_ref[0, pl.ds(j, n16, stride=CMP_STRIDE), :]
        first = first + xj * w[:, j:j + 1]
        second = second + xj * w[:, CMP_STRIDE + j:CMP_STRIDE + j + 1]
    shifted = jnp.concatenate([second[1:], jnp.zeros((1, width), F32)], axis=0)
    out = first + shifted
    if n16 < ncp:
        out = jnp.concatenate([out, jnp.zeros((ncp - n16, width), F32)], axis=0)
    out = out[:ncp]
    o_ref[0] = jnp.where(_iota((ncp, 1), 0) < n_blocks, out, 0.0)


def _compress(rows3, w_pos, length, ncp):
    n_seq, lp, width = rows3.shape
    assert lp % CMP_STRIDE == 0
    n_blocks = -(-length // CMP_STRIDE) - 1
    kern = functools.partial(_compress_kernel, n_blocks=n_blocks, ncp=ncp)
    return pl.pallas_call(
        kern, grid=(n_seq,),
        in_specs=[pl.BlockSpec((1, lp, width), lambda b: (b, 0, 0)),
                  pl.BlockSpec((1, CMP_LEN), lambda b: (0, 0))],
        out_specs=pl.BlockSpec((1, ncp, width), lambda b: (b, 0, 0)),
        out_shape=jax.ShapeDtypeStruct((n_seq, ncp, width), F32),
        compiler_params=_cp(("arbitrary",)), name="compress",
    )(rows3, w_pos.reshape(1, CMP_LEN))


def _masked_softmax(s, mask):
    s = jnp.where(mask, s, NEG)
    m = jnp.max(s, axis=-1, keepdims=True)
    e = jnp.where(mask, jnp.exp(s - m), 0.0)
    return e / jnp.maximum(jnp.sum(e, axis=-1, keepdims=True), 1e-30)


def _nsa_sel_kernel(q_ref, misc_ref, kc_ref, win_ref, ocw_ref, bias_ref, *, tq, nk, qoff, n_cmp, n_blk, nbp, n_top,
                    win_rows, win_dyn, win_pos0, block_out):
    tk = KEY_TILE
    i = pl.program_id(1)
    q0 = qoff + i * tq
    qpos = q0 + _iota((tq, 1), 0)
    nlim = jnp.minimum((q0 + tq - 1) // tk + 1, nk)
    qs = _stack_heads((q_ref[...] * (HEAD_DIM ** -0.5)).astype(BF16), HEAD_DIM)
    ncp = kc_ref.shape[1]

    kcv = kc_ref[0]
    kc = kcv[:, :HEAD_DIM].astype(BF16)
    vc = kcv[:, HEAD_DIM:].astype(BF16)
    cidx = _iota((1, ncp), 1)
    c_mask = ((cidx * CMP_STRIDE + CMP_LEN - 1) <= qpos) & (cidx < n_cmp)
    s_c = _dot_nt(qs, kc).reshape(N_HEADS, tq, ncp)
    p_c = _masked_softmax(s_c, c_mask[None])
    o_c = _dot(p_c.reshape(N_HEADS * tq, ncp).astype(BF16), vc).reshape(N_HEADS, tq, HEAD_DIM)

    c_start = _iota((ncp, 1), 0) * CMP_STRIDE
    s_start = _iota((1, nbp), 1) * SLC_BLOCK
    overlap = (c_start < s_start + SLC_BLOCK) & (c_start + CMP_LEN > s_start) & (_iota((ncp, 1), 0) < n_cmp)
    imp = _dot_x_exact01(p_c[0] + p_c[1] + p_c[2] + p_c[3], jnp.where(overlap, 1.0, 0.0).astype(BF16))
    blk = _iota((1, nbp), 1)
    cur = qpos // SLC_BLOCK
    forced = (blk == 0) | (blk == cur) | (blk == cur - 1)
    live = jnp.where(forced, jnp.inf, jnp.where((blk <= cur) & (blk < n_blk), imp, -jnp.inf))
    blk_f = blk.astype(F32)

    def pick(t, carry):
        live, sel = carry
        top = jnp.max(live, axis=-1, keepdims=True)
        first = jnp.min(jnp.where(live == top, blk_f, float(nbp)), axis=-1, keepdims=True)
        hit = blk_f == first
        return jnp.where(hit, -jnp.inf, live), jnp.where(hit, 1.0, sel)

    if block_out:
        blk_t = _iota((nbp, 1), 0).astype(F32)

        def pick_t(t, carry):
            live_t, sel_t = carry
            top = jnp.max(live_t, axis=0, keepdims=True)
            first = jnp.min(jnp.where(live_t == top, blk_t, float(nbp)), axis=0, keepdims=True)
            hit = blk_t == first
            return jnp.where(hit, -jnp.inf, live_t), jnp.where(hit, 1.0, sel_t)

        _, sel_t = lax.fori_loop(0, n_top, pick_t, (live.T, jnp.zeros((nbp, tq), F32)))
        bias_ref[...] = sel_t
    else:
        _, sel = lax.fori_loop(0, n_top, pick, (live, jnp.zeros((tq, nbp), F32)))
        sel = sel.astype(BF16)
        row_blk = _iota((nbp, 1), 0)

        def bias_tile(j, c):
            kpos = j * tk + _iota((1, tk), 1)
            expand = jnp.where(row_blk == kpos // SLC_BLOCK, 1.0, 0.0).astype(BF16)
            on = _dot(sel, expand)
            ok = (on > 0.5) & (kpos <= qpos)
            bias_ref[j] = jnp.where(ok, 0.0, NEG).astype(bias_ref.dtype)
            return c

        lax.fori_loop(0, nlim, bias_tile, 0)

        def fill(j, c):
            bias_ref[j] = jnp.full((tq, tk), NEG, bias_ref.dtype)
            return c

        lax.fori_loop(nlim, nk, fill, 0)

    if win_dyn:
        start = pl.multiple_of(jnp.maximum(q0 - WINDOW, 0), 8)
        wkv = win_ref[0, pl.ds(start, win_rows), :]
        kwpos = start + _iota((1, win_rows), 1)
    else:
        wkv = win_ref[0]
        kwpos = win_pos0 + _iota((1, win_rows), 1)
    kw = wkv[:, :HEAD_DIM].astype(BF16)
    vw = wkv[:, HEAD_DIM:].astype(BF16)
    rel = qpos - kwpos
    w_mask = (rel >= 0) & (rel < WINDOW) & (kwpos >= 0)
    s_w = _dot_nt(qs, kw).reshape(N_HEADS, tq, win_rows)
    p_w = _masked_softmax(s_w, w_mask[None])
    o_w = _dot(p_w.reshape(N_HEADS * tq, win_rows).astype(BF16), vw).reshape(N_HEADS, tq, HEAD_DIM)

    g = jax.nn.sigmoid(misc_ref[...])
    parts = []
    for h in range(N_HEADS):
        c = MISC_G + 3 * h
        parts.append(g[:, c:c + 1] * o_c[h] + g[:, c + 2:c + 3] * o_w[h])
    ocw_ref[...] = jnp.concatenate(parts, axis=1).astype(ocw_ref.dtype)


def _nsa_select(q_arr, p32, kcvc, win3, win_col, *, n_seq, t_seq, tq, qoff, seq_len, lp, win_dyn, win_pos0, bias_dtype,
                out_dtype):
    nk, nq = lp // KEY_TILE, t_seq // tq
    rows = n_seq * t_seq
    n_cmp = -(-seq_len // CMP_STRIDE) - 1
    n_blk = -(-seq_len // SLC_BLOCK)
    nbp = -(-n_blk // 128) * 128
    win_rows = (tq + WINDOW) if win_dyn else win3.shape[1]
    block_out = bias_dtype is None
    kern = functools.partial(_nsa_sel_kernel, tq=tq, nk=nk, qoff=qoff, n_cmp=n_cmp, n_blk=n_blk, nbp=nbp,
                             n_top=min(N_SLC, n_blk), win_rows=win_rows, win_dyn=win_dyn, win_pos0=win_pos0,
                             block_out=block_out)
    if block_out:
        sel_spec = pl.BlockSpec((nbp, tq), lambda b, i: (0, b * nq + i))
        sel_shape = jax.ShapeDtypeStruct((nbp, rows), F32)
    else:
        sel_spec = pl.BlockSpec((nk, tq, KEY_TILE), lambda b, i: (0, b * nq + i, 0))
        sel_shape = jax.ShapeDtypeStruct((nk, rows, KEY_TILE), bias_dtype)
    return pl.pallas_call(
        kern, grid=(n_seq, nq),
        in_specs=[pl.BlockSpec((tq, MIX_W), lambda b, i: (b * nq + i, _OFF['b_q'] // MIX_W)),
                  pl.BlockSpec((tq, 128), lambda b, i: (b * nq + i, _OFF['misc'] // 128)),
                  pl.BlockSpec((1,) + kcvc.shape[1:], lambda b, i: (b, 0, 0)),
                  pl.BlockSpec((1, win3.shape[1], 128), lambda b, i: (b, 0, win_col))],
        out_specs=[pl.BlockSpec((tq, MIX_W), lambda b, i: (b * nq + i, 0)), sel_spec],
        out_shape=[jax.ShapeDtypeStruct((rows, MIX_W), out_dtype), sel_shape],
        compiler_params=_cp(("arbitrary", "arbitrary")), name="nsa_select",
    )(q_arr, p32, kcvc, win3)


def _gla_tables(c):
    levels = []
    b = c
    while b >= 2:
        levels.append(b)
        b //= 2
    r = np.arange(c)
    mats = [(r[None, :] <= r[:, None]), (r[None, :] > r[:, None])]
    qm, km = [], []
    for b in levels:
        mid = (r // b) * b + b // 2
        upper = r >= mid
        qm.append(upper[:, None] & (r[None, :] >= mid[:, None]) & (r[None, :] <= r[:, None]))
        km.append((~upper)[:, None] & (r[None, :] > r[:, None]) & (r[None, :] < mid[:, None]))
    return np.concatenate(mats + qm + km, axis=0).astype(np.float32), levels


def _gla_kernel(q_ref, k_ref, v_ref, r_ref, misc_ref, wal_ref, bal_ref, gg_ref, m_ref, s0_ref, o_ref, st_ref, s_scr,
                *, tt, c, levels, n_tiles):
    t_idx = pl.program_id(1)
    nl = len(levels)

    @pl.when(t_idx == 0)
    def _():
        s_scr[...] = s0_ref[0]

    lane_head = _iota((1, MIX_W), 1) // HEAD_DIM
    same_head = (_iota((MIX_W, 1), 0) // HEAD_DIM) == lane_head
    eye = jnp.where(_iota((MIX_W, MIX_W), 0) == _iota((MIX_W, MIX_W), 1), 1.0, 0.0).astype(BF16)
    ones_head = jnp.where(same_head, 1.0, 0.0)
    rr, cc = _iota((c, c), 0), _iota((c, c), 1)
    pair_masks = [((rr // b) == (cc // b)) & ((rr % b) >= b // 2) & ((cc % b) < b // 2) for b in levels]
    diag_mask = rr == cc

    z = _dot_hi(misc_ref[...], wal_ref[...]) + bal_ref[...]
    la = (jnp.minimum(z, 0.0) - jnp.log1p(jnp.exp(-jnp.abs(z)))) * (1.0 / GATE_TAU)
    m_all = m_ref[...]

    def bd(x):
        return jnp.concatenate([jnp.where(lane_head == h, x, jnp.zeros_like(x)) for h in range(N_HEADS)], axis=0)

    def unbd(x):
        out = jnp.where(lane_head == 0, x[0:c], 0.0)
        for h in range(1, N_HEADS):
            out = out + jnp.where(lane_head == h, x[h * c:(h + 1) * c], 0.0)
        return out

    for ci in range(tt // c):
        sl = slice(ci * c, (ci + 1) * c)
        e = jnp.exp(_dot_exact01(m_all, la[sl]))
        q = q_ref[sl, :] * (HEAD_DIM ** -0.5)
        k = k_ref[sl, :]
        v = v_ref[sl, :].astype(BF16)
        e_cum, e_rest = e[0:c], e[c:2 * c]
        att = jnp.where(diag_mask[None], _dot_nt(bd(q.astype(BF16)), k.astype(BF16)).reshape(N_HEADS, c, c), 0.0)
        for li in range(nl):
            eq = e[(2 + li) * c:(3 + li) * c]
            ek = e[(2 + nl + li) * c:(3 + nl + li) * c]
            a = _dot_nt(bd((q * eq).astype(BF16)), (k * ek).astype(BF16)).reshape(N_HEADS, c, c)
            att = att + jnp.where(pair_masks[li][None], a, 0.0)
        o_intra = unbd(_dot(att.reshape(N_HEADS * c, c).astype(BF16), v))
        st = s_scr[...]
        o_inter = _dot_nt((q * e_cum).astype(BF16), st.astype(BF16))
        v_t = _dot_nt(eye, v).astype(BF16)
        upd = _dot(v_t, (k * e_rest).astype(BF16))
        s_scr[...] = st * e_cum[c - 1:c, :] + jnp.where(same_head, upd, 0.0)
        o = o_inter + o_intra
        ms = _dot_hi(o * o, ones_head) * (1.0 / HEAD_DIM)
        o = o * lax.rsqrt(ms + EPS) * gg_ref[...]
        r = r_ref[sl, :]
        o_ref[sl, :] = (o * (r * jax.nn.sigmoid(r))).astype(o_ref.dtype)

    @pl.when(t_idx == n_tiles - 1)
    def _():
        st_ref[0] = s_scr[...]


def _gla(p32, wal_pad, b_alpha, g_gla4, st0, *, n_seq, t_seq, tt, out_dtype):
    c = min(GLA_CHUNK, t_seq)
    m_all, levels = _gla_tables(c)
    n_tiles = t_seq // tt
    rows = n_seq * t_seq
    col = lambda name: pl.BlockSpec((tt, MIX_W), lambda b, t, o=_OFF[name] // MIX_W: (b * n_tiles + t, o))
    kern = functools.partial(_gla_kernel, tt=tt, c=c, levels=levels, n_tiles=n_tiles)
    return pl.pallas_call(
        kern, grid=(n_seq, n_tiles),
        in_specs=[col('d_q'), col('d_k'), col('d_v'), col('d_r'),
                  pl.BlockSpec((tt, 128), lambda b, t: (b * n_tiles + t, _OFF['misc'] // 128)),
                  pl.BlockSpec((128, MIX_W), lambda b, t: (0, 0)),
                  pl.BlockSpec((1, MIX_W), lambda b, t: (0, 0)),
                  pl.BlockSpec((1, MIX_W), lambda b, t: (0, 0)),
                  pl.BlockSpec(m_all.shape, lambda b, t: (0, 0)),
                  pl.BlockSpec((1, MIX_W, MIX_W), lambda b, t: (b, 0, 0))],
        out_specs=[pl.BlockSpec((tt, MIX_W), lambda b, t: (b * n_tiles + t, 0)),
                   pl.BlockSpec((1, MIX_W, MIX_W), lambda b, t: (b, 0, 0))],
        out_shape=[jax.ShapeDtypeStruct((rows, MIX_W), out_dtype),
                   jax.ShapeDtypeStruct((n_seq, MIX_W, MIX_W), F32)],
        scratch_shapes=[pltpu.VMEM((MIX_W, MIX_W), F32)],
        compiler_params=_cp(("arbitrary", "arbitrary")), name="gla",
    )(p32, p32, p32, p32, p32, wal_pad, b_alpha.reshape(1, MIX_W), g_gla4, jnp.asarray(m_all, dtype=BF16), st0)


def _dwconv3(u, w, carry_ref, fix_refs, t_seq, tm, first_tile):
    row = _iota((tm, 1), 0)
    u1 = pltpu.roll(u, 1, axis=0)
    u2 = pltpu.roll(u, 2, axis=0)
    if fix_refs is None:
        prev = jnp.where(first_tile, 0.0, carry_ref[0:2, :])
        u1 = jnp.where(row == 0, prev[1:2], u1)
        u2 = jnp.where(row == 0, prev[0:1], jnp.where(row == 1, prev[1:2], u2))
        carry_ref[0:2, :] = u[tm - 2:tm]
    else:
        pos = row % t_seq
        u1 = jnp.where(pos == 0, fix_refs[0][...], u1)
        u2 = jnp.where(pos < 2, fix_refs[1][...], u2)
    return w[0:1] * u2 + w[1:2] * u1 + w[2:3] * u


def _merge_kernel(oa_ref, ob_ref, od_ref, cin_ref, cb_ref, cc_ref, gate_ref, x_ref, g1_ref, wconv_ref, wb_ref,
                  wo_ref, *rest, tm, t_seq, tiles_per_seq, per_row):
    if per_row:
        fix1_ref, fix2_ref, o_ref, carry_ref = rest
        fix = (fix1_ref, fix2_ref)
    else:
        o_ref, carry_ref = rest
        fix = None
    first = (pl.program_id(0) % tiles_per_seq) == 0
    u = cc_ref[...] * cin_ref[...]
    o_c = cb_ref[...] * _dwconv3(u, wconv_ref[...], carry_ref, fix, t_seq, tm, first)
    branches = (oa_ref[...], ob_ref[...], o_c, od_ref[...])
    merged = jnp.zeros((tm, D_MODEL), F32)
    for bi, br in enumerate(branches):
        gate = jax.nn.sigmoid(gate_ref[:, bi * D_MODEL:(bi + 1) * D_MODEL])
        merged = merged + gate * _dot(br.astype(BF16), wb_ref[bi])
    o_ref[...] = x_ref[...] + g1_ref[0] * _dot(merged.astype(BF16), wo_ref[...])


def _merge(oa, ob, od, p32, x, g1, conv_c, wb, wo, fix, *, tm, t_seq):
    rows = x.shape[0]
    per_row = fix is not None
    tiles_per_seq = max(t_seq // tm, 1)
    rowblk = lambda width, colblk=0: pl.BlockSpec((tm, width), lambda i: (i, colblk))
    in_specs = [rowblk(MIX_W), rowblk(MIX_W), rowblk(MIX_W),
                rowblk(MIX_W, _OFF['c_in'] // MIX_W), rowblk(MIX_W, _OFF['c_b'] // MIX_W),
                rowblk(MIX_W, _OFF['c_c'] // MIX_W), rowblk(N_HEADS * D_MODEL, 0), rowblk(D_MODEL),
                _mod_spec(g1, tm, t_seq), _const_spec((CONV_W, MIX_W)), _const_spec((N_HEADS, MIX_W, D_MODEL)),
                _const_spec((D_MODEL, D_MODEL))]
    args = [oa, ob, od, p32, p32, p32, p32, x, g1, conv_c, wb, wo]
    if per_row:
        in_specs += [rowblk(MIX_W), rowblk(MIX_W)]
        args += list(fix)
    kern = functools.partial(_merge_kernel, tm=tm, t_seq=t_seq, tiles_per_seq=tiles_per_seq, per_row=per_row)
    return pl.pallas_call(
        kern, grid=(rows // tm,), in_specs=in_specs, out_specs=rowblk(D_MODEL),
        out_shape=jax.ShapeDtypeStruct((rows, D_MODEL), F32),
        scratch_shapes=[pltpu.VMEM((8, MIX_W), F32)],
        compiler_params=_cp(("arbitrary",)), name="merge",
    )(*args)


def _ffn_kernel(x_ref, gn_ref, sc_ref, sh_ref, g2_ref, wa_ref, wg_ref, wconv_ref, bf_ref, wout_ref, gf_ref, *rest,
                tm, tf, t_seq, tiles_per_seq, per_row):
    if per_row:
        fix1_ref, fix2_ref, o_ref, y_ref, a_ref, carry_ref = rest
    else:
        o_ref, y_ref, a_ref, carry_ref = rest
    first = (pl.program_id(0) % tiles_per_seq) == 0
    x = x_ref[...]
    h = _norm_mod(x, gn_ref[...], sc_ref[0], sh_ref[0]).astype(BF16)
    acc = jnp.zeros((tm, D_MODEL), F32)
    for f in range(D_FF // tf):
        fs = slice(f * tf, (f + 1) * tf)
        a = _dot(h, wa_ref[:, fs])
        g = _dot(h, wg_ref[:, fs])
        a_ref[:, fs] = a
        fix = (fix1_ref.at[:, fs], fix2_ref.at[:, fs]) if per_row else None
        conv = _dwconv3(a, wconv_ref[:, fs], carry_ref.at[:, fs], fix, t_seq, tm, first)
        pre = conv + bf_ref[:, fs]
        act = pre * jax.nn.sigmoid(pre) * g
        acc = acc + _dot(act.astype(BF16), wout_ref[fs, :])
    xn = x + g2_ref[0] * acc
    o_ref[...] = xn
    y_ref[...] = xn * lax.rsqrt(jnp.mean(xn * xn, axis=-1, keepdims=True) + EPS) * gf_ref[...]


def _ffn(x, gn, sc, sh, g2, wa, wg, conv_ffn, b_ffn, wout, g_final, fix, *, tm, t_seq):
    rows = x.shape[0]
    per_row = fix is not None
    tiles_per_seq = max(t_seq // tm, 1)
    tf = 256
    rowblk = lambda width: pl.BlockSpec((tm, width), lambda i: (i, 0))
    in_specs = [rowblk(D_MODEL), _const_spec((1, D_MODEL)), _mod_spec(sc, tm, t_seq), _mod_spec(sh, tm, t_seq),
                _mod_spec(g2, tm, t_seq), _const_spec((D_MODEL, D_FF)), _const_spec((D_MODEL, D_FF)),
                _const_spec((CONV_W, D_FF)), _const_spec((1, D_FF)), _const_spec((D_FF, D_MODEL)),
                _const_spec((1, D_MODEL))]
    args = [x, gn, sc, sh, g2, wa, wg, conv_ffn, b_ffn, wout, g_final]
    if per_row:
        in_specs += [rowblk(D_FF), rowblk(D_FF)]
        args += list(fix)
    kern = functools.partial(_ffn_kernel, tm=tm, tf=tf, t_seq=t_seq, tiles_per_seq=tiles_per_seq, per_row=per_row)
    return pl.pallas_call(
        kern, grid=(rows // tm,), in_specs=in_specs,
        out_specs=[rowblk(D_MODEL), rowblk(D_MODEL), rowblk(D_FF)],
        out_shape=[jax.ShapeDtypeStruct((rows, D_MODEL), F32), jax.ShapeDtypeStruct((rows, D_MODEL), F32),
                   jax.ShapeDtypeStruct((rows, D_FF), F32)],
        scratch_shapes=[pltpu.VMEM((8, D_FF), F32)],
        compiler_params=_cp(("arbitrary",)), name="ffn",
    )(*args)


def _permute_w_in(w):
    offs, o = {}, 0
    for name, n in _IN_SPLITS:
        offs[name] = (o, n)
        o += n

    def c(name):
        s, n = offs[name]
        return w[:, s:s + n]

    pieces = [c('gate'), c('a_k'), c('a_v'), c('a_q'), c('a_qi'), c('b_q'), c('b_cmp'), c('b_slc'), c('b_win'),
              c('a_ki'), c('a_wi'), c('b_g'), c('d_a'), jnp.zeros((w.shape[0], 32), w.dtype),
              c('c_in'), c('c_b'), c('c_c'), c('d_q'), c('d_k'), c('d_v'), c('d_r')]
    return jnp.concatenate(pieces, axis=1).astype(BF16)


def _cols(p, name, width):
    return p[:, _OFF[name]:_OFF[name] + width]


def _state_to_bd(s0):
    b = s0.shape[0]
    s0t = jnp.swapaxes(s0, 2, 3)
    eye = jnp.eye(N_HEADS, dtype=s0.dtype)[None, :, None, :, None]
    return (s0t[:, :, :, None, :] * eye).reshape(b, MIX_W, MIX_W)


def _bd_to_state(st):
    b = st.shape[0]
    st5 = st.reshape(b, N_HEADS, HEAD_DIM, N_HEADS, HEAD_DIM)
    return jnp.stack([jnp.swapaxes(st5[:, h, :, h, :], 1, 2) for h in range(N_HEADS)], axis=1)


def _conv_fix(state, t_seq):
    b, _, c = state.shape
    fix1 = jnp.concatenate([state[:, 1:2], jnp.zeros((b, t_seq - 1, c), state.dtype)], axis=1)
    fix2 = jnp.concatenate([state, jnp.zeros((b, t_seq - 2, c), state.dtype)], axis=1)
    return fix1.reshape(b * t_seq, c), fix2.reshape(b * t_seq, c)


def _layer_weights(l, w_in, w_alpha, g_gla, w_branch, w_out, w_ffn_in, w_ffn_out):
    wal_pad = jnp.zeros((128, MIX_W), F32).at[MISC_DA:MISC_DA + GATE_RANK].set(w_alpha[l])
    return dict(w_in=_permute_w_in(w_in[l]), wal=wal_pad, gg=jnp.tile(g_gla[l], N_HEADS).reshape(1, MIX_W),
                wb=w_branch[l].astype(BF16), wo=w_out[l].astype(BF16),
                wa=w_ffn_in[l][:, :D_FF].astype(BF16), wg=w_ffn_in[l][:, D_FF:].astype(BF16),
                wout=w_ffn_out[l].astype(BF16))


def kernel(x_prompt, x_sample, cache_a_kv, cache_a_idx, cache_b_cmp, cache_b_slc, state_b_win, state_c_conv, state_d_gla, state_ffn_conv, page_table, c_prompt, c_sample, w_ada, b_ada, g_norm1, w_in, w_cmp_pos, conv_c, w_alpha, b_alpha, g_gla, w_branch, w_out, g_norm2, w_ffn_in, conv_ffn, b_ffn, w_ffn_out, g_final):
    bp, seq, d = x_prompt.shape
    bs, tdec, _ = x_sample.shape
    depth = w_ada.shape[0]
    n_pages = page_table.shape[1]
    past = n_pages * PAGE
    w_buf = state_b_win.shape[2]
    assert bp == 1 and d == D_MODEL and seq % 1024 == 0 and tdec == 8 and w_buf == WINDOW

    n_c = bp + bs
    c_all = jnp.concatenate([c_prompt, c_sample, jnp.zeros((-n_c % 8, d), F32)], axis=0)
    mod = _ada(c_all, w_ada, b_ada)

    xp = x_prompt.reshape(bp * seq, d)
    xs = x_sample.reshape(bs * tdec, d)
    rs = bs * tdec
    tail_rows = KEY_TILE
    lps = past + tail_rows
    tkm_s = max(t for t in (1, 2, 3, 4, 6, 8, 11) if (lps // KEY_TILE) % t == 0)
    gfin = g_final.reshape(1, d)
    outs_p, outs_s = [], []
    y_p = y_s = None
    for l in range(depth):
        lw = _layer_weights(l, w_in, w_alpha, g_gla, w_branch, w_out, w_ffn_in, w_ffn_out)
        mp = [mod[l, :bp, k * d:(k + 1) * d].reshape(bp, 1, d) for k in range(6)]
        ms = [jnp.repeat(mod[l, bp:n_c, k * d:(k + 1) * d], tdec, axis=0).reshape(1, rs, d) for k in range(6)]
        gn1, gn2 = g_norm1[l].reshape(1, d), g_norm2[l].reshape(1, d)
        bffn = b_ffn[l].reshape(1, D_FF)

        p32, _ = _proj(xs, gn1, ms[1], ms[0], lw['w_in'], rs, rs)
        new = {n: _cols(p32, n, w).reshape(bs, tdec, w) for n, w in
               (('a_k', 2 * MIX_W), ('misc', IDX_DIM), ('b_cmp', 128), ('b_slc', 128), ('b_win', 128))}
        tails = [jnp.concatenate([new[n], jnp.zeros((bs, tail_rows - tdec, new[n].shape[-1]), F32)], axis=1)
                 for n in ('a_k', 'misc', 'b_cmp', 'b_slc')]
        kv_s, ki_s, cmp_s, slc_s = _gather_past(page_table, [cache_a_kv, cache_a_idx, cache_b_cmp, cache_b_slc],
                                                tails, (BF16, BF16, F32, BF16), l)
        seq_s = past + tdec
        bias_a = _dsa_select(p32, p32, ki_s, 0, IDX_DIM, n_seq=bs, t_seq=tdec, tq=tdec, qoff=past,
                             k_sel=min(A_TOPK, seq_s // 4), bias_dtype=F32)
        o_a = _flash(p32, _OFF['a_q'] // MIX_W, kv_s, 0, bias_a, n_seq=bs, t_seq=tdec, tq=tdec, tkm=tkm_s, qoff=past,
                     shared=False, out_dtype=F32)
        ncp = -(-(-(-seq_s // CMP_STRIDE) - 1) // 128) * 128
        kcvc = _compress(cmp_s, w_cmp_pos[l], seq_s, ncp)
        win_full = jnp.concatenate([state_b_win[l], new['b_win']], axis=1)
        win_pad = jnp.concatenate([win_full, jnp.zeros((bs, -(w_buf + tdec) % 16, 128), F32)], axis=1)
        ocw, bias_b = _nsa_select(p32, p32, kcvc, win_pad, 0, n_seq=bs, t_seq=tdec, tq=tdec, qoff=past, seq_len=seq_s,
                                  lp=lps, win_dyn=False, win_pos0=past - w_buf, bias_dtype=F32, out_dtype=F32)
        o_b = _flash(p32, _OFF['b_q'] // MIX_W, slc_s, 0, bias_b, n_seq=bs, t_seq=tdec, tq=tdec, tkm=tkm_s, qoff=past,
                     shared=True, out_dtype=F32, gate_args=(p32, ocw))
        o_d, st = _gla(p32, lw['wal'], b_alpha[l], lw['gg'], _state_to_bd(state_d_gla[l]), n_seq=bs, t_seq=tdec,
                       tt=tdec, out_dtype=F32)
        u_tail = (_cols(p32, 'c_c', MIX_W) * _cols(p32, 'c_in', MIX_W)).reshape(bs, tdec, MIX_W)[:, tdec - 2:]
        xs = _merge(o_a, o_b, o_d, p32, xs, ms[2], conv_c[l], lw['wb'], lw['wo'], _conv_fix(state_c_conv[l], tdec),
                    tm=rs, t_seq=tdec)
        xs, y_s, a_full = _ffn(xs, gn2, ms[4], ms[3], ms[5], lw['wa'], lw['wg'], conv_ffn[l], bffn, lw['wout'], gfin,
                               _conv_fix(state_ffn_conv[l], tdec), tm=rs, t_seq=tdec)
        outs_s.append((new['a_k'], new['misc'], new['b_cmp'], new['b_slc'], win_full[:, tdec:], u_tail,
                       _bd_to_state(st), a_full.reshape(bs, tdec, D_FF)[:, tdec - 2:]))

        p32, p16 = _proj(xp, gn1, mp[1], mp[0], lw['w_in'], 1024, seq)
        p16_3 = p16.reshape(bp, seq, NP)
        a_kv = _cols(p32, 'a_k', 2 * MIX_W).reshape(bp, seq, 2 * MIX_W)
        a_idx = _cols(p32, 'misc', IDX_DIM).reshape(bp, seq, IDX_DIM)
        b_cmp = _cols(p32, 'b_cmp', 128).reshape(bp, seq, 128)
        b_slc = _cols(p32, 'b_slc', 128).reshape(bp, seq, 128)
        b_win = _cols(p32, 'b_win', 128).reshape(bp, seq, 128)[:, seq - min(WINDOW, seq):]
        k_sel = min(A_TOPK, seq // 4)
        bias_a = _dsa_select(p16, p32, p16_3, _OFF['misc'] // 128, 128, n_seq=bp, t_seq=seq, tq=128, qoff=0,
                             k_sel=k_sel, bias_dtype=BF16)
        q_scale = HEAD_DIM ** -0.5 * float(np.log2(np.e))
        o_a = jnp.transpose(_flash_t(jnp.transpose(_cols(p32, 'a_q', MIX_W) * q_scale).astype(BF16), p16,
                                     _OFF['a_k'] // MIX_W, MIX_W,
                                     jnp.transpose(_cols(p16, 'a_v', MIX_W)), bias_a, t_seq=seq, tq=512, tkm=2,
                                     shared=False))
        ncp = -(-(-(-seq // CMP_STRIDE) - 1) // 128) * 128
        kcvc = _compress(b_cmp, w_cmp_pos[l], seq, ncp)
        ocw, sel_b = _nsa_select(p16, p32, kcvc, p16_3, _OFF['b_win'] // 128, n_seq=bp, t_seq=seq, tq=128, qoff=0,
                                 seq_len=seq, lp=seq, win_dyn=True, win_pos0=0, bias_dtype=None, out_dtype=BF16)
        vt_b = jnp.transpose(p16[:, _OFF['b_slc'] + HEAD_DIM:_OFF['b_slc'] + 2 * HEAD_DIM])
        o_b = jnp.transpose(_flash_t(jnp.transpose(_cols(p32, 'b_q', MIX_W) * q_scale).astype(BF16), p16,
                                     _OFF['b_slc'] // 128, 128, vt_b,
                                     sel_b, t_seq=seq, tq=512, tkm=2, shared=True,
                                     gate_args=(jnp.transpose(_cols(p32, 'misc', 128)), jnp.transpose(ocw))))
        o_d, st = _gla(p32, lw['wal'], b_alpha[l], lw['gg'], jnp.zeros((bp, MIX_W, MIX_W), F32), n_seq=bp, t_seq=seq,
                       tt=512, out_dtype=BF16)
        u_tail = (_cols(p32, 'c_c', MIX_W) * _cols(p32, 'c_in', MIX_W)).reshape(bp, seq, MIX_W)[:, seq - 2:]
        xp = _merge(o_a, o_b, o_d, p32, xp, mp[2], conv_c[l], lw['wb'], lw['wo'], None, tm=256, t_seq=seq)
        xp, y_p, a_full = _ffn(xp, gn2, mp[4], mp[3], mp[5], lw['wa'], lw['wg'], conv_ffn[l], bffn, lw['wout'], gfin,
                               None, tm=512, t_seq=seq)
        outs_p.append((a_kv, a_idx, b_cmp, b_slc, b_win, u_tail, _bd_to_state(st),
                       a_full.reshape(bp, seq, D_FF)[:, seq - 2:]))

    sp = [jnp.stack(z) for z in zip(*outs_p)]
    ss = [jnp.stack(z) for z in zip(*outs_s)]
    res = [y_p.reshape(bp, seq, d), y_s.reshape(bs, tdec, d)]
    for a, b in zip(sp, ss):
        res += [a, b]
    return tuple(res)
```

```python
import functools

import numpy as np
import jax
import jax.numpy as jnp
from jax import lax
from jax.experimental import pallas as pl
from jax.experimental.pallas import tpu as pltpu

F32 = jnp.float32
BF16 = jnp.bfloat16
HI = lax.Precision.HIGHEST

D_MODEL = 1024
PAGE = 128
HEAD_DIM = 64
MIX_W = 256
N_HEADS = 4
IDX_DIM = 64
A_TOPK = 256
CMP_LEN = 32
CMP_STRIDE = 16
SLC_BLOCK = 64
N_SLC = 16
WINDOW = 512
CONV_W = 3
GATE_RANK = 16
GATE_TAU = 16.0
GLA_CHUNK = 64
D_FF = 2816
EPS = 1e-6
NEG = -1e30
INT_MIN = -2 ** 31
KEY_TILE = 512

_IN_SPLITS = (('a_q', 256), ('a_k', 256), ('a_v', 256), ('a_qi', 256), ('a_ki', 64), ('a_wi', 4),
              ('b_q', 256), ('b_cmp', 128), ('b_slc', 128), ('b_win', 128), ('b_g', 12),
              ('c_in', 256), ('c_b', 256), ('c_c', 256),
              ('d_q', 256), ('d_k', 256), ('d_v', 256), ('d_r', 256), ('d_a', 16), ('gate', 4096))
_OFF = dict(gate=0, a_k=4096, a_v=4352, a_q=4608, a_qi=4864, b_q=5120, b_cmp=5376, b_slc=5504, b_win=5632,
            misc=5760, c_in=5888, c_b=6144, c_c=6400, d_q=6656, d_k=6912, d_v=7168, d_r=7424)
NP = 7680
MISC_KI, MISC_WI, MISC_G, MISC_DA = 0, 64, 68, 80
VMEM_LIMIT = 56 * 1024 * 1024


def _cp(sem):
    return pltpu.CompilerParams(dimension_semantics=sem, vmem_limit_bytes=VMEM_LIMIT)


def _dot(a, b):
    return jnp.dot(a, b, preferred_element_type=F32)


def _dot_nt(a, b):
    return lax.dot_general(a, b, (((1,), (1,)), ((), ())), preferred_element_type=F32)


def _dot_hi(a, b):
    return jnp.dot(a, b, preferred_element_type=F32, precision=HI)


def _dot_exact01(a01, x):
    hi = x.astype(BF16)
    lo = (x - hi.astype(F32)).astype(BF16)
    return _dot(a01, hi) + _dot(a01, lo)


def _dot_x_exact01(x, b01):
    hi = x.astype(BF16)
    lo = (x - hi.astype(F32)).astype(BF16)
    return _dot(hi, b01) + _dot(lo, b01)


def _const_spec(shape):
    nd = len(shape)
    return pl.BlockSpec(shape, lambda *a: (0,) * nd, pipeline_mode=pl.Buffered(1))


def _iota(shape, dim):
    return lax.broadcasted_iota(jnp.int32, shape, dim)


def _ada_kernel(c_ref, w_ref, b_ref, o_ref):
    o_ref[0] = _dot_hi(c_ref[...], w_ref[0]) + b_ref[0]


def _ada(c_all, w_ada, b_ada):
    depth, d, n6 = w_ada.shape
    rows = c_all.shape[0]
    tn = 1024
    return pl.pallas_call(
        _ada_kernel, grid=(depth, n6 // tn),
        in_specs=[pl.BlockSpec((rows, d), lambda l, j: (0, 0)),
                  pl.BlockSpec((1, d, tn), lambda l, j: (l, 0, j)),
                  pl.BlockSpec((1, 1, tn), lambda l, j: (l, 0, j))],
        out_specs=pl.BlockSpec((1, rows, tn), lambda l, j: (l, 0, j)),
        out_shape=jax.ShapeDtypeStruct((depth, rows, n6), F32),
        compiler_params=_cp(("arbitrary", "arbitrary")), name="ada",
    )(c_all, w_ada, b_ada.reshape(depth, 1, n6))


def _norm_mod(x, g, sc, sh):
    y = x * lax.rsqrt(jnp.mean(x * x, axis=-1, keepdims=True) + EPS) * g
    return y * (1.0 + sc) + sh


def _proj_kernel(x_ref, g_ref, sc_ref, sh_ref, w_ref, o32_ref, o16_ref, h_ref):
    @pl.when(pl.program_id(1) == 0)
    def _():
        h_ref[...] = _norm_mod(x_ref[...], g_ref[...], sc_ref[0], sh_ref[0]).astype(BF16)

    acc = _dot(h_ref[...], w_ref[...])
    o32_ref[...] = acc
    o16_ref[...] = acc.astype(BF16)


def _mod_spec(mod, tm, rows_per_group):
    mb = mod.shape[1]
    tiles = max(rows_per_group // tm, 1)
    return pl.BlockSpec((1, mb, D_MODEL), lambda i, *_: (i // tiles, 0, 0))


def _proj(x, g, sc, sh, w, tm, rows_per_group):
    rows = x.shape[0]
    tn = 768
    return pl.pallas_call(
        _proj_kernel, grid=(rows // tm, NP // tn),
        in_specs=[pl.BlockSpec((tm, D_MODEL), lambda i, j: (i, 0)),
                  pl.BlockSpec((1, D_MODEL), lambda i, j: (0, 0)),
                  _mod_spec(sc, tm, rows_per_group), _mod_spec(sh, tm, rows_per_group),
                  pl.BlockSpec((D_MODEL, tn), lambda i, j: (0, j))],
        out_specs=[pl.BlockSpec((tm, tn), lambda i, j: (i, j)), pl.BlockSpec((tm, tn), lambda i, j: (i, j))],
        out_shape=[jax.ShapeDtypeStruct((rows, NP), F32), jax.ShapeDtypeStruct((rows, NP), BF16)],
        scratch_shapes=[pltpu.VMEM((tm, D_MODEL), BF16)],
        compiler_params=_cp(("arbitrary", "arbitrary")), name="proj",
    )(x, g, sc, sh, w)


PAGES_PER_STEP = KEY_TILE // PAGE


def _gather_kernel(pt_ref, *refs, n_arr, n_chunks, feature_major):
    ppc = PAGES_PER_STEP
    pages, tails, outs = refs[:n_arr * ppc], refs[n_arr * ppc:n_arr * (ppc + 1)], refs[n_arr * (ppc + 1):]
    c = pl.program_id(1)

    @pl.when(c < n_chunks - 1)
    def _():
        for k in range(n_arr):
            for r in range(ppc):
                page = pages[k * ppc + r][0, 0].astype(outs[k].dtype)
                if feature_major[k]:
                    outs[k][0, 0, :, r * PAGE:(r + 1) * PAGE] = page
                else:
                    outs[k][0, r * PAGE:(r + 1) * PAGE, :] = page

    @pl.when(c == n_chunks - 1)
    def _():
        for k in range(n_arr):
            if feature_major[k]:
                outs[k][0, 0] = tails[k][0].astype(outs[k].dtype)
            else:
                outs[k][0] = tails[k][0].astype(outs[k].dtype)


def _gather_past(page_table, pools, tails, out_dtypes, feature_major, layer):
    n_seq, n_pages = page_table.shape
    n_arr, ppc = len(pools), PAGES_PER_STEP
    assert n_pages % ppc == 0
    n_chunks = n_pages // ppc + 1
    chunk = ppc * PAGE
    in_specs, out_specs, out_shapes = [], [], []
    for p in pools:
        for r in range(ppc):
            in_specs.append(pl.BlockSpec(
                (1, 1) + p.shape[2:],
                lambda b, c, pt, r=r: (layer, pt[b, jnp.minimum(c * ppc + r, n_pages - 1)], 0, 0)))
    in_specs += [pl.BlockSpec((1,) + t.shape[1:], lambda b, c, pt: (b, 0, 0)) for t in tails]
    for p, dt, fm in zip(pools, out_dtypes, feature_major):
        if fm:
            out_specs.append(pl.BlockSpec((1, 1, p.shape[2], chunk), lambda b, c, pt: (b, c, 0, 0)))
            out_shapes.append(jax.ShapeDtypeStruct((n_seq, n_chunks, p.shape[2], chunk), dt))
        else:
            out_specs.append(pl.BlockSpec((1, chunk, p.shape[3]), lambda b, c, pt: (b, c, 0)))
            out_shapes.append(jax.ShapeDtypeStruct((n_seq, n_chunks * chunk, p.shape[3]), dt))
    args = [p for p in pools for _ in range(ppc)] + list(tails)
    kern = functools.partial(_gather_kernel, n_arr=n_arr, n_chunks=n_chunks, feature_major=tuple(feature_major))
    return pl.pallas_call(
        kern,
        grid_spec=pltpu.PrefetchScalarGridSpec(
            num_scalar_prefetch=1, grid=(n_seq, n_chunks), in_specs=in_specs, out_specs=out_specs),
        out_shape=out_shapes,
        compiler_params=_cp(("arbitrary", "arbitrary")), name="gather_past",
    )(page_table, *args)


def _stack_heads(q, width):
    t = q.shape[0]
    parts = []
    for h in range(N_HEADS):
        p = q[:, h * HEAD_DIM:(h + 1) * HEAD_DIM]
        if width > HEAD_DIM:
            p = jnp.concatenate([p, jnp.zeros((t, width - HEAD_DIM), q.dtype)], axis=1)
        parts.append(p)
    return jnp.concatenate(parts, axis=0)


def _dsa_sel_kernel(qi_ref, misc_ref, ki_ref, bias_ref, s_ref, *planes, tq, nk, k_sel, qoff, fk):
    tk = KEY_TILE
    i = pl.program_id(1)
    q0 = qoff + i * tq
    qpos = q0 + _iota((tq, 1), 0)
    nlim = jnp.minimum((q0 + tq - 1) // tk + 1, nk)
    qs = _stack_heads(qi_ref[...].astype(BF16), fk)
    wi = misc_ref[:, MISC_WI:MISC_WI + N_HEADS]

    def score_tile(j, c):
        if len(ki_ref.shape) == 4:
            d = _dot(qs, ki_ref[0, j].astype(BF16)).reshape(N_HEADS, tq, tk)
        else:
            kt = ki_ref[0, pl.ds(pl.multiple_of(j * tk, tk), tk), :].astype(BF16)
            d = _dot_nt(qs, kt).reshape(N_HEADS, tq, tk)
        sc = wi[:, 0:1] * jnp.maximum(d[0], 0.0)
        for h in range(1, N_HEADS):
            sc = sc + wi[:, h:h + 1] * jnp.maximum(d[h], 0.0)
        bits = lax.bitcast_convert_type(sc, jnp.int32)
        key = bits ^ ((bits >> 31) & 0x7FFFFFFF)
        key = jnp.where(key == -1, 0, key)
        kpos = j * tk + _iota((1, tk), 1)
        s_ref[j] = jnp.where(kpos <= qpos, key, INT_MIN)
        return c

    def score_pair(j2, c):
        score_tile(2 * j2, c)
        return score_tile(2 * j2 + 1, c)

    lax.fori_loop(0, nlim // 2, score_pair, 0)

    @pl.when(nlim % 2 == 1)
    def _():
        score_tile(nlim - 1, 0)

    if planes:
        p_ref, cand_ref, candx_ref = planes
        n_main, n_extra = min(nk, 32), max(nk - 32, 0)

        def unused_tile(j, c):
            s_ref[j] = jnp.full((tq, tk), INT_MIN, jnp.int32)
            return c

        lax.fori_loop(nlim, nk, unused_tile, 0)

        def transpose_bits(rg, c):
            rows = pl.ds(pl.multiple_of(rg * 8, 8), 8)
            for lc in range(tk // 128):
                lanes = slice(lc * 128, (lc + 1) * 128)
                x = [s_ref[j, rows, lanes] ^ INT_MIN if j < n_main else jnp.zeros((8, 128), jnp.int32)
                     for j in range(32)]
                s, m = 16, 0x0000FFFF
                while s >= 1:
                    for a in range(32):
                        if (a & s) == 0:
                            t = (lax.shift_right_logical(x[a], s) ^ x[a + s]) & m
                            x[a + s] = x[a + s] ^ t
                            x[a] = x[a] ^ lax.shift_left(t, s)
                    s //= 2
                    if s:
                        m = (m ^ (m << s)) & 0xFFFFFFFF
                        m = m - (1 << 32) if m >= (1 << 31) else m
                for b in range(32):
                    p_ref[b, rows, lanes] = x[b]
            return c

        lax.fori_loop(0, tq // 8, transpose_bits, 0)
        cand_ref[...] = jnp.full((tq, tk), -1, jnp.int32)
        for e in range(n_extra):
            candx_ref[e] = jnp.ones((tq, tk), jnp.int32)

        def lane_sum(counts):
            part = counts[:, 0:128]
            for c in range(1, tk // 128):
                part = part + counts[:, c * 128:(c + 1) * 128]
            return jnp.sum(part.astype(F32), axis=-1, keepdims=True)

        def bit_step(t, carry):
            need, thr_u = carry
            b = 31 - t
            ones = cand_ref[...] & p_ref[b]
            cnt = lane_sum(lax.population_count(ones))
            ones_x = []
            for e in range(n_extra):
                bit = lax.shift_right_logical(s_ref[32 + e] ^ INT_MIN, b) & 1
                ones_x.append(candx_ref[e] & bit)
                cnt = cnt + lane_sum(ones_x[e])
            take = cnt >= need
            cand_ref[...] = jnp.where(take, ones, cand_ref[...] & ~p_ref[b])
            for e in range(n_extra):
                candx_ref[e] = jnp.where(take, ones_x[e], candx_ref[e] ^ ones_x[e])
            return jnp.where(take, need, need - cnt), jnp.where(take, thr_u | lax.shift_left(jnp.int32(1), b), thr_u)

        ties_allowed, thr_u = lax.fori_loop(
            0, 32, bit_step, (jnp.full((tq, 1), float(k_sel), F32), jnp.zeros((tq, 1), jnp.int32)))
        thr = thr_u ^ INT_MIN
        n_ties = lane_sum(lax.population_count(cand_ref[...]))
        for e in range(n_extra):
            n_ties = n_ties + lane_sum(candx_ref[e])
        need_rank = jnp.max(jnp.where(n_ties > ties_allowed, 1.0, 0.0)) > 0.5
        ties_allowed_fn = lambda: ties_allowed
    else:
        def count(pred, level):
            level_b = jnp.broadcast_to(level, (tq, 128))

            def body(j, acc):
                for c in range(tk // 128):
                    acc = acc + jnp.where(pred(s_ref[j, :, c * 128:(c + 1) * 128], level_b), 1.0, 0.0)
                return acc
            return jnp.sum(lax.fori_loop(0, nlim, body, jnp.zeros((tq, 128), F32)), axis=-1, keepdims=True)

        def bit_step(t, carry):
            c, n_ge = carry
            trial = c + lax.shift_left(jnp.int32(1), 31 - t)
            cnt = count(lambda s, lv: s >= lv, trial)
            ok = cnt >= k_sel
            return jnp.where(ok, trial, c), jnp.where(ok, cnt, n_ge)

        thr, n_ge = lax.fori_loop(0, 32, bit_step, (jnp.full((tq, 1), INT_MIN, jnp.int32),
                                                    jnp.full((tq, 1), float(nk * tk), F32)))
        need_rank = jnp.max(jnp.where(n_ge > k_sel, 1.0, 0.0)) > 0.5
        ties_allowed_fn = lambda: k_sel - count(lambda s, lv: s > lv, thr)

    @pl.when(need_rank)
    def _():
        ties_allowed = ties_allowed_fn()
        tri = jnp.where(_iota((tk, tk), 0) <= _iota((tk, tk), 1), 1.0, 0.0).astype(BF16)

        def out_tile(j, carry):
            key = s_ref[j]
            eq = jnp.where(key == thr, 1.0, 0.0)
            rank = _dot(eq.astype(BF16), tri) + carry
            kpos = j * tk + _iota((1, tk), 1)
            take = jnp.where(key > thr, 1.0, jnp.where(rank <= ties_allowed, eq, 0.0))
            take = jnp.where(kpos <= qpos, take, 0.0)
            bias_ref[j] = jnp.where(take > 0.5, 0.0, NEG).astype(bias_ref.dtype)
            return rank[:, tk - 1:tk]

        lax.fori_loop(0, nlim, out_tile, jnp.zeros((tq, 1), F32))

    @pl.when(jnp.logical_not(need_rank))
    def _():
        def out_tile(j, c):
            kpos = j * tk + _iota((1, tk), 1)
            take = jnp.where(kpos <= qpos, jnp.where(s_ref[j] >= thr, 1.0, 0.0), 0.0)
            bias_ref[j] = jnp.where(take > 0.5, 0.0, NEG).astype(bias_ref.dtype)
            return c

        lax.fori_loop(0, nlim, out_tile, 0)

    def fill(j, c):
        bias_ref[j] = jnp.full((tq, tk), NEG, bias_ref.dtype)
        return c

    lax.fori_loop(nlim, nk, fill, 0)


def _dsa_select(q_arr, p32, ki3, ki_col, fk, *, n_seq, t_seq, tq, qoff, k_sel, bias_dtype):
    if ki3.ndim == 4:
        nk = ki3.shape[1]
        ki_spec = pl.BlockSpec((1,) + ki3.shape[1:], lambda b, i: (b, 0, 0, 0))
    else:
        nk = ki3.shape[1] // KEY_TILE
        ki_spec = pl.BlockSpec((1, ki3.shape[1], fk), lambda b, i: (b, 0, ki_col))
    nq = t_seq // tq
    rows = n_seq * t_seq
    kern = functools.partial(_dsa_sel_kernel, tq=tq, nk=nk, k_sel=k_sel, qoff=qoff, fk=fk)
    return pl.pallas_call(
        kern, grid=(n_seq, nq),
        in_specs=[pl.BlockSpec((tq, MIX_W), lambda b, i: (b * nq + i, _OFF['a_qi'] // MIX_W)),
                  pl.BlockSpec((tq, 128), lambda b, i: (b * nq + i, _OFF['misc'] // 128)), ki_spec],
        out_specs=pl.BlockSpec((nk, tq, KEY_TILE), lambda b, i: (0, b * nq + i, 0)),
        out_shape=jax.ShapeDtypeStruct((nk, rows, KEY_TILE), bias_dtype),
        scratch_shapes=[pltpu.VMEM((nk, tq, KEY_TILE), jnp.int32)]
        + ([pltpu.VMEM((32, tq, KEY_TILE), jnp.int32), pltpu.VMEM((tq, KEY_TILE), jnp.int32),
            pltpu.VMEM((max(nk - 32, 1), tq, KEY_TILE), jnp.int32)] if nk <= 36 else []),
        compiler_params=_cp(("arbitrary", "arbitrary")), name="dsa_select",
    )(q_arr, p32, ki3)


def _flash_kernel(sb, sq, sk, sfl, q_ref, kv_ref, bias_ref, *rest, tq, tkm, shared, gated):
    if gated:
        misc_ref, add_ref, o_ref, m_ref, l_ref, acc_ref, qs_ref = rest
    else:
        o_ref, m_ref, l_ref, acc_ref, qs_ref = rest
    tk = KEY_TILE
    flags = sfl[pl.program_id(0)]
    dv = HEAD_DIM if shared else MIX_W
    lane_head = _iota((1, MIX_W), 1) // HEAD_DIM

    @pl.when((flags & 1) != 0)
    def _():
        m_ref[...] = jnp.full(m_ref.shape, NEG, F32)
        l_ref[...] = jnp.zeros(l_ref.shape, F32)
        acc_ref[...] = jnp.zeros(acc_ref.shape, F32)
        q = (q_ref[...] * (HEAD_DIM ** -0.5)).astype(BF16)
        if shared:
            qs_ref[...] = _stack_heads(q, HEAD_DIM)
        else:
            qs_ref[...] = jnp.concatenate([jnp.where(lane_head == h, q, jnp.zeros_like(q)) for h in range(N_HEADS)],
                                          axis=0)

    tkk = tkm * tk
    kv = kv_ref[0]
    k = kv[:, :dv].astype(BF16)
    v = kv[:, dv:2 * dv].astype(BF16)
    bias = jnp.concatenate([bias_ref[u].astype(F32) for u in range(tkm)], axis=1)
    s = _dot_nt(qs_ref[...], k).reshape(N_HEADS, tq, tkk) + bias[None]
    m_old = m_ref[...]
    m_new = jnp.maximum(m_old, jnp.max(s, axis=-1, keepdims=True))
    alpha = jnp.exp(m_old - m_new)
    p = jnp.exp(s - m_new)
    l_ref[...] = alpha * l_ref[...] + jnp.sum(p, axis=-1, keepdims=True)
    pv = _dot(p.reshape(N_HEADS * tq, tkk).astype(BF16), v).reshape(N_HEADS, tq, dv)
    acc_ref[...] = alpha * acc_ref[...] + pv
    m_ref[...] = m_new

    @pl.when((flags & 2) != 0)
    def _():
        o = acc_ref[...] / l_ref[...]
        if shared:
            out = jnp.concatenate([o[h] for h in range(N_HEADS)], axis=1)
        else:
            out = jnp.where(lane_head == 0, o[0], 0.0)
            for h in range(1, N_HEADS):
                out = out + jnp.where(lane_head == h, o[h], 0.0)
        if gated:
            g = jax.nn.sigmoid(misc_ref[...])
            gate = jnp.zeros((tq, MIX_W), F32)
            for h in range(N_HEADS):
                c = MISC_G + 3 * h + 1
                gate = gate + jnp.where(lane_head == h, g[:, c:c + 1], 0.0)
            out = add_ref[...].astype(F32) + gate * out
        o_ref[...] = out.astype(o_ref.dtype)


def _flash_steps(n_seq, t_seq, tq, tk, lp, qoff):
    sb, sq, sk, sfl = [], [], [], []
    for b in range(n_seq):
        for i in range(t_seq // tq):
            nlim = min((qoff + (i + 1) * tq - 1) // tk + 1, lp // tk)
            for j in range(nlim):
                sb.append(b), sq.append(i), sk.append(j)
                sfl.append((1 if j == 0 else 0) | (2 if j == nlim - 1 else 0))
    return [jnp.asarray(np.asarray(a, np.int32)) for a in (sb, sq, sk, sfl)]


def _flash(q_arr, q_col, kv3, kv_col, bias, *, n_seq, t_seq, tq, tkm, qoff, shared, out_dtype, gate_args=None):
    lp = kv3.shape[1]
    tk = tkm * KEY_TILE
    nq = t_seq // tq
    rows = n_seq * t_seq
    wblk = 2 * (HEAD_DIM if shared else MIX_W)
    dv = HEAD_DIM if shared else MIX_W
    steps = _flash_steps(n_seq, t_seq, tq, tk, lp, qoff)
    gated = gate_args is not None
    row = lambda s, sb, sq, sk, sfl: sb[s] * nq + sq[s]
    in_specs = [pl.BlockSpec((tq, MIX_W), lambda s, sb, sq, sk, sfl: (row(s, sb, sq, sk, sfl), q_col)),
                pl.BlockSpec((1, tk, wblk), lambda s, sb, sq, sk, sfl: (sb[s], sk[s], kv_col)),
                pl.BlockSpec((tkm, tq, KEY_TILE), lambda s, sb, sq, sk, sfl: (sk[s], row(s, sb, sq, sk, sfl), 0))]
    args = [q_arr, kv3, bias]
    if gated:
        in_specs += [pl.BlockSpec((tq, 128), lambda s, sb, sq, sk, sfl: (row(s, sb, sq, sk, sfl), _OFF['misc'] // 128)),
                     pl.BlockSpec((tq, MIX_W), lambda s, sb, sq, sk, sfl: (row(s, sb, sq, sk, sfl), 0))]
        args += list(gate_args)
    kern = functools.partial(_flash_kernel, tq=tq, tkm=tkm, shared=shared, gated=gated)
    return pl.pallas_call(
        kern,
        grid_spec=pltpu.PrefetchScalarGridSpec(
            num_scalar_prefetch=4, grid=(int(steps[0].shape[0]),), in_specs=in_specs,
            out_specs=pl.BlockSpec((tq, MIX_W), lambda s, sb, sq, sk, sfl: (row(s, sb, sq, sk, sfl), 0)),
            scratch_shapes=[pltpu.VMEM((N_HEADS, tq, 1), F32), pltpu.VMEM((N_HEADS, tq, 1), F32),
                            pltpu.VMEM((N_HEADS, tq, dv), F32), pltpu.VMEM((N_HEADS * tq, dv), BF16)]),
        out_shape=jax.ShapeDtypeStruct((rows, MIX_W), out_dtype),
        compiler_params=_cp(("arbitrary",)), name="flash_shared" if shared else "flash_heads",
    )(*steps, *args)


def _flash_t_kernel(sq, sk, sfl, qt_ref, k_ref, vt_ref, bias_ref, *rest, tq, tkm, shared, gated, block_sel):
    if gated:
        misct_ref, addt_ref, o_ref, m_ref, acc_ref = rest
    else:
        o_ref, m_ref, acc_ref = rest
    tk = KEY_TILE
    flags = sfl[pl.program_id(0)]
    row_head = _iota((MIX_W, 1), 0) // HEAD_DIM
    ones_row = jnp.where(_iota((8, tk), 0) == 0, 1.0, 0.0).astype(BF16)

    @pl.when((flags & 1) != 0)
    def _():
        m_ref[...] = jnp.full(m_ref.shape, NEG, F32)
        acc_ref[...] = jnp.zeros(acc_ref.shape, F32)

    qt = qt_ref[...]
    for u in range(tkm):
        k = k_ref[u * tk:(u + 1) * tk, :]
        k = k[:, :HEAD_DIM] if shared else k
        if block_sel:
            nb = tk // SLC_BLOCK
            kt = sk[pl.program_id(0)] * tkm + u
            blocks = bias_ref[pl.ds(pl.multiple_of(kt * nb, nb), nb), :]
            on = jnp.concatenate([jnp.broadcast_to(blocks[r:r + 1, :], (SLC_BLOCK, tq)) for r in range(nb)], axis=0)
            kpos = kt * tk + _iota((tk, 1), 0)
            qpos = sq[pl.program_id(0)] * tq + _iota((1, tq), 1)
            bias_t = jnp.where((on > 0.5) & (kpos <= qpos), 0.0, NEG)
        else:
            bias_t = bias_ref[u].astype(F32).T
        for h in range(N_HEADS):
            hs = slice(h * HEAD_DIM, (h + 1) * HEAD_DIM)
            q_h = qt[hs, :] if shared else jnp.where(row_head == h, qt, jnp.zeros_like(qt))
            s = _dot(k, q_h) + bias_t
            m_old = m_ref[h]
            m_new = jnp.maximum(m_old, jnp.max(s, axis=0, keepdims=True))
            alpha = jnp.exp2(m_old - m_new)
            p = jnp.exp2(s - m_new)
            vt_h = vt_ref[0:HEAD_DIM, u * tk:(u + 1) * tk] if shared else vt_ref[hs, u * tk:(u + 1) * tk]
            acc_ref[h] = alpha * acc_ref[h] + _dot(jnp.concatenate([vt_h, ones_row], axis=0), p.astype(BF16))
            m_ref[h] = m_new

    @pl.when((flags & 2) != 0)
    def _():
        for h in range(N_HEADS):
            hs = slice(h * HEAD_DIM, (h + 1) * HEAD_DIM)
            acc = acc_ref[h]
            out = acc[0:HEAD_DIM] / acc[HEAD_DIM:HEAD_DIM + 1]
            if gated:
                c = MISC_G + 3 * h + 1
                out = addt_ref[hs, :].astype(F32) + jax.nn.sigmoid(misct_ref[c:c + 1, :]) * out
            o_ref[hs, :] = out.astype(o_ref.dtype)


def _flash_t(qt, k_arr, k_col, k_width, vt, bias, *, t_seq, tq, tkm, shared, gate_args=None):
    tk = tkm * KEY_TILE
    steps = _flash_steps(1, t_seq, tq, tk, t_seq, 0)[1:]
    gated = gate_args is not None
    vrows = vt.shape[0]
    block_sel = bias.ndim == 2
    if block_sel:
        bias_spec = pl.BlockSpec((bias.shape[0], tq), lambda s, sq, sk, sfl: (0, sq[s]))
    else:
        bias_spec = pl.BlockSpec((tkm, tq, KEY_TILE), lambda s, sq, sk, sfl: (sk[s], sq[s], 0))
    in_specs = [pl.BlockSpec((MIX_W, tq), lambda s, sq, sk, sfl: (0, sq[s])),
                pl.BlockSpec((tk, k_width), lambda s, sq, sk, sfl: (sk[s], k_col)),
                pl.BlockSpec((vrows, tk), lambda s, sq, sk, sfl: (0, sk[s])), bias_spec]
    args = [qt, k_arr, vt, bias]
    if gated:
        in_specs += [pl.BlockSpec((128, tq), lambda s, sq, sk, sfl: (0, sq[s])),
                     pl.BlockSpec((MIX_W, tq), lambda s, sq, sk, sfl: (0, sq[s]))]
        args += list(gate_args)
    kern = functools.partial(_flash_t_kernel, tq=tq, tkm=tkm, shared=shared, gated=gated, block_sel=block_sel)
    return pl.pallas_call(
        kern,
        grid_spec=pltpu.PrefetchScalarGridSpec(
            num_scalar_prefetch=3, grid=(int(steps[0].shape[0]),), in_specs=in_specs,
            out_specs=pl.BlockSpec((MIX_W, tq), lambda s, sq, sk, sfl: (0, sq[s])),
            scratch_shapes=[pltpu.VMEM((N_HEADS, 1, tq), F32), pltpu.VMEM((N_HEADS, HEAD_DIM + 8, tq), F32)]),
        out_shape=jax.ShapeDtypeStruct((MIX_W, t_seq), BF16),
        compiler_params=_cp(("arbitrary",)), name="flash_t_shared" if shared else "flash_t_heads",
    )(*steps, *args)


def _compress_kernel(x_ref, w_ref, o_ref, *, n_blocks, ncp):
    w = w_ref[...]
    e = jnp.exp(w - jnp.max(w, axis=-1, keepdims=True))
    w = e / jnp.sum(e, axis=-1, keepdims=True)
    n16 = x_ref.shape[1] // CMP_STRIDE
    width = x_ref.shape[2]
    first = jnp.zeros((n16, width), F32)
    second = jnp.zeros((n16, width), F32)
    for j in range(CMP_STRIDE):
        xj = x_ref[0, pl.ds(j, n16, stride=CMP_STRIDE), :]
        first = first + xj * w[:, j:j + 1]
        second = second + xj * w[:, CMP_STRIDE + j:CMP_STRIDE + j + 1]
    shifted = jnp.concatenate([second[1:], jnp.zeros((1, width), F32)], axis=0)
    out = first + shifted
    if n16 < ncp:
        out = jnp.concatenate([out, jnp.zeros((ncp - n16, width), F32)], axis=0)
    out = out[:ncp]
    o_ref[0] = jnp.where(_iota((ncp, 1), 0) < n_blocks, out, 0.0)


def _compress(rows3, w_pos, length, ncp):
    n_seq, lp, width = rows3.shape
    assert lp % CMP_STRIDE == 0
    n_blocks = -(-length // CMP_STRIDE) - 1
    kern = functools.partial(_compress_kernel, n_blocks=n_blocks, ncp=ncp)
    return pl.pallas_call(
        kern, grid=(n_seq,),
        in_specs=[pl.BlockSpec((1, lp, width), lambda b: (b, 0, 0)),
                  pl.BlockSpec((1, CMP_LEN), lambda b: (0, 0))],
        out_specs=pl.BlockSpec((1, ncp, width), lambda b: (b, 0, 0)),
        out_shape=jax.ShapeDtypeStruct((n_seq, ncp, width), F32),
        compiler_params=_cp(("arbitrary",)), name="compress",
    )(rows3, w_pos.reshape(1, CMP_LEN))


def _masked_softmax(s, mask):
    s = jnp.where(mask, s, NEG)
    m = jnp.max(s, axis=-1, keepdims=True)
    e = jnp.where(mask, jnp.exp(s - m), 0.0)
    return e / jnp.maximum(jnp.sum(e, axis=-1, keepdims=True), 1e-30)


def _nsa_sel_kernel(q_ref, misc_ref, kc_ref, win_ref, ocw_ref, bias_ref, *, tq, nk, qoff, n_cmp, n_blk, nbp, n_top,
                    win_rows, win_dyn, win_pos0, block_out):
    tk = KEY_TILE
    i = pl.program_id(1)
    q0 = qoff + i * tq
    qpos = q0 + _iota((tq, 1), 0)
    nlim = jnp.minimum((q0 + tq - 1) // tk + 1, nk)
    qs = _stack_heads((q_ref[...] * (HEAD_DIM ** -0.5)).astype(BF16), HEAD_DIM)
    ncp = kc_ref.shape[1]

    kcv = kc_ref[0]
    kc = kcv[:, :HEAD_DIM].astype(BF16)
    vc = kcv[:, HEAD_DIM:].astype(BF16)
    cidx = _iota((1, ncp), 1)
    c_mask = ((cidx * CMP_STRIDE + CMP_LEN - 1) <= qpos) & (cidx < n_cmp)
    s_c = _dot_nt(qs, kc).reshape(N_HEADS, tq, ncp)
    p_c = _masked_softmax(s_c, c_mask[None])
    o_c = _dot(p_c.reshape(N_HEADS * tq, ncp).astype(BF16), vc).reshape(N_HEADS, tq, HEAD_DIM)

    c_start = _iota((ncp, 1), 0) * CMP_STRIDE
    s_start = _iota((1, nbp), 1) * SLC_BLOCK
    overlap = (c_start < s_start + SLC_BLOCK) & (c_start + CMP_LEN > s_start) & (_iota((ncp, 1), 0) < n_cmp)
    imp = _dot_x_exact01(p_c[0] + p_c[1] + p_c[2] + p_c[3], jnp.where(overlap, 1.0, 0.0).astype(BF16))
    blk = _iota((1, nbp), 1)
    cur = qpos // SLC_BLOCK
    forced = (blk == 0) | (blk == cur) | (blk == cur - 1)
    live = jnp.where(forced, jnp.inf, jnp.where((blk <= cur) & (blk < n_blk), imp, -jnp.inf))
    blk_f = blk.astype(F32)

    def pick(t, carry):
        live, sel = carry
        top = jnp.max(live, axis=-1, keepdims=True)
        first = jnp.min(jnp.where(live == top, blk_f, float(nbp)), axis=-1, keepdims=True)
        hit = blk_f == first
        return jnp.where(hit, -jnp.inf, live), jnp.where(hit, 1.0, sel)

    if block_out:
        blk_t = _iota((nbp, 1), 0).astype(F32)

        def pick_t(t, carry):
            live_t, sel_t = carry
            top = jnp.max(live_t, axis=0, keepdims=True)
            first = jnp.min(jnp.where(live_t == top, blk_t, float(nbp)), axis=0, keepdims=True)
            hit = blk_t == first
            return jnp.where(hit, -jnp.inf, live_t), jnp.where(hit, 1.0, sel_t)

        _, sel_t = lax.fori_loop(0, n_top, pick_t, (live.T, jnp.zeros((nbp, tq), F32)))
        bias_ref[...] = sel_t
    else:
        _, sel = lax.fori_loop(0, n_top, pick, (live, jnp.zeros((tq, nbp), F32)))
        sel = sel.astype(BF16)
        row_blk = _iota((nbp, 1), 0)

        def bias_tile(j, c):
            kpos = j * tk + _iota((1, tk), 1)
            expand = jnp.where(row_blk == kpos // SLC_BLOCK, 1.0, 0.0).astype(BF16)
            on = _dot(sel, expand)
            ok = (on > 0.5) & (kpos <= qpos)
            bias_ref[j] = jnp.where(ok, 0.0, NEG).astype(bias_ref.dtype)
            return c

        lax.fori_loop(0, nlim, bias_tile, 0)

        def fill(j, c):
            bias_ref[j] = jnp.full((tq, tk), NEG, bias_ref.dtype)
            return c

        lax.fori_loop(nlim, nk, fill, 0)

    if win_dyn:
        start = pl.multiple_of(jnp.maximum(q0 - WINDOW, 0), 8)
        wkv = win_ref[0, pl.ds(start, win_rows), :]
        kwpos = start + _iota((1, win_rows), 1)
    else:
        wkv = win_ref[0]
        kwpos = win_pos0 + _iota((1, win_rows), 1)
    kw = wkv[:, :HEAD_DIM].astype(BF16)
    vw = wkv[:, HEAD_DIM:].astype(BF16)
    rel = qpos - kwpos
    w_mask = (rel >= 0) & (rel < WINDOW) & (kwpos >= 0)
    s_w = _dot_nt(qs, kw).reshape(N_HEADS, tq, win_rows)
    p_w = _masked_softmax(s_w, w_mask[None])
    o_w = _dot(p_w.reshape(N_HEADS * tq, win_rows).astype(BF16), vw).reshape(N_HEADS, tq, HEAD_DIM)

    g = jax.nn.sigmoid(misc_ref[...])
    parts = []
    for h in range(N_HEADS):
        c = MISC_G + 3 * h
        parts.append(g[:, c:c + 1] * o_c[h] + g[:, c + 2:c + 3] * o_w[h])
    ocw_ref[...] = jnp.concatenate(parts, axis=1).astype(ocw_ref.dtype)


def _nsa_select(q_arr, p32, kcvc, win3, win_col, *, n_seq, t_seq, tq, qoff, seq_len, lp, win_dyn, win_pos0, bias_dtype,
                out_dtype):
    nk, nq = lp // KEY_TILE, t_seq // tq
    rows = n_seq * t_seq
    n_cmp = -(-seq_len // CMP_STRIDE) - 1
    n_blk = -(-seq_len // SLC_BLOCK)
    nbp = -(-n_blk // 128) * 128
    win_rows = (tq + WINDOW) if win_dyn else win3.shape[1]
    block_out = bias_dtype is None
    kern = functools.partial(_nsa_sel_kernel, tq=tq, nk=nk, qoff=qoff, n_cmp=n_cmp, n_blk=n_blk, nbp=nbp,
                             n_top=min(N_SLC, n_blk), win_rows=win_rows, win_dyn=win_dyn, win_pos0=win_pos0,
                             block_out=block_out)
    if block_out:
        sel_spec = pl.BlockSpec((nbp, tq), lambda b, i: (0, b * nq + i))
        sel_shape = jax.ShapeDtypeStruct((nbp, rows), F32)
    else:
        sel_spec = pl.BlockSpec((nk, tq, KEY_TILE), lambda b, i: (0, b * nq + i, 0))
        sel_shape = jax.ShapeDtypeStruct((nk, rows, KEY_TILE), bias_dtype)
    return pl.pallas_call(
        kern, grid=(n_seq, nq),
        in_specs=[pl.BlockSpec((tq, MIX_W), lambda b, i: (b * nq + i, _OFF['b_q'] // MIX_W)),
                  pl.BlockSpec((tq, 128), lambda b, i: (b * nq + i, _OFF['misc'] // 128)),
                  pl.BlockSpec((1,) + kcvc.shape[1:], lambda b, i: (b, 0, 0)),
                  pl.BlockSpec((1, win3.shape[1], 128), lambda b, i: (b, 0, win_col))],
        out_specs=[pl.BlockSpec((tq, MIX_W), lambda b, i: (b * nq + i, 0)), sel_spec],
        out_shape=[jax.ShapeDtypeStruct((rows, MIX_W), out_dtype), sel_shape],
        compiler_params=_cp(("arbitrary", "arbitrary")), name="nsa_select",
    )(q_arr, p32, kcvc, win3)


def _gla_tables(c):
    levels = []
    b = c
    while b >= 2:
        levels.append(b)
        b //= 2
    r = np.arange(c)
    mats = [(r[None, :] <= r[:, None]), (r[None, :] > r[:, None])]
    qm, km = [], []
    for b in levels:
        mid = (r // b) * b + b // 2
        upper = r >= mid
        qm.append(upper[:, None] & (r[None, :] >= mid[:, None]) & (r[None, :] <= r[:, None]))
        km.append((~upper)[:, None] & (r[None, :] > r[:, None]) & (r[None, :] < mid[:, None]))
    return np.concatenate(mats + qm + km, axis=0).astype(np.float32), levels


def _gla_kernel(q_ref, k_ref, v_ref, r_ref, misc_ref, wal_ref, bal_ref, gg_ref, m_ref, s0_ref, o_ref, st_ref, s_scr,
                *, tt, c, levels, n_tiles):
    t_idx = pl.program_id(1)
    nl = len(levels)

    @pl.when(t_idx == 0)
    def _():
        s_scr[...] = s0_ref[0]

    lane_head = _iota((1, MIX_W), 1) // HEAD_DIM
    same_head = (_iota((MIX_W, 1), 0) // HEAD_DIM) == lane_head
    eye = jnp.where(_iota((MIX_W, MIX_W), 0) == _iota((MIX_W, MIX_W), 1), 1.0, 0.0).astype(BF16)
    ones_head = jnp.where(same_head, 1.0, 0.0)
    rr, cc = _iota((c, c), 0), _iota((c, c), 1)
    pair_masks = [((rr // b) == (cc // b)) & ((rr % b) >= b // 2) & ((cc % b) < b // 2) for b in levels]
    diag_mask = rr == cc

    z = _dot_hi(misc_ref[...], wal_ref[...]) + bal_ref[...]
    la = (jnp.minimum(z, 0.0) - jnp.log1p(jnp.exp(-jnp.abs(z)))) * (1.0 / GATE_TAU)
    m_all = m_ref[...]

    def bd(x):
        return jnp.concatenate([jnp.where(lane_head == h, x, jnp.zeros_like(x)) for h in range(N_HEADS)], axis=0)

    def unbd(x):
        out = jnp.where(lane_head == 0, x[0:c], 0.0)
        for h in range(1, N_HEADS):
            out = out + jnp.where(lane_head == h, x[h * c:(h + 1) * c], 0.0)
        return out

    for ci in range(tt // c):
        sl = slice(ci * c, (ci + 1) * c)
        e = jnp.exp(_dot_exact01(m_all, la[sl]))
        q = q_ref[sl, :] * (HEAD_DIM ** -0.5)
        k = k_ref[sl, :]
        v = v_ref[sl, :].astype(BF16)
        e_cum, e_rest = e[0:c], e[c:2 * c]
        att = jnp.where(diag_mask[None], _dot_nt(bd(q.astype(BF16)), k.astype(BF16)).reshape(N_HEADS, c, c), 0.0)
        for li in range(nl):
            eq = e[(2 + li) * c:(3 + li) * c]
            ek = e[(2 + nl + li) * c:(3 + nl + li) * c]
            a = _dot_nt(bd((q * eq).astype(BF16)), (k * ek).astype(BF16)).reshape(N_HEADS, c, c)
            att = att + jnp.where(pair_masks[li][None], a, 0.0)
        o_intra = unbd(_dot(att.reshape(N_HEADS * c, c).astype(BF16), v))
        st = s_scr[...]
        o_inter = _dot_nt((q * e_cum).astype(BF16), st.astype(BF16))
        v_t = _dot_nt(eye, v).astype(BF16)
        upd = _dot(v_t, (k * e_rest).astype(BF16))
        s_scr[...] = st * e_cum[c - 1:c, :] + jnp.where(same_head, upd, 0.0)
        o = o_inter + o_intra
        ms = _dot_hi(o * o, ones_head) * (1.0 / HEAD_DIM)
        o = o * lax.rsqrt(ms + EPS) * gg_ref[...]
        r = r_ref[sl, :]
        o_ref[sl, :] = (o * (r * jax.nn.sigmoid(r))).astype(o_ref.dtype)

    @pl.when(t_idx == n_tiles - 1)
    def _():
        st_ref[0] = s_scr[...]


def _gla(p32, wal_pad, b_alpha, g_gla4, st0, *, n_seq, t_seq, tt, out_dtype):
    c = min(GLA_CHUNK, t_seq)
    m_all, levels = _gla_tables(c)
    n_tiles = t_seq // tt
    rows = n_seq * t_seq
    col = lambda name: pl.BlockSpec((tt, MIX_W), lambda b, t, o=_OFF[name] // MIX_W: (b * n_tiles + t, o))
    kern = functools.partial(_gla_kernel, tt=tt, c=c, levels=levels, n_tiles=n_tiles)
    return pl.pallas_call(
        kern, grid=(n_seq, n_tiles),
        in_specs=[col('d_q'), col('d_k'), col('d_v'), col('d_r'),
                  pl.BlockSpec((tt, 128), lambda b, t: (b * n_tiles + t, _OFF['misc'] // 128)),
                  pl.BlockSpec((128, MIX_W), lambda b, t: (0, 0)),
                  pl.BlockSpec((1, MIX_W), lambda b, t: (0, 0)),
                  pl.BlockSpec((1, MIX_W), lambda b, t: (0, 0)),
                  pl.BlockSpec(m_all.shape, lambda b, t: (0, 0)),
                  pl.BlockSpec((1, MIX_W, MIX_W), lambda b, t: (b, 0, 0))],
        out_specs=[pl.BlockSpec((tt, MIX_W), lambda b, t: (b * n_tiles + t, 0)),
                   pl.BlockSpec((1, MIX_W, MIX_W), lambda b, t: (b, 0, 0))],
        out_shape=[jax.ShapeDtypeStruct((rows, MIX_W), out_dtype),
                   jax.ShapeDtypeStruct((n_seq, MIX_W, MIX_W), F32)],
        scratch_shapes=[pltpu.VMEM((MIX_W, MIX_W), F32)],
        compiler_params=_cp(("arbitrary", "arbitrary")), name="gla",
    )(p32, p32, p32, p32, p32, wal_pad, b_alpha.reshape(1, MIX_W), g_gla4, jnp.asarray(m_all, dtype=BF16), st0)


def _dwconv3(u, w, carry_ref, fix_refs, t_seq, tm, first_tile):
    row = _iota((tm, 1), 0)
    u1 = pltpu.roll(u, 1, axis=0)
    u2 = pltpu.roll(u, 2, axis=0)
    if fix_refs is None:
        prev = jnp.where(first_tile, 0.0, carry_ref[0:2, :])
        u1 = jnp.where(row == 0, prev[1:2], u1)
        u2 = jnp.where(row == 0, prev[0:1], jnp.where(row == 1, prev[1:2], u2))
        carry_ref[0:2, :] = u[tm - 2:tm]
    else:
        pos = row % t_seq
        u1 = jnp.where(pos == 0, fix_refs[0][...], u1)
        u2 = jnp.where(pos < 2, fix_refs[1][...], u2)
    return w[0:1] * u2 + w[1:2] * u1 + w[2:3] * u


def _merge_kernel(oa_ref, ob_ref, od_ref, cin_ref, cb_ref, cc_ref, gate_ref, x_ref, g1_ref, wconv_ref, wb_ref,
                  wo_ref, *rest, tm, t_seq, tiles_per_seq, per_row):
    if per_row:
        fix1_ref, fix2_ref, o_ref, carry_ref = rest
        fix = (fix1_ref, fix2_ref)
    else:
        o_ref, carry_ref = rest
        fix = None
    first = (pl.program_id(0) % tiles_per_seq) == 0
    u = cc_ref[...] * cin_ref[...]
    o_c = cb_ref[...] * _dwconv3(u, wconv_ref[...], carry_ref, fix, t_seq, tm, first)
    branches = (oa_ref[...], ob_ref[...], o_c, od_ref[...])
    merged = jnp.zeros((tm, D_MODEL), F32)
    for bi, br in enumerate(branches):
        gate = jax.nn.sigmoid(gate_ref[:, bi * D_MODEL:(bi + 1) * D_MODEL])
        merged = merged + gate * _dot(br.astype(BF16), wb_ref[bi])
    o_ref[...] = x_ref[...] + g1_ref[0] * _dot(merged.astype(BF16), wo_ref[...])


def _merge(oa, ob, od, p32, x, g1, conv_c, wb, wo, fix, *, tm, t_seq):
    rows = x.shape[0]
    per_row = fix is not None
    tiles_per_seq = max(t_seq // tm, 1)
    rowblk = lambda width, colblk=0: pl.BlockSpec((tm, width), lambda i: (i, colblk))
    in_specs = [rowblk(MIX_W), rowblk(MIX_W), rowblk(MIX_W),
                rowblk(MIX_W, _OFF['c_in'] // MIX_W), rowblk(MIX_W, _OFF['c_b'] // MIX_W),
                rowblk(MIX_W, _OFF['c_c'] // MIX_W), rowblk(N_HEADS * D_MODEL, 0), rowblk(D_MODEL),
                _mod_spec(g1, tm, t_seq), _const_spec((CONV_W, MIX_W)), _const_spec((N_HEADS, MIX_W, D_MODEL)),
                _const_spec((D_MODEL, D_MODEL))]
    args = [oa, ob, od, p32, p32, p32, p32, x, g1, conv_c, wb, wo]
    if per_row:
        in_specs += [rowblk(MIX_W), rowblk(MIX_W)]
        args += list(fix)
    kern = functools.partial(_merge_kernel, tm=tm, t_seq=t_seq, tiles_per_seq=tiles_per_seq, per_row=per_row)
    return pl.pallas_call(
        kern, grid=(rows // tm,), in_specs=in_specs, out_specs=rowblk(D_MODEL),
        out_shape=jax.ShapeDtypeStruct((rows, D_MODEL), F32),
        scratch_shapes=[pltpu.VMEM((8, MIX_W), F32)],
        compiler_params=_cp(("arbitrary",)), name="merge",
    )(*args)


def _ffn_kernel(x_ref, gn_ref, sc_ref, sh_ref, g2_ref, wa_ref, wg_ref, wconv_ref, bf_ref, wout_ref, gf_ref, *rest,
                tm, tf, t_seq, tiles_per_seq, per_row):
    if per_row:
        fix1_ref, fix2_ref, o_ref, y_ref, a_ref, carry_ref = rest
    else:
        o_ref, y_ref, a_ref, carry_ref = rest
    first = (pl.program_id(0) % tiles_per_seq) == 0
    x = x_ref[...]
    h = _norm_mod(x, gn_ref[...], sc_ref[0], sh_ref[0]).astype(BF16)
    acc = jnp.zeros((tm, D_MODEL), F32)
    for f in range(D_FF // tf):
        fs = slice(f * tf, (f + 1) * tf)
        a = _dot(h, wa_ref[:, fs])
        g = _dot(h, wg_ref[:, fs])
        a_ref[:, fs] = a
        fix = (fix1_ref.at[:, fs], fix2_ref.at[:, fs]) if per_row else None
        conv = _dwconv3(a, wconv_ref[:, fs], carry_ref.at[:, fs], fix, t_seq, tm, first)
        pre = conv + bf_ref[:, fs]
        act = pre * jax.nn.sigmoid(pre) * g
        acc = acc + _dot(act.astype(BF16), wout_ref[fs, :])
    xn = x + g2_ref[0] * acc
    o_ref[...] = xn
    y_ref[...] = xn * lax.rsqrt(jnp.mean(xn * xn, axis=-1, keepdims=True) + EPS) * gf_ref[...]


def _ffn(x, gn, sc, sh, g2, wa, wg, conv_ffn, b_ffn, wout, g_final, fix, *, tm, t_seq):
    rows = x.shape[0]
    per_row = fix is not None
    tiles_per_seq = max(t_seq // tm, 1)
    tf = 256
    rowblk = lambda width: pl.BlockSpec((tm, width), lambda i: (i, 0))
    in_specs = [rowblk(D_MODEL), _const_spec((1, D_MODEL)), _mod_spec(sc, tm, t_seq), _mod_spec(sh, tm, t_seq),
                _mod_spec(g2, tm, t_seq), _const_spec((D_MODEL, D_FF)), _const_spec((D_MODEL, D_FF)),
                _const_spec((CONV_W, D_FF)), _const_spec((1, D_FF)), _const_spec((D_FF, D_MODEL)),
                _const_spec((1, D_MODEL))]
    args = [x, gn, sc, sh, g2, wa, wg, conv_ffn, b_ffn, wout, g_final]
    if per_row:
        in_specs += [rowblk(D_FF), rowblk(D_FF)]
        args += list(fix)
    kern = functools.partial(_ffn_kernel, tm=tm, tf=tf, t_seq=t_seq, tiles_per_seq=tiles_per_seq, per_row=per_row)
    return pl.pallas_call(
        kern, grid=(rows // tm,), in_specs=in_specs,
        out_specs=[rowblk(D_MODEL), rowblk(D_MODEL), rowblk(D_FF)],
        out_shape=[jax.ShapeDtypeStruct((rows, D_MODEL), F32), jax.ShapeDtypeStruct((rows, D_MODEL), F32),
                   jax.ShapeDtypeStruct((rows, D_FF), F32)],
        scratch_shapes=[pltpu.VMEM((8, D_FF), F32)],
        compiler_params=_cp(("arbitrary",)), name="ffn",
    )(*args)


def _permute_w_in(w):
    offs, o = {}, 0
    for name, n in _IN_SPLITS:
        offs[name] = (o, n)
        o += n

    def c(name):
        s, n = offs[name]
        return w[:, s:s + n]

    pieces = [c('gate'), c('a_k'), c('a_v'), c('a_q'), c('a_qi'), c('b_q'), c('b_cmp'), c('b_slc'), c('b_win'),
              c('a_ki'), c('a_wi'), c('b_g'), c('d_a'), jnp.zeros((w.shape[0], 32), w.dtype),
              c('c_in'), c('c_b'), c('c_c'), c('d_q'), c('d_k'), c('d_v'), c('d_r')]
    return jnp.concatenate(pieces, axis=1).astype(BF16)


def _cols(p, name, width):
    return p[:, _OFF[name]:_OFF[name] + width]


def _state_to_bd(s0):
    b = s0.shape[0]
    s0t = jnp.swapaxes(s0, 2, 3)
    eye = jnp.eye(N_HEADS, dtype=s0.dtype)[None, :, None, :, None]
    return (s0t[:, :, :, None, :] * eye).reshape(b, MIX_W, MIX_W)


def _bd_to_state(st):
    b = st.shape[0]
    st5 = st.reshape(b, N_HEADS, HEAD_DIM, N_HEADS, HEAD_DIM)
    return jnp.stack([jnp.swapaxes(st5[:, h, :, h, :], 1, 2) for h in range(N_HEADS)], axis=1)


def _conv_fix(state, t_seq):
    b, _, c = state.shape
    fix1 = jnp.concatenate([state[:, 1:2], jnp.zeros((b, t_seq - 1, c), state.dtype)], axis=1)
    fix2 = jnp.concatenate([state, jnp.zeros((b, t_seq - 2, c), state.dtype)], axis=1)
    return fix1.reshape(b * t_seq, c), fix2.reshape(b * t_seq, c)


def _layer_weights(l, w_in, w_alpha, g_gla, w_branch, w_out, w_ffn_in, w_ffn_out):
    wal_pad = jnp.zeros((128, MIX_W), F32).at[MISC_DA:MISC_DA + GATE_RANK].set(w_alpha[l])
    return dict(w_in=_permute_w_in(w_in[l]), wal=wal_pad, gg=jnp.tile(g_gla[l], N_HEADS).reshape(1, MIX_W),
                wb=w_branch[l].astype(BF16), wo=w_out[l].astype(BF16),
                wa=w_ffn_in[l][:, :D_FF].astype(BF16), wg=w_ffn_in[l][:, D_FF:].astype(BF16),
                wout=w_ffn_out[l].astype(BF16))


def kernel(x_prompt, x_sample, cache_a_kv, cache_a_idx, cache_b_cmp, cache_b_slc, state_b_win, state_c_conv, state_d_gla, state_ffn_conv, page_table, c_prompt, c_sample, w_ada, b_ada, g_norm1, w_in, w_cmp_pos, conv_c, w_alpha, b_alpha, g_gla, w_branch, w_out, g_norm2, w_ffn_in, conv_ffn, b_ffn, w_ffn_out, g_final):
    bp, seq, d = x_prompt.shape
    bs, tdec, _ = x_sample.shape
    depth = w_ada.shape[0]
    n_pages = page_table.shape[1]
    past = n_pages * PAGE
    w_buf = state_b_win.shape[2]
    assert bp == 1 and d == D_MODEL and seq % 1024 == 0 and tdec == 8 and w_buf == WINDOW

    n_c = bp + bs
    c_all = jnp.concatenate([c_prompt, c_sample, jnp.zeros((-n_c % 8, d), F32)], axis=0)
    mod = _ada(c_all, w_ada, b_ada)

    xp = x_prompt.reshape(bp * seq, d)
    xs = x_sample.reshape(bs * tdec, d)
    rs = bs * tdec
    tail_rows = KEY_TILE
    lps = past + tail_rows
    tkm_s = max(t for t in (1, 2, 3, 4, 6, 8, 11) if (lps // KEY_TILE) % t == 0)
    gfin = g_final.reshape(1, d)
    outs_p, outs_s = [], []
    y_p = y_s = None
    for l in range(depth):
        lw = _layer_weights(l, w_in, w_alpha, g_gla, w_branch, w_out, w_ffn_in, w_ffn_out)
        mp = [mod[l, :bp, k * d:(k + 1) * d].reshape(bp, 1, d) for k in range(6)]
        ms = [jnp.repeat(mod[l, bp:n_c, k * d:(k + 1) * d], tdec, axis=0).reshape(1, rs, d) for k in range(6)]
        gn1, gn2 = g_norm1[l].reshape(1, d), g_norm2[l].reshape(1, d)
        bffn = b_ffn[l].reshape(1, D_FF)

        p32, _ = _proj(xs, gn1, ms[1], ms[0], lw['w_in'], rs, rs)
        new = {n: _cols(p32, n, w).reshape(bs, tdec, w) for n, w in
               (('a_k', 2 * MIX_W), ('misc', IDX_DIM), ('b_cmp', 128), ('b_slc', 128), ('b_win', 128))}
        tails = [jnp.concatenate([new[n], jnp.zeros((bs, tail_rows - tdec, new[n].shape[-1]), F32)], axis=1)
                 for n in ('a_k', 'misc', 'b_cmp', 'b_slc')]
        tails[1] = jnp.swapaxes(tails[1], 1, 2)
        kv_s, ki_s, cmp_s, slc_s = _gather_past(
            page_table, [cache_a_kv, jnp.swapaxes(cache_a_idx, 2, 3), cache_b_cmp, cache_b_slc], tails,
            (BF16, BF16, F32, BF16), (False, True, False, False), l)
        seq_s = past + tdec
        bias_a = _dsa_select(p32, p32, ki_s, 0, IDX_DIM, n_seq=bs, t_seq=tdec, tq=tdec, qoff=past,
                             k_sel=min(A_TOPK, seq_s // 4), bias_dtype=F32)
        o_a = _flash(p32, _OFF['a_q'] // MIX_W, kv_s, 0, bias_a, n_seq=bs, t_seq=tdec, tq=tdec, tkm=tkm_s, qoff=past,
                     shared=False, out_dtype=F32)
        ncp = -(-(-(-seq_s // CMP_STRIDE) - 1) // 128) * 128
        kcvc = _compress(cmp_s, w_cmp_pos[l], seq_s, ncp)
        win_full = jnp.concatenate([state_b_win[l], new['b_win']], axis=1)
        win_pad = jnp.concatenate([win_full, jnp.zeros((bs, -(w_buf + tdec) % 16, 128), F32)], axis=1)
        ocw, bias_b = _nsa_select(p32, p32, kcvc, win_pad, 0, n_seq=bs, t_seq=tdec, tq=tdec, qoff=past, seq_len=seq_s,
                                  lp=lps, win_dyn=False, win_pos0=past - w_buf, bias_dtype=F32, out_dtype=F32)
        o_b = _flash(p32, _OFF['b_q'] // MIX_W, slc_s, 0, bias_b, n_seq=bs, t_seq=tdec, tq=tdec, tkm=tkm_s, qoff=past,
                     shared=True, out_dtype=F32, gate_args=(p32, ocw))
        o_d, st = _gla(p32, lw['wal'], b_alpha[l], lw['gg'], _state_to_bd(state_d_gla[l]), n_seq=bs, t_seq=tdec,
                       tt=tdec, out_dtype=F32)
        u_tail = (_cols(p32, 'c_c', MIX_W) * _cols(p32, 'c_in', MIX_W)).reshape(bs, tdec, MIX_W)[:, tdec - 2:]
        xs = _merge(o_a, o_b, o_d, p32, xs, ms[2], conv_c[l], lw['wb'], lw['wo'], _conv_fix(state_c_conv[l], tdec),
                    tm=rs, t_seq=tdec)
        xs, y_s, a_full = _ffn(xs, gn2, ms[4], ms[3], ms[5], lw['wa'], lw['wg'], conv_ffn[l], bffn, lw['wout'], gfin,
                               _conv_fix(state_ffn_conv[l], tdec), tm=rs, t_seq=tdec)
        outs_s.append((new['a_k'], new['misc'], new['b_cmp'], new['b_slc'], win_full[:, tdec:], u_tail,
                       _bd_to_state(st), a_full.reshape(bs, tdec, D_FF)[:, tdec - 2:]))

        p32, p16 = _proj(xp, gn1, mp[1], mp[0], lw['w_in'], 1024, seq)
        p16_3 = p16.reshape(bp, seq, NP)
        a_kv = _cols(p32, 'a_k', 2 * MIX_W).reshape(bp, seq, 2 * MIX_W)
        a_idx = _cols(p32, 'misc', IDX_DIM).reshape(bp, seq, IDX_DIM)
        b_cmp = _cols(p32, 'b_cmp', 128).reshape(bp, seq, 128)
        b_slc = _cols(p32, 'b_slc', 128).reshape(bp, seq, 128)
        b_win = _cols(p32, 'b_win', 128).reshape(bp, seq, 128)[:, seq - min(WINDOW, seq):]
        k_sel = min(A_TOPK, seq // 4)
        bias_a = _dsa_select(p16, p32, p16_3, _OFF['misc'] // 128, 128, n_seq=bp, t_seq=seq, tq=128, qoff=0,
                             k_sel=k_sel, bias_dtype=BF16)
        q_scale = HEAD_DIM ** -0.5 * float(np.log2(np.e))
        o_a = jnp.transpose(_flash_t(jnp.transpose(_cols(p32, 'a_q', MIX_W) * q_scale).astype(BF16), p16,
                                     _OFF['a_k'] // MIX_W, MIX_W,
                                     jnp.transpose(_cols(p16, 'a_v', MIX_W)), bias_a, t_seq=seq, tq=512, tkm=2,
                                     shared=False))
        ncp = -(-(-(-seq // CMP_STRIDE) - 1) // 128) * 128
        kcvc = _compress(b_cmp, w_cmp_pos[l], seq, ncp)
        ocw, sel_b = _nsa_select(p16, p32, kcvc, p16_3, _OFF['b_win'] // 128, n_seq=bp, t_seq=seq, tq=128, qoff=0,
                                 seq_len=seq, lp=seq, win_dyn=True, win_pos0=0, bias_dtype=None, out_dtype=BF16)
        vt_b = jnp.transpose(p16[:, _OFF['b_slc'] + HEAD_DIM:_OFF['b_slc'] + 2 * HEAD_DIM])
        o_b = jnp.transpose(_flash_t(jnp.transpose(_cols(p32, 'b_q', MIX_W) * q_scale).astype(BF16), p16,
                                     _OFF['b_slc'] // 128, 128, vt_b,
                                     sel_b, t_seq=seq, tq=512, tkm=2, shared=True,
                                     gate_args=(jnp.transpose(_cols(p32, 'misc', 128)), jnp.transpose(ocw))))
        o_d, st = _gla(p32, lw['wal'], b_alpha[l], lw['gg'], jnp.zeros((bp, MIX_W, MIX_W), F32), n_seq=bp, t_seq=seq,
                       tt=512, out_dtype=BF16)
        u_tail = (_cols(p32, 'c_c', MIX_W) * _cols(p32, 'c_in', MIX_W)).reshape(bp, seq, MIX_W)[:, seq - 2:]
        xp = _merge(o_a, o_b, o_d, p32, xp, mp[2], conv_c[l], lw['wb'], lw['wo'], None, tm=256, t_seq=seq)
        xp, y_p, a_full = _ffn(xp, gn2, mp[4], mp[3], mp[5], lw['wa'], lw['wg'], conv_ffn[l], bffn, lw['wout'], gfin,
                               None, tm=512, t_seq=seq)
        outs_p.append((a_kv, a_idx, b_cmp, b_slc, b_win, u_tail, _bd_to_state(st),
                       a_full.reshape(bp, seq, D_FF)[:, seq - 2:]))

    sp = [jnp.stack(z) for z in zip(*outs_p)]
    ss = [jnp.stack(z) for z in zip(*outs_s)]
    res = [y_p.reshape(bp, seq, d), y_s.reshape(bs, tdec, d)]
    for a, b in zip(sp, ss):
        res += [a, b]
    return tuple(res)
```

```python
import functools

import numpy as np
import jax
import jax.numpy as jnp
from jax import lax
from jax.experimental import pallas as pl
from jax.experimental.pallas import tpu as pltpu

F32 = jnp.float32
BF16 = jnp.bfloat16
HI = lax.Precision.HIGHEST

D_MODEL = 1024
PAGE = 128
HEAD_DIM = 64
MIX_W = 256
N_HEADS = 4
IDX_DIM = 64
A_TOPK = 256
CMP_LEN = 32
CMP_STRIDE = 16
SLC_BLOCK = 64
N_SLC = 16
WINDOW = 512
CONV_W = 3
GATE_RANK = 16
GATE_TAU = 16.0
GLA_CHUNK = 64
D_FF = 2816
EPS = 1e-6
NEG = -1e30
INT_MIN = -2 ** 31
KEY_TILE = 512

_IN_SPLITS = (('a_q', 256), ('a_k', 256), ('a_v', 256), ('a_qi', 256), ('a_ki', 64), ('a_wi', 4),
              ('b_q', 256), ('b_cmp', 128), ('b_slc', 128), ('b_win', 128), ('b_g', 12),
              ('c_in', 256), ('c_b', 256), ('c_c', 256),
              ('d_q', 256), ('d_k', 256), ('d_v', 256), ('d_r', 256), ('d_a', 16), ('gate', 4096))
_OFF = dict(gate=0, a_k=4096, a_v=4352, a_q=4608, a_qi=4864, b_q=5120, b_cmp=5376, b_slc=5504, b_win=5632,
            misc=5760, c_in=5888, c_b=6144, c_c=6400, d_q=6656, d_k=6912, d_v=7168, d_r=7424)
NP = 7680
MISC_KI, MISC_WI, MISC_G, MISC_DA = 0, 64, 68, 80
VMEM_LIMIT = 56 * 1024 * 1024


def _cp(sem):
    return pltpu.CompilerParams(dimension_semantics=sem, vmem_limit_bytes=VMEM_LIMIT)


def _dot(a, b):
    return jnp.dot(a, b, preferred_element_type=F32)


def _dot_nt(a, b):
    return lax.dot_general(a, b, (((1,), (1,)), ((), ())), preferred_element_type=F32)


def _dot_hi(a, b):
    return jnp.dot(a, b, preferred_element_type=F32, precision=HI)


def _dot_exact01(a01, x):
    hi = x.astype(BF16)
    lo = (x - hi.astype(F32)).astype(BF16)
    return _dot(a01, hi) + _dot(a01, lo)


def _dot_x_exact01(x, b01):
    hi = x.astype(BF16)
    lo = (x - hi.astype(F32)).astype(BF16)
    return _dot(hi, b01) + _dot(lo, b01)


def _const_spec(shape):
    nd = len(shape)
    return pl.BlockSpec(shape, lambda *a: (0,) * nd, pipeline_mode=pl.Buffered(1))


def _iota(shape, dim):
    return lax.broadcasted_iota(jnp.int32, shape, dim)


def _ada_kernel(c_ref, w_ref, b_ref, o_ref):
    o_ref[0] = _dot_hi(c_ref[...], w_ref[0]) + b_ref[0]


def _ada(c_all, w_ada, b_ada):
    depth, d, n6 = w_ada.shape
    rows = c_all.shape[0]
    tn = 1024
    return pl.pallas_call(
        _ada_kernel, grid=(depth, n6 // tn),
        in_specs=[pl.BlockSpec((rows, d), lambda l, j: (0, 0)),
                  pl.BlockSpec((1, d, tn), lambda l, j: (l, 0, j)),
                  pl.BlockSpec((1, 1, tn), lambda l, j: (l, 0, j))],
        out_specs=pl.BlockSpec((1, rows, tn), lambda l, j: (l, 0, j)),
        out_shape=jax.ShapeDtypeStruct((depth, rows, n6), F32),
        compiler_params=_cp(("arbitrary", "arbitrary")), name="ada",
    )(c_all, w_ada, b_ada.reshape(depth, 1, n6))


def _norm_mod(x, g, sc, sh):
    y = x * lax.rsqrt(jnp.mean(x * x, axis=-1, keepdims=True) + EPS) * g
    return y * (1.0 + sc) + sh


def _proj_kernel(x_ref, g_ref, sc_ref, sh_ref, w_ref, o32_ref, o16_ref, h_ref):
    @pl.when(pl.program_id(1) == 0)
    def _():
        h_ref[...] = _norm_mod(x_ref[...], g_ref[...], sc_ref[0], sh_ref[0]).astype(BF16)

    acc = _dot(h_ref[...], w_ref[...])
    o32_ref[...] = acc
    o16_ref[...] = acc.astype(BF16)


def _mod_spec(mod, tm, rows_per_group):
    mb = mod.shape[1]
    tiles = max(rows_per_group // tm, 1)
    return pl.BlockSpec((1, mb, D_MODEL), lambda i, *_: (i // tiles, 0, 0))


def _proj(x, g, sc, sh, w, tm, rows_per_group):
    rows = x.shape[0]
    tn = 1536
    return pl.pallas_call(
        _proj_kernel, grid=(rows // tm, NP // tn),
        in_specs=[pl.BlockSpec((tm, D_MODEL), lambda i, j: (i, 0)),
                  pl.BlockSpec((1, D_MODEL), lambda i, j: (0, 0)),
                  _mod_spec(sc, tm, rows_per_group), _mod_spec(sh, tm, rows_per_group),
                  pl.BlockSpec((D_MODEL, tn), lambda i, j: (0, j))],
        out_specs=[pl.BlockSpec((tm, tn), lambda i, j: (i, j)), pl.BlockSpec((tm, tn), lambda i, j: (i, j))],
        out_shape=[jax.ShapeDtypeStruct((rows, NP), F32), jax.ShapeDtypeStruct((rows, NP), BF16)],
        scratch_shapes=[pltpu.VMEM((tm, D_MODEL), BF16)],
        compiler_params=_cp(("arbitrary", "arbitrary")), name="proj",
    )(x, g, sc, sh, w)


PAGES_PER_STEP = KEY_TILE // PAGE


def _gather_kernel(pt_ref, *refs, n_arr, n_chunks, feature_major):
    ppc = PAGES_PER_STEP
    pages, tails, outs = refs[:n_arr * ppc], refs[n_arr * ppc:n_arr * (ppc + 1)], refs[n_arr * (ppc + 1):]
    c = pl.program_id(1)

    @pl.when(c < n_chunks - 1)
    def _():
        for k in range(n_arr):
            for r in range(ppc):
                page = pages[k * ppc + r][0, 0].astype(outs[k].dtype)
                if feature_major[k]:
                    outs[k][0, 0, :, r * PAGE:(r + 1) * PAGE] = page
                else:
                    outs[k][0, r * PAGE:(r + 1) * PAGE, :] = page

    @pl.when(c == n_chunks - 1)
    def _():
        for k in range(n_arr):
            if feature_major[k]:
                outs[k][0, 0] = tails[k][0].astype(outs[k].dtype)
            else:
                outs[k][0] = tails[k][0].astype(outs[k].dtype)


def _gather_past(page_table, pools, tails, out_dtypes, feature_major, layer):
    n_seq, n_pages = page_table.shape
    n_arr, ppc = len(pools), PAGES_PER_STEP
    assert n_pages % ppc == 0
    n_chunks = n_pages // ppc + 1
    chunk = ppc * PAGE
    in_specs, out_specs, out_shapes = [], [], []
    for p in pools:
        for r in range(ppc):
            in_specs.append(pl.BlockSpec(
                (1, 1) + p.shape[2:],
                lambda b, c, pt, r=r: (layer, pt[b, jnp.minimum(c * ppc + r, n_pages - 1)], 0, 0)))
    in_specs += [pl.BlockSpec((1,) + t.shape[1:], lambda b, c, pt: (b, 0, 0)) for t in tails]
    for p, dt, fm in zip(pools, out_dtypes, feature_major):
        if fm:
            out_specs.append(pl.BlockSpec((1, 1, p.shape[2], chunk), lambda b, c, pt: (b, c, 0, 0)))
            out_shapes.append(jax.ShapeDtypeStruct((n_seq, n_chunks, p.shape[2], chunk), dt))
        else:
            out_specs.append(pl.BlockSpec((1, chunk, p.shape[3]), lambda b, c, pt: (b, c, 0)))
            out_shapes.append(jax.ShapeDtypeStruct((n_seq, n_chunks * chunk, p.shape[3]), dt))
    args = [p for p in pools for _ in range(ppc)] + list(tails)
    kern = functools.partial(_gather_kernel, n_arr=n_arr, n_chunks=n_chunks, feature_major=tuple(feature_major))
    return pl.pallas_call(
        kern,
        grid_spec=pltpu.PrefetchScalarGridSpec(
            num_scalar_prefetch=1, grid=(n_seq, n_chunks), in_specs=in_specs, out_specs=out_specs),
        out_shape=out_shapes,
        compiler_params=_cp(("arbitrary", "arbitrary")), name="gather_past",
    )(page_table, *args)


def _stack_heads(q, width):
    t = q.shape[0]
    parts = []
    for h in range(N_HEADS):
        p = q[:, h * HEAD_DIM:(h + 1) * HEAD_DIM]
        if width > HEAD_DIM:
            p = jnp.concatenate([p, jnp.zeros((t, width - HEAD_DIM), q.dtype)], axis=1)
        parts.append(p)
    return jnp.concatenate(parts, axis=0)


def _dsa_sel_kernel(qi_ref, misc_ref, ki_ref, bias_ref, s_ref, *planes, tq, nk, k_sel, qoff, fk):
    tk = KEY_TILE
    i = pl.program_id(1)
    q0 = qoff + i * tq
    qpos = q0 + _iota((tq, 1), 0)
    nlim = jnp.minimum((q0 + tq - 1) // tk + 1, nk)
    qs = _stack_heads(qi_ref[...].astype(BF16), fk)
    wi = misc_ref[:, MISC_WI:MISC_WI + N_HEADS]

    def score_tile(j, c):
        if len(ki_ref.shape) == 4:
            d = _dot(qs, ki_ref[0, j].astype(BF16)).reshape(N_HEADS, tq, tk)
        else:
            kt = ki_ref[0, pl.ds(pl.multiple_of(j * tk, tk), tk), :].astype(BF16)
            d = _dot_nt(qs, kt).reshape(N_HEADS, tq, tk)
        sc = wi[:, 0:1] * jnp.maximum(d[0], 0.0)
        for h in range(1, N_HEADS):
            sc = sc + wi[:, h:h + 1] * jnp.maximum(d[h], 0.0)
        bits = lax.bitcast_convert_type(sc, jnp.int32)
        key = bits ^ ((bits >> 31) & 0x7FFFFFFF)
        key = jnp.where(key == -1, 0, key)
        kpos = j * tk + _iota((1, tk), 1)
        s_ref[j] = jnp.where(kpos <= qpos, key, INT_MIN)
        return c

    def score_pair(j2, c):
        score_tile(2 * j2, c)
        return score_tile(2 * j2 + 1, c)

    lax.fori_loop(0, nlim // 2, score_pair, 0)

    @pl.when(nlim % 2 == 1)
    def _():
        score_tile(nlim - 1, 0)

    if planes:
        p_ref, cand_ref, candx_ref = planes
        n_main, n_extra = min(nk, 32), max(nk - 32, 0)

        def unused_tile(j, c):
            s_ref[j] = jnp.full((tq, tk), INT_MIN, jnp.int32)
            return c

        lax.fori_loop(nlim, nk, unused_tile, 0)

        def transpose_bits(rg, c):
            rows = pl.ds(pl.multiple_of(rg * 8, 8), 8)
            for lc in range(tk // 128):
                lanes = slice(lc * 128, (lc + 1) * 128)
                x = [s_ref[j, rows, lanes] ^ INT_MIN if j < n_main else jnp.zeros((8, 128), jnp.int32)
                     for j in range(32)]
                s, m = 16, 0x0000FFFF
                while s >= 1:
                    for a in range(32):
                        if (a & s) == 0:
                            t = (lax.shift_right_logical(x[a], s) ^ x[a + s]) & m
                            x[a + s] = x[a + s] ^ t
                            x[a] = x[a] ^ lax.shift_left(t, s)
                    s //= 2
                    if s:
                        m = (m ^ (m << s)) & 0xFFFFFFFF
                        m = m - (1 << 32) if m >= (1 << 31) else m
                for b in range(32):
                    p_ref[b, rows, lanes] = x[b]
            return c

        lax.fori_loop(0, tq // 8, transpose_bits, 0)
        cand_ref[...] = jnp.full((tq, tk), -1, jnp.int32)
        for e in range(n_extra):
            candx_ref[e] = jnp.ones((tq, tk), jnp.int32)

        def lane_sum(counts):
            part = counts[:, 0:128]
            for c in range(1, tk // 128):
                part = part + counts[:, c * 128:(c + 1) * 128]
            return jnp.sum(part.astype(F32), axis=-1, keepdims=True)

        def bit_step(t, carry):
            need, thr_u = carry
            b = 31 - t
            ones = cand_ref[...] & p_ref[b]
            cnt = lane_sum(lax.population_count(ones))
            ones_x = []
            for e in range(n_extra):
                bit = lax.shift_right_logical(s_ref[32 + e] ^ INT_MIN, b) & 1
                ones_x.append(candx_ref[e] & bit)
                cnt = cnt + lane_sum(ones_x[e])
            take = cnt >= need
            cand_ref[...] = jnp.where(take, ones, cand_ref[...] & ~p_ref[b])
            for e in range(n_extra):
                candx_ref[e] = jnp.where(take, ones_x[e], candx_ref[e] ^ ones_x[e])
            return jnp.where(take, need, need - cnt), jnp.where(take, thr_u | lax.shift_left(jnp.int32(1), b), thr_u)

        ties_allowed, thr_u = lax.fori_loop(
            0, 32, bit_step, (jnp.full((tq, 1), float(k_sel), F32), jnp.zeros((tq, 1), jnp.int32)))
        thr = thr_u ^ INT_MIN
        n_ties = lane_sum(lax.population_count(cand_ref[...]))
        for e in range(n_extra):
            n_ties = n_ties + lane_sum(candx_ref[e])
        need_rank = jnp.max(jnp.where(n_ties > ties_allowed, 1.0, 0.0)) > 0.5
        ties_allowed_fn = lambda: ties_allowed
    else:
        def count(pred, level):
            level_b = jnp.broadcast_to(level, (tq, 128))

            def body(j, acc):
                for c in range(tk // 128):
                    acc = acc + jnp.where(pred(s_ref[j, :, c * 128:(c + 1) * 128], level_b), 1.0, 0.0)
                return acc
            return jnp.sum(lax.fori_loop(0, nlim, body, jnp.zeros((tq, 128), F32)), axis=-1, keepdims=True)

        def bit_step(t, carry):
            c, n_ge = carry
            trial = c + lax.shift_left(jnp.int32(1), 31 - t)
            cnt = count(lambda s, lv: s >= lv, trial)
            ok = cnt >= k_sel
            return jnp.where(ok, trial, c), jnp.where(ok, cnt, n_ge)

        thr, n_ge = lax.fori_loop(0, 32, bit_step, (jnp.full((tq, 1), INT_MIN, jnp.int32),
                                                    jnp.full((tq, 1), float(nk * tk), F32)))
        need_rank = jnp.max(jnp.where(n_ge > k_sel, 1.0, 0.0)) > 0.5
        ties_allowed_fn = lambda: k_sel - count(lambda s, lv: s > lv, thr)

    @pl.when(need_rank)
    def _():
        ties_allowed = ties_allowed_fn()
        tri = jnp.where(_iota((tk, tk), 0) <= _iota((tk, tk), 1), 1.0, 0.0).astype(BF16)

        def out_tile(j, carry):
            key = s_ref[j]
            eq = jnp.where(key == thr, 1.0, 0.0)
            rank = _dot(eq.astype(BF16), tri) + carry
            kpos = j * tk + _iota((1, tk), 1)
            take = jnp.where(key > thr, 1.0, jnp.where(rank <= ties_allowed, eq, 0.0))
            take = jnp.where(kpos <= qpos, take, 0.0)
            bias_ref[j] = jnp.where(take > 0.5, 0.0, NEG).astype(bias_ref.dtype)
            return rank[:, tk - 1:tk]

        lax.fori_loop(0, nlim, out_tile, jnp.zeros((tq, 1), F32))

    @pl.when(jnp.logical_not(need_rank))
    def _():
        def out_tile(j, c):
            kpos = j * tk + _iota((1, tk), 1)
            take = jnp.where(kpos <= qpos, jnp.where(s_ref[j] >= thr, 1.0, 0.0), 0.0)
            bias_ref[j] = jnp.where(take > 0.5, 0.0, NEG).astype(bias_ref.dtype)
            return c

        lax.fori_loop(0, nlim, out_tile, 0)

    def fill(j, c):
        bias_ref[j] = jnp.full((tq, tk), NEG, bias_ref.dtype)
        return c

    lax.fori_loop(nlim, nk, fill, 0)


def _dsa_select(q_arr, p32, ki3, ki_col, fk, *, n_seq, t_seq, tq, qoff, k_sel, bias_dtype):
    if ki3.ndim == 4:
        nk = ki3.shape[1]
        ki_spec = pl.BlockSpec((1,) + ki3.shape[1:], lambda b, i: (b, 0, 0, 0))
    else:
        nk = ki3.shape[1] // KEY_TILE
        ki_spec = pl.BlockSpec((1, ki3.shape[1], fk), lambda b, i: (b, 0, ki_col))
    nq = t_seq // tq
    rows = n_seq * t_seq
    kern = functools.partial(_dsa_sel_kernel, tq=tq, nk=nk, k_sel=k_sel, qoff=qoff, fk=fk)
    return pl.pallas_call(
        kern, grid=(n_seq, nq),
        in_specs=[pl.BlockSpec((tq, MIX_W), lambda b, i: (b * nq + i, _OFF['a_qi'] // MIX_W)),
                  pl.BlockSpec((tq, 128), lambda b, i: (b * nq + i, _OFF['misc'] // 128)), ki_spec],
        out_specs=pl.BlockSpec((nk, tq, KEY_TILE), lambda b, i: (0, b * nq + i, 0)),
        out_shape=jax.ShapeDtypeStruct((nk, rows, KEY_TILE), bias_dtype),
        scratch_shapes=[pltpu.VMEM((nk, tq, KEY_TILE), jnp.int32)]
        + ([pltpu.VMEM((32, tq, KEY_TILE), jnp.int32), pltpu.VMEM((tq, KEY_TILE), jnp.int32),
            pltpu.VMEM((max(nk - 32, 1), tq, KEY_TILE), jnp.int32)] if nk <= 36 else []),
        compiler_params=_cp(("arbitrary", "arbitrary")), name="dsa_select",
    )(q_arr, p32, ki3)


def _flash_kernel(sb, sq, sk, sfl, q_ref, kv_ref, bias_ref, *rest, tq, tkm, shared, gated):
    if gated:
        misc_ref, add_ref, o_ref, m_ref, l_ref, acc_ref, qs_ref = rest
    else:
        o_ref, m_ref, l_ref, acc_ref, qs_ref = rest
    tk = KEY_TILE
    flags = sfl[pl.program_id(0)]
    dv = HEAD_DIM if shared else MIX_W
    lane_head = _iota((1, MIX_W), 1) // HEAD_DIM

    @pl.when((flags & 1) != 0)
    def _():
        m_ref[...] = jnp.full(m_ref.shape, NEG, F32)
        l_ref[...] = jnp.zeros(l_ref.shape, F32)
        acc_ref[...] = jnp.zeros(acc_ref.shape, F32)
        q = (q_ref[...] * (HEAD_DIM ** -0.5)).astype(BF16)
        if shared:
            qs_ref[...] = _stack_heads(q, HEAD_DIM)
        else:
            qs_ref[...] = jnp.concatenate([jnp.where(lane_head == h, q, jnp.zeros_like(q)) for h in range(N_HEADS)],
                                          axis=0)

    tkk = tkm * tk
    kv = kv_ref[0]
    k = kv[:, :dv].astype(BF16)
    v = kv[:, dv:2 * dv].astype(BF16)
    bias = jnp.concatenate([bias_ref[u].astype(F32) for u in range(tkm)], axis=1)
    s = _dot_nt(qs_ref[...], k).reshape(N_HEADS, tq, tkk) + bias[None]
    m_old = m_ref[...]
    m_new = jnp.maximum(m_old, jnp.max(s, axis=-1, keepdims=True))
    alpha = jnp.exp(m_old - m_new)
    p = jnp.exp(s - m_new)
    l_ref[...] = alpha * l_ref[...] + jnp.sum(p, axis=-1, keepdims=True)
    pv = _dot(p.reshape(N_HEADS * tq, tkk).astype(BF16), v).reshape(N_HEADS, tq, dv)
    acc_ref[...] = alpha * acc_ref[...] + pv
    m_ref[...] = m_new

    @pl.when((flags & 2) != 0)
    def _():
        o = acc_ref[...] / l_ref[...]
        if shared:
            out = jnp.concatenate([o[h] for h in range(N_HEADS)], axis=1)
        else:
            out = jnp.where(lane_head == 0, o[0], 0.0)
            for h in range(1, N_HEADS):
                out = out + jnp.where(lane_head == h, o[h], 0.0)
        if gated:
            g = jax.nn.sigmoid(misc_ref[...])
            gate = jnp.zeros((tq, MIX_W), F32)
            for h in range(N_HEADS):
                c = MISC_G + 3 * h + 1
                gate = gate + jnp.where(lane_head == h, g[:, c:c + 1], 0.0)
            out = add_ref[...].astype(F32) + gate * out
        o_ref[...] = out.astype(o_ref.dtype)


def _flash_steps(n_seq, t_seq, tq, tk, lp, qoff):
    sb, sq, sk, sfl = [], [], [], []
    for b in range(n_seq):
        for i in range(t_seq // tq):
            nlim = min((qoff + (i + 1) * tq - 1) // tk + 1, lp // tk)
            for j in range(nlim):
                sb.append(b), sq.append(i), sk.append(j)
                sfl.append((1 if j == 0 else 0) | (2 if j == nlim - 1 else 0))
    return [jnp.asarray(np.asarray(a, np.int32)) for a in (sb, sq, sk, sfl)]


def _flash(q_arr, q_col, kv3, kv_col, bias, *, n_seq, t_seq, tq, tkm, qoff, shared, out_dtype, gate_args=None):
    lp = kv3.shape[1]
    tk = tkm * KEY_TILE
    nq = t_seq // tq
    rows = n_seq * t_seq
    wblk = 2 * (HEAD_DIM if shared else MIX_W)
    dv = HEAD_DIM if shared else MIX_W
    steps = _flash_steps(n_seq, t_seq, tq, tk, lp, qoff)
    gated = gate_args is not None
    row = lambda s, sb, sq, sk, sfl: sb[s] * nq + sq[s]
    in_specs = [pl.BlockSpec((tq, MIX_W), lambda s, sb, sq, sk, sfl: (row(s, sb, sq, sk, sfl), q_col)),
                pl.BlockSpec((1, tk, wblk), lambda s, sb, sq, sk, sfl: (sb[s], sk[s], kv_col)),
                pl.BlockSpec((tkm, tq, KEY_TILE), lambda s, sb, sq, sk, sfl: (sk[s], row(s, sb, sq, sk, sfl), 0))]
    args = [q_arr, kv3, bias]
    if gated:
        in_specs += [pl.BlockSpec((tq, 128), lambda s, sb, sq, sk, sfl: (row(s, sb, sq, sk, sfl), _OFF['misc'] // 128)),
                     pl.BlockSpec((tq, MIX_W), lambda s, sb, sq, sk, sfl: (row(s, sb, sq, sk, sfl), 0))]
        args += list(gate_args)
    kern = functools.partial(_flash_kernel, tq=tq, tkm=tkm, shared=shared, gated=gated)
    return pl.pallas_call(
        kern,
        grid_spec=pltpu.PrefetchScalarGridSpec(
            num_scalar_prefetch=4, grid=(int(steps[0].shape[0]),), in_specs=in_specs,
            out_specs=pl.BlockSpec((tq, MIX_W), lambda s, sb, sq, sk, sfl: (row(s, sb, sq, sk, sfl), 0)),
            scratch_shapes=[pltpu.VMEM((N_HEADS, tq, 1), F32), pltpu.VMEM((N_HEADS, tq, 1), F32),
                            pltpu.VMEM((N_HEADS, tq, dv), F32), pltpu.VMEM((N_HEADS * tq, dv), BF16)]),
        out_shape=jax.ShapeDtypeStruct((rows, MIX_W), out_dtype),
        compiler_params=_cp(("arbitrary",)), name="flash_shared" if shared else "flash_heads",
    )(*steps, *args)


def _flash_t_kernel(sq, sk, sfl, qt_ref, k_ref, vt_ref, bias_ref, *rest, tq, tkm, shared, gated, block_sel):
    if gated:
        misct_ref, addt_ref, o_ref, m_ref, acc_ref = rest
    else:
        o_ref, m_ref, acc_ref = rest
    tk = KEY_TILE
    flags = sfl[pl.program_id(0)]
    row_head = _iota((MIX_W, 1), 0) // HEAD_DIM
    ones_row = jnp.where(_iota((8, tk), 0) == 0, 1.0, 0.0).astype(BF16)

    @pl.when((flags & 1) != 0)
    def _():
        m_ref[...] = jnp.full(m_ref.shape, NEG, F32)
        acc_ref[...] = jnp.zeros(acc_ref.shape, F32)

    qt = qt_ref[...]
    for u in range(tkm):
        k = k_ref[u * tk:(u + 1) * tk, :]
        k = k[:, :HEAD_DIM] if shared else k
        if block_sel:
            nb = tk // SLC_BLOCK
            kt = sk[pl.program_id(0)] * tkm + u
            blocks = bias_ref[pl.ds(pl.multiple_of(kt * nb, nb), nb), :]
            on = jnp.concatenate([jnp.broadcast_to(blocks[r:r + 1, :], (SLC_BLOCK, tq)) for r in range(nb)], axis=0)
            kpos = kt * tk + _iota((tk, 1), 0)
            qpos = sq[pl.program_id(0)] * tq + _iota((1, tq), 1)
            bias_t = jnp.where((on > 0.5) & (kpos <= qpos), 0.0, NEG)
        else:
            bias_t = bias_ref[u].astype(F32).T
        for h in range(N_HEADS):
            hs = slice(h * HEAD_DIM, (h + 1) * HEAD_DIM)
            q_h = qt[hs, :] if shared else jnp.where(row_head == h, qt, jnp.zeros_like(qt))
            s = _dot(k, q_h) + bias_t
            m_old = m_ref[h]
            m_new = jnp.maximum(m_old, jnp.max(s, axis=0, keepdims=True))
            alpha = jnp.exp2(m_old - m_new)
            p = jnp.exp2(s - m_new)
            vt_h = vt_ref[0:HEAD_DIM, u * tk:(u + 1) * tk] if shared else vt_ref[hs, u * tk:(u + 1) * tk]
            acc_ref[h] = alpha * acc_ref[h] + _dot(jnp.concatenate([vt_h, ones_row], axis=0), p.astype(BF16))
            m_ref[h] = m_new

    @pl.when((flags & 2) != 0)
    def _():
        for h in range(N_HEADS):
            hs = slice(h * HEAD_DIM, (h + 1) * HEAD_DIM)
            acc = acc_ref[h]
            out = acc[0:HEAD_DIM] / acc[HEAD_DIM:HEAD_DIM + 1]
            if gated:
                c = MISC_G + 3 * h + 1
                out = addt_ref[hs, :].astype(F32) + jax.nn.sigmoid(misct_ref[c:c + 1, :]) * out
            o_ref[hs, :] = out.astype(o_ref.dtype)


def _flash_t(qt, k_arr, k_col, k_width, vt, bias, *, t_seq, tq, tkm, shared, gate_args=None):
    tk = tkm * KEY_TILE
    steps = _flash_steps(1, t_seq, tq, tk, t_seq, 0)[1:]
    gated = gate_args is not None
    vrows = vt.shape[0]
    block_sel = bias.ndim == 2
    if block_sel:
        bias_spec = pl.BlockSpec((bias.shape[0], tq), lambda s, sq, sk, sfl: (0, sq[s]))
    else:
        bias_spec = pl.BlockSpec((tkm, tq, KEY_TILE), lambda s, sq, sk, sfl: (sk[s], sq[s], 0))
    in_specs = [pl.BlockSpec((MIX_W, tq), lambda s, sq, sk, sfl: (0, sq[s])),
                pl.BlockSpec((tk, k_width), lambda s, sq, sk, sfl: (sk[s], k_col)),
                pl.BlockSpec((vrows, tk), lambda s, sq, sk, sfl: (0, sk[s])), bias_spec]
    args = [qt, k_arr, vt, bias]
    if gated:
        in_specs += [pl.BlockSpec((128, tq), lambda s, sq, sk, sfl: (0, sq[s])),
                     pl.BlockSpec((MIX_W, tq), lambda s, sq, sk, sfl: (0, sq[s]))]
        args += list(gate_args)
    kern = functools.partial(_flash_t_kernel, tq=tq, tkm=tkm, shared=shared, gated=gated, block_sel=block_sel)
    return pl.pallas_call(
        kern,
        grid_spec=pltpu.PrefetchScalarGridSpec(
            num_scalar_prefetch=3, grid=(int(steps[0].shape[0]),), in_specs=in_specs,
            out_specs=pl.BlockSpec((MIX_W, tq), lambda s, sq, sk, sfl: (0, sq[s])),
            scratch_shapes=[pltpu.VMEM((N_HEADS, 1, tq), F32), pltpu.VMEM((N_HEADS, HEAD_DIM + 8, tq), F32)]),
        out_shape=jax.ShapeDtypeStruct((MIX_W, t_seq), BF16),
        compiler_params=_cp(("arbitrary",)), name="flash_t_shared" if shared else "flash_t_heads",
    )(*steps, *args)


def _compress_kernel(x_ref, w_ref, o_ref, *, n_blocks, ncp):
    w = w_ref[...]
    e = jnp.exp(w - jnp.max(w, axis=-1, keepdims=True))
    w = e / jnp.sum(e, axis=-1, keepdims=True)
    n16 = x_ref.shape[1] // CMP_STRIDE
    width = x_ref.shape[2]
    first = jnp.zeros((n16, width), F32)
    second = jnp.zeros((n16, width), F32)
    for j in range(CMP_STRIDE):
        xj = x_ref[0, pl.ds(j, n16, stride=CMP_STRIDE), :]
        first = first + xj * w[:, j:j + 1]
        second = second + xj * w[:, CMP_STRIDE + j:CMP_STRIDE + j + 1]
    shifted = jnp.concatenate([second[1:], jnp.zeros((1, width), F32)], axis=0)
    out = first + shifted
    if n16 < ncp:
        out = jnp.concatenate([out, jnp.zeros((ncp - n16, width), F32)], axis=0)
    out = out[:ncp]
    o_ref[0] = jnp.where(_iota((ncp, 1), 0) < n_blocks, out, 0.0)


def _compress(rows3, w_pos, length, ncp):
    n_seq, lp, width = rows3.shape
    assert lp % CMP_STRIDE == 0
    n_blocks = -(-length // CMP_STRIDE) - 1
    kern = functools.partial(_compress_kernel, n_blocks=n_blocks, ncp=ncp)
    return pl.pallas_call(
        kern, grid=(n_seq,),
        in_specs=[pl.BlockSpec((1, lp, width), lambda b: (b, 0, 0)),
                  pl.BlockSpec((1, CMP_LEN), lambda b: (0, 0))],
        out_specs=pl.BlockSpec((1, ncp, width), lambda b: (b, 0, 0)),
        out_shape=jax.ShapeDtypeStruct((n_seq, ncp, width), F32),
        compiler_params=_cp(("arbitrary",)), name="compress",
    )(rows3, w_pos.reshape(1, CMP_LEN))


def _masked_softmax(s, mask):
    s = jnp.where(mask, s, NEG)
    m = jnp.max(s, axis=-1, keepdims=True)
    e = jnp.where(mask, jnp.exp(s - m), 0.0)
    return e / jnp.maximum(jnp.sum(e, axis=-1, keepdims=True), 1e-30)


def _nsa_sel_kernel(q_ref, misc_ref, kc_ref, win_ref, ocw_ref, bias_ref, *, tq, nk, qoff, n_cmp, n_blk, nbp, n_top,
                    win_rows, win_dyn, win_pos0, block_out):
    tk = KEY_TILE
    i = pl.program_id(1)
    q0 = qoff + i * tq
    qpos = q0 + _iota((tq, 1), 0)
    nlim = jnp.minimum((q0 + tq - 1) // tk + 1, nk)
    qs = _stack_heads((q_ref[...] * (HEAD_DIM ** -0.5)).astype(BF16), HEAD_DIM)
    ncp = kc_ref.shape[1]

    kcv = kc_ref[0]
    kc = kcv[:, :HEAD_DIM].astype(BF16)
    vc = kcv[:, HEAD_DIM:].astype(BF16)
    cidx = _iota((1, ncp), 1)
    c_mask = ((cidx * CMP_STRIDE + CMP_LEN - 1) <= qpos) & (cidx < n_cmp)
    s_c = _dot_nt(qs, kc).reshape(N_HEADS, tq, ncp)
    p_c = _masked_softmax(s_c, c_mask[None])
    o_c = _dot(p_c.reshape(N_HEADS * tq, ncp).astype(BF16), vc).reshape(N_HEADS, tq, HEAD_DIM)

    c_start = _iota((ncp, 1), 0) * CMP_STRIDE
    s_start = _iota((1, nbp), 1) * SLC_BLOCK
    overlap = (c_start < s_start + SLC_BLOCK) & (c_start + CMP_LEN > s_start) & (_iota((ncp, 1), 0) < n_cmp)
    imp = _dot_x_exact01(p_c[0] + p_c[1] + p_c[2] + p_c[3], jnp.where(overlap, 1.0, 0.0).astype(BF16))
    blk = _iota((1, nbp), 1)
    cur = qpos // SLC_BLOCK
    forced = (blk == 0) | (blk == cur) | (blk == cur - 1)
    live = jnp.where(forced, jnp.inf, jnp.where((blk <= cur) & (blk < n_blk), imp, -jnp.inf))
    blk_f = blk.astype(F32)

    def pick(t, carry):
        live, sel = carry
        top = jnp.max(live, axis=-1, keepdims=True)
        first = jnp.min(jnp.where(live == top, blk_f, float(nbp)), axis=-1, keepdims=True)
        hit = blk_f == first
        return jnp.where(hit, -jnp.inf, live), jnp.where(hit, 1.0, sel)

    if block_out:
        blk_t = _iota((nbp, 1), 0).astype(F32)

        def pick_t(t, carry):
            live_t, sel_t = carry
            top = jnp.max(live_t, axis=0, keepdims=True)
            first = jnp.min(jnp.where(live_t == top, blk_t, float(nbp)), axis=0, keepdims=True)
            hit = blk_t == first
            return jnp.where(hit, -jnp.inf, live_t), jnp.where(hit, 1.0, sel_t)

        _, sel_t = lax.fori_loop(0, n_top, pick_t, (live.T, jnp.zeros((nbp, tq), F32)))
        bias_ref[...] = sel_t
    else:
        _, sel = lax.fori_loop(0, n_top, pick, (live, jnp.zeros((tq, nbp), F32)))
        sel = sel.astype(BF16)
        row_blk = _iota((nbp, 1), 0)

        def bias_tile(j, c):
            kpos = j * tk + _iota((1, tk), 1)
            expand = jnp.where(row_blk == kpos // SLC_BLOCK, 1.0, 0.0).astype(BF16)
            on = _dot(sel, expand)
            ok = (on > 0.5) & (kpos <= qpos)
            bias_ref[j] = jnp.where(ok, 0.0, NEG).astype(bias_ref.dtype)
            return c

        lax.fori_loop(0, nlim, bias_tile, 0)

        def fill(j, c):
            bias_ref[j] = jnp.full((tq, tk), NEG, bias_ref.dtype)
            return c

        lax.fori_loop(nlim, nk, fill, 0)

    if win_dyn:
        start = pl.multiple_of(jnp.maximum(q0 - WINDOW, 0), 8)
        wkv = win_ref[0, pl.ds(start, win_rows), :]
        kwpos = start + _iota((1, win_rows), 1)
    else:
        wkv = win_ref[0]
        kwpos = win_pos0 + _iota((1, win_rows), 1)
    kw = wkv[:, :HEAD_DIM].astype(BF16)
    vw = wkv[:, HEAD_DIM:].astype(BF16)
    rel = qpos - kwpos
    w_mask = (rel >= 0) & (rel < WINDOW) & (kwpos >= 0)
    s_w = _dot_nt(qs, kw).reshape(N_HEADS, tq, win_rows)
    p_w = _masked_softmax(s_w, w_mask[None])
    o_w = _dot(p_w.reshape(N_HEADS * tq, win_rows).astype(BF16), vw).reshape(N_HEADS, tq, HEAD_DIM)

    g = jax.nn.sigmoid(misc_ref[...])
    parts = []
    for h in range(N_HEADS):
        c = MISC_G + 3 * h
        parts.append(g[:, c:c + 1] * o_c[h] + g[:, c + 2:c + 3] * o_w[h])
    ocw_ref[...] = jnp.concatenate(parts, axis=1).astype(ocw_ref.dtype)


def _nsa_select(q_arr, p32, kcvc, win3, win_col, *, n_seq, t_seq, tq, qoff, seq_len, lp, win_dyn, win_pos0, bias_dtype,
                out_dtype):
    nk, nq = lp // KEY_TILE, t_seq // tq
    rows = n_seq * t_seq
    n_cmp = -(-seq_len // CMP_STRIDE) - 1
    n_blk = -(-seq_len // SLC_BLOCK)
    nbp = -(-n_blk // 128) * 128
    win_rows = (tq + WINDOW) if win_dyn else win3.shape[1]
    block_out = bias_dtype is None
    kern = functools.partial(_nsa_sel_kernel, tq=tq, nk=nk, qoff=qoff, n_cmp=n_cmp, n_blk=n_blk, nbp=nbp,
                             n_top=min(N_SLC, n_blk), win_rows=win_rows, win_dyn=win_dyn, win_pos0=win_pos0,
                             block_out=block_out)
    if block_out:
        sel_spec = pl.BlockSpec((nbp, tq), lambda b, i: (0, b * nq + i))
        sel_shape = jax.ShapeDtypeStruct((nbp, rows), F32)
    else:
        sel_spec = pl.BlockSpec((nk, tq, KEY_TILE), lambda b, i: (0, b * nq + i, 0))
        sel_shape = jax.ShapeDtypeStruct((nk, rows, KEY_TILE), bias_dtype)
    return pl.pallas_call(
        kern, grid=(n_seq, nq),
        in_specs=[pl.BlockSpec((tq, MIX_W), lambda b, i: (b * nq + i, _OFF['b_q'] // MIX_W)),
                  pl.BlockSpec((tq, 128), lambda b, i: (b * nq + i, _OFF['misc'] // 128)),
                  pl.BlockSpec((1,) + kcvc.shape[1:], lambda b, i: (b, 0, 0)),
                  pl.BlockSpec((1, win3.shape[1], 128), lambda b, i: (b, 0, win_col))],
        out_specs=[pl.BlockSpec((tq, MIX_W), lambda b, i: (b * nq + i, 0)), sel_spec],
        out_shape=[jax.ShapeDtypeStruct((rows, MIX_W), out_dtype), sel_shape],
        compiler_params=_cp(("arbitrary", "arbitrary")), name="nsa_select",
    )(q_arr, p32, kcvc, win3)


def _gla_tables(c):
    levels = []
    b = c
    while b >= 2:
        levels.append(b)
        b //= 2
    r = np.arange(c)
    mats = [(r[None, :] <= r[:, None]), (r[None, :] > r[:, None])]
    qm, km = [], []
    for b in levels:
        mid = (r // b) * b + b // 2
        upper = r >= mid
        qm.append(upper[:, None] & (r[None, :] >= mid[:, None]) & (r[None, :] <= r[:, None]))
        km.append((~upper)[:, None] & (r[None, :] > r[:, None]) & (r[None, :] < mid[:, None]))
    return np.concatenate(mats + qm + km, axis=0).astype(np.float32), levels


def _gla_kernel(q_ref, k_ref, v_ref, r_ref, misc_ref, wal_ref, bal_ref, gg_ref, m_ref, s0_ref, o_ref, st_ref, s_scr,
                *, tt, c, levels, n_tiles):
    t_idx = pl.program_id(1)
    nl = len(levels)

    @pl.when(t_idx == 0)
    def _():
        s_scr[...] = s0_ref[0]

    lane_head = _iota((1, MIX_W), 1) // HEAD_DIM
    same_head = (_iota((MIX_W, 1), 0) // HEAD_DIM) == lane_head
    eye = jnp.where(_iota((MIX_W, MIX_W), 0) == _iota((MIX_W, MIX_W), 1), 1.0, 0.0).astype(BF16)
    ones_head = jnp.where(same_head, 1.0, 0.0)
    rr, cc = _iota((c, c), 0), _iota((c, c), 1)
    pair_masks = [((rr // b) == (cc // b)) & ((rr % b) >= b // 2) & ((cc % b) < b // 2) for b in levels]
    diag_mask = rr == cc

    z = _dot_hi(misc_ref[...], wal_ref[...]) + bal_ref[...]
    la = (jnp.minimum(z, 0.0) - jnp.log1p(jnp.exp(-jnp.abs(z)))) * (1.0 / GATE_TAU)
    m_all = m_ref[...]

    def bd(x):
        return jnp.concatenate([jnp.where(lane_head == h, x, jnp.zeros_like(x)) for h in range(N_HEADS)], axis=0)

    def unbd(x):
        out = jnp.where(lane_head == 0, x[0:c], 0.0)
        for h in range(1, N_HEADS):
            out = out + jnp.where(lane_head == h, x[h * c:(h + 1) * c], 0.0)
        return out

    for ci in range(tt // c):
        sl = slice(ci * c, (ci + 1) * c)
        e = jnp.exp(_dot_exact01(m_all, la[sl]))
        q = q_ref[sl, :] * (HEAD_DIM ** -0.5)
        k = k_ref[sl, :]
        v = v_ref[sl, :].astype(BF16)
        e_cum, e_rest = e[0:c], e[c:2 * c]
        att = jnp.where(diag_mask[None], _dot_nt(bd(q.astype(BF16)), k.astype(BF16)).reshape(N_HEADS, c, c), 0.0)
        for li in range(nl):
            eq = e[(2 + li) * c:(3 + li) * c]
            ek = e[(2 + nl + li) * c:(3 + nl + li) * c]
            a = _dot_nt(bd((q * eq).astype(BF16)), (k * ek).astype(BF16)).reshape(N_HEADS, c, c)
            att = att + jnp.where(pair_masks[li][None], a, 0.0)
        o_intra = unbd(_dot(att.reshape(N_HEADS * c, c).astype(BF16), v))
        st = s_scr[...]
        o_inter = _dot_nt((q * e_cum).astype(BF16), st.astype(BF16))
        v_t = _dot_nt(eye, v).astype(BF16)
        upd = _dot(v_t, (k * e_rest).astype(BF16))
        s_scr[...] = st * e_cum[c - 1:c, :] + jnp.where(same_head, upd, 0.0)
        o = o_inter + o_intra
        ms = _dot_hi(o * o, ones_head) * (1.0 / HEAD_DIM)
        o = o * lax.rsqrt(ms + EPS) * gg_ref[...]
        r = r_ref[sl, :]
        o_ref[sl, :] = (o * (r * jax.nn.sigmoid(r))).astype(o_ref.dtype)

    @pl.when(t_idx == n_tiles - 1)
    def _():
        st_ref[0] = s_scr[...]


def _gla(p32, wal_pad, b_alpha, g_gla4, st0, *, n_seq, t_seq, tt, out_dtype):
    c = min(GLA_CHUNK, t_seq)
    m_all, levels = _gla_tables(c)
    n_tiles = t_seq // tt
    rows = n_seq * t_seq
    col = lambda name: pl.BlockSpec((tt, MIX_W), lambda b, t, o=_OFF[name] // MIX_W: (b * n_tiles + t, o))
    kern = functools.partial(_gla_kernel, tt=tt, c=c, levels=levels, n_tiles=n_tiles)
    return pl.pallas_call(
        kern, grid=(n_seq, n_tiles),
        in_specs=[col('d_q'), col('d_k'), col('d_v'), col('d_r'),
                  pl.BlockSpec((tt, 128), lambda b, t: (b * n_tiles + t, _OFF['misc'] // 128)),
                  pl.BlockSpec((128, MIX_W), lambda b, t: (0, 0)),
                  pl.BlockSpec((1, MIX_W), lambda b, t: (0, 0)),
                  pl.BlockSpec((1, MIX_W), lambda b, t: (0, 0)),
                  pl.BlockSpec(m_all.shape, lambda b, t: (0, 0)),
                  pl.BlockSpec((1, MIX_W, MIX_W), lambda b, t: (b, 0, 0))],
        out_specs=[pl.BlockSpec((tt, MIX_W), lambda b, t: (b * n_tiles + t, 0)),
                   pl.BlockSpec((1, MIX_W, MIX_W), lambda b, t: (b, 0, 0))],
        out_shape=[jax.ShapeDtypeStruct((rows, MIX_W), out_dtype),
                   jax.ShapeDtypeStruct((n_seq, MIX_W, MIX_W), F32)],
        scratch_shapes=[pltpu.VMEM((MIX_W, MIX_W), F32)],
        compiler_params=_cp(("arbitrary", "arbitrary")), name="gla",
    )(p32, p32, p32, p32, p32, wal_pad, b_alpha.reshape(1, MIX_W), g_gla4, jnp.asarray(m_all, dtype=BF16), st0)


def _dwconv3(u, w, carry_ref, fix_refs, t_seq, tm, first_tile):
    row = _iota((tm, 1), 0)
    u1 = pltpu.roll(u, 1, axis=0)
    u2 = pltpu.roll(u, 2, axis=0)
    if fix_refs is None:
        prev = jnp.where(first_tile, 0.0, carry_ref[0:2, :])
        u1 = jnp.where(row == 0, prev[1:2], u1)
        u2 = jnp.where(row == 0, prev[0:1], jnp.where(row == 1, prev[1:2], u2))
        carry_ref[0:2, :] = u[tm - 2:tm]
    else:
        pos = row % t_seq
        u1 = jnp.where(pos == 0, fix_refs[0][...], u1)
        u2 = jnp.where(pos < 2, fix_refs[1][...], u2)
    return w[0:1] * u2 + w[1:2] * u1 + w[2:3] * u


def _merge_kernel(oa_ref, ob_ref, od_ref, cin_ref, cb_ref, cc_ref, gate_ref, x_ref, g1_ref, wconv_ref, wb_ref,
                  wo_ref, *rest, tm, t_seq, tiles_per_seq, per_row):
    if per_row:
        fix1_ref, fix2_ref, o_ref, carry_ref = rest
        fix = (fix1_ref, fix2_ref)
    else:
        o_ref, carry_ref = rest
        fix = None
    first = (pl.program_id(0) % tiles_per_seq) == 0
    u = cc_ref[...] * cin_ref[...]
    o_c = cb_ref[...] * _dwconv3(u, wconv_ref[...], carry_ref, fix, t_seq, tm, first)
    branches = (oa_ref[...], ob_ref[...], o_c, od_ref[...])
    merged = jnp.zeros((tm, D_MODEL), F32)
    for bi, br in enumerate(branches):
        gate = jax.nn.sigmoid(gate_ref[:, bi * D_MODEL:(bi + 1) * D_MODEL])
        merged = merged + gate * _dot(br.astype(BF16), wb_ref[bi])
    o_ref[...] = x_ref[...] + g1_ref[0] * _dot(merged.astype(BF16), wo_ref[...])


def _merge(oa, ob, od, p32, x, g1, conv_c, wb, wo, fix, *, tm, t_seq):
    rows = x.shape[0]
    per_row = fix is not None
    tiles_per_seq = max(t_seq // tm, 1)
    rowblk = lambda width, colblk=0: pl.BlockSpec((tm, width), lambda i: (i, colblk))
    in_specs = [rowblk(MIX_W), rowblk(MIX_W), rowblk(MIX_W),
                rowblk(MIX_W, _OFF['c_in'] // MIX_W), rowblk(MIX_W, _OFF['c_b'] // MIX_W),
                rowblk(MIX_W, _OFF['c_c'] // MIX_W), rowblk(N_HEADS * D_MODEL, 0), rowblk(D_MODEL),
                _mod_spec(g1, tm, t_seq), _const_spec((CONV_W, MIX_W)), _const_spec((N_HEADS, MIX_W, D_MODEL)),
                _const_spec((D_MODEL, D_MODEL))]
    args = [oa, ob, od, p32, p32, p32, p32, x, g1, conv_c, wb, wo]
    if per_row:
        in_specs += [rowblk(MIX_W), rowblk(MIX_W)]
        args += list(fix)
    kern = functools.partial(_merge_kernel, tm=tm, t_seq=t_seq, tiles_per_seq=tiles_per_seq, per_row=per_row)
    return pl.pallas_call(
        kern, grid=(rows // tm,), in_specs=in_specs, out_specs=rowblk(D_MODEL),
        out_shape=jax.ShapeDtypeStruct((rows, D_MODEL), F32),
        scratch_shapes=[pltpu.VMEM((8, MIX_W), F32)],
        compiler_params=_cp(("arbitrary",)), name="merge",
    )(*args)


def _ffn_kernel(x_ref, gn_ref, sc_ref, sh_ref, g2_ref, wa_ref, wg_ref, wconv_ref, bf_ref, wout_ref, gf_ref, *rest,
                tm, tf, t_seq, tiles_per_seq, per_row):
    if per_row:
        fix1_ref, fix2_ref, o_ref, y_ref, a_ref, carry_ref = rest
    else:
        o_ref, y_ref, a_ref, carry_ref = rest
    first = (pl.program_id(0) % tiles_per_seq) == 0
    x = x_ref[...]
    h = _norm_mod(x, gn_ref[...], sc_ref[0], sh_ref[0]).astype(BF16)
    acc = jnp.zeros((tm, D_MODEL), F32)
    for f in range(D_FF // tf):
        fs = slice(f * tf, (f + 1) * tf)
        a = _dot(h, wa_ref[:, fs])
        g = _dot(h, wg_ref[:, fs])
        a_ref[:, fs] = a
        fix = (fix1_ref.at[:, fs], fix2_ref.at[:, fs]) if per_row else None
        conv = _dwconv3(a, wconv_ref[:, fs], carry_ref.at[:, fs], fix, t_seq, tm, first)
        pre = conv + bf_ref[:, fs]
        act = pre * jax.nn.sigmoid(pre) * g
        acc = acc + _dot(act.astype(BF16), wout_ref[fs, :])
    xn = x + g2_ref[0] * acc
    o_ref[...] = xn
    y_ref[...] = xn * lax.rsqrt(jnp.mean(xn * xn, axis=-1, keepdims=True) + EPS) * gf_ref[...]


def _ffn(x, gn, sc, sh, g2, wa, wg, conv_ffn, b_ffn, wout, g_final, fix, *, tm, t_seq):
    rows = x.shape[0]
    per_row = fix is not None
    tiles_per_seq = max(t_seq // tm, 1)
    tf = 1408
    rowblk = lambda width: pl.BlockSpec((tm, width), lambda i: (i, 0))
    in_specs = [rowblk(D_MODEL), _const_spec((1, D_MODEL)), _mod_spec(sc, tm, t_seq), _mod_spec(sh, tm, t_seq),
                _mod_spec(g2, tm, t_seq), _const_spec((D_MODEL, D_FF)), _const_spec((D_MODEL, D_FF)),
                _const_spec((CONV_W, D_FF)), _const_spec((1, D_FF)), _const_spec((D_FF, D_MODEL)),
                _const_spec((1, D_MODEL))]
    args = [x, gn, sc, sh, g2, wa, wg, conv_ffn, b_ffn, wout, g_final]
    if per_row:
        in_specs += [rowblk(D_FF), rowblk(D_FF)]
        args += list(fix)
    kern = functools.partial(_ffn_kernel, tm=tm, tf=tf, t_seq=t_seq, tiles_per_seq=tiles_per_seq, per_row=per_row)
    return pl.pallas_call(
        kern, grid=(rows // tm,), in_specs=in_specs,
        out_specs=[rowblk(D_MODEL), rowblk(D_MODEL), rowblk(D_FF)],
        out_shape=[jax.ShapeDtypeStruct((rows, D_MODEL), F32), jax.ShapeDtypeStruct((rows, D_MODEL), F32),
                   jax.ShapeDtypeStruct((rows, D_FF), F32)],
        scratch_shapes=[pltpu.VMEM((8, D_FF), F32)],
        compiler_params=_cp(("arbitrary",)), name="ffn",
    )(*args)


def _permute_w_in(w):
    offs, o = {}, 0
    for name, n in _IN_SPLITS:
        offs[name] = (o, n)
        o += n

    def c(name):
        s, n = offs[name]
        return w[:, s:s + n]

    pieces = [c('gate'), c('a_k'), c('a_v'), c('a_q'), c('a_qi'), c('b_q'), c('b_cmp'), c('b_slc'), c('b_win'),
              c('a_ki'), c('a_wi'), c('b_g'), c('d_a'), jnp.zeros((w.shape[0], 32), w.dtype),
              c('c_in'), c('c_b'), c('c_c'), c('d_q'), c('d_k'), c('d_v'), c('d_r')]
    return jnp.concatenate(pieces, axis=1).astype(BF16)


def _cols(p, name, width):
    return p[:, _OFF[name]:_OFF[name] + width]


def _state_to_bd(s0):
    b = s0.shape[0]
    s0t = jnp.swapaxes(s0, 2, 3)
    eye = jnp.eye(N_HEADS, dtype=s0.dtype)[None, :, None, :, None]
    return (s0t[:, :, :, None, :] * eye).reshape(b, MIX_W, MIX_W)


def _bd_to_state(st):
    b = st.shape[0]
    st5 = st.reshape(b, N_HEADS, HEAD_DIM, N_HEADS, HEAD_DIM)
    return jnp.stack([jnp.swapaxes(st5[:, h, :, h, :], 1, 2) for h in range(N_HEADS)], axis=1)


def _conv_fix(state, t_seq):
    b, _, c = state.shape
    fix1 = jnp.concatenate([state[:, 1:2], jnp.zeros((b, t_seq - 1, c), state.dtype)], axis=1)
    fix2 = jnp.concatenate([state, jnp.zeros((b, t_seq - 2, c), state.dtype)], axis=1)
    return fix1.reshape(b * t_seq, c), fix2.reshape(b * t_seq, c)


def _layer_weights(l, w_in, w_alpha, g_gla, w_branch, w_out, w_ffn_in, w_ffn_out):
    wal_pad = jnp.zeros((128, MIX_W), F32).at[MISC_DA:MISC_DA + GATE_RANK].set(w_alpha[l])
    return dict(w_in=_permute_w_in(w_in[l]), wal=wal_pad, gg=jnp.tile(g_gla[l], N_HEADS).reshape(1, MIX_W),
                wb=w_branch[l].astype(BF16), wo=w_out[l].astype(BF16),
                wa=w_ffn_in[l][:, :D_FF].astype(BF16), wg=w_ffn_in[l][:, D_FF:].astype(BF16),
                wout=w_ffn_out[l].astype(BF16))


def kernel(x_prompt, x_sample, cache_a_kv, cache_a_idx, cache_b_cmp, cache_b_slc, state_b_win, state_c_conv, state_d_gla, state_ffn_conv, page_table, c_prompt, c_sample, w_ada, b_ada, g_norm1, w_in, w_cmp_pos, conv_c, w_alpha, b_alpha, g_gla, w_branch, w_out, g_norm2, w_ffn_in, conv_ffn, b_ffn, w_ffn_out, g_final):
    bp, seq, d = x_prompt.shape
    bs, tdec, _ = x_sample.shape
    depth = w_ada.shape[0]
    n_pages = page_table.shape[1]
    past = n_pages * PAGE
    w_buf = state_b_win.shape[2]
    assert bp == 1 and d == D_MODEL and seq % 1024 == 0 and tdec == 8 and w_buf == WINDOW

    n_c = bp + bs
    c_all = jnp.concatenate([c_prompt, c_sample, jnp.zeros((-n_c % 8, d), F32)], axis=0)
    mod = _ada(c_all, w_ada, b_ada)

    xp = x_prompt.reshape(bp * seq, d)
    xs = x_sample.reshape(bs * tdec, d)
    rs = bs * tdec
    tail_rows = KEY_TILE
    lps = past + tail_rows
    tkm_s = max(t for t in (1, 2, 3, 4, 6, 8, 11) if (lps // KEY_TILE) % t == 0)
    gfin = g_final.reshape(1, d)
    outs_p, outs_s = [], []
    y_p = y_s = None
    for l in range(depth):
        lw = _layer_weights(l, w_in, w_alpha, g_gla, w_branch, w_out, w_ffn_in, w_ffn_out)
        mp = [mod[l, :bp, k * d:(k + 1) * d].reshape(bp, 1, d) for k in range(6)]
        ms = [jnp.repeat(mod[l, bp:n_c, k * d:(k + 1) * d], tdec, axis=0).reshape(1, rs, d) for k in range(6)]
        gn1, gn2 = g_norm1[l].reshape(1, d), g_norm2[l].reshape(1, d)
        bffn = b_ffn[l].reshape(1, D_FF)

        p32, _ = _proj(xs, gn1, ms[1], ms[0], lw['w_in'], rs, rs)
        new = {n: _cols(p32, n, w).reshape(bs, tdec, w) for n, w in
               (('a_k', 2 * MIX_W), ('misc', IDX_DIM), ('b_cmp', 128), ('b_slc', 128), ('b_win', 128))}
        tails = [jnp.concatenate([new[n], jnp.zeros((bs, tail_rows - tdec, new[n].shape[-1]), F32)], axis=1)
                 for n in ('a_k', 'misc', 'b_cmp', 'b_slc')]
        tails[1] = jnp.swapaxes(tails[1], 1, 2)
        kv_s, ki_s, cmp_s, slc_s = _gather_past(
            page_table, [cache_a_kv, jnp.swapaxes(cache_a_idx, 2, 3), cache_b_cmp, cache_b_slc], tails,
            (BF16, BF16, F32, BF16), (False, True, False, False), l)
        seq_s = past + tdec
        bias_a = _dsa_select(p32, p32, ki_s, 0, IDX_DIM, n_seq=bs, t_seq=tdec, tq=tdec, qoff=past,
                             k_sel=min(A_TOPK, seq_s // 4), bias_dtype=F32)
        o_a = _flash(p32, _OFF['a_q'] // MIX_W, kv_s, 0, bias_a, n_seq=bs, t_seq=tdec, tq=tdec, tkm=tkm_s, qoff=past,
                     shared=False, out_dtype=F32)
        ncp = -(-(-(-seq_s // CMP_STRIDE) - 1) // 128) * 128
        kcvc = _compress(cmp_s, w_cmp_pos[l], seq_s, ncp)
        win_full = jnp.concatenate([state_b_win[l], new['b_win']], axis=1)
        win_pad = jnp.concatenate([win_full, jnp.zeros((bs, -(w_buf + tdec) % 16, 128), F32)], axis=1)
        ocw, bias_b = _nsa_select(p32, p32, kcvc, win_pad, 0, n_seq=bs, t_seq=tdec, tq=tdec, qoff=past, seq_len=seq_s,
                                  lp=lps, win_dyn=False, win_pos0=past - w_buf, bias_dtype=F32, out_dtype=F32)
        o_b = _flash(p32, _OFF['b_q'] // MIX_W, slc_s, 0, bias_b, n_seq=bs, t_seq=tdec, tq=tdec, tkm=tkm_s, qoff=past,
                     shared=True, out_dtype=F32, gate_args=(p32, ocw))
        o_d, st = _gla(p32, lw['wal'], b_alpha[l], lw['gg'], _state_to_bd(state_d_gla[l]), n_seq=bs, t_seq=tdec,
                       tt=tdec, out_dtype=F32)
        u_tail = (_cols(p32, 'c_c', MIX_W) * _cols(p32, 'c_in', MIX_W)).reshape(bs, tdec, MIX_W)[:, tdec - 2:]
        xs = _merge(o_a, o_b, o_d, p32, xs, ms[2], conv_c[l], lw['wb'], lw['wo'], _conv_fix(state_c_conv[l], tdec),
                    tm=rs, t_seq=tdec)
        xs, y_s, a_full = _ffn(xs, gn2, ms[4], ms[3], ms[5], lw['wa'], lw['wg'], conv_ffn[l], bffn, lw['wout'], gfin,
                               _conv_fix(state_ffn_conv[l], tdec), tm=rs, t_seq=tdec)
        outs_s.append((new['a_k'], new['misc'], new['b_cmp'], new['b_slc'], win_full[:, tdec:], u_tail,
                       _bd_to_state(st), a_full.reshape(bs, tdec, D_FF)[:, tdec - 2:]))

        p32, p16 = _proj(xp, gn1, mp[1], mp[0], lw['w_in'], 1024, seq)
        p16_3 = p16.reshape(bp, seq, NP)
        a_kv = _cols(p32, 'a_k', 2 * MIX_W).reshape(bp, seq, 2 * MIX_W)
        a_idx = _cols(p32, 'misc', IDX_DIM).reshape(bp, seq, IDX_DIM)
        b_cmp = _cols(p32, 'b_cmp', 128).reshape(bp, seq, 128)
        b_slc = _cols(p32, 'b_slc', 128).reshape(bp, seq, 128)
        b_win = _cols(p32, 'b_win', 128).reshape(bp, seq, 128)[:, seq - min(WINDOW, seq):]
        k_sel = min(A_TOPK, seq // 4)
        bias_a = _dsa_select(p16, p32, p16_3, _OFF['misc'] // 128, 128, n_seq=bp, t_seq=seq, tq=128, qoff=0,
                             k_sel=k_sel, bias_dtype=BF16)
        q_scale = HEAD_DIM ** -0.5 * float(np.log2(np.e))
        o_a = jnp.transpose(_flash_t(jnp.transpose(_cols(p32, 'a_q', MIX_W) * q_scale).astype(BF16), p16,
                                     _OFF['a_k'] // MIX_W, MIX_W,
                                     jnp.transpose(_cols(p16, 'a_v', MIX_W)), bias_a, t_seq=seq, tq=512, tkm=2,
                                     shared=False))
        ncp = -(-(-(-seq // CMP_STRIDE) - 1) // 128) * 128
        kcvc = _compress(b_cmp, w_cmp_pos[l], seq, ncp)
        ocw, sel_b = _nsa_select(p16, p32, kcvc, p16_3, _OFF['b_win'] // 128, n_seq=bp, t_seq=seq, tq=128, qoff=0,
                                 seq_len=seq, lp=seq, win_dyn=True, win_pos0=0, bias_dtype=None, out_dtype=BF16)
        vt_b = jnp.transpose(p16[:, _OFF['b_slc'] + HEAD_DIM:_OFF['b_slc'] + 2 * HEAD_DIM])
        o_b = jnp.transpose(_flash_t(jnp.transpose(_cols(p32, 'b_q', MIX_W) * q_scale).astype(BF16), p16,
                                     _OFF['b_slc'] // 128, 128, vt_b,
                                     sel_b, t_seq=seq, tq=512, tkm=2, shared=True,
                                     gate_args=(jnp.transpose(_cols(p32, 'misc', 128)), jnp.transpose(ocw))))
        o_d, st = _gla(p32, lw['wal'], b_alpha[l], lw['gg'], jnp.zeros((bp, MIX_W, MIX_W), F32), n_seq=bp, t_seq=seq,
                       tt=512, out_dtype=BF16)
        u_tail = (_cols(p32, 'c_c', MIX_W) * _cols(p32, 'c_in', MIX_W)).reshape(bp, seq, MIX_W)[:, seq - 2:]
        xp = _merge(o_a, o_b, o_d, p32, xp, mp[2], conv_c[l], lw['wb'], lw['wo'], None, tm=256, t_seq=seq)
        xp, y_p, a_full = _ffn(xp, gn2, mp[4], mp[3], mp[5], lw['wa'], lw['wg'], conv_ffn[l], bffn, lw['wout'], gfin,
                               None, tm=512, t_seq=seq)
        outs_p.append((a_kv, a_idx, b_cmp, b_slc, b_win, u_tail, _bd_to_state(st),
                       a_full.reshape(bp, seq, D_FF)[:, seq - 2:]))

    sp = [jnp.stack(z) for z in zip(*outs_p)]
    ss = [jnp.stack(z) for z in zip(*outs_s)]
    res = [y_p.reshape(bp, seq, d), y_s.reshape(bs, tdec, d)]
    for a, b in zip(sp, ss):
        res += [a, b]
    return tuple(res)
```

```python
import functools

import numpy as np
import jax
import jax.numpy as jnp
from jax import lax
from jax.experimental import pallas as pl
from jax.experimental.pallas import tpu as pltpu

F32 = jnp.float32
BF16 = jnp.bfloat16
HI = lax.Precision.HIGHEST

D_MODEL = 1024
PAGE = 128
HEAD_DIM = 64
MIX_W = 256
N_HEADS = 4
IDX_DIM = 64
A_TOPK = 256
CMP_LEN = 32
CMP_STRIDE = 16
SLC_BLOCK = 64
N_SLC = 16
WINDOW = 512
CONV_W = 3
GATE_RANK = 16
GATE_TAU = 16.0
GLA_CHUNK = 64
D_FF = 2816
EPS = 1e-6
NEG = -1e30
INT_MIN = -2 ** 31
KEY_TILE = 512

_IN_SPLITS = (('a_q', 256), ('a_k', 256), ('a_v', 256), ('a_qi', 256), ('a_ki', 64), ('a_wi', 4),
              ('b_q', 256), ('b_cmp', 128), ('b_slc', 128), ('b_win', 128), ('b_g', 12),
              ('c_in', 256), ('c_b', 256), ('c_c', 256),
              ('d_q', 256), ('d_k', 256), ('d_v', 256), ('d_r', 256), ('d_a', 16), ('gate', 4096))
_OFF = dict(gate=0, a_k=4096, a_v=4352, a_q=4608, a_qi=4864, b_q=5120, b_cmp=5376, b_slc=5504, b_win=5632,
            misc=5760, c_in=5888, c_b=6144, c_c=6400, d_q=6656, d_k=6912, d_v=7168, d_r=7424)
NP = 7680
MISC_KI, MISC_WI, MISC_G, MISC_DA = 0, 64, 68, 80
VMEM_LIMIT = 56 * 1024 * 1024


def _cp(sem):
    return pltpu.CompilerParams(dimension_semantics=sem, vmem_limit_bytes=VMEM_LIMIT)


def _dot(a, b):
    return jnp.dot(a, b, preferred_element_type=F32)


def _dot_nt(a, b):
    return lax.dot_general(a, b, (((1,), (1,)), ((), ())), preferred_element_type=F32)


def _dot_hi(a, b):
    return jnp.dot(a, b, preferred_element_type=F32, precision=HI)


def _dot_exact01(a01, x):
    hi = x.astype(BF16)
    lo = (x - hi.astype(F32)).astype(BF16)
    return _dot(a01, hi) + _dot(a01, lo)


def _dot_x_exact01(x, b01):
    hi = x.astype(BF16)
    lo = (x - hi.astype(F32)).astype(BF16)
    return _dot(hi, b01) + _dot(lo, b01)


def _const_spec(shape):
    nd = len(shape)
    return pl.BlockSpec(shape, lambda *a: (0,) * nd, pipeline_mode=pl.Buffered(1))


def _iota(shape, dim):
    return lax.broadcasted_iota(jnp.int32, shape, dim)


def _ada_kernel(c_ref, w_ref, b_ref, o_ref):
    o_ref[0] = _dot_hi(c_ref[...], w_ref[0]) + b_ref[0]


def _ada(c_all, w_ada, b_ada):
    depth, d, n6 = w_ada.shape
    rows = c_all.shape[0]
    tn = 1024
    return pl.pallas_call(
        _ada_kernel, grid=(depth, n6 // tn),
        in_specs=[pl.BlockSpec((rows, d), lambda l, j: (0, 0)),
                  pl.BlockSpec((1, d, tn), lambda l, j: (l, 0, j)),
                  pl.BlockSpec((1, 1, tn), lambda l, j: (l, 0, j))],
        out_specs=pl.BlockSpec((1, rows, tn), lambda l, j: (l, 0, j)),
        out_shape=jax.ShapeDtypeStruct((depth, rows, n6), F32),
        compiler_params=_cp(("arbitrary", "arbitrary")), name="ada",
    )(c_all, w_ada, b_ada.reshape(depth, 1, n6))


def _norm_mod(x, g, sc, sh):
    y = x * lax.rsqrt(jnp.mean(x * x, axis=-1, keepdims=True) + EPS) * g
    return y * (1.0 + sc) + sh


def _proj_kernel(x_ref, g_ref, sc_ref, sh_ref, w_ref, o32_ref, o16_ref, h_ref):
    @pl.when(pl.program_id(1) == 0)
    def _():
        h_ref[...] = _norm_mod(x_ref[...], g_ref[...], sc_ref[0], sh_ref[0]).astype(BF16)

    acc = _dot(h_ref[...], w_ref[...])
    o32_ref[...] = acc
    o16_ref[...] = acc.astype(BF16)


def _mod_spec(mod, tm, rows_per_group):
    mb = mod.shape[1]
    tiles = max(rows_per_group // tm, 1)
    return pl.BlockSpec((1, mb, D_MODEL), lambda i, *_: (i // tiles, 0, 0))


def _proj(x, g, sc, sh, w, tm, rows_per_group):
    rows = x.shape[0]
    tn = 1536
    return pl.pallas_call(
        _proj_kernel, grid=(rows // tm, NP // tn),
        in_specs=[pl.BlockSpec((tm, D_MODEL), lambda i, j: (i, 0)),
                  pl.BlockSpec((1, D_MODEL), lambda i, j: (0, 0)),
                  _mod_spec(sc, tm, rows_per_group), _mod_spec(sh, tm, rows_per_group),
                  pl.BlockSpec((D_MODEL, tn), lambda i, j: (0, j))],
        out_specs=[pl.BlockSpec((tm, tn), lambda i, j: (i, j)), pl.BlockSpec((tm, tn), lambda i, j: (i, j))],
        out_shape=[jax.ShapeDtypeStruct((rows, NP), F32), jax.ShapeDtypeStruct((rows, NP), BF16)],
        scratch_shapes=[pltpu.VMEM((tm, D_MODEL), BF16)],
        compiler_params=_cp(("arbitrary", "arbitrary")), name="proj",
    )(x, g, sc, sh, w)


PAGES_PER_STEP = KEY_TILE // PAGE


def _gather_kernel(pt_ref, *refs, n_arr, n_chunks, feature_major):
    ppc = PAGES_PER_STEP
    pages, tails, outs = refs[:n_arr * ppc], refs[n_arr * ppc:n_arr * (ppc + 1)], refs[n_arr * (ppc + 1):]
    c = pl.program_id(1)

    @pl.when(c < n_chunks - 1)
    def _():
        for k in range(n_arr):
            for r in range(ppc):
                page = pages[k * ppc + r][0, 0].astype(outs[k].dtype)
                if feature_major[k]:
                    outs[k][0, 0, :, r * PAGE:(r + 1) * PAGE] = page
                else:
                    outs[k][0, r * PAGE:(r + 1) * PAGE, :] = page

    @pl.when(c == n_chunks - 1)
    def _():
        for k in range(n_arr):
            if feature_major[k]:
                outs[k][0, 0] = tails[k][0].astype(outs[k].dtype)
            else:
                outs[k][0] = tails[k][0].astype(outs[k].dtype)


def _gather_past(page_table, pools, tails, out_dtypes, feature_major, layer):
    n_seq, n_pages = page_table.shape
    n_arr, ppc = len(pools), PAGES_PER_STEP
    assert n_pages % ppc == 0
    n_chunks = n_pages // ppc + 1
    chunk = ppc * PAGE
    in_specs, out_specs, out_shapes = [], [], []
    for p in pools:
        for r in range(ppc):
            in_specs.append(pl.BlockSpec(
                (1, 1) + p.shape[2:],
                lambda b, c, pt, r=r: (layer, pt[b, jnp.minimum(c * ppc + r, n_pages - 1)], 0, 0)))
    in_specs += [pl.BlockSpec((1,) + t.shape[1:], lambda b, c, pt: (b, 0, 0)) for t in tails]
    for p, dt, fm in zip(pools, out_dtypes, feature_major):
        if fm:
            out_specs.append(pl.BlockSpec((1, 1, p.shape[2], chunk), lambda b, c, pt: (b, c, 0, 0)))
            out_shapes.append(jax.ShapeDtypeStruct((n_seq, n_chunks, p.shape[2], chunk), dt))
        else:
            out_specs.append(pl.BlockSpec((1, chunk, p.shape[3]), lambda b, c, pt: (b, c, 0)))
            out_shapes.append(jax.ShapeDtypeStruct((n_seq, n_chunks * chunk, p.shape[3]), dt))
    args = [p for p in pools for _ in range(ppc)] + list(tails)
    kern = functools.partial(_gather_kernel, n_arr=n_arr, n_chunks=n_chunks, feature_major=tuple(feature_major))
    return pl.pallas_call(
        kern,
        grid_spec=pltpu.PrefetchScalarGridSpec(
            num_scalar_prefetch=1, grid=(n_seq, n_chunks), in_specs=in_specs, out_specs=out_specs),
        out_shape=out_shapes,
        compiler_params=_cp(("arbitrary", "arbitrary")), name="gather_past",
    )(page_table, *args)


def _stack_heads(q, width):
    t = q.shape[0]
    parts = []
    for h in range(N_HEADS):
        p = q[:, h * HEAD_DIM:(h + 1) * HEAD_DIM]
        if width > HEAD_DIM:
            p = jnp.concatenate([p, jnp.zeros((t, width - HEAD_DIM), q.dtype)], axis=1)
        parts.append(p)
    return jnp.concatenate(parts, axis=0)


def _dsa_sel_kernel(qi_ref, misc_ref, ki_ref, bias_ref, s_ref, *planes, tq, nk, k_sel, qoff, fk):
    tk = KEY_TILE
    i = pl.program_id(1)
    q0 = qoff + i * tq
    qpos = q0 + _iota((tq, 1), 0)
    nlim = jnp.minimum((q0 + tq - 1) // tk + 1, nk)
    qs = _stack_heads(qi_ref[...].astype(BF16), fk)
    wi = misc_ref[:, MISC_WI:MISC_WI + N_HEADS]

    def score_tile(j, c):
        if len(ki_ref.shape) == 4:
            d = _dot(qs, ki_ref[0, j].astype(BF16)).reshape(N_HEADS, tq, tk)
        else:
            kt = ki_ref[0, pl.ds(pl.multiple_of(j * tk, tk), tk), :].astype(BF16)
            d = _dot_nt(qs, kt).reshape(N_HEADS, tq, tk)
        sc = wi[:, 0:1] * jnp.maximum(d[0], 0.0)
        for h in range(1, N_HEADS):
            sc = sc + wi[:, h:h + 1] * jnp.maximum(d[h], 0.0)
        bits = lax.bitcast_convert_type(sc, jnp.int32)
        key = bits ^ ((bits >> 31) & 0x7FFFFFFF)
        key = jnp.where(key == -1, 0, key)
        kpos = j * tk + _iota((1, tk), 1)
        s_ref[j] = jnp.where(kpos <= qpos, key, INT_MIN)
        return c

    def score_pair(j2, c):
        score_tile(2 * j2, c)
        return score_tile(2 * j2 + 1, c)

    lax.fori_loop(0, nlim // 2, score_pair, 0)

    @pl.when(nlim % 2 == 1)
    def _():
        score_tile(nlim - 1, 0)

    if planes:
        p_ref, cand_ref, candx_ref = planes
        n_main, n_extra = min(nk, 32), max(nk - 32, 0)

        def unused_tile(j, c):
            s_ref[j] = jnp.full((tq, tk), INT_MIN, jnp.int32)
            return c

        lax.fori_loop(nlim, nk, unused_tile, 0)

        def transpose_bits(rg, c):
            rows = pl.ds(pl.multiple_of(rg * 8, 8), 8)
            for lc in range(tk // 128):
                lanes = slice(lc * 128, (lc + 1) * 128)
                x = [s_ref[j, rows, lanes] ^ INT_MIN if j < n_main else jnp.zeros((8, 128), jnp.int32)
                     for j in range(32)]
                s, m = 16, 0x0000FFFF
                while s >= 1:
                    for a in range(32):
                        if (a & s) == 0:
                            t = (lax.shift_right_logical(x[a], s) ^ x[a + s]) & m
                            x[a + s] = x[a + s] ^ t
                            x[a] = x[a] ^ lax.shift_left(t, s)
                    s //= 2
                    if s:
                        m = (m ^ (m << s)) & 0xFFFFFFFF
                        m = m - (1 << 32) if m >= (1 << 31) else m
                for b in range(32):
                    p_ref[b, rows, lanes] = x[b]
            return c

        lax.fori_loop(0, tq // 8, transpose_bits, 0)
        cand_ref[...] = jnp.full((tq, tk), -1, jnp.int32)
        for e in range(n_extra):
            candx_ref[e] = jnp.ones((tq, tk), jnp.int32)

        def lane_sum(counts):
            part = counts[:, 0:128]
            for c in range(1, tk // 128):
                part = part + counts[:, c * 128:(c + 1) * 128]
            return jnp.sum(part.astype(F32), axis=-1, keepdims=True)

        def bit_step(t, carry):
            need, thr_u = carry
            b = 31 - t
            ones = cand_ref[...] & p_ref[b]
            cnt = lane_sum(lax.population_count(ones))
            ones_x = []
            for e in range(n_extra):
                bit = lax.shift_right_logical(s_ref[32 + e] ^ INT_MIN, b) & 1
                ones_x.append(candx_ref[e] & bit)
                cnt = cnt + lane_sum(ones_x[e])
            take = cnt >= need
            cand_ref[...] = jnp.where(take, ones, cand_ref[...] & ~p_ref[b])
            for e in range(n_extra):
                candx_ref[e] = jnp.where(take, ones_x[e], candx_ref[e] ^ ones_x[e])
            return jnp.where(take, need, need - cnt), jnp.where(take, thr_u | lax.shift_left(jnp.int32(1), b), thr_u)

        ties_allowed, thr_u = lax.fori_loop(
            0, 32, bit_step, (jnp.full((tq, 1), float(k_sel), F32), jnp.zeros((tq, 1), jnp.int32)))
        thr = thr_u ^ INT_MIN
        n_ties = lane_sum(lax.population_count(cand_ref[...]))
        for e in range(n_extra):
            n_ties = n_ties + lane_sum(candx_ref[e])
        need_rank = jnp.max(jnp.where(n_ties > ties_allowed, 1.0, 0.0)) > 0.5
        ties_allowed_fn = lambda: ties_allowed
    else:
        def count(pred, level):
            level_b = jnp.broadcast_to(level, (tq, 128))

            def body(j, acc):
                for c in range(tk // 128):
                    acc = acc + jnp.where(pred(s_ref[j, :, c * 128:(c + 1) * 128], level_b), 1.0, 0.0)
                return acc
            return jnp.sum(lax.fori_loop(0, nlim, body, jnp.zeros((tq, 128), F32)), axis=-1, keepdims=True)

        def bit_step(t, carry):
            c, n_ge = carry
            trial = c + lax.shift_left(jnp.int32(1), 31 - t)
            cnt = count(lambda s, lv: s >= lv, trial)
            ok = cnt >= k_sel
            return jnp.where(ok, trial, c), jnp.where(ok, cnt, n_ge)

        thr, n_ge = lax.fori_loop(0, 32, bit_step, (jnp.full((tq, 1), INT_MIN, jnp.int32),
                                                    jnp.full((tq, 1), float(nk * tk), F32)))
        need_rank = jnp.max(jnp.where(n_ge > k_sel, 1.0, 0.0)) > 0.5
        ties_allowed_fn = lambda: k_sel - count(lambda s, lv: s > lv, thr)

    @pl.when(need_rank)
    def _():
        ties_allowed = ties_allowed_fn()
        tri = jnp.where(_iota((tk, tk), 0) <= _iota((tk, tk), 1), 1.0, 0.0).astype(BF16)

        def out_tile(j, carry):
            key = s_ref[j]
            eq = jnp.where(key == thr, 1.0, 0.0)
            rank = _dot(eq.astype(BF16), tri) + carry
            kpos = j * tk + _iota((1, tk), 1)
            take = jnp.where(key > thr, 1.0, jnp.where(rank <= ties_allowed, eq, 0.0))
            take = jnp.where(kpos <= qpos, take, 0.0)
            bias_ref[j] = jnp.where(take > 0.5, 0.0, NEG).astype(bias_ref.dtype)
            return rank[:, tk - 1:tk]

        lax.fori_loop(0, nlim, out_tile, jnp.zeros((tq, 1), F32))

    @pl.when(jnp.logical_not(need_rank))
    def _():
        def out_tile(j, c):
            kpos = j * tk + _iota((1, tk), 1)
            take = jnp.where(kpos <= qpos, jnp.where(s_ref[j] >= thr, 1.0, 0.0), 0.0)
            bias_ref[j] = jnp.where(take > 0.5, 0.0, NEG).astype(bias_ref.dtype)
            return c

        lax.fori_loop(0, nlim, out_tile, 0)

    def fill(j, c):
        bias_ref[j] = jnp.full((tq, tk), NEG, bias_ref.dtype)
        return c

    lax.fori_loop(nlim, nk, fill, 0)


def _dsa_select(q_arr, p32, ki3, ki_col, fk, *, n_seq, t_seq, tq, qoff, k_sel, bias_dtype):
    if ki3.ndim == 4:
        nk = ki3.shape[1]
        ki_spec = pl.BlockSpec((1,) + ki3.shape[1:], lambda b, i: (b, 0, 0, 0))
    else:
        nk = ki3.shape[1] // KEY_TILE
        ki_spec = pl.BlockSpec((1, ki3.shape[1], fk), lambda b, i: (b, 0, ki_col))
    nq = t_seq // tq
    rows = n_seq * t_seq
    kern = functools.partial(_dsa_sel_kernel, tq=tq, nk=nk, k_sel=k_sel, qoff=qoff, fk=fk)
    return pl.pallas_call(
        kern, grid=(n_seq, nq),
        in_specs=[pl.BlockSpec((tq, MIX_W), lambda b, i: (b * nq + i, _OFF['a_qi'] // MIX_W)),
                  pl.BlockSpec((tq, 128), lambda b, i: (b * nq + i, _OFF['misc'] // 128)), ki_spec],
        out_specs=pl.BlockSpec((nk, tq, KEY_TILE), lambda b, i: (0, b * nq + i, 0)),
        out_shape=jax.ShapeDtypeStruct((nk, rows, KEY_TILE), bias_dtype),
        scratch_shapes=[pltpu.VMEM((nk, tq, KEY_TILE), jnp.int32)]
        + ([pltpu.VMEM((32, tq, KEY_TILE), jnp.int32), pltpu.VMEM((tq, KEY_TILE), jnp.int32),
            pltpu.VMEM((max(nk - 32, 1), tq, KEY_TILE), jnp.int32)] if nk <= 36 else []),
        compiler_params=_cp(("arbitrary", "arbitrary")), name="dsa_select",
    )(q_arr, p32, ki3)


def _flash_kernel(sb, sq, sk, sfl, q_ref, kv_ref, bias_ref, *rest, tq, tkm, shared, gated):
    if gated:
        misc_ref, add_ref, o_ref, m_ref, l_ref, acc_ref, qs_ref = rest
    else:
        o_ref, m_ref, l_ref, acc_ref, qs_ref = rest
    tk = KEY_TILE
    flags = sfl[pl.program_id(0)]
    dv = HEAD_DIM if shared else MIX_W
    lane_head = _iota((1, MIX_W), 1) // HEAD_DIM

    @pl.when((flags & 1) != 0)
    def _():
        m_ref[...] = jnp.full(m_ref.shape, NEG, F32)
        l_ref[...] = jnp.zeros(l_ref.shape, F32)
        acc_ref[...] = jnp.zeros(acc_ref.shape, F32)
        q = (q_ref[...] * (HEAD_DIM ** -0.5)).astype(BF16)
        if shared:
            qs_ref[...] = _stack_heads(q, HEAD_DIM)
        else:
            qs_ref[...] = jnp.concatenate([jnp.where(lane_head == h, q, jnp.zeros_like(q)) for h in range(N_HEADS)],
                                          axis=0)

    tkk = tkm * tk
    kv = kv_ref[0]
    k = kv[:, :dv].astype(BF16)
    v = kv[:, dv:2 * dv].astype(BF16)
    bias = jnp.concatenate([bias_ref[u].astype(F32) for u in range(tkm)], axis=1)
    s = _dot_nt(qs_ref[...], k).reshape(N_HEADS, tq, tkk) + bias[None]
    m_old = m_ref[...]
    m_new = jnp.maximum(m_old, jnp.max(s, axis=-1, keepdims=True))
    alpha = jnp.exp(m_old - m_new)
    p = jnp.exp(s - m_new)
    l_ref[...] = alpha * l_ref[...] + jnp.sum(p, axis=-1, keepdims=True)
    pv = _dot(p.reshape(N_HEADS * tq, tkk).astype(BF16), v).reshape(N_HEADS, tq, dv)
    acc_ref[...] = alpha * acc_ref[...] + pv
    m_ref[...] = m_new

    @pl.when((flags & 2) != 0)
    def _():
        o = acc_ref[...] / l_ref[...]
        if shared:
            out = jnp.concatenate([o[h] for h in range(N_HEADS)], axis=1)
        else:
            out = jnp.where(lane_head == 0, o[0], 0.0)
            for h in range(1, N_HEADS):
                out = out + jnp.where(lane_head == h, o[h], 0.0)
        if gated:
            g = jax.nn.sigmoid(misc_ref[...])
            gate = jnp.zeros((tq, MIX_W), F32)
            for h in range(N_HEADS):
                c = MISC_G + 3 * h + 1
                gate = gate + jnp.where(lane_head == h, g[:, c:c + 1], 0.0)
            out = add_ref[...].astype(F32) + gate * out
        o_ref[...] = out.astype(o_ref.dtype)


def _flash_steps(n_seq, t_seq, tq, tk, lp, qoff):
    sb, sq, sk, sfl = [], [], [], []
    for b in range(n_seq):
        for i in range(t_seq // tq):
            nlim = min((qoff + (i + 1) * tq - 1) // tk + 1, lp // tk)
            for j in range(nlim):
                sb.append(b), sq.append(i), sk.append(j)
                sfl.append((1 if j == 0 else 0) | (2 if j == nlim - 1 else 0))
    return [jnp.asarray(np.asarray(a, np.int32)) for a in (sb, sq, sk, sfl)]


def _flash(q_arr, q_col, kv3, kv_col, bias, *, n_seq, t_seq, tq, tkm, qoff, shared, out_dtype, gate_args=None):
    lp = kv3.shape[1]
    tk = tkm * KEY_TILE
    nq = t_seq // tq
    rows = n_seq * t_seq
    wblk = 2 * (HEAD_DIM if shared else MIX_W)
    dv = HEAD_DIM if shared else MIX_W
    steps = _flash_steps(n_seq, t_seq, tq, tk, lp, qoff)
    gated = gate_args is not None
    row = lambda s, sb, sq, sk, sfl: sb[s] * nq + sq[s]
    in_specs = [pl.BlockSpec((tq, MIX_W), lambda s, sb, sq, sk, sfl: (row(s, sb, sq, sk, sfl), q_col)),
                pl.BlockSpec((1, tk, wblk), lambda s, sb, sq, sk, sfl: (sb[s], sk[s], kv_col)),
                pl.BlockSpec((tkm, tq, KEY_TILE), lambda s, sb, sq, sk, sfl: (sk[s], row(s, sb, sq, sk, sfl), 0))]
    args = [q_arr, kv3, bias]
    if gated:
        in_specs += [pl.BlockSpec((tq, 128), lambda s, sb, sq, sk, sfl: (row(s, sb, sq, sk, sfl), _OFF['misc'] // 128)),
                     pl.BlockSpec((tq, MIX_W), lambda s, sb, sq, sk, sfl: (row(s, sb, sq, sk, sfl), 0))]
        args += list(gate_args)
    kern = functools.partial(_flash_kernel, tq=tq, tkm=tkm, shared=shared, gated=gated)
    return pl.pallas_call(
        kern,
        grid_spec=pltpu.PrefetchScalarGridSpec(
            num_scalar_prefetch=4, grid=(int(steps[0].shape[0]),), in_specs=in_specs,
            out_specs=pl.BlockSpec((tq, MIX_W), lambda s, sb, sq, sk, sfl: (row(s, sb, sq, sk, sfl), 0)),
            scratch_shapes=[pltpu.VMEM((N_HEADS, tq, 1), F32), pltpu.VMEM((N_HEADS, tq, 1), F32),
                            pltpu.VMEM((N_HEADS, tq, dv), F32), pltpu.VMEM((N_HEADS * tq, dv), BF16)]),
        out_shape=jax.ShapeDtypeStruct((rows, MIX_W), out_dtype),
        compiler_params=_cp(("arbitrary",)), name="flash_shared" if shared else "flash_heads",
    )(*steps, *args)


def _flash_t_kernel(sq, sk, sfl, qt_ref, k_ref, vt_ref, bias_ref, *rest, tq, tkm, shared, gated, block_sel):
    if gated:
        misct_ref, addt_ref, o_ref, m_ref, acc_ref = rest
    else:
        o_ref, m_ref, acc_ref = rest
    tk = KEY_TILE
    flags = sfl[pl.program_id(0)]
    row_head = _iota((MIX_W, 1), 0) // HEAD_DIM
    ones_row = jnp.where(_iota((8, tk), 0) == 0, 1.0, 0.0).astype(BF16)

    @pl.when((flags & 1) != 0)
    def _():
        m_ref[...] = jnp.full(m_ref.shape, NEG, F32)
        acc_ref[...] = jnp.zeros(acc_ref.shape, F32)

    qt = qt_ref[...]
    for u in range(tkm):
        k = k_ref[u * tk:(u + 1) * tk, :]
        k = k[:, :HEAD_DIM] if shared else k
        if block_sel:
            nb = tk // SLC_BLOCK
            kt = sk[pl.program_id(0)] * tkm + u
            blocks = bias_ref[pl.ds(pl.multiple_of(kt * nb, nb), nb), :]
            on = jnp.concatenate([jnp.broadcast_to(blocks[r:r + 1, :], (SLC_BLOCK, tq)) for r in range(nb)], axis=0)
            kpos = kt * tk + _iota((tk, 1), 0)
            qpos = sq[pl.program_id(0)] * tq + _iota((1, tq), 1)
            bias_t = jnp.where((on > 0.5) & (kpos <= qpos), 0.0, NEG)
        else:
            bias_t = bias_ref[u].astype(F32).T
        for h in range(N_HEADS):
            hs = slice(h * HEAD_DIM, (h + 1) * HEAD_DIM)
            q_h = qt[hs, :] if shared else jnp.where(row_head == h, qt, jnp.zeros_like(qt))
            s = _dot(k, q_h) + bias_t
            m_old = m_ref[h]
            m_new = jnp.maximum(m_old, jnp.max(s, axis=0, keepdims=True))
            alpha = jnp.exp2(m_old - m_new)
            p = jnp.exp2(s - m_new)
            vt_h = vt_ref[0:HEAD_DIM, u * tk:(u + 1) * tk] if shared else vt_ref[hs, u * tk:(u + 1) * tk]
            acc_ref[h] = alpha * acc_ref[h] + _dot(jnp.concatenate([vt_h, ones_row], axis=0), p.astype(BF16))
            m_ref[h] = m_new

    @pl.when((flags & 2) != 0)
    def _():
        for h in range(N_HEADS):
            hs = slice(h * HEAD_DIM, (h + 1) * HEAD_DIM)
            acc = acc_ref[h]
            out = acc[0:HEAD_DIM] / acc[HEAD_DIM:HEAD_DIM + 1]
            if gated:
                c = MISC_G + 3 * h + 1
                out = addt_ref[hs, :].astype(F32) + jax.nn.sigmoid(misct_ref[c:c + 1, :]) * out
            o_ref[hs, :] = out.astype(o_ref.dtype)


def _flash_t(qt, k_arr, k_col, k_width, vt, bias, *, t_seq, tq, tkm, shared, gate_args=None):
    tk = tkm * KEY_TILE
    steps = _flash_steps(1, t_seq, tq, tk, t_seq, 0)[1:]
    gated = gate_args is not None
    vrows = vt.shape[0]
    block_sel = bias.ndim == 2
    if block_sel:
        bias_spec = pl.BlockSpec((bias.shape[0], tq), lambda s, sq, sk, sfl: (0, sq[s]))
    else:
        bias_spec = pl.BlockSpec((tkm, tq, KEY_TILE), lambda s, sq, sk, sfl: (sk[s], sq[s], 0))
    in_specs = [pl.BlockSpec((MIX_W, tq), lambda s, sq, sk, sfl: (0, sq[s])),
                pl.BlockSpec((tk, k_width), lambda s, sq, sk, sfl: (sk[s], k_col)),
                pl.BlockSpec((vrows, tk), lambda s, sq, sk, sfl: (0, sk[s])), bias_spec]
    args = [qt, k_arr, vt, bias]
    if gated:
        in_specs += [pl.BlockSpec((128, tq), lambda s, sq, sk, sfl: (0, sq[s])),
                     pl.BlockSpec((MIX_W, tq), lambda s, sq, sk, sfl: (0, sq[s]))]
        args += list(gate_args)
    kern = functools.partial(_flash_t_kernel, tq=tq, tkm=tkm, shared=shared, gated=gated, block_sel=block_sel)
    return pl.pallas_call(
        kern,
        grid_spec=pltpu.PrefetchScalarGridSpec(
            num_scalar_prefetch=3, grid=(int(steps[0].shape[0]),), in_specs=in_specs,
            out_specs=pl.BlockSpec((MIX_W, tq), lambda s, sq, sk, sfl: (0, sq[s])),
            scratch_shapes=[pltpu.VMEM((N_HEADS, 1, tq), F32), pltpu.VMEM((N_HEADS, HEAD_DIM + 8, tq), F32)]),
        out_shape=jax.ShapeDtypeStruct((MIX_W, t_seq), BF16),
        compiler_params=_cp(("arbitrary",)), name="flash_t_shared" if shared else "flash_t_heads",
    )(*steps, *args)


def _compress_kernel(x_ref, w_ref, o_ref, *, n_blocks, ncp):
    w = w_ref[...]
    e = jnp.exp(w - jnp.max(w, axis=-1, keepdims=True))
    w = e / jnp.sum(e, axis=-1, keepdims=True)
    n16 = x_ref.shape[1] // CMP_STRIDE
    width = x_ref.shape[2]
    first = jnp.zeros((n16, width), F32)
    second = jnp.zeros((n16, width), F32)
    for j in range(CMP_STRIDE):
        xj = x_ref[0, pl.ds(j, n16, stride=CMP_STRIDE), :]
        first = first + xj * w[:, j:j + 1]
        second = second + xj * w[:, CMP_STRIDE + j:CMP_STRIDE + j + 1]
    shifted = jnp.concatenate([second[1:], jnp.zeros((1, width), F32)], axis=0)
    out = first + shifted
    if n16 < ncp:
        out = jnp.concatenate([out, jnp.zeros((ncp - n16, width), F32)], axis=0)
    out = out[:ncp]
    o_ref[0] = jnp.where(_iota((ncp, 1), 0) < n_blocks, out, 0.0)


def _compress(rows3, w_pos, length, ncp):
    n_seq, lp, width = rows3.shape
    assert lp % CMP_STRIDE == 0
    n_blocks = -(-length // CMP_STRIDE) - 1
    kern = functools.partial(_compress_kernel, n_blocks=n_blocks, ncp=ncp)
    return pl.pallas_call(
        kern, grid=(n_seq,),
        in_specs=[pl.BlockSpec((1, lp, width), lambda b: (b, 0, 0)),
                  pl.BlockSpec((1, CMP_LEN), lambda b: (0, 0))],
        out_specs=pl.BlockSpec((1, ncp, width), lambda b: (b, 0, 0)),
        out_shape=jax.ShapeDtypeStruct((n_seq, ncp, width), F32),
        compiler_params=_cp(("arbitrary",)), name="compress",
    )(rows3, w_pos.reshape(1, CMP_LEN))


def _masked_softmax(s, mask):
    s = jnp.where(mask, s, NEG)
    m = jnp.max(s, axis=-1, keepdims=True)
    e = jnp.where(mask, jnp.exp(s - m), 0.0)
    return e / jnp.maximum(jnp.sum(e, axis=-1, keepdims=True), 1e-30)


def _nsa_sel_kernel(q_ref, misc_ref, kc_ref, win_ref, ocw_ref, bias_ref, *, tq, nk, qoff, n_cmp, n_blk, nbp, n_top,
                    win_rows, win_dyn, win_pos0, block_out):
    tk = KEY_TILE
    i = pl.program_id(1)
    q0 = qoff + i * tq
    qpos = q0 + _iota((tq, 1), 0)
    nlim = jnp.minimum((q0 + tq - 1) // tk + 1, nk)
    qs = _stack_heads((q_ref[...] * (HEAD_DIM ** -0.5)).astype(BF16), HEAD_DIM)
    ncp = kc_ref.shape[1]

    kcv = kc_ref[0]
    kc = kcv[:, :HEAD_DIM].astype(BF16)
    vc = kcv[:, HEAD_DIM:].astype(BF16)
    cidx = _iota((1, ncp), 1)
    c_mask = ((cidx * CMP_STRIDE + CMP_LEN - 1) <= qpos) & (cidx < n_cmp)
    s_c = _dot_nt(qs, kc).reshape(N_HEADS, tq, ncp)
    p_c = _masked_softmax(s_c, c_mask[None])
    o_c = _dot(p_c.reshape(N_HEADS * tq, ncp).astype(BF16), vc).reshape(N_HEADS, tq, HEAD_DIM)

    c_start = _iota((ncp, 1), 0) * CMP_STRIDE
    s_start = _iota((1, nbp), 1) * SLC_BLOCK
    overlap = (c_start < s_start + SLC_BLOCK) & (c_start + CMP_LEN > s_start) & (_iota((ncp, 1), 0) < n_cmp)
    imp = _dot_x_exact01(p_c[0] + p_c[1] + p_c[2] + p_c[3], jnp.where(overlap, 1.0, 0.0).astype(BF16))
    blk = _iota((1, nbp), 1)
    cur = qpos // SLC_BLOCK
    forced = (blk == 0) | (blk == cur) | (blk == cur - 1)
    live = jnp.where(forced, jnp.inf, jnp.where((blk <= cur) & (blk < n_blk), imp, -jnp.inf))
    blk_f = blk.astype(F32)

    def pick(t, carry):
        live, sel = carry
        top = jnp.max(live, axis=-1, keepdims=True)
        first = jnp.min(jnp.where(live == top, blk_f, float(nbp)), axis=-1, keepdims=True)
        hit = blk_f == first
        return jnp.where(hit, -jnp.inf, live), jnp.where(hit, 1.0, sel)

    if block_out:
        blk_t = _iota((nbp, 1), 0).astype(F32)

        def pick_t(t, carry):
            live_t, sel_t = carry
            top = jnp.max(live_t, axis=0, keepdims=True)
            first = jnp.min(jnp.where(live_t == top, blk_t, float(nbp)), axis=0, keepdims=True)
            hit = blk_t == first
            return jnp.where(hit, -jnp.inf, live_t), jnp.where(hit, 1.0, sel_t)

        _, sel_t = lax.fori_loop(0, n_top, pick_t, (live.T, jnp.zeros((nbp, tq), F32)))
        bias_ref[...] = sel_t
    else:
        _, sel = lax.fori_loop(0, n_top, pick, (live, jnp.zeros((tq, nbp), F32)))
        sel = sel.astype(BF16)
        row_blk = _iota((nbp, 1), 0)

        def bias_tile(j, c):
            kpos = j * tk + _iota((1, tk), 1)
            expand = jnp.where(row_blk == kpos // SLC_BLOCK, 1.0, 0.0).astype(BF16)
            on = _dot(sel, expand)
            ok = (on > 0.5) & (kpos <= qpos)
            bias_ref[j] = jnp.where(ok, 0.0, NEG).astype(bias_ref.dtype)
            return c

        lax.fori_loop(0, nlim, bias_tile, 0)

        def fill(j, c):
            bias_ref[j] = jnp.full((tq, tk), NEG, bias_ref.dtype)
            return c

        lax.fori_loop(nlim, nk, fill, 0)

    if win_dyn:
        start = pl.multiple_of(jnp.maximum(q0 - WINDOW, 0), 8)
        wkv = win_ref[0, pl.ds(start, win_rows), :]
        kwpos = start + _iota((1, win_rows), 1)
    else:
        wkv = win_ref[0]
        kwpos = win_pos0 + _iota((1, win_rows), 1)
    kw = wkv[:, :HEAD_DIM].astype(BF16)
    vw = wkv[:, HEAD_DIM:].astype(BF16)
    rel = qpos - kwpos
    w_mask = (rel >= 0) & (rel < WINDOW) & (kwpos >= 0)
    s_w = _dot_nt(qs, kw).reshape(N_HEADS, tq, win_rows)
    p_w = _masked_softmax(s_w, w_mask[None])
    o_w = _dot(p_w.reshape(N_HEADS * tq, win_rows).astype(BF16), vw).reshape(N_HEADS, tq, HEAD_DIM)

    g = jax.nn.sigmoid(misc_ref[...])
    parts = []
    for h in range(N_HEADS):
        c = MISC_G + 3 * h
        parts.append(g[:, c:c + 1] * o_c[h] + g[:, c + 2:c + 3] * o_w[h])
    ocw_ref[...] = jnp.concatenate(parts, axis=1).astype(ocw_ref.dtype)


def _nsa_select(q_arr, p32, kcvc, win3, win_col, *, n_seq, t_seq, tq, qoff, seq_len, lp, win_dyn, win_pos0, bias_dtype,
                out_dtype):
    nk, nq = lp // KEY_TILE, t_seq // tq
    rows = n_seq * t_seq
    n_cmp = -(-seq_len // CMP_STRIDE) - 1
    n_blk = -(-seq_len // SLC_BLOCK)
    nbp = -(-n_blk // 128) * 128
    win_rows = (tq + WINDOW) if win_dyn else win3.shape[1]
    block_out = bias_dtype is None
    kern = functools.partial(_nsa_sel_kernel, tq=tq, nk=nk, qoff=qoff, n_cmp=n_cmp, n_blk=n_blk, nbp=nbp,
                             n_top=min(N_SLC, n_blk), win_rows=win_rows, win_dyn=win_dyn, win_pos0=win_pos0,
                             block_out=block_out)
    if block_out:
        sel_spec = pl.BlockSpec((nbp, tq), lambda b, i: (0, b * nq + i))
        sel_shape = jax.ShapeDtypeStruct((nbp, rows), F32)
    else:
        sel_spec = pl.BlockSpec((nk, tq, KEY_TILE), lambda b, i: (0, b * nq + i, 0))
        sel_shape = jax.ShapeDtypeStruct((nk, rows, KEY_TILE), bias_dtype)
    return pl.pallas_call(
        kern, grid=(n_seq, nq),
        in_specs=[pl.BlockSpec((tq, MIX_W), lambda b, i: (b * nq + i, _OFF['b_q'] // MIX_W)),
                  pl.BlockSpec((tq, 128), lambda b, i: (b * nq + i, _OFF['misc'] // 128)),
                  pl.BlockSpec((1,) + kcvc.shape[1:], lambda b, i: (b, 0, 0)),
                  pl.BlockSpec((1, win3.shape[1], 128), lambda b, i: (b, 0, win_col))],
        out_specs=[pl.BlockSpec((tq, MIX_W), lambda b, i: (b * nq + i, 0)), sel_spec],
        out_shape=[jax.ShapeDtypeStruct((rows, MIX_W), out_dtype), sel_shape],
        compiler_params=_cp(("arbitrary", "arbitrary")), name="nsa_select",
    )(q_arr, p32, kcvc, win3)


def _gla_tables(c):
    levels = []
    b = c
    while b >= 2:
        levels.append(b)
        b //= 2
    r = np.arange(c)
    mats = [(r[None, :] <= r[:, None]), (r[None, :] > r[:, None])]
    qm, km = [], []
    for b in levels:
        mid = (r // b) * b + b // 2
        upper = r >= mid
        qm.append(upper[:, None] & (r[None, :] >= mid[:, None]) & (r[None, :] <= r[:, None]))
        km.append((~upper)[:, None] & (r[None, :] > r[:, None]) & (r[None, :] < mid[:, None]))
    return np.concatenate(mats + qm + km, axis=0).astype(np.float32), levels


def _gla_kernel(q_ref, k_ref, v_ref, r_ref, misc_ref, wal_ref, bal_ref, gg_ref, m_ref, s0_ref, o_ref, st_ref, s_scr,
                *, tt, c, levels, n_tiles):
    t_idx = pl.program_id(1)
    nl = len(levels)

    @pl.when(t_idx == 0)
    def _():
        s_scr[...] = s0_ref[0]

    lane_head = _iota((1, MIX_W), 1) // HEAD_DIM
    same_head = (_iota((MIX_W, 1), 0) // HEAD_DIM) == lane_head
    eye = jnp.where(_iota((MIX_W, MIX_W), 0) == _iota((MIX_W, MIX_W), 1), 1.0, 0.0).astype(BF16)
    ones_head = jnp.where(same_head, 1.0, 0.0)
    rr, cc = _iota((c, c), 0), _iota((c, c), 1)
    pair_masks = [((rr // b) == (cc // b)) & ((rr % b) >= b // 2) & ((cc % b) < b // 2) for b in levels]
    diag_mask = rr == cc

    z = _dot_hi(misc_ref[...], wal_ref[...]) + bal_ref[...]
    la = (jnp.minimum(z, 0.0) - jnp.log1p(jnp.exp(-jnp.abs(z)))) * (1.0 / GATE_TAU)
    m_all = m_ref[...]

    def bd(x):
        return jnp.concatenate([jnp.where(lane_head == h, x, jnp.zeros_like(x)) for h in range(N_HEADS)], axis=0)

    def unbd(x):
        out = jnp.where(lane_head == 0, x[0:c], 0.0)
        for h in range(1, N_HEADS):
            out = out + jnp.where(lane_head == h, x[h * c:(h + 1) * c], 0.0)
        return out

    for ci in range(tt // c):
        sl = slice(ci * c, (ci + 1) * c)
        e = jnp.exp(_dot_exact01(m_all, la[sl]))
        q = q_ref[sl, :] * (HEAD_DIM ** -0.5)
        k = k_ref[sl, :]
        v = v_ref[sl, :].astype(BF16)
        e_cum, e_rest = e[0:c], e[c:2 * c]
        att = jnp.where(diag_mask[None], _dot_nt(bd(q.astype(BF16)), k.astype(BF16)).reshape(N_HEADS, c, c), 0.0)
        for li in range(nl):
            eq = e[(2 + li) * c:(3 + li) * c]
            ek = e[(2 + nl + li) * c:(3 + nl + li) * c]
            a = _dot_nt(bd((q * eq).astype(BF16)), (k * ek).astype(BF16)).reshape(N_HEADS, c, c)
            att = att + jnp.where(pair_masks[li][None], a, 0.0)
        o_intra = unbd(_dot(att.reshape(N_HEADS * c, c).astype(BF16), v))
        st = s_scr[...]
        o_inter = _dot_nt((q * e_cum).astype(BF16), st.astype(BF16))
        v_t = _dot_nt(eye, v).astype(BF16)
        upd = _dot(v_t, (k * e_rest).astype(BF16))
        s_scr[...] = st * e_cum[c - 1:c, :] + jnp.where(same_head, upd, 0.0)
        o = o_inter + o_intra
        ms = _dot_hi(o * o, ones_head) * (1.0 / HEAD_DIM)
        o = o * lax.rsqrt(ms + EPS) * gg_ref[...]
        r = r_ref[sl, :]
        o_ref[sl, :] = (o * (r * jax.nn.sigmoid(r))).astype(o_ref.dtype)

    @pl.when(t_idx == n_tiles - 1)
    def _():
        st_ref[0] = s_scr[...]


def _gla(p32, wal_pad, b_alpha, g_gla4, st0, *, n_seq, t_seq, tt, out_dtype):
    c = min(GLA_CHUNK, t_seq)
    m_all, levels = _gla_tables(c)
    n_tiles = t_seq // tt
    rows = n_seq * t_seq
    col = lambda name: pl.BlockSpec((tt, MIX_W), lambda b, t, o=_OFF[name] // MIX_W: (b * n_tiles + t, o))
    kern = functools.partial(_gla_kernel, tt=tt, c=c, levels=levels, n_tiles=n_tiles)
    return pl.pallas_call(
        kern, grid=(n_seq, n_tiles),
        in_specs=[col('d_q'), col('d_k'), col('d_v'), col('d_r'),
                  pl.BlockSpec((tt, 128), lambda b, t: (b * n_tiles + t, _OFF['misc'] // 128)),
                  pl.BlockSpec((128, MIX_W), lambda b, t: (0, 0)),
                  pl.BlockSpec((1, MIX_W), lambda b, t: (0, 0)),
                  pl.BlockSpec((1, MIX_W), lambda b, t: (0, 0)),
                  pl.BlockSpec(m_all.shape, lambda b, t: (0, 0)),
                  pl.BlockSpec((1, MIX_W, MIX_W), lambda b, t: (b, 0, 0))],
        out_specs=[pl.BlockSpec((tt, MIX_W), lambda b, t: (b * n_tiles + t, 0)),
                   pl.BlockSpec((1, MIX_W, MIX_W), lambda b, t: (b, 0, 0))],
        out_shape=[jax.ShapeDtypeStruct((rows, MIX_W), out_dtype),
                   jax.ShapeDtypeStruct((n_seq, MIX_W, MIX_W), F32)],
        scratch_shapes=[pltpu.VMEM((MIX_W, MIX_W), F32)],
        compiler_params=_cp(("arbitrary", "arbitrary")), name="gla",
    )(p32, p32, p32, p32, p32, wal_pad, b_alpha.reshape(1, MIX_W), g_gla4, jnp.asarray(m_all, dtype=BF16), st0)


def _dwconv3(u, w, carry_ref, fix_refs, t_seq, tm, first_tile):
    row = _iota((tm, 1), 0)
    u1 = pltpu.roll(u, 1, axis=0)
    u2 = pltpu.roll(u, 2, axis=0)
    if fix_refs is None:
        prev = jnp.where(first_tile, 0.0, carry_ref[0:2, :])
        u1 = jnp.where(row == 0, prev[1:2], u1)
        u2 = jnp.where(row == 0, prev[0:1], jnp.where(row == 1, prev[1:2], u2))
        carry_ref[0:2, :] = u[tm - 2:tm]
    else:
        pos = row % t_seq
        u1 = jnp.where(pos == 0, fix_refs[0][...], u1)
        u2 = jnp.where(pos < 2, fix_refs[1][...], u2)
    return w[0:1] * u2 + w[1:2] * u1 + w[2:3] * u


def _merge_kernel(oa_ref, ob_ref, od_ref, cin_ref, cb_ref, cc_ref, gate_ref, x_ref, g1_ref, wconv_ref, wb_ref,
                  wo_ref, *rest, tm, t_seq, tiles_per_seq, per_row):
    if per_row:
        fix1_ref, fix2_ref, o_ref, carry_ref = rest
        fix = (fix1_ref, fix2_ref)
    else:
        o_ref, carry_ref = rest
        fix = None
    first = (pl.program_id(0) % tiles_per_seq) == 0
    u = cc_ref[...] * cin_ref[...]
    o_c = cb_ref[...] * _dwconv3(u, wconv_ref[...], carry_ref, fix, t_seq, tm, first)
    branches = (oa_ref[...], ob_ref[...], o_c, od_ref[...])
    merged = jnp.zeros((tm, D_MODEL), F32)
    for bi, br in enumerate(branches):
        gate = jax.nn.sigmoid(gate_ref[:, bi * D_MODEL:(bi + 1) * D_MODEL])
        merged = merged + gate * _dot(br.astype(BF16), wb_ref[bi])
    o_ref[...] = x_ref[...] + g1_ref[0] * _dot(merged.astype(BF16), wo_ref[...])


def _merge(oa, ob, od, p32, x, g1, conv_c, wb, wo, fix, *, tm, t_seq):
    rows = x.shape[0]
    per_row = fix is not None
    tiles_per_seq = max(t_seq // tm, 1)
    rowblk = lambda width, colblk=0: pl.BlockSpec((tm, width), lambda i: (i, colblk))
    in_specs = [rowblk(MIX_W), rowblk(MIX_W), rowblk(MIX_W),
                rowblk(MIX_W, _OFF['c_in'] // MIX_W), rowblk(MIX_W, _OFF['c_b'] // MIX_W),
                rowblk(MIX_W, _OFF['c_c'] // MIX_W), rowblk(N_HEADS * D_MODEL, 0), rowblk(D_MODEL),
                _mod_spec(g1, tm, t_seq), _const_spec((CONV_W, MIX_W)), _const_spec((N_HEADS, MIX_W, D_MODEL)),
                _const_spec((D_MODEL, D_MODEL))]
    args = [oa, ob, od, p32, p32, p32, p32, x, g1, conv_c, wb, wo]
    if per_row:
        in_specs += [rowblk(MIX_W), rowblk(MIX_W)]
        args += list(fix)
    kern = functools.partial(_merge_kernel, tm=tm, t_seq=t_seq, tiles_per_seq=tiles_per_seq, per_row=per_row)
    return pl.pallas_call(
        kern, grid=(rows // tm,), in_specs=in_specs, out_specs=rowblk(D_MODEL),
        out_shape=jax.ShapeDtypeStruct((rows, D_MODEL), F32),
        scratch_shapes=[pltpu.VMEM((8, MIX_W), F32)],
        compiler_params=_cp(("arbitrary",)), name="merge",
    )(*args)


def _ffn_kernel(x_ref, gn_ref, sc_ref, sh_ref, g2_ref, wa_ref, wg_ref, wconv_ref, bf_ref, wout_ref, gf_ref, *rest,
                tm, tf, t_seq, tiles_per_seq, per_row):
    if per_row:
        fix1_ref, fix2_ref, o_ref, y_ref, a_ref, carry_ref = rest
    else:
        o_ref, y_ref, a_ref, carry_ref = rest
    first = (pl.program_id(0) % tiles_per_seq) == 0
    x = x_ref[...]
    h = _norm_mod(x, gn_ref[...], sc_ref[0], sh_ref[0]).astype(BF16)
    acc = jnp.zeros((tm, D_MODEL), F32)
    for f in range(D_FF // tf):
        fs = slice(f * tf, (f + 1) * tf)
        a = _dot(h, wa_ref[:, fs])
        g = _dot(h, wg_ref[:, fs])
        a_ref[:, fs] = a if per_row else a[tm - 8:tm]
        fix = (fix1_ref.at[:, fs], fix2_ref.at[:, fs]) if per_row else None
        conv = _dwconv3(a, wconv_ref[:, fs], carry_ref.at[:, fs], fix, t_seq, tm, first)
        pre = conv + bf_ref[:, fs]
        act = pre * jax.nn.sigmoid(pre) * g
        acc = acc + _dot(act.astype(BF16), wout_ref[fs, :])
    xn = x + g2_ref[0] * acc
    o_ref[...] = xn
    y_ref[...] = xn * lax.rsqrt(jnp.mean(xn * xn, axis=-1, keepdims=True) + EPS) * gf_ref[...]


def _ffn(x, gn, sc, sh, g2, wa, wg, conv_ffn, b_ffn, wout, g_final, fix, *, tm, t_seq):
    rows = x.shape[0]
    per_row = fix is not None
    tiles_per_seq = max(t_seq // tm, 1)
    tf = 1408
    rowblk = lambda width: pl.BlockSpec((tm, width), lambda i: (i, 0))
    in_specs = [rowblk(D_MODEL), _const_spec((1, D_MODEL)), _mod_spec(sc, tm, t_seq), _mod_spec(sh, tm, t_seq),
                _mod_spec(g2, tm, t_seq), _const_spec((D_MODEL, D_FF)), _const_spec((D_MODEL, D_FF)),
                _const_spec((CONV_W, D_FF)), _const_spec((1, D_FF)), _const_spec((D_FF, D_MODEL)),
                _const_spec((1, D_MODEL))]
    args = [x, gn, sc, sh, g2, wa, wg, conv_ffn, b_ffn, wout, g_final]
    if per_row:
        in_specs += [rowblk(D_FF), rowblk(D_FF)]
        args += list(fix)
    kern = functools.partial(_ffn_kernel, tm=tm, tf=tf, t_seq=t_seq, tiles_per_seq=tiles_per_seq, per_row=per_row)
    return pl.pallas_call(
        kern, grid=(rows // tm,), in_specs=in_specs,
        out_specs=[rowblk(D_MODEL), rowblk(D_MODEL), pl.BlockSpec((tm if per_row else 8, D_FF), lambda i: (i, 0))],
        out_shape=[jax.ShapeDtypeStruct((rows, D_MODEL), F32), jax.ShapeDtypeStruct((rows, D_MODEL), F32),
                   jax.ShapeDtypeStruct((rows if per_row else rows // tm * 8, D_FF), F32)],
        scratch_shapes=[pltpu.VMEM((8, D_FF), F32)],
        compiler_params=_cp(("arbitrary",)), name="ffn",
    )(*args)


def _permute_w_in(w):
    offs, o = {}, 0
    for name, n in _IN_SPLITS:
        offs[name] = (o, n)
        o += n

    def c(name):
        s, n = offs[name]
        return w[:, s:s + n]

    pieces = [c('gate'), c('a_k'), c('a_v'), c('a_q'), c('a_qi'), c('b_q'), c('b_cmp'), c('b_slc'), c('b_win'),
              c('a_ki'), c('a_wi'), c('b_g'), c('d_a'), jnp.zeros((w.shape[0], 32), w.dtype),
              c('c_in'), c('c_b'), c('c_c'), c('d_q'), c('d_k'), c('d_v'), c('d_r')]
    return jnp.concatenate(pieces, axis=1).astype(BF16)


def _cols(p, name, width):
    return p[:, _OFF[name]:_OFF[name] + width]


def _state_to_bd(s0):
    b = s0.shape[0]
    s0t = jnp.swapaxes(s0, 2, 3)
    eye = jnp.eye(N_HEADS, dtype=s0.dtype)[None, :, None, :, None]
    return (s0t[:, :, :, None, :] * eye).reshape(b, MIX_W, MIX_W)


def _bd_to_state(st):
    b = st.shape[0]
    st5 = st.reshape(b, N_HEADS, HEAD_DIM, N_HEADS, HEAD_DIM)
    return jnp.stack([jnp.swapaxes(st5[:, h, :, h, :], 1, 2) for h in range(N_HEADS)], axis=1)


def _conv_fix(state, t_seq):
    b, _, c = state.shape
    fix1 = jnp.concatenate([state[:, 1:2], jnp.zeros((b, t_seq - 1, c), state.dtype)], axis=1)
    fix2 = jnp.concatenate([state, jnp.zeros((b, t_seq - 2, c), state.dtype)], axis=1)
    return fix1.reshape(b * t_seq, c), fix2.reshape(b * t_seq, c)


def _layer_weights(l, w_in, w_alpha, g_gla, w_branch, w_out, w_ffn_in, w_ffn_out):
    wal_pad = jnp.zeros((128, MIX_W), F32).at[MISC_DA:MISC_DA + GATE_RANK].set(w_alpha[l])
    return dict(w_in=_permute_w_in(w_in[l]), wal=wal_pad, gg=jnp.tile(g_gla[l], N_HEADS).reshape(1, MIX_W),
                wb=w_branch[l].astype(BF16), wo=w_out[l].astype(BF16),
                wa=w_ffn_in[l][:, :D_FF].astype(BF16), wg=w_ffn_in[l][:, D_FF:].astype(BF16),
                wout=w_ffn_out[l].astype(BF16))


def kernel(x_prompt, x_sample, cache_a_kv, cache_a_idx, cache_b_cmp, cache_b_slc, state_b_win, state_c_conv, state_d_gla, state_ffn_conv, page_table, c_prompt, c_sample, w_ada, b_ada, g_norm1, w_in, w_cmp_pos, conv_c, w_alpha, b_alpha, g_gla, w_branch, w_out, g_norm2, w_ffn_in, conv_ffn, b_ffn, w_ffn_out, g_final):
    bp, seq, d = x_prompt.shape
    bs, tdec, _ = x_sample.shape
    depth = w_ada.shape[0]
    n_pages = page_table.shape[1]
    past = n_pages * PAGE
    w_buf = state_b_win.shape[2]
    assert bp == 1 and d == D_MODEL and seq % 1024 == 0 and tdec == 8 and w_buf == WINDOW

    n_c = bp + bs
    c_all = jnp.concatenate([c_prompt, c_sample, jnp.zeros((-n_c % 8, d), F32)], axis=0)
    mod = _ada(c_all, w_ada, b_ada)

    xp = x_prompt.reshape(bp * seq, d)
    xs = x_sample.reshape(bs * tdec, d)
    rs = bs * tdec
    tail_rows = KEY_TILE
    lps = past + tail_rows
    tkm_s = max(t for t in (1, 2, 3, 4, 6, 8, 11) if (lps // KEY_TILE) % t == 0)
    gfin = g_final.reshape(1, d)
    outs_p, outs_s = [], []
    y_p = y_s = None
    for l in range(depth):
        lw = _layer_weights(l, w_in, w_alpha, g_gla, w_branch, w_out, w_ffn_in, w_ffn_out)
        mp = [mod[l, :bp, k * d:(k + 1) * d].reshape(bp, 1, d) for k in range(6)]
        ms = [jnp.repeat(mod[l, bp:n_c, k * d:(k + 1) * d], tdec, axis=0).reshape(1, rs, d) for k in range(6)]
        gn1, gn2 = g_norm1[l].reshape(1, d), g_norm2[l].reshape(1, d)
        bffn = b_ffn[l].reshape(1, D_FF)

        p32, _ = _proj(xs, gn1, ms[1], ms[0], lw['w_in'], rs, rs)
        new = {n: _cols(p32, n, w).reshape(bs, tdec, w) for n, w in
               (('a_k', 2 * MIX_W), ('misc', IDX_DIM), ('b_cmp', 128), ('b_slc', 128), ('b_win', 128))}
        tails = [jnp.concatenate([new[n], jnp.zeros((bs, tail_rows - tdec, new[n].shape[-1]), F32)], axis=1)
                 for n in ('a_k', 'misc', 'b_cmp', 'b_slc')]
        tails[1] = jnp.swapaxes(tails[1], 1, 2)
        kv_s, ki_s, cmp_s, slc_s = _gather_past(
            page_table, [cache_a_kv, jnp.swapaxes(cache_a_idx, 2, 3), cache_b_cmp, cache_b_slc], tails,
            (BF16, BF16, F32, BF16), (False, True, False, False), l)
        seq_s = past + tdec
        bias_a = _dsa_select(p32, p32, ki_s, 0, IDX_DIM, n_seq=bs, t_seq=tdec, tq=tdec, qoff=past,
                             k_sel=min(A_TOPK, seq_s // 4), bias_dtype=F32)
        o_a = _flash(p32, _OFF['a_q'] // MIX_W, kv_s, 0, bias_a, n_seq=bs, t_seq=tdec, tq=tdec, tkm=tkm_s, qoff=past,
                     shared=False, out_dtype=F32)
        ncp = -(-(-(-seq_s // CMP_STRIDE) - 1) // 128) * 128
        kcvc = _compress(cmp_s, w_cmp_pos[l], seq_s, ncp)
        win_full = jnp.concatenate([state_b_win[l], new['b_win']], axis=1)
        win_pad = jnp.concatenate([win_full, jnp.zeros((bs, -(w_buf + tdec) % 16, 128), F32)], axis=1)
        ocw, bias_b = _nsa_select(p32, p32, kcvc, win_pad, 0, n_seq=bs, t_seq=tdec, tq=tdec, qoff=past, seq_len=seq_s,
                                  lp=lps, win_dyn=False, win_pos0=past - w_buf, bias_dtype=F32, out_dtype=F32)
        o_b = _flash(p32, _OFF['b_q'] // MIX_W, slc_s, 0, bias_b, n_seq=bs, t_seq=tdec, tq=tdec, tkm=tkm_s, qoff=past,
                     shared=True, out_dtype=F32, gate_args=(p32, ocw))
        o_d, st = _gla(p32, lw['wal'], b_alpha[l], lw['gg'], _state_to_bd(state_d_gla[l]), n_seq=bs, t_seq=tdec,
                       tt=tdec, out_dtype=F32)
        u_tail = (_cols(p32, 'c_c', MIX_W) * _cols(p32, 'c_in', MIX_W)).reshape(bs, tdec, MIX_W)[:, tdec - 2:]
        xs = _merge(o_a, o_b, o_d, p32, xs, ms[2], conv_c[l], lw['wb'], lw['wo'], _conv_fix(state_c_conv[l], tdec),
                    tm=rs, t_seq=tdec)
        xs, y_s, a_full = _ffn(xs, gn2, ms[4], ms[3], ms[5], lw['wa'], lw['wg'], conv_ffn[l], bffn, lw['wout'], gfin,
                               _conv_fix(state_ffn_conv[l], tdec), tm=rs, t_seq=tdec)
        outs_s.append((new['a_k'], new['misc'], new['b_cmp'], new['b_slc'], win_full[:, tdec:], u_tail,
                       _bd_to_state(st), a_full.reshape(bs, tdec, D_FF)[:, tdec - 2:]))

        p32, p16 = _proj(xp, gn1, mp[1], mp[0], lw['w_in'], 1024, seq)
        p16_3 = p16.reshape(bp, seq, NP)
        a_kv = _cols(p32, 'a_k', 2 * MIX_W).reshape(bp, seq, 2 * MIX_W)
        a_idx = _cols(p32, 'misc', IDX_DIM).reshape(bp, seq, IDX_DIM)
        b_cmp = _cols(p32, 'b_cmp', 128).reshape(bp, seq, 128)
        b_slc = _cols(p32, 'b_slc', 128).reshape(bp, seq, 128)
        b_win = _cols(p32, 'b_win', 128).reshape(bp, seq, 128)[:, seq - min(WINDOW, seq):]
        k_sel = min(A_TOPK, seq // 4)
        bias_a = _dsa_select(p16, p32, p16_3, _OFF['misc'] // 128, 128, n_seq=bp, t_seq=seq, tq=128, qoff=0,
                             k_sel=k_sel, bias_dtype=BF16)
        q_scale = HEAD_DIM ** -0.5 * float(np.log2(np.e))
        o_a = jnp.transpose(_flash_t(jnp.transpose(_cols(p32, 'a_q', MIX_W) * q_scale).astype(BF16), p16,
                                     _OFF['a_k'] // MIX_W, MIX_W,
                                     jnp.transpose(_cols(p16, 'a_v', MIX_W)), bias_a, t_seq=seq, tq=512, tkm=2,
                                     shared=False))
        ncp = -(-(-(-seq // CMP_STRIDE) - 1) // 128) * 128
        kcvc = _compress(b_cmp, w_cmp_pos[l], seq, ncp)
        ocw, sel_b = _nsa_select(p16, p32, kcvc, p16_3, _OFF['b_win'] // 128, n_seq=bp, t_seq=seq, tq=128, qoff=0,
                                 seq_len=seq, lp=seq, win_dyn=True, win_pos0=0, bias_dtype=None, out_dtype=BF16)
        vt_b = jnp.transpose(p16[:, _OFF['b_slc'] + HEAD_DIM:_OFF['b_slc'] + 2 * HEAD_DIM])
        o_b = jnp.transpose(_flash_t(jnp.transpose(_cols(p32, 'b_q', MIX_W) * q_scale).astype(BF16), p16,
                                     _OFF['b_slc'] // 128, 128, vt_b,
                                     sel_b, t_seq=seq, tq=512, tkm=2, shared=True,
                                     gate_args=(jnp.transpose(_cols(p32, 'misc', 128)), jnp.transpose(ocw))))
        o_d, st = _gla(p32, lw['wal'], b_alpha[l], lw['gg'], jnp.zeros((bp, MIX_W, MIX_W), F32), n_seq=bp, t_seq=seq,
                       tt=512, out_dtype=BF16)
        u_tail = (_cols(p32, 'c_c', MIX_W) * _cols(p32, 'c_in', MIX_W)).reshape(bp, seq, MIX_W)[:, seq - 2:]
        xp = _merge(o_a, o_b, o_d, p32, xp, mp[2], conv_c[l], lw['wb'], lw['wo'], None, tm=256, t_seq=seq)
        xp, y_p, a_full = _ffn(xp, gn2, mp[4], mp[3], mp[5], lw['wa'], lw['wg'], conv_ffn[l], bffn, lw['wout'], gfin,
                               None, tm=512, t_seq=seq)
        outs_p.append((a_kv, a_idx, b_cmp, b_slc, b_win, u_tail, _bd_to_state(st),
                       a_full[a_full.shape[0] - 2:].reshape(bp, CONV_W - 1, D_FF)))

    sp = [jnp.stack(z) for z in zip(*outs_p)]
    ss = [jnp.stack(z) for z in zip(*outs_s)]
    res = [y_p.reshape(bp, seq, d), y_s.reshape(bs, tdec, d)]
    for a, b in zip(sp, ss):
        res += [a, b]
    return tuple(res)
```
